```python
import math
import jax, jax.numpy as jnp
from jax import lax
import numpy as np

D_MODEL = 4096
BATCH = 4
SEQ = 2048
DEPTH = 2
DEC_BATCH = 16
DEC_SEQ = 32
PAST_LEN = 1024

CHUNK = 64
N_EVEN = (DEPTH + 1) // 2
N_ODD = DEPTH // 2
EPS = 1e-6
HEAD_DIM_AB = 128
N_HEADS_SB = D_MODEL // (2 * HEAD_DIM_AB)
N_HEADS_BAND = D_MODEL // (2 * HEAD_DIM_AB)
W_SB = N_HEADS_SB * HEAD_DIM_AB
W_BAND = N_HEADS_BAND * HEAD_DIM_AB
SB_BLOCK = 128
BAND_PREV_CHUNKS = 8
BAND_HIST = BAND_PREV_CHUNKS * CHUNK
REL_CLIP = 2 * CHUNK
HEAD_DIM_C = 64
N_HEADS_C = D_MODEL // HEAD_DIM_C
N_KV_C = 8
WINDOW = 128
SWA_PREV_CHUNKS = WINDOW // CHUNK
ROPE_THETA = 500000.0
ROPE_DIM = HEAD_DIM_C // 4
D_FF = ((8 * D_MODEL // 3 + 255) // 256) * 256

kernel_name = "stickbreak_band_swa_stream_step"


def _rmsnorm(x, g):
    x32 = x.astype(jnp.float32)
    y = x32 * lax.rsqrt(jnp.mean(x32 * x32, axis=-1, keepdims=True) + EPS)
    return (y * g.astype(jnp.float32)).astype(x.dtype)


def _swiglu(x, wg, wu, wd):
    return (jax.nn.silu(x @ wg) * (x @ wu)) @ wd


def _rope_partial(x, pos):
    half = ROPE_DIM // 2
    inv = jnp.power(ROPE_THETA, -jnp.arange(half, dtype=jnp.float32) / half)
    ang = pos.astype(jnp.float32)[:, None] * inv[None, :]
    cos = jnp.cos(ang)[None, :, None, :]
    sin = jnp.sin(ang)[None, :, None, :]
    x1 = x[..., :half].astype(jnp.float32)
    x2 = x[..., half:ROPE_DIM].astype(jnp.float32)
    rot = jnp.concatenate([x1 * cos - x2 * sin, x2 * cos + x1 * sin], axis=-1).astype(x.dtype)
    return jnp.concatenate([rot, x[..., ROPE_DIM:]], axis=-1)


def _band_mask(q_pos, k_pos, n_prev):
    qc = q_pos[:, None] // CHUNK
    kc = k_pos[None, :] // CHUNK
    return (k_pos[None, :] >= 0) & (kc <= qc) & (kc >= qc - n_prev)


def _band_logit_bias(q_pos, k_pos, n_prev, rel_table):
    mask = _band_mask(q_pos, k_pos, n_prev)
    if rel_table is None:
        b = jnp.zeros((1,) + mask.shape, jnp.float32)
    else:
        d = jnp.clip(q_pos[:, None] - k_pos[None, :], -REL_CLIP, REL_CLIP) + REL_CLIP
        b = rel_table.astype(jnp.float32)[:, d]
    return jnp.where(mask[None], b, -jnp.inf)


def _softmax_attn(q, k, v, bias, sinks):
    B, Tq, Hq, D = q.shape
    Tk, Hkv = k.shape[1], k.shape[2]
    G = Hq // Hkv
    qg = q.reshape(B, Tq, Hkv, G, D)
    s = jnp.einsum('bqhgd,bkhd->bhgqk', qg, k).astype(jnp.float32) * (D ** -0.5)
    s = s + jnp.broadcast_to(bias, (Hq, Tq, Tk)).reshape(Hkv, G, Tq, Tk)
    if sinks is None:
        p = jax.nn.softmax(s, axis=-1)
    else:
        sk = jnp.broadcast_to(sinks.astype(jnp.float32).reshape(Hkv, G, 1, 1), s.shape[:-1] + (1,))
        p = jax.nn.softmax(jnp.concatenate([s, sk], axis=-1), axis=-1)[..., :-1]
    o = jnp.einsum('bhgqk,bkhd->bqhgd', p.astype(v.dtype), v)
    return o.reshape(B, Tq, Hq, D)


def _band_attn_prompt(q, k, v, n_prev, rel_table, sinks):
    B, S, Hq, D = q.shape
    nc = S // CHUNK
    hist = n_prev * CHUNK
    band = hist + CHUNK
    pad = ((0, 0), (hist, 0), (0, 0), (0, 0))
    kp = jnp.pad(k, pad)
    vp = jnp.pad(v, pad)
    qc = jnp.moveaxis(q.reshape(B, nc, CHUNK, Hq, D), 1, 0)

    def one(args):
        c, q_blk = args
        start = c * CHUNK
        kb = lax.dynamic_slice_in_dim(kp, start, band, axis=1)
        vb = lax.dynamic_slice_in_dim(vp, start, band, axis=1)
        q_pos = start + jnp.arange(CHUNK)
        k_pos = start - hist + jnp.arange(band)
        return _softmax_attn(q_blk, kb, vb, _band_logit_bias(q_pos, k_pos, n_prev, rel_table), sinks)

    out = lax.map(one, (jnp.arange(nc), qc))
    return jnp.moveaxis(out, 0, 1).reshape(B, S, Hq, D)


def _band_attn_step(q, k_new, v_new, k_cache, v_cache, n_prev, rel_table, sinks):
    P = k_cache.shape[1]
    T = q.shape[1]
    k = jnp.concatenate([k_cache, k_new], axis=1)
    v = jnp.concatenate([v_cache, v_new], axis=1)
    q_pos = PAST_LEN + jnp.arange(T)
    k_pos = PAST_LEN - P + jnp.arange(P + T)
    o = _softmax_attn(q, k, v, _band_logit_bias(q_pos, k_pos, n_prev, rel_table), sinks)
    return o, k[:, -P:], v[:, -P:]


def _stick_breaking(q, k, v, q_pos, k_pos):
    D = q.shape[-1]
    z = jnp.einsum('bqhd,bkhd->bhqk', q, k).astype(jnp.float32) * (D ** -0.5)
    valid = k_pos[None, :] < q_pos[:, None]
    log_beta = jax.nn.log_sigmoid(z)
    log_1m = jnp.where(valid, jax.nn.log_sigmoid(-z), 0.0)
    after = lax.cumsum(log_1m, axis=3, reverse=True) - log_1m
    w = jnp.where(valid, jnp.exp(log_beta + after), 0.0)
    return jnp.einsum('bhqk,bkhd->bqhd', w.astype(v.dtype), v)


def _stick_breaking_prompt(q, k, v):
    B, S, H, D = q.shape
    nb = S // SB_BLOCK
    qb = jnp.moveaxis(q.reshape(B, nb, SB_BLOCK, H, D), 1, 0)
    k_pos = jnp.arange(S)

    def one(args):
        i, q_blk = args
        q_pos = i * SB_BLOCK + jnp.arange(SB_BLOCK)
        return _stick_breaking(q_blk, k, v, q_pos, k_pos)

    out = lax.map(one, (jnp.arange(nb), qb))
    return jnp.moveaxis(out, 0, 1).reshape(B, S, H, D)


def _split_ab(hn, w_in):
    B, T, _ = hn.shape
    proj = hn @ w_in
    idx = [W_SB, 2 * W_SB, 3 * W_SB, 3 * W_SB + W_BAND, 3 * W_SB + 2 * W_BAND]
    qa, ka, va, qb, kb, vb = jnp.split(proj, idx, axis=-1)
    ra = lambda t: t.reshape(B, T, N_HEADS_SB, HEAD_DIM_AB)
    rb = lambda t: t.reshape(B, T, N_HEADS_BAND, HEAD_DIM_AB)
    return ra(qa), ra(ka), ra(va), rb(qb), rb(kb), rb(vb)


def _split_c(hn, w_qkv, pos):
    B, T, _ = hn.shape
    proj = hn @ w_qkv
    wq = N_HEADS_C * HEAD_DIM_C
    wk = N_KV_C * HEAD_DIM_C
    q, k, v = jnp.split(proj, [wq, wq + wk], axis=-1)
    q = _rope_partial(q.reshape(B, T, N_HEADS_C, HEAD_DIM_C), pos)
    k = _rope_partial(k.reshape(B, T, N_KV_C, HEAD_DIM_C), pos)
    return q, k, v.reshape(B, T, N_KV_C, HEAD_DIM_C)


def setup_inputs(seed: int = 0) -> dict:
    key = jax.random.key(seed)
    ks = jax.random.split(key, 24)
    f32 = jnp.float32

    def nrm(k, shape, scale):
        return jax.random.normal(k, shape, f32) * scale

    p_band = min(BAND_HIST, PAST_LEN)
    p_swa = min(WINDOW, PAST_LEN)
    in_ab = 3 * W_SB + 3 * W_BAND
    qkv_c = (N_HEADS_C + 2 * N_KV_C) * HEAD_DIM_C
    return {
        "x_prompt": nrm(ks[0], (BATCH, SEQ, D_MODEL), 1.0),
        "x_sample": nrm(ks[1], (DEC_BATCH, DEC_SEQ, D_MODEL), 1.0),
        "cache_sb_k": nrm(ks[2], (N_EVEN, DEC_BATCH, PAST_LEN, N_HEADS_SB, HEAD_DIM_AB), 1.0),
        "cache_sb_v": nrm(ks[3], (N_EVEN, DEC_BATCH, PAST_LEN, N_HEADS_SB, HEAD_DIM_AB), 1.0),
        "cache_band_k": nrm(ks[4], (N_EVEN, DEC_BATCH, p_band, N_HEADS_BAND, HEAD_DIM_AB), 1.0),
        "cache_band_v": nrm(ks[5], (N_EVEN, DEC_BATCH, p_band, N_HEADS_BAND, HEAD_DIM_AB), 1.0),
        "cache_swa_k": nrm(ks[6], (N_ODD, DEC_BATCH, p_swa, N_KV_C, HEAD_DIM_C), 1.0),
        "cache_swa_v": nrm(ks[7], (N_ODD, DEC_BATCH, p_swa, N_KV_C, HEAD_DIM_C), 1.0),
        "norm_mix": 1.0 + nrm(ks[8], (DEPTH, D_MODEL), 0.02),
        "norm_ffn": 1.0 + nrm(ks[9], (DEPTH, D_MODEL), 0.02),
        "norm_final": 1.0 + nrm(ks[10], (D_MODEL,), 0.02),
        "w_in_ab": nrm(ks[11], (N_EVEN, D_MODEL, in_ab), D_MODEL ** -0.5),
        "w_out_ab": nrm(ks[12], (N_EVEN, W_SB + W_BAND, D_MODEL), (W_SB + W_BAND) ** -0.5),
        "rel_bias": nrm(ks[13], (N_EVEN, N_HEADS_BAND, 2 * REL_CLIP + 1), 0.1),
        "w_qkv_c": nrm(ks[14], (N_ODD, D_MODEL, qkv_c), D_MODEL ** -0.5),
        "w_out_c": nrm(ks[15], (N_ODD, N_HEADS_C * HEAD_DIM_C, D_MODEL), (N_HEADS_C * HEAD_DIM_C) ** -0.5),
        "sinks": nrm(ks[16], (N_ODD, N_HEADS_C), 1.0),
        "w_gate": nrm(ks[17], (DEPTH, D_MODEL, D_FF), D_MODEL ** -0.5),
        "w_up": nrm(ks[18], (DEPTH, D_MODEL, D_FF), D_MODEL ** -0.5),
        "w_down": nrm(ks[19], (DEPTH, D_FF, D_MODEL), D_FF ** -0.5),
    }


def reference(x_prompt, x_sample, cache_sb_k, cache_sb_v, cache_band_k, cache_band_v,
              cache_swa_k, cache_swa_v, norm_mix, norm_ffn, norm_final, w_in_ab, w_out_ab,
              rel_bias, w_qkv_c, w_out_c, sinks, w_gate, w_up, w_down):
    hp, hs = x_prompt, x_sample
    Bp, S, _ = hp.shape
    Bs, T, _ = hs.shape
    pos_p = jnp.arange(S)
    pos_s = PAST_LEN + jnp.arange(T)
    k_pos_sb = jnp.arange(PAST_LEN + T)

    sbk_p, sbv_p, bk_p, bv_p, ck_p, cv_p = [], [], [], [], [], []
    sbk_s, sbv_s, bk_s, bv_s, ck_s, cv_s = [], [], [], [], [], []

    for l in range(DEPTH):
        hn_p = _rmsnorm(hp, norm_mix[l])
        hn_s = _rmsnorm(hs, norm_mix[l])
        if l % 2 == 0:
            e = l // 2
            qa, ka, va, qb, kb, vb = _split_ab(hn_p, w_in_ab[e])
            oa = _stick_breaking_prompt(qa, ka, va)
            ob = _band_attn_prompt(qb, kb, vb, BAND_PREV_CHUNKS, rel_bias[e], None)
            hp = hp + jnp.concatenate([oa.reshape(Bp, S, W_SB), ob.reshape(Bp, S, W_BAND)], axis=-1) @ w_out_ab[e]
            nb = min(BAND_HIST, S)
            sbk_p.append(ka)
            sbv_p.append(va)
            bk_p.append(kb[:, S - nb:])
            bv_p.append(vb[:, S - nb:])
            qa, ka, va, qb, kb, vb = _split_ab(hn_s, w_in_ab[e])
            k_all = jnp.concatenate([cache_sb_k[e], ka], axis=1)
            v_all = jnp.concatenate([cache_sb_v[e], va], axis=1)
            oa = _stick_breaking(qa, k_all, v_all, pos_s, k_pos_sb)
            ob, nbk, nbv = _band_attn_step(qb, kb, vb, cache_band_k[e], cache_band_v[e],
                                           BAND_PREV_CHUNKS, rel_bias[e], None)
            hs = hs + jnp.concatenate([oa.reshape(Bs, T, W_SB), ob.reshape(Bs, T, W_BAND)], axis=-1) @ w_out_ab[e]
            sbk_s.append(ka)
            sbv_s.append(va)
            bk_s.append(nbk)
            bv_s.append(nbv)
        else:
            o = l // 2
            q, k, v = _split_c(hn_p, w_qkv_c[o], pos_p)
            oc = _band_attn_prompt(q, k, v, SWA_PREV_CHUNKS, None, sinks[o])
            hp = hp + oc.reshape(Bp, S, N_HEADS_C * HEAD_DIM_C) @ w_out_c[o]
            nw = min(WINDOW, S)
            ck_p.append(k[:, S - nw:])
            cv_p.append(v[:, S - nw:])
            q, k, v = _split_c(hn_s, w_qkv_c[o], pos_s)
            oc, nck, ncv = _band_attn_step(q, k, v, cache_swa_k[o], cache_swa_v[o],
                                           SWA_PREV_CHUNKS, None, sinks[o])
            hs = hs + oc.reshape(Bs, T, N_HEADS_C * HEAD_DIM_C) @ w_out_c[o]
            ck_s.append(nck)
            cv_s.append(ncv)
        hp = hp + _swiglu(_rmsnorm(hp, norm_ffn[l]), w_gate[l], w_up[l], w_down[l])
        hs = hs + _swiglu(_rmsnorm(hs, norm_ffn[l]), w_gate[l], w_up[l], w_down[l])

    y_prompt = _rmsnorm(hp, norm_final)
    y_sample = _rmsnorm(hs, norm_final)
    return (y_prompt, y_sample,
            jnp.stack(sbk_p), jnp.stack(sbv_p), jnp.stack(bk_p), jnp.stack(bv_p),
            jnp.stack(ck_p), jnp.stack(cv_p),
            jnp.stack(sbk_s), jnp.stack(sbv_s), jnp.stack(bk_s), jnp.stack(bv_s),
            jnp.stack(ck_s), jnp.stack(cv_s))
```

```python
import functools
import math

import numpy as np
import jax
import jax.numpy as jnp
from jax import lax
from jax.experimental import pallas as pl
from jax.experimental.pallas import tpu as pltpu

F32 = jnp.float32
BF16 = jnp.bfloat16

CHUNK = 64
EPS = 1e-6
HEAD_DIM_AB = 128
BAND_PREV_CHUNKS = 8
BAND_HIST = BAND_PREV_CHUNKS * CHUNK
REL_CLIP = 2 * CHUNK
HEAD_DIM_C = 64
N_KV_C = 8
SWA_PREV_CHUNKS = 2
WINDOW = SWA_PREV_CHUNKS * CHUNK
ROPE_THETA = 500000.0
ROPE_DIM = HEAD_DIM_C // 4

LANES = 128
V7X_VMEM_LIMIT_BYTES = 56 * 1024 * 1024

NEG = -1e30

SB_TILE = 256
SB_STEP_KEYS = 128
BAND_TQ = 128
SWA_TQ = 128


def _params(*sem):
    return pltpu.CompilerParams(dimension_semantics=sem,
                                vmem_limit_bytes=V7X_VMEM_LIMIT_BYTES)


def _chunk_of(pos):
    return lax.shift_right_logical(pos, int(math.log2(CHUNK)))


def _pick_tm(m):
    return 1024 if m % 1024 == 0 else m


def _rmsnorm_kernel(x_ref, g_ref, o_ref):
    x = x_ref[...]
    ms = jnp.mean(x * x, axis=-1, keepdims=True)
    y = x * lax.rsqrt(ms + EPS)
    o_ref[...] = (y * g_ref[...]).astype(o_ref.dtype)


def _rmsnorm(x, g3, layer, out_dtype):
    m, d = x.shape
    tm = 256
    return pl.pallas_call(
        _rmsnorm_kernel,
        grid=(m // tm,),
        in_specs=[pl.BlockSpec((tm, d), lambda i: (i, 0)),
                  pl.BlockSpec((None, 1, d), lambda i: (layer, 0, 0))],
        out_specs=pl.BlockSpec((tm, d), lambda i: (i, 0)),
        out_shape=jax.ShapeDtypeStruct((m, d), out_dtype),
        compiler_params=_params("arbitrary"),
        name="rmsnorm",
    )(x, g3)


def _proj_kernel(a_ref, w_ref, o_ref):
    w = w_ref[...].astype(BF16)
    acc = jnp.dot(a_ref[...], w, preferred_element_type=F32)
    o_ref[...] = acc.astype(o_ref.dtype)


def _rope_tile(x, c, s1, s2):
    return (x * c + pltpu.roll(x, 8, 1) * s1
            + pltpu.roll(x, LANES - 8, 1) * s2)


def _proj_rope_kernel(a_ref, w_ref, c_ref, s1_ref, s2_ref, o_ref):
    w = w_ref[...].astype(BF16)
    acc = jnp.dot(a_ref[...], w, preferred_element_type=F32)
    c, s1, s2 = c_ref[...], s1_ref[...], s2_ref[...]
    for j in range(acc.shape[1] // LANES):
        sl = slice(j * LANES, (j + 1) * LANES)
        o_ref[:, sl] = _rope_tile(acc[:, sl], c, s1, s2).astype(o_ref.dtype)


def _proj(a, w3, layer, col0, ncols, out_dtype, rope=None, tn=512):
    m, k = a.shape
    tm = _pick_tm(m)
    tn = min(tn, ncols)
    jb = col0 // tn
    in_specs = [pl.BlockSpec((tm, k), lambda i, j: (i, 0)),
                pl.BlockSpec((None, k, tn), lambda i, j: (layer, 0, jb + j))]
    args = [a, w3]
    kern = _proj_kernel
    if rope is not None:
        nrep = rope[0].shape[0] // tm
        for t in rope:
            in_specs.append(pl.BlockSpec((tm, LANES), lambda i, j: (i % nrep, 0)))
            args.append(t)
        kern = _proj_rope_kernel
    return pl.pallas_call(
        kern,
        grid=(m // tm, ncols // tn),
        in_specs=in_specs,
        out_specs=pl.BlockSpec((tm, tn), lambda i, j: (i, j)),
        out_shape=jax.ShapeDtypeStruct((m, ncols), out_dtype),
        compiler_params=_params("arbitrary", "arbitrary"),
        name="proj",
    )(*args)


def _gate_up_kernel(a_ref, wg_ref, wu_ref, o_ref):
    a = a_ref[...]
    g = jnp.dot(a, wg_ref[...].astype(BF16), preferred_element_type=F32)
    u = jnp.dot(a, wu_ref[...].astype(BF16), preferred_element_type=F32)
    o_ref[...] = (g * jax.nn.sigmoid(g) * u).astype(o_ref.dtype)


def _gate_up(a, wg3, wu3, layer, tn=256):
    m, k = a.shape
    f = wg3.shape[2]
    tm = _pick_tm(m)
    wspec = pl.BlockSpec((None, k, tn), lambda i, j: (layer, 0, j))
    return pl.pallas_call(
        _gate_up_kernel,
        grid=(m // tm, f // tn),
        in_specs=[pl.BlockSpec((tm, k), lambda i, j: (i, 0)), wspec, wspec],
        out_specs=pl.BlockSpec((tm, tn), lambda i, j: (i, j)),
        out_shape=jax.ShapeDtypeStruct((m, f), BF16),
        compiler_params=_params("arbitrary", "arbitrary"),
        name="gate_up",
    )(a, wg3, wu3)


def _mm_res_kernel(*refs, nterms):
    a_refs = refs[:nterms]
    w_refs = refs[nterms:2 * nterms]
    r_ref = refs[2 * nterms]
    o_ref = refs[2 * nterms + 1]
    acc = r_ref[...]
    for a_ref, w_ref in zip(a_refs, w_refs):
        acc = acc + jnp.dot(a_ref[...], w_ref[...].astype(BF16),
                            preferred_element_type=F32)
    o_ref[...] = acc


def _mm_res(terms, w3, layer, tk, res, tn):
    m, n = res.shape
    tm = _pick_tm(m)
    in_specs, args = [], []
    for a, ka, _ in terms:
        in_specs.append(pl.BlockSpec((tm, tk), lambda i, j, ka=ka: (i, ka)))
        args.append(a)
    for _, _, kw in terms:
        in_specs.append(
            pl.BlockSpec((None, tk, tn), lambda i, j, kw=kw: (layer, kw, j)))
        args.append(w3)
    in_specs.append(pl.BlockSpec((tm, tn), lambda i, j: (i, j)))
    args.append(res)
    return pl.pallas_call(
        functools.partial(_mm_res_kernel, nterms=len(terms)),
        grid=(m // tm, n // tn),
        in_specs=in_specs,
        out_specs=pl.BlockSpec((tm, tn), lambda i, j: (i, j)),
        out_shape=jax.ShapeDtypeStruct((m, n), F32),
        compiler_params=_params("arbitrary", "arbitrary"),
        name="mm_res",
    )(*args)


def _sb_block(q, ks, vs, u, carry, o, scale, masked):
    z = lax.dot_general(q, ks, (((1,), (1,)), ((), ())),
                        preferred_element_type=F32) * scale
    lb = jnp.minimum(z, 0.0) - jnp.log(1.0 + jnp.exp(-jnp.abs(z)))
    l1m = lb - z
    if masked:
        row = lax.broadcasted_iota(jnp.int32, z.shape, 0)
        col = lax.broadcasted_iota(jnp.int32, z.shape, 1)
        valid = col < row
        l1m = jnp.where(valid, l1m, 0.0)
    hi = l1m.astype(BF16)
    lo = (l1m - hi.astype(F32)).astype(BF16)
    suf = (jnp.dot(hi, u, preferred_element_type=F32)
           + jnp.dot(lo, u, preferred_element_type=F32))
    w = jnp.exp(lb + suf + carry)
    if masked:
        w = jnp.where(valid, w, 0.0)
    o = o + jnp.dot(w.astype(BF16), vs, preferred_element_type=F32)
    carry = carry + jnp.sum(l1m, axis=1, keepdims=True)
    return carry, o


def _sb_prompt_kernel(q_ref, k_ref, v_ref, u_ref, o_ref, kb_ref, vb_ref, *,
                      scale):
    qi = pl.program_id(2)
    tq = q_ref.shape[0]

    @pl.when(qi == 0)
    def _():
        kb_ref[...] = k_ref[...].astype(BF16)
        vb_ref[...] = v_ref[...].astype(BF16)

    q = q_ref[...]
    u = u_ref[...]

    def blk(j, carry, o, masked):
        start = pl.multiple_of(j * tq, tq)
        ks = kb_ref[pl.ds(start, tq), :]
        vs = vb_ref[pl.ds(start, tq), :]
        return _sb_block(q, ks, vs, u, carry, o, scale, masked)

    carry = jnp.zeros((tq, 1), F32)
    o = jnp.zeros((tq, q_ref.shape[1]), F32)
    carry, o = blk(qi, carry, o, True)
    carry, o = lax.fori_loop(
        0, qi, lambda t, c: blk(qi - 1 - t, c[0], c[1], False), (carry, o))
    o_ref[...] = o.astype(o_ref.dtype)


def _suffix_matrix(n):
    return jnp.asarray(np.tril(np.ones((n, n), np.float32), -1), BF16)


def _sb_prompt(q, k, v, batch, seq):
    m, w = q.shape
    d = HEAD_DIM_AB
    heads = w // d
    tq = SB_TILE
    nq = seq // tq
    return pl.pallas_call(
        functools.partial(_sb_prompt_kernel, scale=d ** -0.5),
        grid=(batch, heads, nq),
        in_specs=[pl.BlockSpec((tq, d), lambda b, h, i: (b * nq + i, h)),
                  pl.BlockSpec((seq, d), lambda b, h, i: (b, h)),
                  pl.BlockSpec((seq, d), lambda b, h, i: (b, h)),
                  pl.BlockSpec((tq, tq), lambda b, h, i: (0, 0))],
        out_specs=pl.BlockSpec((tq, d), lambda b, h, i: (b * nq + i, h)),
        out_shape=jax.ShapeDtypeStruct((m, w), BF16),
        scratch_shapes=[pltpu.VMEM((seq, d), BF16), pltpu.VMEM((seq, d), BF16)],
        compiler_params=_params("arbitrary", "arbitrary", "arbitrary"),
        name="sb_prompt",
    )(q, k, v, _suffix_matrix(tq))


def _sb_step_kernel(q_ref, kn_ref, vn_ref, kc_ref, vc_ref, u_ref, o_ref,
                    kb_ref, vb_ref, *, scale):
    t = q_ref.shape[0]
    past = kc_ref.shape[0]
    tk = u_ref.shape[0]
    d = q_ref.shape[1]
    kb_ref[0:past] = kc_ref[...].astype(BF16)
    vb_ref[0:past] = vc_ref[...].astype(BF16)
    kb_ref[past:past + t] = kn_ref[...].astype(BF16)
    vb_ref[past:past + t] = vn_ref[...].astype(BF16)
    kb_ref[past + t:past + tk] = jnp.zeros((tk - t, d), BF16)
    vb_ref[past + t:past + tk] = jnp.zeros((tk - t, d), BF16)

    q = q_ref[...]
    u = u_ref[...]
    carry = jnp.zeros((t, 1), F32)
    o = jnp.zeros((t, d), F32)
    carry, o = _sb_block(q, kb_ref[past:past + tk], vb_ref[past:past + tk], u,
                         carry, o, scale, True)
    for j in reversed(range(past // tk)):
        carry, o = _sb_block(q, kb_ref[j * tk:(j + 1) * tk],
                             vb_ref[j * tk:(j + 1) * tk], u, carry, o, scale,
                             False)
    o_ref[...] = o.astype(o_ref.dtype)


def _sb_step(q, k_new, v_new, k_cache, v_cache, batch):
    m, w = q.shape
    d = HEAD_DIM_AB
    heads = w // d
    t = m // batch
    past = k_cache.shape[1]
    tk = SB_STEP_KEYS
    assert past % tk == 0 and t <= tk
    row = pl.BlockSpec((t, d), lambda b, h: (b, h))
    cache = pl.BlockSpec((None, past, d), lambda b, h: (b, 0, h))
    return pl.pallas_call(
        functools.partial(_sb_step_kernel, scale=d ** -0.5),
        grid=(batch, heads),
        in_specs=[row, row, row, cache, cache,
                  pl.BlockSpec((tk, tk), lambda b, h: (0, 0))],
        out_specs=row,
        out_shape=jax.ShapeDtypeStruct((m, w), BF16),
        scratch_shapes=[pltpu.VMEM((past + tk, d), BF16),
                        pltpu.VMEM((past + tk, d), BF16)],
        compiler_params=_params("arbitrary", "arbitrary"),
        name="sb_step",
    )(q, k_new, v_new, k_cache, v_cache, _suffix_matrix(tk))


def _band_bias_kernel(r_ref, o_ref, *, n_valid):
    tq, w = o_ref.shape
    wr = r_ref.shape[1]
    t = jnp.broadcast_to(r_ref[...], (tq, wr))
    t = pltpu.roll(t, 0, 1, stride=1, stride_axis=0)
    row = lax.broadcasted_iota(jnp.int32, (tq, w), 0)
    col = lax.broadcasted_iota(jnp.int32, (tq, w), 1)
    qc = _chunk_of(row)
    kc = _chunk_of(col)
    valid = (kc >= qc) & (kc <= qc + BAND_PREV_CHUNKS) & (col < n_valid)
    o_ref[...] = jnp.where(valid, t[:, :w], NEG)


def _band_bias(rel, tq, n_valid):
    heads = rel.shape[0]
    w = BAND_HIST + BAND_TQ
    wr = w + LANES
    p = np.arange(wr)
    delta = np.where(p < w, p, p - wr)
    idx = np.clip(BAND_HIST - delta, -REL_CLIP, REL_CLIP) + REL_CLIP
    r = jnp.take(rel.astype(F32), jnp.asarray(idx, jnp.int32), axis=1)
    r = r.reshape(heads, 1, wr)
    return pl.pallas_call(
        functools.partial(_band_bias_kernel, n_valid=n_valid),
        grid=(heads,),
        in_specs=[pl.BlockSpec((None, 1, wr), lambda h: (h, 0, 0))],
        out_specs=pl.BlockSpec((None, tq, w), lambda h: (h, 0, 0)),
        out_shape=jax.ShapeDtypeStruct((heads, tq, w), F32),
        compiler_params=_params("arbitrary"),
        name="band_bias",
    )(r)


def _softmax_pv(s, vs):
    m = jnp.max(s, axis=1, keepdims=True)
    p = jnp.exp(s - m)
    l = jnp.sum(p, axis=1, keepdims=True)
    return jnp.dot(p.astype(BF16), vs, preferred_element_type=F32) / l


def _band_prompt_kernel(q_ref, k_ref, v_ref, bias_ref, o_ref, kp_ref, vp_ref,
                        *, scale):
    qi = pl.program_id(2)
    tq, d = q_ref.shape
    seq = k_ref.shape[0]
    w = bias_ref.shape[1]
    hist = w - tq

    @pl.when(qi == 0)
    def _():
        kp_ref[0:hist] = jnp.zeros((hist, d), BF16)
        vp_ref[0:hist] = jnp.zeros((hist, d), BF16)
        kp_ref[hist:hist + seq] = k_ref[...].astype(BF16)
        vp_ref[hist:hist + seq] = v_ref[...].astype(BF16)

    start = pl.multiple_of(qi * tq, tq)
    ks = kp_ref[pl.ds(start, w), :]
    vs = vp_ref[pl.ds(start, w), :]
    s = lax.dot_general(q_ref[...], ks, (((1,), (1,)), ((), ())),
                        preferred_element_type=F32) * scale + bias_ref[...]
    col = lax.broadcasted_iota(jnp.int32, s.shape, 1)
    s = jnp.where(col >= hist - qi * tq, s, NEG)
    o_ref[...] = _softmax_pv(s, vs).astype(o_ref.dtype)


def _band_prompt(q, k, v, bias, batch, seq):
    m, wd = q.shape
    d = HEAD_DIM_AB
    heads = wd // d
    tq = BAND_TQ
    nq = seq // tq
    w = bias.shape[2]
    return pl.pallas_call(
        functools.partial(_band_prompt_kernel, scale=d ** -0.5),
        grid=(batch, heads, nq),
        in_specs=[pl.BlockSpec((tq, d), lambda b, h, i: (b * nq + i, h)),
                  pl.BlockSpec((seq, d), lambda b, h, i: (b, h)),
                  pl.BlockSpec((seq, d), lambda b, h, i: (b, h)),
                  pl.BlockSpec((None, tq, w), lambda b, h, i: (h, 0, 0))],
        out_specs=pl.BlockSpec((tq, d), lambda b, h, i: (b * nq + i, h)),
        out_shape=jax.ShapeDtypeStruct((m, wd), BF16),
        scratch_shapes=[pltpu.VMEM((BAND_HIST + seq, d), BF16),
                        pltpu.VMEM((BAND_HIST + seq, d), BF16)],
        compiler_params=_params("arbitrary", "arbitrary", "arbitrary"),
        name="band_prompt",
    )(q, k, v, bias)


def _band_step_kernel(q_ref, kn_ref, vn_ref, kc_ref, vc_ref, bias_ref, o_ref,
                      nk_ref, nv_ref, kb_ref, vb_ref, *, scale):
    t, d = q_ref.shape
    past = kc_ref.shape[0]
    w = bias_ref.shape[1]
    kc, vc = kc_ref[...], vc_ref[...]
    kn, vn = kn_ref[...], vn_ref[...]
    nk_ref[0:past - t] = kc[t:]
    nv_ref[0:past - t] = vc[t:]
    nk_ref[past - t:past] = kn
    nv_ref[past - t:past] = vn
    kb_ref[0:past] = kc.astype(BF16)
    vb_ref[0:past] = vc.astype(BF16)
    kb_ref[past:past + t] = kn.astype(BF16)
    vb_ref[past:past + t] = vn.astype(BF16)
    kb_ref[past + t:w] = jnp.zeros((w - past - t, d), BF16)
    vb_ref[past + t:w] = jnp.zeros((w - past - t, d), BF16)
    s = lax.dot_general(q_ref[...], kb_ref[...], (((1,), (1,)), ((), ())),
                        preferred_element_type=F32) * scale + bias_ref[...]
    o_ref[...] = _softmax_pv(s, vb_ref[...]).astype(o_ref.dtype)


def _band_step(q, k_new, v_new, k_cache, v_cache, bias, batch):
    m, wd = q.shape
    d = HEAD_DIM_AB
    heads = wd // d
    t = m // batch
    past = k_cache.shape[1]
    w = bias.shape[2]
    assert past == BAND_HIST and past + t <= w
    row = pl.BlockSpec((t, d), lambda b, h: (b, h))
    cache = pl.BlockSpec((None, past, d), lambda b, h: (b, 0, h))
    return pl.pallas_call(
        functools.partial(_band_step_kernel, scale=d ** -0.5),
        grid=(batch, heads),
        in_specs=[row, row, row, cache, cache,
                  pl.BlockSpec((None, t, w), lambda b, h: (h, 0, 0))],
        out_specs=[row, cache, cache],
        out_shape=[jax.ShapeDtypeStruct((m, wd), BF16),
                   jax.ShapeDtypeStruct(k_cache.shape, F32),
                   jax.ShapeDtypeStruct(v_cache.shape, F32)],
        scratch_shapes=[pltpu.VMEM((w, d), BF16), pltpu.VMEM((w, d), BF16)],
        compiler_params=_params("arbitrary", "arbitrary"),
        name="band_step",
    )(q, k_new, v_new, k_cache, v_cache, bias)


def _split_pair(blk):
    rolled = pltpu.roll(blk, HEAD_DIM_C, 1)
    lane = lax.broadcasted_iota(jnp.int32, blk.shape, 1)
    low = lane < HEAD_DIM_C
    zero = jnp.zeros_like(blk)
    even = (jnp.where(low, blk, zero), jnp.where(low, zero, rolled))
    odd = (jnp.where(low, rolled, zero), jnp.where(low, zero, blk))
    return even, odd


def _sink_softmax(s, sink):
    m = jnp.maximum(jnp.max(s, axis=1, keepdims=True), sink)
    p = jnp.exp(s - m)
    l = jnp.sum(p, axis=1, keepdims=True) + jnp.exp(sink - m)
    return p.astype(BF16), l


def _swa_pair(qp, kl, kh, vl, vh, valid, sink_a, sink_b, scale):
    dn = (((1,), (1,)), ((), ()))
    sa = lax.dot_general(qp, kl, dn, preferred_element_type=F32) * scale
    sb = lax.dot_general(qp, kh, dn, preferred_element_type=F32) * scale
    pa, la = _sink_softmax(jnp.where(valid, sa, NEG), sink_a)
    pb, lb = _sink_softmax(jnp.where(valid, sb, NEG), sink_b)
    return (jnp.dot(pa, vl, preferred_element_type=F32) / la
            + jnp.dot(pb, vh, preferred_element_type=F32) / lb)


def _swa_prompt_kernel(sinks_ref, q_ref, k_ref, v_ref, o_ref, ks_ref, vs_ref,
                       *, scale):
    hp = pl.program_id(1)
    qi = pl.program_id(2)
    tq = q_ref.shape[0]
    seq = k_ref.shape[0]
    band = WINDOW + tq
    group = q_ref.shape[1] // (2 * LANES)

    @pl.when(qi == 0)
    def _():
        for src, dst in ((k_ref, ks_ref), (v_ref, vs_ref)):
            even, odd = _split_pair(src[...])
            for n, x in enumerate(even + odd):
                dst[n, 0:WINDOW] = jnp.zeros((WINDOW, LANES), BF16)
                dst[n, WINDOW:WINDOW + seq] = x.astype(BF16)

    start = pl.multiple_of(qi * tq, tq)
    row = lax.broadcasted_iota(jnp.int32, (tq, band), 0)
    col = lax.broadcasted_iota(jnp.int32, (tq, band), 1)
    qc = _chunk_of(row)
    kc = _chunk_of(col)
    valid = ((kc >= qc) & (kc <= qc + SWA_PREV_CHUNKS)
             & (col >= WINDOW - qi * tq))
    for kv in range(2):
        kl = ks_ref[2 * kv, pl.ds(start, band), :]
        kh = ks_ref[2 * kv + 1, pl.ds(start, band), :]
        vl = vs_ref[2 * kv, pl.ds(start, band), :]
        vh = vs_ref[2 * kv + 1, pl.ds(start, band), :]
        for p in range(group):
            pair = kv * group + p
            head = (hp * 2 * group + pair) * 2
            sl = slice(pair * LANES, (pair + 1) * LANES)
            o = _swa_pair(q_ref[:, sl], kl, kh, vl, vh, valid,
                          sinks_ref[head], sinks_ref[head + 1], scale)
            o_ref[:, sl] = o.astype(o_ref.dtype)


def _swa_prompt(q, k, v, sinks, batch, seq):
    m, wq = q.shape
    wk = k.shape[1]
    tq = SWA_TQ
    nq = seq // tq
    npairs = wk // LANES
    qw = wq // npairs
    grid_spec = pltpu.PrefetchScalarGridSpec(
        num_scalar_prefetch=1,
        grid=(batch, npairs, nq),
        in_specs=[pl.BlockSpec((tq, qw), lambda b, h, i, s: (b * nq + i, h)),
                  pl.BlockSpec((seq, LANES), lambda b, h, i, s: (b, h)),
                  pl.BlockSpec((seq, LANES), lambda b, h, i, s: (b, h))],
        out_specs=pl.BlockSpec((tq, qw), lambda b, h, i, s: (b * nq + i, h)),
        scratch_shapes=[pltpu.VMEM((4, WINDOW + seq, LANES), BF16),
                        pltpu.VMEM((4, WINDOW + seq, LANES), BF16)],
    )
    return pl.pallas_call(
        functools.partial(_swa_prompt_kernel, scale=HEAD_DIM_C ** -0.5),
        grid_spec=grid_spec,
        out_shape=jax.ShapeDtypeStruct((m, wq), BF16),
        compiler_params=_params("arbitrary", "arbitrary", "arbitrary"),
        name="swa_prompt",
    )(sinks, q, k, v)


def _swa_step_kernel(sinks_ref, q_ref, kn_ref, vn_ref, kc_ref, vc_ref, o_ref,
                     nk_ref, nv_ref, *, scale):
    t = q_ref.shape[0]
    past, wk = kc_ref.shape
    band = 2 * LANES
    npairs = wk // LANES
    group = q_ref.shape[1] // (npairs * 2 * LANES)
    kc, vc = kc_ref[...], vc_ref[...]
    kn, vn = kn_ref[...], vn_ref[...]
    nk_ref[0:past - t] = kc[t:]
    nv_ref[0:past - t] = vc[t:]
    nk_ref[past - t:past] = kn
    nv_ref[past - t:past] = vn
    pad = jnp.zeros((band - past - t, wk), F32)
    k_all = jnp.concatenate([kc, kn, pad], axis=0)
    v_all = jnp.concatenate([vc, vn, pad], axis=0)
    col = lax.broadcasted_iota(jnp.int32, (t, band), 1)
    valid = col < past + t
    for hp in range(npairs):
        cs = slice(hp * LANES, (hp + 1) * LANES)
        k_eo = _split_pair(k_all[:, cs])
        v_eo = _split_pair(v_all[:, cs])
        for kv in range(2):
            kl, kh = (x.astype(BF16) for x in k_eo[kv])
            vl, vh = (x.astype(BF16) for x in v_eo[kv])
            for p in range(group):
                pair = (hp * 2 + kv) * group + p
                sl = slice(pair * LANES, (pair + 1) * LANES)
                o = _swa_pair(q_ref[:, sl], kl, kh, vl, vh, valid,
                              sinks_ref[2 * pair], sinks_ref[2 * pair + 1],
                              scale)
                o_ref[:, sl] = o.astype(o_ref.dtype)


def _swa_step(q, k_new, v_new, k_cache, v_cache, sinks, batch):
    m, wq = q.shape
    wk = k_new.shape[1]
    t = m // batch
    past = k_cache.shape[1]
    assert past + t <= 2 * LANES
    qrow = pl.BlockSpec((t, wq), lambda b, s: (b, 0))
    krow = pl.BlockSpec((t, wk), lambda b, s: (b, 0))
    cache = pl.BlockSpec((None, past, wk), lambda b, s: (b, 0, 0))
    grid_spec = pltpu.PrefetchScalarGridSpec(
        num_scalar_prefetch=1,
        grid=(batch,),
        in_specs=[qrow, krow, krow, cache, cache],
        out_specs=[qrow, cache, cache],
    )
    return pl.pallas_call(
        functools.partial(_swa_step_kernel, scale=HEAD_DIM_C ** -0.5),
        grid_spec=grid_spec,
        out_shape=[jax.ShapeDtypeStruct((m, wq), BF16),
                   jax.ShapeDtypeStruct(k_cache.shape, F32),
                   jax.ShapeDtypeStruct(v_cache.shape, F32)],
        compiler_params=_params("arbitrary"),
        name="swa_step",
    )(sinks, q, k_new, v_new, k_cache, v_cache)


def _rope_tables(pos):
    half = ROPE_DIM // 2
    inv = jnp.power(ROPE_THETA, -jnp.arange(half, dtype=F32) / half)
    ang = pos.astype(F32)[:, None] * inv[None, :]
    cos, sin = jnp.cos(ang), jnp.sin(ang)
    n = pos.shape[0]
    reps = LANES // HEAD_DIM_C
    rest = HEAD_DIM_C - ROPE_DIM
    ones = jnp.ones((n, rest), F32)
    zeros = jnp.zeros((n, rest), F32)
    zh = jnp.zeros((n, half), F32)
    c = jnp.tile(jnp.concatenate([cos, cos, ones], axis=1), (1, reps))
    s1 = jnp.tile(jnp.concatenate([zh, sin, zeros], axis=1), (1, reps))
    s2 = jnp.tile(jnp.concatenate([-sin, zh, zeros], axis=1), (1, reps))
    return c, s1, s2


def _ffn(h, norm_ffn3, w_gate, w_up, w_down, layer):
    hn = _rmsnorm(h, norm_ffn3, layer, BF16)
    act = _gate_up(hn, w_gate, w_up, layer)
    tk = w_down.shape[1] // 2
    h = _mm_res([(act, 0, 0)], w_down, layer, tk, h, tn=256)
    return _mm_res([(act, 1, 1)], w_down, layer, tk, h, tn=256)


def kernel(x_prompt, x_sample, cache_sb_k, cache_sb_v, cache_band_k, cache_band_v, cache_swa_k, cache_swa_v, norm_mix, norm_ffn, norm_final, w_in_ab, w_out_ab, rel_bias, w_qkv_c, w_out_c, sinks, w_gate, w_up, w_down):
    bp, seq, dm = x_prompt.shape
    bs, t, _ = x_sample.shape
    depth = norm_mix.shape[0]
    past = cache_sb_k.shape[2]
    hp = x_prompt.reshape(bp * seq, dm)
    hs = x_sample.reshape(bs * t, dm)
    norm_mix3 = norm_mix.reshape(depth, 1, dm)
    norm_ffn3 = norm_ffn.reshape(depth, 1, dm)
    norm_final3 = norm_final.reshape(1, 1, dm)

    w_sb = cache_sb_k.shape[3] * cache_sb_k.shape[4]
    w_band = cache_band_k.shape[3] * cache_band_k.shape[4]
    wq_c = w_out_c.shape[1]
    wk_c = cache_swa_k.shape[3] * cache_swa_k.shape[4]
    p_band = cache_band_k.shape[2]
    p_swa = cache_swa_k.shape[2]

    rope_p = _rope_tables(jnp.arange(seq))
    rope_s = tuple(jnp.tile(x, (bs, 1)) for x in _rope_tables(past + jnp.arange(t)))

    outs = {name: [] for name in (
        "sbk_p", "sbv_p", "bk_p", "bv_p", "ck_p", "cv_p",
        "sbk_s", "sbv_s", "bk_s", "bv_s", "ck_s", "cv_s")}

    for l in range(depth):
        hn_p = _rmsnorm(hp, norm_mix3, l, BF16)
        hn_s = _rmsnorm(hs, norm_mix3, l, BF16)
        if l % 2 == 0:
            e = l // 2
            cols = (0, w_sb, 2 * w_sb, 3 * w_sb, 3 * w_sb + w_band, 3 * w_sb + 2 * w_band)
            bias_p = _band_bias(rel_bias[e], BAND_TQ, BAND_HIST + BAND_TQ)
            bias_s = _band_bias(rel_bias[e], t, p_band + t)

            def split(hn):
                qa = _proj(hn, w_in_ab, e, cols[0], w_sb, BF16)
                ka = _proj(hn, w_in_ab, e, cols[1], w_sb, F32)
                va = _proj(hn, w_in_ab, e, cols[2], w_sb, F32)
                qb = _proj(hn, w_in_ab, e, cols[3], w_band, BF16)
                kb = _proj(hn, w_in_ab, e, cols[4], w_band, F32)
                vb = _proj(hn, w_in_ab, e, cols[5], w_band, F32)
                return qa, ka, va, qb, kb, vb

            qa, ka, va, qb, kb, vb = split(hn_p)
            oa = _sb_prompt(qa, ka, va, bp, seq)
            ob = _band_prompt(qb, kb, vb, bias_p, bp, seq)
            hp = _mm_res([(oa, 0, 0), (ob, 0, 1)], w_out_ab, e, w_sb, hp, tn=512)
            nb = min(BAND_HIST, seq)
            outs["sbk_p"].append(ka.reshape(bp, seq, *cache_sb_k.shape[3:]))
            outs["sbv_p"].append(va.reshape(bp, seq, *cache_sb_k.shape[3:]))
            outs["bk_p"].append(kb.reshape(bp, seq, *cache_band_k.shape[3:])[:, seq - nb:])
            outs["bv_p"].append(vb.reshape(bp, seq, *cache_band_k.shape[3:])[:, seq - nb:])
            qa, ka, va, qb, kb, vb = split(hn_s)
            oa = _sb_step(qa, ka, va, cache_sb_k[e].reshape(bs, past, w_sb),
                          cache_sb_v[e].reshape(bs, past, w_sb), bs)
            ob, nbk, nbv = _band_step(
                qb, kb, vb, cache_band_k[e].reshape(bs, p_band, w_band),
                cache_band_v[e].reshape(bs, p_band, w_band), bias_s, bs)
            hs = _mm_res([(oa, 0, 0), (ob, 0, 1)], w_out_ab, e, w_sb, hs, tn=512)
            outs["sbk_s"].append(ka.reshape(bs, t, *cache_sb_k.shape[3:]))
            outs["sbv_s"].append(va.reshape(bs, t, *cache_sb_k.shape[3:]))
            outs["bk_s"].append(nbk.reshape(cache_band_k.shape[1:]))
            outs["bv_s"].append(nbv.reshape(cache_band_v.shape[1:]))
        else:
            o = l // 2

            def split_c(hn, rope):
                q = _proj(hn, w_qkv_c, o, 0, wq_c, BF16, rope=rope)
                k = _proj(hn, w_qkv_c, o, wq_c, wk_c, F32, rope=rope)
                v = _proj(hn, w_qkv_c, o, wq_c + wk_c, wk_c, F32)
                return q, k, v

            q, k, v = split_c(hn_p, rope_p)
            oc = _swa_prompt(q, k, v, sinks[o], bp, seq)
            hp = _mm_res([(oc, 0, 0)], w_out_c, o, wq_c, hp, tn=512)
            nw = min(WINDOW, seq)
            outs["ck_p"].append(k.reshape(bp, seq, *cache_swa_k.shape[3:])[:, seq - nw:])
            outs["cv_p"].append(v.reshape(bp, seq, *cache_swa_k.shape[3:])[:, seq - nw:])
            q, k, v = split_c(hn_s, rope_s)
            oc, nck, ncv = _swa_step(
                q, k, v, cache_swa_k[o].reshape(bs, p_swa, wk_c),
                cache_swa_v[o].reshape(bs, p_swa, wk_c), sinks[o], bs)
            hs = _mm_res([(oc, 0, 0)], w_out_c, o, wq_c, hs, tn=512)
            outs["ck_s"].append(nck.reshape(cache_swa_k.shape[1:]))
            outs["cv_s"].append(ncv.reshape(cache_swa_v.shape[1:]))
        hp = _ffn(hp, norm_ffn3, w_gate, w_up, w_down, l)
        hs = _ffn(hs, norm_ffn3, w_gate, w_up, w_down, l)

    y_prompt = _rmsnorm(hp, norm_final3, 0, F32).reshape(bp, seq, dm)
    y_sample = _rmsnorm(hs, norm_final3, 0, F32).reshape(bs, t, dm)
    st = lambda name: jnp.stack(outs[name])
    return (y_prompt, y_sample,
            st("sbk_p"), st("sbv_p"), st("bk_p"), st("bv_p"), st("ck_p"), st("cv_p"),
            st("sbk_s"), st("sbv_s"), st("bk_s"), st("bv_s"), st("ck_s"), st("cv_s"))
```

```python
import functools
import math

import numpy as np
import jax
import jax.numpy as jnp
from jax import lax
from jax.experimental import pallas as pl
from jax.experimental.pallas import tpu as pltpu

F32 = jnp.float32
BF16 = jnp.bfloat16

CHUNK = 64
EPS = 1e-6
HEAD_DIM_AB = 128
BAND_PREV_CHUNKS = 8
BAND_HIST = BAND_PREV_CHUNKS * CHUNK
REL_CLIP = 2 * CHUNK
HEAD_DIM_C = 64
N_KV_C = 8
SWA_PREV_CHUNKS = 2
WINDOW = SWA_PREV_CHUNKS * CHUNK
ROPE_THETA = 500000.0
ROPE_DIM = HEAD_DIM_C // 4

LANES = 128
V7X_VMEM_LIMIT_BYTES = 56 * 1024 * 1024

NEG = -1e30

SB_TILE = 256
SB_HEADS = 4
SB_STEP_KEYS = 128
BAND_TQ = 128
BAND_HEADS = 4
STEP_HEADS = 8
SWA_TQ = 128


def _params(*sem):
    return pltpu.CompilerParams(dimension_semantics=sem,
                                vmem_limit_bytes=V7X_VMEM_LIMIT_BYTES)


def _chunk_of(pos):
    return lax.shift_right_logical(pos, int(math.log2(CHUNK)))


def _pick_tm(m):
    return 1024 if m % 1024 == 0 else m


def _rmsnorm_kernel(x_ref, g_ref, o_ref):
    x = x_ref[...]
    ms = jnp.mean(x * x, axis=-1, keepdims=True)
    y = x * lax.rsqrt(ms + EPS)
    o_ref[...] = (y * g_ref[...]).astype(o_ref.dtype)


def _rmsnorm(x, g3, layer, out_dtype):
    m, d = x.shape
    tm = 256
    return pl.pallas_call(
        _rmsnorm_kernel,
        grid=(m // tm,),
        in_specs=[pl.BlockSpec((tm, d), lambda i: (i, 0)),
                  pl.BlockSpec((None, 1, d), lambda i: (layer, 0, 0))],
        out_specs=pl.BlockSpec((tm, d), lambda i: (i, 0)),
        out_shape=jax.ShapeDtypeStruct((m, d), out_dtype),
        compiler_params=_params("arbitrary"),
        name="rmsnorm",
    )(x, g3)


def _proj_kernel(a_ref, w_ref, o_ref):
    w = w_ref[...].astype(BF16)
    acc = jnp.dot(a_ref[...], w, preferred_element_type=F32)
    o_ref[...] = acc.astype(o_ref.dtype)


def _rope_tile(x, c, s1, s2):
    return (x * c + pltpu.roll(x, 8, 1) * s1
            + pltpu.roll(x, LANES - 8, 1) * s2)


def _proj_rope_kernel(a_ref, w_ref, c_ref, s1_ref, s2_ref, o_ref):
    w = w_ref[...].astype(BF16)
    acc = jnp.dot(a_ref[...], w, preferred_element_type=F32)
    c, s1, s2 = c_ref[...], s1_ref[...], s2_ref[...]
    for j in range(acc.shape[1] // LANES):
        sl = slice(j * LANES, (j + 1) * LANES)
        o_ref[:, sl] = _rope_tile(acc[:, sl], c, s1, s2).astype(o_ref.dtype)


def _proj(a, w3, layer, col0, ncols, out_dtype, rope=None, tn=512):
    m, k = a.shape
    tm = _pick_tm(m)
    tn = min(tn, ncols)
    jb = col0 // tn
    in_specs = [pl.BlockSpec((tm, k), lambda i, j: (i, 0)),
                pl.BlockSpec((None, k, tn), lambda i, j: (layer, 0, jb + j))]
    args = [a, w3]
    kern = _proj_kernel
    if rope is not None:
        nrep = rope[0].shape[0] // tm
        for t in rope:
            in_specs.append(pl.BlockSpec((tm, LANES), lambda i, j: (i % nrep, 0)))
            args.append(t)
        kern = _proj_rope_kernel
    return pl.pallas_call(
        kern,
        grid=(m // tm, ncols // tn),
        in_specs=in_specs,
        out_specs=pl.BlockSpec((tm, tn), lambda i, j: (i, j)),
        out_shape=jax.ShapeDtypeStruct((m, ncols), out_dtype),
        compiler_params=_params("arbitrary", "arbitrary"),
        name="proj",
    )(*args)


def _gate_up_kernel(a_ref, wg_ref, wu_ref, o_ref):
    a = a_ref[...]
    g = jnp.dot(a, wg_ref[...].astype(BF16), preferred_element_type=F32)
    u = jnp.dot(a, wu_ref[...].astype(BF16), preferred_element_type=F32)
    o_ref[...] = (g * jax.nn.sigmoid(g) * u).astype(o_ref.dtype)


def _gate_up(a, wg3, wu3, layer, tn=256):
    m, k = a.shape
    f = wg3.shape[2]
    tm = _pick_tm(m)
    wspec = pl.BlockSpec((None, k, tn), lambda i, j: (layer, 0, j))
    return pl.pallas_call(
        _gate_up_kernel,
        grid=(m // tm, f // tn),
        in_specs=[pl.BlockSpec((tm, k), lambda i, j: (i, 0)), wspec, wspec],
        out_specs=pl.BlockSpec((tm, tn), lambda i, j: (i, j)),
        out_shape=jax.ShapeDtypeStruct((m, f), BF16),
        compiler_params=_params("arbitrary", "arbitrary"),
        name="gate_up",
    )(a, wg3, wu3)


def _mm_res_kernel(*refs, nterms):
    a_refs = refs[:nterms]
    w_refs = refs[nterms:2 * nterms]
    r_ref = refs[2 * nterms]
    o_ref = refs[2 * nterms + 1]
    acc = r_ref[...]
    for a_ref, w_ref in zip(a_refs, w_refs):
        acc = acc + jnp.dot(a_ref[...], w_ref[...].astype(BF16),
                            preferred_element_type=F32)
    o_ref[...] = acc


def _mm_res(terms, w3, layer, tk, res, tn):
    m, n = res.shape
    tm = _pick_tm(m)
    in_specs, args = [], []
    for a, ka, _ in terms:
        in_specs.append(pl.BlockSpec((tm, tk), lambda i, j, ka=ka: (i, ka)))
        args.append(a)
    for _, _, kw in terms:
        in_specs.append(
            pl.BlockSpec((None, tk, tn), lambda i, j, kw=kw: (layer, kw, j)))
        args.append(w3)
    in_specs.append(pl.BlockSpec((tm, tn), lambda i, j: (i, j)))
    args.append(res)
    return pl.pallas_call(
        functools.partial(_mm_res_kernel, nterms=len(terms)),
        grid=(m // tm, n // tn),
        in_specs=in_specs,
        out_specs=pl.BlockSpec((tm, tn), lambda i, j: (i, j)),
        out_shape=jax.ShapeDtypeStruct((m, n), F32),
        compiler_params=_params("arbitrary", "arbitrary"),
        name="mm_res",
    )(*args)


_DN_T = (((1,), (1,)), ((), ()))


def _head(h, d=HEAD_DIM_AB):
    return slice(h * d, (h + 1) * d)


def _heads_major(x):
    return pltpu.einshape("phd->hpd", x)


def _softmax_stage(ss, sinks=None):
    ms = [jnp.max(s, axis=1, keepdims=True) for s in ss]
    if sinks is not None:
        ms = [jnp.maximum(m, sk) for m, sk in zip(ms, sinks)]
    ps = [jnp.exp(s - m) for s, m in zip(ss, ms)]
    ls = [jnp.sum(p, axis=1, keepdims=True) for p in ps]
    if sinks is not None:
        ls = [l + jnp.exp(sk - m) for l, sk, m in zip(ls, sinks, ms)]
    return [p.astype(BF16) for p in ps], ls


def _sb_logs(z, valid):
    lb = jnp.minimum(z, 0.0) - jnp.log(1.0 + jnp.exp(-jnp.abs(z)))
    l1m = lb - z
    if valid is not None:
        l1m = jnp.where(valid, l1m, 0.0)
    return lb, l1m


def _split_hi_lo(x):
    hi = x.astype(BF16)
    lo = (x - hi.astype(F32)).astype(BF16)
    return jnp.concatenate([hi, lo], axis=1)


def _suffix_matrix(n):
    u = np.tril(np.ones((n, n), np.float32), -1)
    return jnp.asarray(np.concatenate([u, u], axis=0), BF16)


def _sb_prompt_kernel(q_ref, k_ref, v_ref, uu_ref, o_ref, kb_ref, vb_ref,
                      oacc_ref, cacc_ref, *, scale):
    qi = pl.program_id(2)
    tq = q_ref.shape[0]
    nh = oacc_ref.shape[0]

    @pl.when(qi == 0)
    def _():
        kb_ref[...] = k_ref[...].astype(BF16)
        vb_ref[...] = v_ref[...].astype(BF16)

    oacc_ref[...] = jnp.zeros(oacc_ref.shape, F32)
    cacc_ref[...] = jnp.zeros(cacc_ref.shape, F32)

    def tile(j, masked):
        start = pl.multiple_of(j * tq, tq)
        valid = None
        if masked:
            row = lax.broadcasted_iota(jnp.int32, (tq, tq), 0)
            col = lax.broadcasted_iota(jnp.int32, (tq, tq), 1)
            valid = col < row
        zs = [lax.dot_general(q_ref[:, _head(h)],
                              kb_ref[pl.ds(start, tq), _head(h)], _DN_T,
                              preferred_element_type=F32) * scale
              for h in range(nh)]
        logs = [_sb_logs(z, valid) for z in zs]
        sufs = [jnp.dot(_split_hi_lo(l1m), uu_ref[...],
                        preferred_element_type=F32) for _, l1m in logs]
        ws = [jnp.exp(lb + suf + cacc_ref[h])
              for h, ((lb, _), suf) in enumerate(zip(logs, sufs))]
        if masked:
            ws = [jnp.where(valid, w, 0.0) for w in ws]
        for h in range(nh):
            oacc_ref[h] += jnp.dot(ws[h].astype(BF16),
                                   vb_ref[pl.ds(start, tq), _head(h)],
                                   preferred_element_type=F32)
            cacc_ref[h] += jnp.sum(logs[h][1], axis=1, keepdims=True)

    tile(qi, True)

    def body(t, c):
        tile(qi - 1 - t, False)
        return c

    lax.fori_loop(0, qi, body, 0)
    for h in range(nh):
        o_ref[:, _head(h)] = oacc_ref[h].astype(o_ref.dtype)


def _sb_prompt(q, k, v, batch, seq):
    m, w = q.shape
    d = HEAD_DIM_AB
    nh = SB_HEADS
    tq = SB_TILE
    nq = seq // tq
    qo = pl.BlockSpec((tq, nh * d), lambda b, g, i: (b * nq + i, g))
    kv = pl.BlockSpec((seq, nh * d), lambda b, g, i: (b, g))
    return pl.pallas_call(
        functools.partial(_sb_prompt_kernel, scale=d ** -0.5),
        grid=(batch, w // (nh * d), nq),
        in_specs=[qo, kv, kv, pl.BlockSpec((2 * tq, tq), lambda b, g, i: (0, 0))],
        out_specs=qo,
        out_shape=jax.ShapeDtypeStruct((m, w), BF16),
        scratch_shapes=[pltpu.VMEM((seq, nh * d), BF16),
                        pltpu.VMEM((seq, nh * d), BF16),
                        pltpu.VMEM((nh, tq, d), F32),
                        pltpu.VMEM((nh, tq, 1), F32)],
        compiler_params=_params("arbitrary", "arbitrary", "arbitrary"),
        name="sb_prompt",
    )(q, k, v, _suffix_matrix(tq))


def _sb_step_kernel(q_ref, kn_ref, vn_ref, kc_ref, vc_ref, uu_ref, o_ref,
                    kb_ref, vb_ref, *, scale):
    t = q_ref.shape[0]
    past, nh, d = kc_ref.shape
    tk = uu_ref.shape[1]
    n = past + tk
    kb_ref[:, 0:past, :] = _heads_major(kc_ref[...]).astype(BF16)
    vb_ref[:, 0:past, :] = _heads_major(vc_ref[...]).astype(BF16)
    for h in range(nh):
        kb_ref[h, past:past + t] = kn_ref[:, _head(h)].astype(BF16)
        vb_ref[h, past:past + t] = vn_ref[:, _head(h)].astype(BF16)
        kb_ref[h, past + t:n] = jnp.zeros((tk - t, d), BF16)
        vb_ref[h, past + t:n] = jnp.zeros((tk - t, d), BF16)

    z = jnp.concatenate(
        [lax.dot_general(q_ref[:, _head(h)], kb_ref[h], _DN_T,
                         preferred_element_type=F32) for h in range(nh)],
        axis=0) * scale
    row = lax.broadcasted_iota(jnp.int32, z.shape, 0) & (t - 1)
    col = lax.broadcasted_iota(jnp.int32, z.shape, 1)
    valid = col < past + row
    lb, l1m = _sb_logs(z, valid)
    ntiles = n // tk
    sufs = [jnp.dot(_split_hi_lo(l1m[:, j * tk:(j + 1) * tk]), uu_ref[...],
                    preferred_element_type=F32) for j in range(ntiles)]
    tots = [jnp.sum(l1m[:, j * tk:(j + 1) * tk], axis=1, keepdims=True)
            for j in range(ntiles)]
    carries = [None] * ntiles
    carry = jnp.zeros_like(tots[0])
    for j in reversed(range(ntiles)):
        carries[j] = carry
        carry = carry + tots[j]
    after = jnp.concatenate([sufs[j] + carries[j] for j in range(ntiles)],
                            axis=1)
    w = jnp.where(valid, jnp.exp(lb + after), 0.0).astype(BF16)
    for h in range(nh):
        o = jnp.dot(w[h * t:(h + 1) * t], vb_ref[h], preferred_element_type=F32)
        o_ref[:, _head(h)] = o.astype(o_ref.dtype)


def _sb_step(q, k_new, v_new, k_cache, v_cache, batch):
    m, w = q.shape
    _, past, heads, d = k_cache.shape
    t = m // batch
    tk = SB_STEP_KEYS
    nh = STEP_HEADS
    assert past % tk == 0 and t <= tk and t & (t - 1) == 0
    row = pl.BlockSpec((t, nh * d), lambda b, g: (b, g))
    cache = pl.BlockSpec((None, past, nh, d), lambda b, g: (b, 0, g, 0))
    return pl.pallas_call(
        functools.partial(_sb_step_kernel, scale=d ** -0.5),
        grid=(batch, heads // nh),
        in_specs=[row, row, row, cache, cache,
                  pl.BlockSpec((2 * tk, tk), lambda b, g: (0, 0))],
        out_specs=row,
        out_shape=jax.ShapeDtypeStruct((m, w), BF16),
        scratch_shapes=[pltpu.VMEM((nh, past + tk, d), BF16),
                        pltpu.VMEM((nh, past + tk, d), BF16)],
        compiler_params=_params("arbitrary", "arbitrary"),
        name="sb_step",
    )(q, k_new, v_new, k_cache, v_cache, _suffix_matrix(tk))


def _band_bias_kernel(r_ref, o_ref, *, n_valid):
    tq, w = o_ref.shape
    wr = r_ref.shape[1]
    t = jnp.broadcast_to(r_ref[...], (tq, wr))
    t = pltpu.roll(t, 0, 1, stride=1, stride_axis=0)
    row = lax.broadcasted_iota(jnp.int32, (tq, w), 0)
    col = lax.broadcasted_iota(jnp.int32, (tq, w), 1)
    qc = _chunk_of(row)
    kc = _chunk_of(col)
    valid = (kc >= qc) & (kc <= qc + BAND_PREV_CHUNKS) & (col < n_valid)
    o_ref[...] = jnp.where(valid, t[:, :w], NEG)


def _band_bias(rel, tq, n_valid):
    heads = rel.shape[0]
    w = BAND_HIST + BAND_TQ
    wr = w + LANES
    p = np.arange(wr)
    delta = np.where(p < w, p, p - wr)
    idx = np.clip(BAND_HIST - delta, -REL_CLIP, REL_CLIP) + REL_CLIP
    r = jnp.take(rel.astype(F32), jnp.asarray(idx, jnp.int32), axis=1)
    r = r.reshape(heads, 1, wr)
    return pl.pallas_call(
        functools.partial(_band_bias_kernel, n_valid=n_valid),
        grid=(heads,),
        in_specs=[pl.BlockSpec((None, 1, wr), lambda h: (h, 0, 0))],
        out_specs=pl.BlockSpec((None, tq, w), lambda h: (h, 0, 0)),
        out_shape=jax.ShapeDtypeStruct((heads, tq, w), F32),
        compiler_params=_params("arbitrary"),
        name="band_bias",
    )(r)


def _band_prompt_kernel(q_ref, k_ref, v_ref, bias_ref, o_ref, kp_ref, vp_ref,
                        *, scale, tq):
    seq, wd = k_ref.shape
    d = HEAD_DIM_AB
    nh = wd // d
    w = bias_ref.shape[1]
    hist = w - tq
    kp_ref[0:hist] = jnp.zeros((hist, wd), BF16)
    vp_ref[0:hist] = jnp.zeros((hist, wd), BF16)
    kp_ref[hist:hist + seq] = k_ref[...].astype(BF16)
    vp_ref[hist:hist + seq] = v_ref[...].astype(BF16)
    col = lax.broadcasted_iota(jnp.int32, (tq, w), 1)

    def body(i, c):
        start = pl.multiple_of(i * tq, tq)
        in_seq = col >= hist - start
        ss = []
        for h in range(nh):
            s = lax.dot_general(q_ref[pl.ds(start, tq), _head(h)],
                                kp_ref[pl.ds(start, w), _head(h)], _DN_T,
                                preferred_element_type=F32)
            s = s * scale + bias_ref[h * tq:(h + 1) * tq, :]
            ss.append(jnp.where(in_seq, s, NEG))
        ps, ls = _softmax_stage(ss)
        for h in range(nh):
            o = jnp.dot(ps[h], vp_ref[pl.ds(start, w), _head(h)],
                        preferred_element_type=F32) / ls[h]
            o_ref[pl.ds(start, tq), _head(h)] = o.astype(o_ref.dtype)
        return c

    lax.fori_loop(0, seq // tq, body, 0)


def _band_prompt(q, k, v, bias, batch, seq):
    m, wd = q.shape
    d = HEAD_DIM_AB
    nh = BAND_HEADS
    heads, tq, w = bias.shape
    bias = bias.reshape(heads // nh, nh * tq, w)
    blk = pl.BlockSpec((seq, nh * d), lambda b, g: (b, g))
    return pl.pallas_call(
        functools.partial(_band_prompt_kernel, scale=d ** -0.5, tq=tq),
        grid=(batch, heads // nh),
        in_specs=[blk, blk, blk,
                  pl.BlockSpec((None, nh * tq, w), lambda b, g: (g, 0, 0))],
        out_specs=blk,
        out_shape=jax.ShapeDtypeStruct((m, wd), BF16),
        scratch_shapes=[pltpu.VMEM((BAND_HIST + seq, nh * d), BF16),
                        pltpu.VMEM((BAND_HIST + seq, nh * d), BF16)],
        compiler_params=_params("arbitrary", "arbitrary"),
        name="band_prompt",
    )(q, k, v, bias)


def _roll_cache(new_ref, cache, new_rows_ref, t):
    past, nh, d = cache.shape
    new_ref[0:past - t] = cache[t:]
    for h in range(nh):
        new_ref[past - t:past, h, :] = new_rows_ref[:, _head(h, d)]


def _band_step_kernel(q_ref, kn_ref, vn_ref, kc_ref, vc_ref, bias_ref, o_ref,
                      nk_ref, nv_ref, kb_ref, vb_ref, *, scale):
    t = q_ref.shape[0]
    past, nh, d = kc_ref.shape
    w = bias_ref.shape[1]
    kc, vc = kc_ref[...], vc_ref[...]
    _roll_cache(nk_ref, kc, kn_ref, t)
    _roll_cache(nv_ref, vc, vn_ref, t)
    kb_ref[:, 0:past, :] = _heads_major(kc).astype(BF16)
    vb_ref[:, 0:past, :] = _heads_major(vc).astype(BF16)
    for h in range(nh):
        kb_ref[h, past:past + t] = kn_ref[:, _head(h)].astype(BF16)
        vb_ref[h, past:past + t] = vn_ref[:, _head(h)].astype(BF16)
        kb_ref[h, past + t:w] = jnp.zeros((w - past - t, d), BF16)
        vb_ref[h, past + t:w] = jnp.zeros((w - past - t, d), BF16)
    ss = [lax.dot_general(q_ref[:, _head(h)], kb_ref[h], _DN_T,
                          preferred_element_type=F32) * scale
          + bias_ref[h * t:(h + 1) * t, :] for h in range(nh)]
    ps, ls = _softmax_stage(ss)
    for h in range(nh):
        o = jnp.dot(ps[h], vb_ref[h], preferred_element_type=F32) / ls[h]
        o_ref[:, _head(h)] = o.astype(o_ref.dtype)


def _band_step(q, k_new, v_new, k_cache, v_cache, bias, batch):
    m, wd = q.shape
    _, past, heads, d = k_cache.shape
    t = m // batch
    nh = STEP_HEADS
    w = bias.shape[2]
    assert past == BAND_HIST and past + t <= w
    bias = bias.reshape(heads // nh, nh * t, w)
    row = pl.BlockSpec((t, nh * d), lambda b, g: (b, g))
    cache = pl.BlockSpec((None, past, nh, d), lambda b, g: (b, 0, g, 0))
    return pl.pallas_call(
        functools.partial(_band_step_kernel, scale=d ** -0.5),
        grid=(batch, heads // nh),
        in_specs=[row, row, row, cache, cache,
                  pl.BlockSpec((None, nh * t, w), lambda b, g: (g, 0, 0))],
        out_specs=[row, cache, cache],
        out_shape=[jax.ShapeDtypeStruct((m, wd), BF16),
                   jax.ShapeDtypeStruct(k_cache.shape, F32),
                   jax.ShapeDtypeStruct(v_cache.shape, F32)],
        scratch_shapes=[pltpu.VMEM((nh, w, d), BF16),
                        pltpu.VMEM((nh, w, d), BF16)],
        compiler_params=_params("arbitrary", "arbitrary"),
        name="band_step",
    )(q, k_new, v_new, k_cache, v_cache, bias)


def _split_pair(blk):
    rolled = pltpu.roll(blk, HEAD_DIM_C, 1)
    lane = lax.broadcasted_iota(jnp.int32, blk.shape, 1)
    low = lane < HEAD_DIM_C
    zero = jnp.zeros_like(blk)
    return [jnp.where(low, blk, zero), jnp.where(low, zero, rolled),
            jnp.where(low, rolled, zero), jnp.where(low, zero, blk)]


def _swa_stage(qs, kls, khs, vls, vhs, valid, sinks, scale):
    ss = []
    for q, kl, kh in zip(qs, kls, khs):
        for kx in (kl, kh):
            s = lax.dot_general(q, kx, _DN_T, preferred_element_type=F32)
            ss.append(jnp.where(valid, s * scale, NEG))
    ps, ls = _softmax_stage(ss, sinks)
    return [jnp.dot(ps[2 * i], vls[i], preferred_element_type=F32) / ls[2 * i]
            + jnp.dot(ps[2 * i + 1], vhs[i], preferred_element_type=F32)
            / ls[2 * i + 1] for i in range(len(qs))]


def _swa_prompt_kernel(sinks_ref, q_ref, k_ref, v_ref, o_ref, ks_ref, vs_ref,
                       *, scale, tq):
    hp = pl.program_id(1)
    seq = k_ref.shape[0]
    band = WINDOW + tq
    npair = q_ref.shape[1] // LANES
    group = npair // 2

    for src, dst in ((k_ref, ks_ref), (v_ref, vs_ref)):
        for n, x in enumerate(_split_pair(src[...])):
            dst[n, 0:WINDOW] = jnp.zeros((WINDOW, LANES), BF16)
            dst[n, WINDOW:WINDOW + seq] = x.astype(BF16)

    row = lax.broadcasted_iota(jnp.int32, (tq, band), 0)
    col = lax.broadcasted_iota(jnp.int32, (tq, band), 1)
    qc = _chunk_of(row)
    kc = _chunk_of(col)
    in_band = (kc >= qc) & (kc <= qc + SWA_PREV_CHUNKS)
    sinks = [sinks_ref[hp * 2 * npair + n] for n in range(2 * npair)]

    def body(i, c):
        start = pl.multiple_of(i * tq, tq)
        valid = in_band & (col >= WINDOW - start)
        kv = [[r[n, pl.ds(start, band), :] for n in range(4)]
              for r in (ks_ref, vs_ref)]
        qs = [q_ref[pl.ds(start, tq), _head(p, LANES)] for p in range(npair)]
        sel = [2 * (p // group) for p in range(npair)]
        outs = _swa_stage(qs, [kv[0][s] for s in sel], [kv[0][s + 1] for s in sel],
                          [kv[1][s] for s in sel], [kv[1][s + 1] for s in sel],
                          valid, sinks, scale)
        for p in range(npair):
            o_ref[pl.ds(start, tq), _head(p, LANES)] = outs[p].astype(o_ref.dtype)
        return c

    lax.fori_loop(0, seq // tq, body, 0)


def _swa_prompt(q, k, v, sinks, batch, seq):
    m, wq = q.shape
    wk = k.shape[1]
    npairs = wk // LANES
    qw = wq // npairs
    qo = pl.BlockSpec((seq, qw), lambda b, h, s: (b, h))
    kv = pl.BlockSpec((seq, LANES), lambda b, h, s: (b, h))
    grid_spec = pltpu.PrefetchScalarGridSpec(
        num_scalar_prefetch=1,
        grid=(batch, npairs),
        in_specs=[qo, kv, kv],
        out_specs=qo,
        scratch_shapes=[pltpu.VMEM((4, WINDOW + seq, LANES), BF16),
                        pltpu.VMEM((4, WINDOW + seq, LANES), BF16)],
    )
    return pl.pallas_call(
        functools.partial(_swa_prompt_kernel, scale=HEAD_DIM_C ** -0.5,
                          tq=SWA_TQ),
        grid_spec=grid_spec,
        out_shape=jax.ShapeDtypeStruct((m, wq), BF16),
        compiler_params=_params("arbitrary", "arbitrary"),
        name="swa_prompt",
    )(sinks, q, k, v)


def _swa_step_kernel(sinks_ref, q_ref, kn_ref, vn_ref, kc_ref, vc_ref, o_ref,
                     nk_ref, nv_ref, *, scale):
    t = q_ref.shape[0]
    past, wk = kc_ref.shape
    band = 2 * LANES
    nkv = wk // LANES
    npair = q_ref.shape[1] // LANES
    group = npair // (2 * nkv)
    kc, vc = kc_ref[...], vc_ref[...]
    kn, vn = kn_ref[...], vn_ref[...]
    nk_ref[0:past - t] = kc[t:]
    nv_ref[0:past - t] = vc[t:]
    nk_ref[past - t:past] = kn
    nv_ref[past - t:past] = vn
    pad = jnp.zeros((band - past - t, wk), F32)
    k_all = jnp.concatenate([kc, kn, pad], axis=0)
    v_all = jnp.concatenate([vc, vn, pad], axis=0)
    col = lax.broadcasted_iota(jnp.int32, (t, band), 1)
    valid = col < past + t
    kx = [[x.astype(BF16) for x in _split_pair(k_all[:, _head(g, LANES)])]
          for g in range(nkv)]
    vx = [[x.astype(BF16) for x in _split_pair(v_all[:, _head(g, LANES)])]
          for g in range(nkv)]
    qs = [q_ref[:, _head(p, LANES)] for p in range(npair)]
    where = [(p // (2 * group), 2 * ((p // group) % 2)) for p in range(npair)]
    sinks = [sinks_ref[n] for n in range(2 * npair)]
    outs = _swa_stage(qs, [kx[g][s] for g, s in where],
                      [kx[g][s + 1] for g, s in where],
                      [vx[g][s] for g, s in where],
                      [vx[g][s + 1] for g, s in where], valid, sinks, scale)
    for p in range(npair):
        o_ref[:, _head(p, LANES)] = outs[p].astype(o_ref.dtype)


def _swa_step(q, k_new, v_new, k_cache, v_cache, sinks, batch):
    m, wq = q.shape
    wk = k_new.shape[1]
    t = m // batch
    past = k_cache.shape[1]
    assert past + t <= 2 * LANES
    qrow = pl.BlockSpec((t, wq), lambda b, s: (b, 0))
    krow = pl.BlockSpec((t, wk), lambda b, s: (b, 0))
    cache = pl.BlockSpec((None, past, wk), lambda b, s: (b, 0, 0))
    grid_spec = pltpu.PrefetchScalarGridSpec(
        num_scalar_prefetch=1,
        grid=(batch,),
        in_specs=[qrow, krow, krow, cache, cache],
        out_specs=[qrow, cache, cache],
    )
    return pl.pallas_call(
        functools.partial(_swa_step_kernel, scale=HEAD_DIM_C ** -0.5),
        grid_spec=grid_spec,
        out_shape=[jax.ShapeDtypeStruct((m, wq), BF16),
                   jax.ShapeDtypeStruct(k_cache.shape, F32),
                   jax.ShapeDtypeStruct(v_cache.shape, F32)],
        compiler_params=_params("arbitrary"),
        name="swa_step",
    )(sinks, q, k_new, v_new, k_cache, v_cache)


def _rope_tables(pos):
    half = ROPE_DIM // 2
    inv = jnp.power(ROPE_THETA, -jnp.arange(half, dtype=F32) / half)
    ang = pos.astype(F32)[:, None] * inv[None, :]
    cos, sin = jnp.cos(ang), jnp.sin(ang)
    n = pos.shape[0]
    reps = LANES // HEAD_DIM_C
    rest = HEAD_DIM_C - ROPE_DIM
    ones = jnp.ones((n, rest), F32)
    zeros = jnp.zeros((n, rest), F32)
    zh = jnp.zeros((n, half), F32)
    c = jnp.tile(jnp.concatenate([cos, cos, ones], axis=1), (1, reps))
    s1 = jnp.tile(jnp.concatenate([zh, sin, zeros], axis=1), (1, reps))
    s2 = jnp.tile(jnp.concatenate([-sin, zh, zeros], axis=1), (1, reps))
    return c, s1, s2


def _ffn(h, norm_ffn3, w_gate, w_up, w_down, layer):
    hn = _rmsnorm(h, norm_ffn3, layer, BF16)
    act = _gate_up(hn, w_gate, w_up, layer)
    tk = w_down.shape[1] // 2
    h = _mm_res([(act, 0, 0)], w_down, layer, tk, h, tn=256)
    return _mm_res([(act, 1, 1)], w_down, layer, tk, h, tn=256)


def kernel(x_prompt, x_sample, cache_sb_k, cache_sb_v, cache_band_k, cache_band_v, cache_swa_k, cache_swa_v, norm_mix, norm_ffn, norm_final, w_in_ab, w_out_ab, rel_bias, w_qkv_c, w_out_c, sinks, w_gate, w_up, w_down):
    bp, seq, dm = x_prompt.shape
    bs, t, _ = x_sample.shape
    depth = norm_mix.shape[0]
    past = cache_sb_k.shape[2]
    hp = x_prompt.reshape(bp * seq, dm)
    hs = x_sample.reshape(bs * t, dm)
    norm_mix3 = norm_mix.reshape(depth, 1, dm)
    norm_ffn3 = norm_ffn.reshape(depth, 1, dm)
    norm_final3 = norm_final.reshape(1, 1, dm)

    w_sb = cache_sb_k.shape[3] * cache_sb_k.shape[4]
    w_band = cache_band_k.shape[3] * cache_band_k.shape[4]
    wq_c = w_out_c.shape[1]
    wk_c = cache_swa_k.shape[3] * cache_swa_k.shape[4]
    p_band = cache_band_k.shape[2]
    p_swa = cache_swa_k.shape[2]

    rope_p = _rope_tables(jnp.arange(seq))
    rope_s = tuple(jnp.tile(x, (bs, 1)) for x in _rope_tables(past + jnp.arange(t)))

    outs = {name: [] for name in (
        "sbk_p", "sbv_p", "bk_p", "bv_p", "ck_p", "cv_p",
        "sbk_s", "sbv_s", "bk_s", "bv_s", "ck_s", "cv_s")}

    for l in range(depth):
        hn_p = _rmsnorm(hp, norm_mix3, l, BF16)
        hn_s = _rmsnorm(hs, norm_mix3, l, BF16)
        if l % 2 == 0:
            e = l // 2
            cols = (0, w_sb, 2 * w_sb, 3 * w_sb, 3 * w_sb + w_band, 3 * w_sb + 2 * w_band)
            bias_p = _band_bias(rel_bias[e], BAND_TQ, BAND_HIST + BAND_TQ)
            bias_s = _band_bias(rel_bias[e], t, p_band + t)

            def split(hn):
                qa = _proj(hn, w_in_ab, e, cols[0], w_sb, BF16)
                ka = _proj(hn, w_in_ab, e, cols[1], w_sb, F32)
                va = _proj(hn, w_in_ab, e, cols[2], w_sb, F32)
                qb = _proj(hn, w_in_ab, e, cols[3], w_band, BF16)
                kb = _proj(hn, w_in_ab, e, cols[4], w_band, F32)
                vb = _proj(hn, w_in_ab, e, cols[5], w_band, F32)
                return qa, ka, va, qb, kb, vb

            qa, ka, va, qb, kb, vb = split(hn_p)
            oa = _sb_prompt(qa, ka, va, bp, seq)
            ob = _band_prompt(qb, kb, vb, bias_p, bp, seq)
            hp = _mm_res([(oa, 0, 0), (ob, 0, 1)], w_out_ab, e, w_sb, hp, tn=512)
            nb = min(BAND_HIST, seq)
            outs["sbk_p"].append(ka.reshape(bp, seq, *cache_sb_k.shape[3:]))
            outs["sbv_p"].append(va.reshape(bp, seq, *cache_sb_k.shape[3:]))
            outs["bk_p"].append(kb.reshape(bp, seq, *cache_band_k.shape[3:])[:, seq - nb:])
            outs["bv_p"].append(vb.reshape(bp, seq, *cache_band_k.shape[3:])[:, seq - nb:])
            qa, ka, va, qb, kb, vb = split(hn_s)
            oa = _sb_step(qa, ka, va, cache_sb_k[e], cache_sb_v[e], bs)
            ob, nbk, nbv = _band_step(qb, kb, vb, cache_band_k[e],
                                      cache_band_v[e], bias_s, bs)
            hs = _mm_res([(oa, 0, 0), (ob, 0, 1)], w_out_ab, e, w_sb, hs, tn=512)
            outs["sbk_s"].append(ka.reshape(bs, t, *cache_sb_k.shape[3:]))
            outs["sbv_s"].append(va.reshape(bs, t, *cache_sb_k.shape[3:]))
            outs["bk_s"].append(nbk)
            outs["bv_s"].append(nbv)
        else:
            o = l // 2

            def split_c(hn, rope):
                q = _proj(hn, w_qkv_c, o, 0, wq_c, BF16, rope=rope)
                k = _proj(hn, w_qkv_c, o, wq_c, wk_c, F32, rope=rope)
                v = _proj(hn, w_qkv_c, o, wq_c + wk_c, wk_c, F32)
                return q, k, v

            q, k, v = split_c(hn_p, rope_p)
            oc = _swa_prompt(q, k, v, sinks[o], bp, seq)
            hp = _mm_res([(oc, 0, 0)], w_out_c, o, wq_c, hp, tn=512)
            nw = min(WINDOW, seq)
            outs["ck_p"].append(k.reshape(bp, seq, *cache_swa_k.shape[3:])[:, seq - nw:])
            outs["cv_p"].append(v.reshape(bp, seq, *cache_swa_k.shape[3:])[:, seq - nw:])
            q, k, v = split_c(hn_s, rope_s)
            oc, nck, ncv = _swa_step(
                q, k, v, cache_swa_k[o].reshape(bs, p_swa, wk_c),
                cache_swa_v[o].reshape(bs, p_swa, wk_c), sinks[o], bs)
            hs = _mm_res([(oc, 0, 0)], w_out_c, o, wq_c, hs, tn=512)
            outs["ck_s"].append(nck.reshape(cache_swa_k.shape[1:]))
            outs["cv_s"].append(ncv.reshape(cache_swa_v.shape[1:]))
        hp = _ffn(hp, norm_ffn3, w_gate, w_up, w_down, l)
        hs = _ffn(hs, norm_ffn3, w_gate, w_up, w_down, l)

    y_prompt = _rmsnorm(hp, norm_final3, 0, F32).reshape(bp, seq, dm)
    y_sample = _rmsnorm(hs, norm_final3, 0, F32).reshape(bs, t, dm)
    st = lambda name: jnp.stack(outs[name])
    return (y_prompt, y_sample,
            st("sbk_p"), st("sbv_p"), st("bk_p"), st("bv_p"), st("ck_p"), st("cv_p"),
            st("sbk_s"), st("sbv_s"), st("bk_s"), st("bv_s"), st("ck_s"), st("cv_s"))
```

```python
import functools
import math

import numpy as np
import jax
import jax.numpy as jnp
from jax import lax
from jax.experimental import pallas as pl
from jax.experimental.pallas import tpu as pltpu

F32 = jnp.float32
BF16 = jnp.bfloat16

CHUNK = 64
EPS = 1e-6
HEAD_DIM_AB = 128
BAND_PREV_CHUNKS = 8
BAND_HIST = BAND_PREV_CHUNKS * CHUNK
REL_CLIP = 2 * CHUNK
HEAD_DIM_C = 64
N_KV_C = 8
SWA_PREV_CHUNKS = 2
WINDOW = SWA_PREV_CHUNKS * CHUNK
ROPE_THETA = 500000.0
ROPE_DIM = HEAD_DIM_C // 4

LANES = 128
V7X_VMEM_LIMIT_BYTES = 60 * 1024 * 1024

DENSE_TM = 2048
DENSE_TN = 256
DENSE_A_DOUBLE_BUFFER_BYTES = 36 * 1024 * 1024

NEG = -1e30

SB_TILE = 256
SB_HEADS = 4
SB_STEP_KEYS = 128
BAND_TQ = 128
BAND_HEADS = 4
STEP_HEADS = 8
SWA_TQ = 128


def _params(*sem):
    return pltpu.CompilerParams(dimension_semantics=sem,
                                vmem_limit_bytes=V7X_VMEM_LIMIT_BYTES)


def _chunk_of(pos):
    return lax.shift_right_logical(pos, int(math.log2(CHUNK)))


def _pick_tm(m):
    return DENSE_TM if m % DENSE_TM == 0 else m


def _rmsnorm_kernel(x_ref, g_ref, o_ref):
    x = x_ref[...]
    ms = jnp.mean(x * x, axis=-1, keepdims=True)
    y = x * lax.rsqrt(ms + EPS)
    o_ref[...] = (y * g_ref[...]).astype(o_ref.dtype)


def _rmsnorm(x, g3, layer, out_dtype):
    m, d = x.shape
    tm = 256
    return pl.pallas_call(
        _rmsnorm_kernel,
        grid=(m // tm,),
        in_specs=[pl.BlockSpec((tm, d), lambda i: (i, 0)),
                  pl.BlockSpec((None, 1, d), lambda i: (layer, 0, 0))],
        out_specs=pl.BlockSpec((tm, d), lambda i: (i, 0)),
        out_shape=jax.ShapeDtypeStruct((m, d), out_dtype),
        compiler_params=_params("arbitrary"),
        name="rmsnorm",
    )(x, g3)


class _Weight:
    def __init__(self, src, layer, kblk=0, col0=0):
        self.src, self.layer, self.kblk, self.col0 = src, layer, kblk, col0

    def spec(self, tk, tn):
        layer, kb, jb = self.layer, self.kblk, self.col0 // tn
        return pl.BlockSpec((None, tk, tn), lambda i, j: (layer, kb, jb + j))


def _rope_tile(x, c, s1, s2):
    return (x * c + pltpu.roll(x, 8, 1) * s1
            + pltpu.roll(x, LANES - 8, 1) * s2)


def _proj_kernel(*refs, rope):
    a_ref, w_ref = refs[:2]
    o_ref = refs[-1]
    acc = jnp.dot(a_ref[...], w_ref[...].astype(BF16),
                  preferred_element_type=F32)
    if not rope:
        o_ref[...] = acc.astype(o_ref.dtype)
        return
    c, s1, s2 = refs[2][...], refs[3][...], refs[4][...]
    for j in range(acc.shape[1] // LANES):
        sl = slice(j * LANES, (j + 1) * LANES)
        o_ref[:, sl] = _rope_tile(acc[:, sl], c, s1, s2).astype(o_ref.dtype)


def _proj(a, w, ncols, out_dtype, rope=None):
    m, k = a.shape
    tm = _pick_tm(m)
    tn = min(DENSE_TN, ncols)
    in_specs = [pl.BlockSpec((tm, k), lambda i, j: (i, 0)), w.spec(k, tn)]
    args = [a, w.src]
    if rope is not None:
        nrep = rope[0].shape[0] // tm
        for t in rope:
            in_specs.append(pl.BlockSpec((tm, LANES), lambda i, j: (i % nrep, 0)))
            args.append(t)
    return pl.pallas_call(
        functools.partial(_proj_kernel, rope=rope is not None),
        grid=(m // tm, ncols // tn),
        in_specs=in_specs,
        out_specs=pl.BlockSpec((tm, tn), lambda i, j: (i, j)),
        out_shape=jax.ShapeDtypeStruct((m, ncols), out_dtype),
        compiler_params=_params("arbitrary", "arbitrary"),
        name="proj",
    )(*args)


def _gate_up_kernel(a_ref, wg_ref, wu_ref, o_ref):
    a = a_ref[...]
    g = jnp.dot(a, wg_ref[...].astype(BF16), preferred_element_type=F32)
    u = jnp.dot(a, wu_ref[...].astype(BF16), preferred_element_type=F32)
    o_ref[...] = (g * jax.nn.sigmoid(g) * u).astype(o_ref.dtype)


def _gate_up(a, wg, wu, f):
    m, k = a.shape
    tm = _pick_tm(m)
    tn = DENSE_TN
    return pl.pallas_call(
        _gate_up_kernel,
        grid=(m // tm, f // tn),
        in_specs=[pl.BlockSpec((tm, k), lambda i, j: (i, 0)),
                  wg.spec(k, tn), wu.spec(k, tn)],
        out_specs=pl.BlockSpec((tm, tn), lambda i, j: (i, j)),
        out_shape=jax.ShapeDtypeStruct((m, f), BF16),
        compiler_params=_params("arbitrary", "arbitrary"),
        name="gate_up",
    )(a, wg.src, wu.src)


def _mm_res_kernel(*refs, nterms):
    a_refs = refs[:nterms]
    w_refs = refs[nterms:2 * nterms]
    r_ref = refs[2 * nterms]
    o_ref = refs[2 * nterms + 1]
    acc = r_ref[...]
    for a_ref, w_ref in zip(a_refs, w_refs):
        acc = acc + jnp.dot(a_ref[...], w_ref[...].astype(BF16),
                            preferred_element_type=F32)
    o_ref[...] = acc


def _mm_res(terms, tk, res):
    m, n = res.shape
    tm = _pick_tm(m)
    tn = DENSE_TN
    in_specs, args = [], []
    a_bytes = len(terms) * tm * tk * jnp.dtype(BF16).itemsize
    mode = {}
    if 2 * a_bytes > DENSE_A_DOUBLE_BUFFER_BYTES:
        mode = dict(pipeline_mode=pl.Buffered(1))
    elif tm == DENSE_TM:
        tm = DENSE_TM // 2
    for a, ka, _ in terms:
        in_specs.append(pl.BlockSpec((tm, tk), lambda i, j, ka=ka: (i, ka), **mode))
        args.append(a)
    for _, _, w in terms:
        in_specs.append(w.spec(tk, tn))
        args.append(w.src)
    in_specs.append(pl.BlockSpec((tm, tn), lambda i, j: (i, j)))
    args.append(res)
    return pl.pallas_call(
        functools.partial(_mm_res_kernel, nterms=len(terms)),
        grid=(m // tm, n // tn),
        in_specs=in_specs,
        out_specs=pl.BlockSpec((tm, tn), lambda i, j: (i, j)),
        out_shape=jax.ShapeDtypeStruct((m, n), F32),
        compiler_params=_params("arbitrary", "arbitrary"),
        name="mm_res",
    )(*args)


_DN_T = (((1,), (1,)), ((), ()))


def _head(h, d=HEAD_DIM_AB):
    return slice(h * d, (h + 1) * d)


def _heads_major(x):
    return pltpu.einshape("phd->hpd", x)


def _softmax_stage(ss, sinks=None):
    ms = [jnp.max(s, axis=1, keepdims=True) for s in ss]
    if sinks is not None:
        ms = [jnp.maximum(m, sk) for m, sk in zip(ms, sinks)]
    ps = [jnp.exp(s - m) for s, m in zip(ss, ms)]
    ls = [jnp.sum(p, axis=1, keepdims=True) for p in ps]
    if sinks is not None:
        ls = [l + jnp.exp(sk - m) for l, sk, m in zip(ls, sinks, ms)]
    return [p.astype(BF16) for p in ps], ls


def _sb_logs(z, valid):
    lb = jnp.minimum(z, 0.0) - jnp.log(1.0 + jnp.exp(-jnp.abs(z)))
    l1m = lb - z
    if valid is not None:
        l1m = jnp.where(valid, l1m, 0.0)
    return lb, l1m


def _split_hi_lo(x):
    hi = x.astype(BF16)
    lo = (x - hi.astype(F32)).astype(BF16)
    return jnp.concatenate([hi, lo], axis=1)


def _suffix_matrix(n):
    u = np.tril(np.ones((n, n), np.float32), -1)
    return jnp.asarray(np.concatenate([u, u], axis=0), BF16)


def _sb_prompt_kernel(q_ref, k_ref, v_ref, uu_ref, o_ref, kb_ref, vb_ref,
                      oacc_ref, cacc_ref, *, scale):
    qi = pl.program_id(2)
    tq = q_ref.shape[0]
    nh = oacc_ref.shape[0]

    @pl.when(qi == 0)
    def _():
        kb_ref[...] = k_ref[...].astype(BF16)
        vb_ref[...] = v_ref[...].astype(BF16)

    oacc_ref[...] = jnp.zeros(oacc_ref.shape, F32)
    cacc_ref[...] = jnp.zeros(cacc_ref.shape, F32)

    def tile(j, masked):
        start = pl.multiple_of(j * tq, tq)
        valid = None
        if masked:
            row = lax.broadcasted_iota(jnp.int32, (tq, tq), 0)
            col = lax.broadcasted_iota(jnp.int32, (tq, tq), 1)
            valid = col < row
        zs = [lax.dot_general(q_ref[:, _head(h)],
                              kb_ref[pl.ds(start, tq), _head(h)], _DN_T,
                              preferred_element_type=F32) * scale
              for h in range(nh)]
        logs = [_sb_logs(z, valid) for z in zs]
        sufs = [jnp.dot(_split_hi_lo(l1m), uu_ref[...],
                        preferred_element_type=F32) for _, l1m in logs]
        ws = [jnp.exp(lb + suf + cacc_ref[h])
              for h, ((lb, _), suf) in enumerate(zip(logs, sufs))]
        if masked:
            ws = [jnp.where(valid, w, 0.0) for w in ws]
        for h in range(nh):
            oacc_ref[h] += jnp.dot(ws[h].astype(BF16),
                                   vb_ref[pl.ds(start, tq), _head(h)],
                                   preferred_element_type=F32)
            cacc_ref[h] += jnp.sum(logs[h][1], axis=1, keepdims=True)

    tile(qi, True)

    def body(t, c):
        tile(qi - 1 - t, False)
        return c

    lax.fori_loop(0, qi, body, 0)
    for h in range(nh):
        o_ref[:, _head(h)] = oacc_ref[h].astype(o_ref.dtype)


def _sb_prompt(q, k, v, batch, seq):
    m, w = q.shape
    d = HEAD_DIM_AB
    nh = SB_HEADS
    tq = SB_TILE
    nq = seq // tq
    qo = pl.BlockSpec((tq, nh * d), lambda b, g, i: (b * nq + i, g))
    kv = pl.BlockSpec((seq, nh * d), lambda b, g, i: (b, g))
    return pl.pallas_call(
        functools.partial(_sb_prompt_kernel, scale=d ** -0.5),
        grid=(batch, w // (nh * d), nq),
        in_specs=[qo, kv, kv, pl.BlockSpec((2 * tq, tq), lambda b, g, i: (0, 0))],
        out_specs=qo,
        out_shape=jax.ShapeDtypeStruct((m, w), BF16),
        scratch_shapes=[pltpu.VMEM((seq, nh * d), BF16),
                        pltpu.VMEM((seq, nh * d), BF16),
                        pltpu.VMEM((nh, tq, d), F32),
                        pltpu.VMEM((nh, tq, 1), F32)],
        compiler_params=_params("arbitrary", "arbitrary", "arbitrary"),
        name="sb_prompt",
    )(q, k, v, _suffix_matrix(tq))


def _sb_step_kernel(q_ref, kn_ref, vn_ref, kc_ref, vc_ref, uu_ref, o_ref,
                    kb_ref, vb_ref, *, scale):
    t = q_ref.shape[0]
    past, nh, d = kc_ref.shape
    tk = uu_ref.shape[1]
    n = past + tk
    kb_ref[:, 0:past, :] = _heads_major(kc_ref[...]).astype(BF16)
    vb_ref[:, 0:past, :] = _heads_major(vc_ref[...]).astype(BF16)
    for h in range(nh):
        kb_ref[h, past:past + t] = kn_ref[:, _head(h)].astype(BF16)
        vb_ref[h, past:past + t] = vn_ref[:, _head(h)].astype(BF16)
        kb_ref[h, past + t:n] = jnp.zeros((tk - t, d), BF16)
        vb_ref[h, past + t:n] = jnp.zeros((tk - t, d), BF16)

    z = jnp.concatenate(
        [lax.dot_general(q_ref[:, _head(h)], kb_ref[h], _DN_T,
                         preferred_element_type=F32) for h in range(nh)],
        axis=0) * scale
    row = lax.broadcasted_iota(jnp.int32, z.shape, 0) & (t - 1)
    col = lax.broadcasted_iota(jnp.int32, z.shape, 1)
    valid = col < past + row
    lb, l1m = _sb_logs(z, valid)
    ntiles = n // tk
    sufs = [jnp.dot(_split_hi_lo(l1m[:, j * tk:(j + 1) * tk]), uu_ref[...],
                    preferred_element_type=F32) for j in range(ntiles)]
    tots = [jnp.sum(l1m[:, j * tk:(j + 1) * tk], axis=1, keepdims=True)
            for j in range(ntiles)]
    carries = [None] * ntiles
    carry = jnp.zeros_like(tots[0])
    for j in reversed(range(ntiles)):
        carries[j] = carry
        carry = carry + tots[j]
    after = jnp.concatenate([sufs[j] + carries[j] for j in range(ntiles)],
                            axis=1)
    w = jnp.where(valid, jnp.exp(lb + after), 0.0).astype(BF16)
    for h in range(nh):
        o = jnp.dot(w[h * t:(h + 1) * t], vb_ref[h], preferred_element_type=F32)
        o_ref[:, _head(h)] = o.astype(o_ref.dtype)


def _sb_step(q, k_new, v_new, k_cache, v_cache, batch):
    m, w = q.shape
    _, past, heads, d = k_cache.shape
    t = m // batch
    tk = SB_STEP_KEYS
    nh = STEP_HEADS
    assert past % tk == 0 and t <= tk and t & (t - 1) == 0
    row = pl.BlockSpec((t, nh * d), lambda b, g: (b, g))
    cache = pl.BlockSpec((None, past, nh, d), lambda b, g: (b, 0, g, 0))
    return pl.pallas_call(
        functools.partial(_sb_step_kernel, scale=d ** -0.5),
        grid=(batch, heads // nh),
        in_specs=[row, row, row, cache, cache,
                  pl.BlockSpec((2 * tk, tk), lambda b, g: (0, 0))],
        out_specs=row,
        out_shape=jax.ShapeDtypeStruct((m, w), BF16),
        scratch_shapes=[pltpu.VMEM((nh, past + tk, d), BF16),
                        pltpu.VMEM((nh, past + tk, d), BF16)],
        compiler_params=_params("arbitrary", "arbitrary"),
        name="sb_step",
    )(q, k_new, v_new, k_cache, v_cache, _suffix_matrix(tk))


def _band_bias_kernel(r_ref, o_ref, *, n_valid):
    tq, w = o_ref.shape
    wr = r_ref.shape[1]
    t = jnp.broadcast_to(r_ref[...], (tq, wr))
    t = pltpu.roll(t, 0, 1, stride=1, stride_axis=0)
    row = lax.broadcasted_iota(jnp.int32, (tq, w), 0)
    col = lax.broadcasted_iota(jnp.int32, (tq, w), 1)
    qc = _chunk_of(row)
    kc = _chunk_of(col)
    valid = (kc >= qc) & (kc <= qc + BAND_PREV_CHUNKS) & (col < n_valid)
    o_ref[...] = jnp.where(valid, t[:, :w], NEG)


def _band_bias(rel, tq, n_valid):
    heads = rel.shape[0]
    w = BAND_HIST + BAND_TQ
    wr = w + LANES
    p = np.arange(wr)
    delta = np.where(p < w, p, p - wr)
    idx = np.clip(BAND_HIST - delta, -REL_CLIP, REL_CLIP) + REL_CLIP
    r = jnp.take(rel.astype(F32), jnp.asarray(idx, jnp.int32), axis=1)
    r = r.reshape(heads, 1, wr)
    return pl.pallas_call(
        functools.partial(_band_bias_kernel, n_valid=n_valid),
        grid=(heads,),
        in_specs=[pl.BlockSpec((None, 1, wr), lambda h: (h, 0, 0))],
        out_specs=pl.BlockSpec((None, tq, w), lambda h: (h, 0, 0)),
        out_shape=jax.ShapeDtypeStruct((heads, tq, w), F32),
        compiler_params=_params("arbitrary"),
        name="band_bias",
    )(r)


def _band_prompt_kernel(q_ref, k_ref, v_ref, bias_ref, o_ref, kp_ref, vp_ref,
                        *, scale, tq):
    seq, wd = k_ref.shape
    d = HEAD_DIM_AB
    nh = wd // d
    w = bias_ref.shape[1]
    hist = w - tq
    kp_ref[0:hist] = jnp.zeros((hist, wd), BF16)
    vp_ref[0:hist] = jnp.zeros((hist, wd), BF16)
    kp_ref[hist:hist + seq] = k_ref[...].astype(BF16)
    vp_ref[hist:hist + seq] = v_ref[...].astype(BF16)
    col = lax.broadcasted_iota(jnp.int32, (tq, w), 1)

    def body(i, c):
        start = pl.multiple_of(i * tq, tq)
        in_seq = col >= hist - start
        ss = []
        for h in range(nh):
            s = lax.dot_general(q_ref[pl.ds(start, tq), _head(h)],
                                kp_ref[pl.ds(start, w), _head(h)], _DN_T,
                                preferred_element_type=F32)
            s = s * scale + bias_ref[h * tq:(h + 1) * tq, :]
            ss.append(jnp.where(in_seq, s, NEG))
        ps, ls = _softmax_stage(ss)
        for h in range(nh):
            o = jnp.dot(ps[h], vp_ref[pl.ds(start, w), _head(h)],
                        preferred_element_type=F32) / ls[h]
            o_ref[pl.ds(start, tq), _head(h)] = o.astype(o_ref.dtype)
        return c

    lax.fori_loop(0, seq // tq, body, 0)


def _band_prompt(q, k, v, bias, batch, seq):
    m, wd = q.shape
    d = HEAD_DIM_AB
    nh = BAND_HEADS
    heads, tq, w = bias.shape
    bias = bias.reshape(heads // nh, nh * tq, w)
    blk = pl.BlockSpec((seq, nh * d), lambda b, g: (b, g))
    return pl.pallas_call(
        functools.partial(_band_prompt_kernel, scale=d ** -0.5, tq=tq),
        grid=(batch, heads // nh),
        in_specs=[blk, blk, blk,
                  pl.BlockSpec((None, nh * tq, w), lambda b, g: (g, 0, 0))],
        out_specs=blk,
        out_shape=jax.ShapeDtypeStruct((m, wd), BF16),
        scratch_shapes=[pltpu.VMEM((BAND_HIST + seq, nh * d), BF16),
                        pltpu.VMEM((BAND_HIST + seq, nh * d), BF16)],
        compiler_params=_params("arbitrary", "arbitrary"),
        name="band_prompt",
    )(q, k, v, bias)


def _roll_cache(new_ref, cache, new_rows_ref, t):
    past, nh, d = cache.shape
    new_ref[0:past - t] = cache[t:]
    for h in range(nh):
        new_ref[past - t:past, h, :] = new_rows_ref[:, _head(h, d)]


def _band_step_kernel(q_ref, kn_ref, vn_ref, kc_ref, vc_ref, bias_ref, o_ref,
                      nk_ref, nv_ref, kb_ref, vb_ref, *, scale):
    t = q_ref.shape[0]
    past, nh, d = kc_ref.shape
    w = bias_ref.shape[1]
    kc, vc = kc_ref[...], vc_ref[...]
    _roll_cache(nk_ref, kc, kn_ref, t)
    _roll_cache(nv_ref, vc, vn_ref, t)
    kb_ref[:, 0:past, :] = _heads_major(kc).astype(BF16)
    vb_ref[:, 0:past, :] = _heads_major(vc).astype(BF16)
    for h in range(nh):
        kb_ref[h, past:past + t] = kn_ref[:, _head(h)].astype(BF16)
        vb_ref[h, past:past + t] = vn_ref[:, _head(h)].astype(BF16)
        kb_ref[h, past + t:w] = jnp.zeros((w - past - t, d), BF16)
        vb_ref[h, past + t:w] = jnp.zeros((w - past - t, d), BF16)
    ss = [lax.dot_general(q_ref[:, _head(h)], kb_ref[h], _DN_T,
                          preferred_element_type=F32) * scale
          + bias_ref[h * t:(h + 1) * t, :] for h in range(nh)]
    ps, ls = _softmax_stage(ss)
    for h in range(nh):
        o = jnp.dot(ps[h], vb_ref[h], preferred_element_type=F32) / ls[h]
        o_ref[:, _head(h)] = o.astype(o_ref.dtype)


def _band_step(q, k_new, v_new, k_cache, v_cache, bias, batch):
    m, wd = q.shape
    _, past, heads, d = k_cache.shape
    t = m // batch
    nh = STEP_HEADS
    w = bias.shape[2]
    assert past == BAND_HIST and past + t <= w
    bias = bias.reshape(heads // nh, nh * t, w)
    row = pl.BlockSpec((t, nh * d), lambda b, g: (b, g))
    cache = pl.BlockSpec((None, past, nh, d), lambda b, g: (b, 0, g, 0))
    return pl.pallas_call(
        functools.partial(_band_step_kernel, scale=d ** -0.5),
        grid=(batch, heads // nh),
        in_specs=[row, row, row, cache, cache,
                  pl.BlockSpec((None, nh * t, w), lambda b, g: (g, 0, 0))],
        out_specs=[row, cache, cache],
        out_shape=[jax.ShapeDtypeStruct((m, wd), BF16),
                   jax.ShapeDtypeStruct(k_cache.shape, F32),
                   jax.ShapeDtypeStruct(v_cache.shape, F32)],
        scratch_shapes=[pltpu.VMEM((nh, w, d), BF16),
                        pltpu.VMEM((nh, w, d), BF16)],
        compiler_params=_params("arbitrary", "arbitrary"),
        name="band_step",
    )(q, k_new, v_new, k_cache, v_cache, bias)


def _split_pair(blk):
    rolled = pltpu.roll(blk, HEAD_DIM_C, 1)
    lane = lax.broadcasted_iota(jnp.int32, blk.shape, 1)
    low = lane < HEAD_DIM_C
    zero = jnp.zeros_like(blk)
    return [jnp.where(low, blk, zero), jnp.where(low, zero, rolled),
            jnp.where(low, rolled, zero), jnp.where(low, zero, blk)]


def _swa_stage(qs, kls, khs, vls, vhs, valid, sinks, scale):
    ss = []
    for q, kl, kh in zip(qs, kls, khs):
        for kx in (kl, kh):
            s = lax.dot_general(q, kx, _DN_T, preferred_element_type=F32)
            ss.append(jnp.where(valid, s * scale, NEG))
    ps, ls = _softmax_stage(ss, sinks)
    return [jnp.dot(ps[2 * i], vls[i], preferred_element_type=F32) / ls[2 * i]
            + jnp.dot(ps[2 * i + 1], vhs[i], preferred_element_type=F32)
            / ls[2 * i + 1] for i in range(len(qs))]


def _swa_prompt_kernel(sinks_ref, q_ref, k_ref, v_ref, o_ref, ks_ref, vs_ref,
                       *, scale, tq):
    hp = pl.program_id(1)
    seq = k_ref.shape[0]
    band = WINDOW + tq
    npair = q_ref.shape[1] // LANES
    group = npair // 2

    for src, dst in ((k_ref, ks_ref), (v_ref, vs_ref)):
        for n, x in enumerate(_split_pair(src[...])):
            dst[n, 0:WINDOW] = jnp.zeros((WINDOW, LANES), BF16)
            dst[n, WINDOW:WINDOW + seq] = x.astype(BF16)

    row = lax.broadcasted_iota(jnp.int32, (tq, band), 0)
    col = lax.broadcasted_iota(jnp.int32, (tq, band), 1)
    qc = _chunk_of(row)
    kc = _chunk_of(col)
    in_band = (kc >= qc) & (kc <= qc + SWA_PREV_CHUNKS)
    sinks = [sinks_ref[hp * 2 * npair + n] for n in range(2 * npair)]

    def body(i, c):
        start = pl.multiple_of(i * tq, tq)
        valid = in_band & (col >= WINDOW - start)
        kv = [[r[n, pl.ds(start, band), :] for n in range(4)]
              for r in (ks_ref, vs_ref)]
        qs = [q_ref[pl.ds(start, tq), _head(p, LANES)] for p in range(npair)]
        sel = [2 * (p // group) for p in range(npair)]
        outs = _swa_stage(qs, [kv[0][s] for s in sel], [kv[0][s + 1] for s in sel],
                          [kv[1][s] for s in sel], [kv[1][s + 1] for s in sel],
                          valid, sinks, scale)
        for p in range(npair):
            o_ref[pl.ds(start, tq), _head(p, LANES)] = outs[p].astype(o_ref.dtype)
        return c

    lax.fori_loop(0, seq // tq, body, 0)


def _swa_prompt(q, k, v, sinks, batch, seq):
    m, wq = q.shape
    wk = k.shape[1]
    npairs = wk // LANES
    qw = wq // npairs
    qo = pl.BlockSpec((seq, qw), lambda b, h, s: (b, h))
    kv = pl.BlockSpec((seq, LANES), lambda b, h, s: (b, h))
    grid_spec = pltpu.PrefetchScalarGridSpec(
        num_scalar_prefetch=1,
        grid=(batch, npairs),
        in_specs=[qo, kv, kv],
        out_specs=qo,
        scratch_shapes=[pltpu.VMEM((4, WINDOW + seq, LANES), BF16),
                        pltpu.VMEM((4, WINDOW + seq, LANES), BF16)],
    )
    return pl.pallas_call(
        functools.partial(_swa_prompt_kernel, scale=HEAD_DIM_C ** -0.5,
                          tq=SWA_TQ),
        grid_spec=grid_spec,
        out_shape=jax.ShapeDtypeStruct((m, wq), BF16),
        compiler_params=_params("arbitrary", "arbitrary"),
        name="swa_prompt",
    )(sinks, q, k, v)


def _swa_step_kernel(sinks_ref, q_ref, kn_ref, vn_ref, kc_ref, vc_ref, o_ref,
                     nk_ref, nv_ref, *, scale):
    t = q_ref.shape[0]
    past, wk = kc_ref.shape
    band = 2 * LANES
    nkv = wk // LANES
    npair = q_ref.shape[1] // LANES
    group = npair // (2 * nkv)
    kc, vc = kc_ref[...], vc_ref[...]
    kn, vn = kn_ref[...], vn_ref[...]
    nk_ref[0:past - t] = kc[t:]
    nv_ref[0:past - t] = vc[t:]
    nk_ref[past - t:past] = kn
    nv_ref[past - t:past] = vn
    pad = jnp.zeros((band - past - t, wk), F32)
    k_all = jnp.concatenate([kc, kn, pad], axis=0)
    v_all = jnp.concatenate([vc, vn, pad], axis=0)
    col = lax.broadcasted_iota(jnp.int32, (t, band), 1)
    valid = col < past + t
    kx = [[x.astype(BF16) for x in _split_pair(k_all[:, _head(g, LANES)])]
          for g in range(nkv)]
    vx = [[x.astype(BF16) for x in _split_pair(v_all[:, _head(g, LANES)])]
          for g in range(nkv)]
    qs = [q_ref[:, _head(p, LANES)] for p in range(npair)]
    where = [(p // (2 * group), 2 * ((p // group) % 2)) for p in range(npair)]
    sinks = [sinks_ref[n] for n in range(2 * npair)]
    outs = _swa_stage(qs, [kx[g][s] for g, s in where],
                      [kx[g][s + 1] for g, s in where],
                      [vx[g][s] for g, s in where],
                      [vx[g][s + 1] for g, s in where], valid, sinks, scale)
    for p in range(npair):
        o_ref[:, _head(p, LANES)] = outs[p].astype(o_ref.dtype)


def _swa_step(q, k_new, v_new, k_cache, v_cache, sinks, batch):
    m, wq = q.shape
    wk = k_new.shape[1]
    t = m // batch
    past = k_cache.shape[1]
    assert past + t <= 2 * LANES
    qrow = pl.BlockSpec((t, wq), lambda b, s: (b, 0))
    krow = pl.BlockSpec((t, wk), lambda b, s: (b, 0))
    cache = pl.BlockSpec((None, past, wk), lambda b, s: (b, 0, 0))
    grid_spec = pltpu.PrefetchScalarGridSpec(
        num_scalar_prefetch=1,
        grid=(batch,),
        in_specs=[qrow, krow, krow, cache, cache],
        out_specs=[qrow, cache, cache],
    )
    return pl.pallas_call(
        functools.partial(_swa_step_kernel, scale=HEAD_DIM_C ** -0.5),
        grid_spec=grid_spec,
        out_shape=[jax.ShapeDtypeStruct((m, wq), BF16),
                   jax.ShapeDtypeStruct(k_cache.shape, F32),
                   jax.ShapeDtypeStruct(v_cache.shape, F32)],
        compiler_params=_params("arbitrary"),
        name="swa_step",
    )(sinks, q, k_new, v_new, k_cache, v_cache)


def _rope_tables(pos):
    half = ROPE_DIM // 2
    inv = jnp.power(ROPE_THETA, -jnp.arange(half, dtype=F32) / half)
    ang = pos.astype(F32)[:, None] * inv[None, :]
    cos, sin = jnp.cos(ang), jnp.sin(ang)
    n = pos.shape[0]
    reps = LANES // HEAD_DIM_C
    rest = HEAD_DIM_C - ROPE_DIM
    ones = jnp.ones((n, rest), F32)
    zeros = jnp.zeros((n, rest), F32)
    zh = jnp.zeros((n, half), F32)
    c = jnp.tile(jnp.concatenate([cos, cos, ones], axis=1), (1, reps))
    s1 = jnp.tile(jnp.concatenate([zh, sin, zeros], axis=1), (1, reps))
    s2 = jnp.tile(jnp.concatenate([-sin, zh, zeros], axis=1), (1, reps))
    return c, s1, s2


def _ffn(h, norm_ffn3, w_gate, w_up, w_down, layer):
    f = w_gate.shape[-1]
    hn = _rmsnorm(h, norm_ffn3, layer, BF16)
    act = _gate_up(hn, _Weight(w_gate, layer), _Weight(w_up, layer), f)
    h = _mm_res([(act, 0, _Weight(w_down, layer, 0))], f // 2, h)
    return _mm_res([(act, 1, _Weight(w_down, layer, 1))], f // 2, h)


def kernel(x_prompt, x_sample, cache_sb_k, cache_sb_v, cache_band_k, cache_band_v, cache_swa_k, cache_swa_v, norm_mix, norm_ffn, norm_final, w_in_ab, w_out_ab, rel_bias, w_qkv_c, w_out_c, sinks, w_gate, w_up, w_down):
    bp, seq, dm = x_prompt.shape
    bs, t, _ = x_sample.shape
    depth = norm_mix.shape[0]
    past = cache_sb_k.shape[2]
    hp = x_prompt.reshape(bp * seq, dm)
    hs = x_sample.reshape(bs * t, dm)
    norm_mix3 = norm_mix.reshape(depth, 1, dm)
    norm_ffn3 = norm_ffn.reshape(depth, 1, dm)
    norm_final3 = norm_final.reshape(1, 1, dm)

    w_sb = cache_sb_k.shape[3] * cache_sb_k.shape[4]
    w_band = cache_band_k.shape[3] * cache_band_k.shape[4]
    wq_c = w_out_c.shape[1]
    wk_c = cache_swa_k.shape[3] * cache_swa_k.shape[4]
    p_band = cache_band_k.shape[2]
    p_swa = cache_swa_k.shape[2]

    rope_p = _rope_tables(jnp.arange(seq))
    rope_s = tuple(jnp.tile(x, (bs, 1)) for x in _rope_tables(past + jnp.arange(t)))

    outs = {name: [] for name in (
        "sbk_p", "sbv_p", "bk_p", "bv_p", "ck_p", "cv_p",
        "sbk_s", "sbv_s", "bk_s", "bv_s", "ck_s", "cv_s")}

    for l in range(depth):
        hn_p = _rmsnorm(hp, norm_mix3, l, BF16)
        hn_s = _rmsnorm(hs, norm_mix3, l, BF16)
        if l % 2 == 0:
            e = l // 2
            cols = (0, w_sb, 2 * w_sb, 3 * w_sb, 3 * w_sb + w_band, 3 * w_sb + 2 * w_band)
            bias_p = _band_bias(rel_bias[e], BAND_TQ, BAND_HIST + BAND_TQ)
            bias_s = _band_bias(rel_bias[e], t, p_band + t)

            widths = (w_sb, w_sb, w_sb, w_band, w_band, w_band)
            dtypes = (BF16, F32, F32, BF16, F32, F32)

            w_in = [_Weight(w_in_ab, e, 0, c) for c in cols]
            w_out = (_Weight(w_out_ab, e, 0), _Weight(w_out_ab, e, 1))

            qa, ka, va, qb, kb, vb = (
                _proj(hn_p, w, n, dt) for w, n, dt in zip(w_in, widths, dtypes))
            oa = _sb_prompt(qa, ka, va, bp, seq)
            ob = _band_prompt(qb, kb, vb, bias_p, bp, seq)
            hp = _mm_res([(oa, 0, w_out[0]), (ob, 0, w_out[1])], w_sb, hp)
            nb = min(BAND_HIST, seq)
            outs["sbk_p"].append(ka.reshape(bp, seq, *cache_sb_k.shape[3:]))
            outs["sbv_p"].append(va.reshape(bp, seq, *cache_sb_k.shape[3:]))
            outs["bk_p"].append(kb.reshape(bp, seq, *cache_band_k.shape[3:])[:, seq - nb:])
            outs["bv_p"].append(vb.reshape(bp, seq, *cache_band_k.shape[3:])[:, seq - nb:])
            qa, ka, va, qb, kb, vb = (
                _proj(hn_s, w, n, dt)
                for w, n, dt in zip(w_in, widths, dtypes))
            oa = _sb_step(qa, ka, va, cache_sb_k[e], cache_sb_v[e], bs)
            ob, nbk, nbv = _band_step(qb, kb, vb, cache_band_k[e],
                                      cache_band_v[e], bias_s, bs)
            hs = _mm_res([(oa, 0, w_out[0]), (ob, 0, w_out[1])], w_sb, hs)
            outs["sbk_s"].append(ka.reshape(bs, t, *cache_sb_k.shape[3:]))
            outs["sbv_s"].append(va.reshape(bs, t, *cache_sb_k.shape[3:]))
            outs["bk_s"].append(nbk)
            outs["bv_s"].append(nbv)
        else:
            o = l // 2

            widths = (wq_c, wk_c, wk_c)
            dtypes = (BF16, F32, F32)

            w_in = [_Weight(w_qkv_c, o, 0, c) for c in (0, wq_c, wq_c + wk_c)]
            w_out = _Weight(w_out_c, o, 0)

            q, k, v = (_proj(hn_p, w, n, dt, rope=rp)
                       for w, n, dt, rp in zip(w_in, widths, dtypes,
                                               (rope_p, rope_p, None)))
            oc = _swa_prompt(q, k, v, sinks[o], bp, seq)
            hp = _mm_res([(oc, 0, w_out)], wq_c, hp)
            nw = min(WINDOW, seq)
            outs["ck_p"].append(k.reshape(bp, seq, *cache_swa_k.shape[3:])[:, seq - nw:])
            outs["cv_p"].append(v.reshape(bp, seq, *cache_swa_k.shape[3:])[:, seq - nw:])
            q, k, v = (_proj(hn_s, w, n, dt, rope=rp)
                       for w, n, dt, rp in zip(w_in, widths, dtypes,
                                               (rope_s, rope_s, None)))
            oc, nck, ncv = _swa_step(
                q, k, v, cache_swa_k[o].reshape(bs, p_swa, wk_c),
                cache_swa_v[o].reshape(bs, p_swa, wk_c), sinks[o], bs)
            hs = _mm_res([(oc, 0, w_out)], wq_c, hs)
            outs["ck_s"].append(nck.reshape(cache_swa_k.shape[1:]))
            outs["cv_s"].append(ncv.reshape(cache_swa_v.shape[1:]))
        hp = _ffn(hp, norm_ffn3, w_gate, w_up, w_down, l)
        hs = _ffn(hs, norm_ffn3, w_gate, w_up, w_down, l)

    y_prompt = _rmsnorm(hp, norm_final3, 0, F32).reshape(bp, seq, dm)
    y_sample = _rmsnorm(hs, norm_final3, 0, F32).reshape(bs, t, dm)
    st = lambda name: jnp.stack(outs[name])
    return (y_prompt, y_sample,
            st("sbk_p"), st("sbv_p"), st("bk_p"), st("bv_p"), st("ck_p"), st("cv_p"),
            st("sbk_s"), st("sbv_s"), st("bk_s"), st("bv_s"), st("ck_s"), st("cv_s"))
```

```python
import functools
import math

import numpy as np
import jax
import jax.numpy as jnp
from jax import lax
from jax.experimental import pallas as pl
from jax.experimental.pallas import tpu as pltpu

F32 = jnp.float32
BF16 = jnp.bfloat16

CHUNK = 64
EPS = 1e-6
HEAD_DIM_AB = 128
BAND_PREV_CHUNKS = 8
BAND_HIST = BAND_PREV_CHUNKS * CHUNK
REL_CLIP = 2 * CHUNK
HEAD_DIM_C = 64
N_KV_C = 8
SWA_PREV_CHUNKS = 2
WINDOW = SWA_PREV_CHUNKS * CHUNK
ROPE_THETA = 500000.0
ROPE_DIM = HEAD_DIM_C // 4

LANES = 128
V7X_VMEM_LIMIT_BYTES = 60 * 1024 * 1024

DENSE_TM = 1024
GATE_UP_TM = 2048
DENSE_TN = 256
PROJ_TN = 512

NEG = -1e30

SB_TILE = 256
SB_HEADS = 4
SB_STEP_KEYS = 128
BAND_TQ = 128
BAND_HEADS = 4
STEP_HEADS = 8
SWA_TQ = 128


def _params(*sem):
    return pltpu.CompilerParams(dimension_semantics=sem,
                                vmem_limit_bytes=V7X_VMEM_LIMIT_BYTES)


def _chunk_of(pos):
    return lax.shift_right_logical(pos, int(math.log2(CHUNK)))


def _pick_tm(m):
    return DENSE_TM if m % DENSE_TM == 0 else m


def _rmsnorm_kernel(x_ref, g_ref, o_ref):
    x = x_ref[...]
    ms = jnp.mean(x * x, axis=-1, keepdims=True)
    y = x * lax.rsqrt(ms + EPS)
    o_ref[...] = (y * g_ref[...]).astype(o_ref.dtype)


def _rmsnorm(x, g3, layer, out_dtype):
    m, d = x.shape
    tm = 256
    return pl.pallas_call(
        _rmsnorm_kernel,
        grid=(m // tm,),
        in_specs=[pl.BlockSpec((tm, d), lambda i: (i, 0)),
                  pl.BlockSpec((None, 1, d), lambda i: (layer, 0, 0))],
        out_specs=pl.BlockSpec((tm, d), lambda i: (i, 0)),
        out_shape=jax.ShapeDtypeStruct((m, d), out_dtype),
        compiler_params=_params("arbitrary"),
        name="rmsnorm",
    )(x, g3)


class _Weight:
    def __init__(self, src, layer, kblk=0, col0=0):
        self.src, self.layer, self.kblk, self.col0 = src, layer, kblk, col0

    def spec(self, tk, tn):
        layer, kb, jb = self.layer, self.kblk, self.col0 // tn
        return pl.BlockSpec((None, tk, tn), lambda i, j: (layer, kb, jb + j))


def _rope_tile(x, c, s1, s2):
    return (x * c + pltpu.roll(x, 8, 1) * s1
            + pltpu.roll(x, LANES - 8, 1) * s2)


def _rider_in(x):
    return pl.BlockSpec(x.shape, lambda i, j: (0, 0), pipeline_mode=pl.Buffered(1))


def _rider_out(rows, tn, ni):
    return pl.BlockSpec((rows, tn),
                        lambda i, j: (0, jnp.where(i == ni - 1, j, 0)))


def _on_last_row_block(fn):
    pl.when(pl.program_id(0) == pl.num_programs(0) - 1)(fn)


def _proj_kernel(*refs, rope, rider):
    w_ref = refs[1]
    n_in = 5 if rope else 2

    def run(a_ref, tabs, o_ref):
        acc = jnp.dot(a_ref[...], w_ref[...].astype(BF16),
                      preferred_element_type=F32)
        if not rope:
            o_ref[...] = acc.astype(o_ref.dtype)
            return
        c, s1, s2 = (t[...] for t in tabs)
        for j in range(acc.shape[1] // LANES):
            sl = slice(j * LANES, (j + 1) * LANES)
            o_ref[:, sl] = _rope_tile(acc[:, sl], c, s1, s2).astype(o_ref.dtype)

    if not rider:
        run(refs[0], refs[2:5], refs[-1])
        return
    run(refs[0], refs[2:5], refs[-2])
    _on_last_row_block(
        lambda: run(refs[n_in], refs[n_in + 1:n_in + 4], refs[-1]))


def _proj(a, w, ncols, out_dtype, rope=None, rider=None):
    m, k = a.shape
    tm = _pick_tm(m)
    tn = min(PROJ_TN, ncols)
    ni = m // tm
    in_specs = [pl.BlockSpec((tm, k), lambda i, j: (i, 0)), w.spec(k, tn)]
    args = [a, w.src]
    if rope is not None:
        nrep = rope[0].shape[0] // tm
        for t in rope:
            in_specs.append(pl.BlockSpec((tm, LANES), lambda i, j: (i % nrep, 0)))
            args.append(t)
    out_specs = [pl.BlockSpec((tm, tn), lambda i, j: (i, j))]
    out_shape = [jax.ShapeDtypeStruct((m, ncols), out_dtype)]
    if rider is not None:
        a_s, rope_s = rider
        for x in (a_s,) + tuple(rope_s or ()):
            in_specs.append(_rider_in(x))
            args.append(x)
        out_specs.append(_rider_out(a_s.shape[0], tn, ni))
        out_shape.append(jax.ShapeDtypeStruct((a_s.shape[0], ncols), out_dtype))
    out = pl.pallas_call(
        functools.partial(_proj_kernel, rope=rope is not None,
                          rider=rider is not None),
        grid=(ni, ncols // tn),
        in_specs=in_specs,
        out_specs=out_specs,
        out_shape=out_shape,
        compiler_params=_params("arbitrary", "arbitrary"),
        name="proj",
    )(*args)
    return out if rider is not None else out[0]


def _gate_up_kernel(a_ref, wg_ref, wu_ref, o_ref):
    a = a_ref[...]
    g = jnp.dot(a, wg_ref[...].astype(BF16), preferred_element_type=F32)
    u = jnp.dot(a, wu_ref[...].astype(BF16), preferred_element_type=F32)
    o_ref[...] = (g * jax.nn.sigmoid(g) * u).astype(o_ref.dtype)


def _gate_up(a, wg, wu, f):
    m, k = a.shape
    tm = GATE_UP_TM if m % GATE_UP_TM == 0 else m
    tn = DENSE_TN
    return pl.pallas_call(
        _gate_up_kernel,
        grid=(m // tm, f // tn),
        in_specs=[pl.BlockSpec((tm, k), lambda i, j: (i, 0)),
                  wg.spec(k, tn), wu.spec(k, tn)],
        out_specs=pl.BlockSpec((tm, tn), lambda i, j: (i, j)),
        out_shape=jax.ShapeDtypeStruct((m, f), BF16),
        compiler_params=_params("arbitrary", "arbitrary"),
        name="gate_up",
    )(a, wg.src, wu.src)


def _mm_res_kernel(*refs, nterms, rider):
    w_refs = refs[nterms:2 * nterms]

    def run(a_refs, r_ref, o_ref):
        acc = r_ref[...]
        for a_ref, w_ref in zip(a_refs, w_refs):
            acc = acc + jnp.dot(a_ref[...], w_ref[...].astype(BF16),
                                preferred_element_type=F32)
        o_ref[...] = acc

    n_in = 2 * nterms + 1
    if not rider:
        run(refs[:nterms], refs[2 * nterms], refs[-1])
        return
    run(refs[:nterms], refs[2 * nterms], refs[-2])
    _on_last_row_block(
        lambda: run(refs[n_in:n_in + nterms], refs[n_in + nterms], refs[-1]))


def _mm_res(terms, tk, res, tn, rider=None):
    m, n = res.shape
    tm = _pick_tm(m)
    ni = m // tm
    in_specs, args = [], []
    for a, ka, _ in terms:
        in_specs.append(pl.BlockSpec((tm, tk), lambda i, j, ka=ka: (i, ka)))
        args.append(a)
    for _, _, w in terms:
        in_specs.append(w.spec(tk, tn))
        args.append(w.src)
    in_specs.append(pl.BlockSpec((tm, tn), lambda i, j: (i, j)))
    args.append(res)
    out_specs = [pl.BlockSpec((tm, tn), lambda i, j: (i, j))]
    out_shape = [jax.ShapeDtypeStruct((m, n), F32)]
    if rider is not None:
        a_s, res_s = rider
        ms = res_s.shape[0]
        for x, (_, ka, _) in zip(a_s, terms):
            in_specs.append(pl.BlockSpec((ms, tk), lambda i, j, ka=ka: (0, ka),
                                         pipeline_mode=pl.Buffered(1)))
            args.append(x)
        in_specs.append(_rider_out(ms, tn, ni))
        args.append(res_s)
        out_specs.append(_rider_out(ms, tn, ni))
        out_shape.append(jax.ShapeDtypeStruct((ms, n), F32))
    out = pl.pallas_call(
        functools.partial(_mm_res_kernel, nterms=len(terms),
                          rider=rider is not None),
        grid=(ni, n // tn),
        in_specs=in_specs,
        out_specs=out_specs,
        out_shape=out_shape,
        compiler_params=_params("arbitrary", "arbitrary"),
        name="mm_res",
    )(*args)
    return out if rider is not None else out[0]


_DN_T = (((1,), (1,)), ((), ()))


def _head(h, d=HEAD_DIM_AB):
    return slice(h * d, (h + 1) * d)


def _heads_major(x):
    return pltpu.einshape("phd->hpd", x)


def _softmax_stage(ss, sinks=None):
    ms = [jnp.max(s, axis=1, keepdims=True) for s in ss]
    if sinks is not None:
        ms = [jnp.maximum(m, sk) for m, sk in zip(ms, sinks)]
    ps = [jnp.exp(s - m) for s, m in zip(ss, ms)]
    ls = [jnp.sum(p, axis=1, keepdims=True) for p in ps]
    if sinks is not None:
        ls = [l + jnp.exp(sk - m) for l, sk, m in zip(ls, sinks, ms)]
    return [p.astype(BF16) for p in ps], ls


def _sb_logs(z, valid):
    lb = jnp.minimum(z, 0.0) - jnp.log(1.0 + jnp.exp(-jnp.abs(z)))
    l1m = lb - z
    if valid is not None:
        l1m = jnp.where(valid, l1m, 0.0)
    return lb, l1m


def _split_hi_lo(x):
    hi = x.astype(BF16)
    lo = (x - hi.astype(F32)).astype(BF16)
    return jnp.concatenate([hi, lo], axis=1)


def _suffix_matrix(n):
    u = np.tril(np.ones((n, n), np.float32), -1)
    return jnp.asarray(np.concatenate([u, u], axis=0), BF16)


def _sb_prompt_kernel(q_ref, k_ref, v_ref, uu_ref, o_ref, kb_ref, vb_ref,
                      oacc_ref, cacc_ref, *, scale):
    qi = pl.program_id(2)
    tq = q_ref.shape[0]
    nh = oacc_ref.shape[0]

    @pl.when(qi == 0)
    def _():
        kb_ref[...] = k_ref[...].astype(BF16)
        vb_ref[...] = v_ref[...].astype(BF16)

    oacc_ref[...] = jnp.zeros(oacc_ref.shape, F32)
    cacc_ref[...] = jnp.zeros(cacc_ref.shape, F32)

    def tile(j, masked):
        start = pl.multiple_of(j * tq, tq)
        valid = None
        if masked:
            row = lax.broadcasted_iota(jnp.int32, (tq, tq), 0)
            col = lax.broadcasted_iota(jnp.int32, (tq, tq), 1)
            valid = col < row
        zs = [lax.dot_general(q_ref[:, _head(h)],
                              kb_ref[pl.ds(start, tq), _head(h)], _DN_T,
                              preferred_element_type=F32) * scale
              for h in range(nh)]
        logs = [_sb_logs(z, valid) for z in zs]
        sufs = [jnp.dot(_split_hi_lo(l1m), uu_ref[...],
                        preferred_element_type=F32) for _, l1m in logs]
        ws = [jnp.exp(lb + suf + cacc_ref[h])
              for h, ((lb, _), suf) in enumerate(zip(logs, sufs))]
        if masked:
            ws = [jnp.where(valid, w, 0.0) for w in ws]
        for h in range(nh):
            oacc_ref[h] += jnp.dot(ws[h].astype(BF16),
                                   vb_ref[pl.ds(start, tq), _head(h)],
                                   preferred_element_type=F32)
            cacc_ref[h] += jnp.sum(logs[h][1], axis=1, keepdims=True)

    tile(qi, True)

    def body(t, c):
        tile(qi - 1 - t, False)
        return c

    lax.fori_loop(0, qi, body, 0)
    for h in range(nh):
        o_ref[:, _head(h)] = oacc_ref[h].astype(o_ref.dtype)


def _sb_prompt(q, k, v, batch, seq):
    m, w = q.shape
    d = HEAD_DIM_AB
    nh = SB_HEADS
    tq = SB_TILE
    nq = seq // tq
    qo = pl.BlockSpec((tq, nh * d), lambda b, g, i: (b * nq + i, g))
    kv = pl.BlockSpec((seq, nh * d), lambda b, g, i: (b, g))
    return pl.pallas_call(
        functools.partial(_sb_prompt_kernel, scale=d ** -0.5),
        grid=(batch, w // (nh * d), nq),
        in_specs=[qo, kv, kv, pl.BlockSpec((2 * tq, tq), lambda b, g, i: (0, 0))],
        out_specs=qo,
        out_shape=jax.ShapeDtypeStruct((m, w), BF16),
        scratch_shapes=[pltpu.VMEM((seq, nh * d), BF16),
                        pltpu.VMEM((seq, nh * d), BF16),
                        pltpu.VMEM((nh, tq, d), F32),
                        pltpu.VMEM((nh, tq, 1), F32)],
        compiler_params=_params("arbitrary", "arbitrary", "arbitrary"),
        name="sb_prompt",
    )(q, k, v, _suffix_matrix(tq))


def _sb_step_kernel(q_ref, kn_ref, vn_ref, kc_ref, vc_ref, uu_ref, o_ref,
                    kb_ref, vb_ref, *, scale):
    t = q_ref.shape[0]
    past, nh, d = kc_ref.shape
    tk = uu_ref.shape[1]
    n = past + tk
    kb_ref[:, 0:past, :] = _heads_major(kc_ref[...]).astype(BF16)
    vb_ref[:, 0:past, :] = _heads_major(vc_ref[...]).astype(BF16)
    for h in range(nh):
        kb_ref[h, past:past + t] = kn_ref[:, _head(h)].astype(BF16)
        vb_ref[h, past:past + t] = vn_ref[:, _head(h)].astype(BF16)
        kb_ref[h, past + t:n] = jnp.zeros((tk - t, d), BF16)
        vb_ref[h, past + t:n] = jnp.zeros((tk - t, d), BF16)

    z = jnp.concatenate(
        [lax.dot_general(q_ref[:, _head(h)], kb_ref[h], _DN_T,
                         preferred_element_type=F32) for h in range(nh)],
        axis=0) * scale
    row = lax.broadcasted_iota(jnp.int32, z.shape, 0) & (t - 1)
    col = lax.broadcasted_iota(jnp.int32, z.shape, 1)
    valid = col < past + row
    lb, l1m = _sb_logs(z, valid)
    ntiles = n // tk
    sufs = [jnp.dot(_split_hi_lo(l1m[:, j * tk:(j + 1) * tk]), uu_ref[...],
                    preferred_element_type=F32) for j in range(ntiles)]
    tots = [jnp.sum(l1m[:, j * tk:(j + 1) * tk], axis=1, keepdims=True)
            for j in range(ntiles)]
    carries = [None] * ntiles
    carry = jnp.zeros_like(tots[0])
    for j in reversed(range(ntiles)):
        carries[j] = carry
        carry = carry + tots[j]
    after = jnp.concatenate([sufs[j] + carries[j] for j in range(ntiles)],
                            axis=1)
    w = jnp.where(valid, jnp.exp(lb + after), 0.0).astype(BF16)
    for h in range(nh):
        o = jnp.dot(w[h * t:(h + 1) * t], vb_ref[h], preferred_element_type=F32)
        o_ref[:, _head(h)] = o.astype(o_ref.dtype)


def _sb_step(q, k_new, v_new, k_cache, v_cache, batch):
    m, w = q.shape
    _, past, heads, d = k_cache.shape
    t = m // batch
    tk = SB_STEP_KEYS
    nh = STEP_HEADS
    assert past % tk == 0 and t <= tk and t & (t - 1) == 0
    row = pl.BlockSpec((t, nh * d), lambda b, g: (b, g))
    cache = pl.BlockSpec((None, past, nh, d), lambda b, g: (b, 0, g, 0))
    return pl.pallas_call(
        functools.partial(_sb_step_kernel, scale=d ** -0.5),
        grid=(batch, heads // nh),
        in_specs=[row, row, row, cache, cache,
                  pl.BlockSpec((2 * tk, tk), lambda b, g: (0, 0))],
        out_specs=row,
        out_shape=jax.ShapeDtypeStruct((m, w), BF16),
        scratch_shapes=[pltpu.VMEM((nh, past + tk, d), BF16),
                        pltpu.VMEM((nh, past + tk, d), BF16)],
        compiler_params=_params("arbitrary", "arbitrary"),
        name="sb_step",
    )(q, k_new, v_new, k_cache, v_cache, _suffix_matrix(tk))


def _band_bias_kernel(r_ref, o_ref, *, n_valid):
    tq, w = o_ref.shape
    wr = r_ref.shape[1]
    t = jnp.broadcast_to(r_ref[...], (tq, wr))
    t = pltpu.roll(t, 0, 1, stride=1, stride_axis=0)
    row = lax.broadcasted_iota(jnp.int32, (tq, w), 0)
    col = lax.broadcasted_iota(jnp.int32, (tq, w), 1)
    qc = _chunk_of(row)
    kc = _chunk_of(col)
    valid = (kc >= qc) & (kc <= qc + BAND_PREV_CHUNKS) & (col < n_valid)
    o_ref[...] = jnp.where(valid, t[:, :w], NEG)


def _band_bias(rel, tq, n_valid):
    heads = rel.shape[0]
    w = BAND_HIST + BAND_TQ
    wr = w + LANES
    p = np.arange(wr)
    delta = np.where(p < w, p, p - wr)
    idx = np.clip(BAND_HIST - delta, -REL_CLIP, REL_CLIP) + REL_CLIP
    r = jnp.take(rel.astype(F32), jnp.asarray(idx, jnp.int32), axis=1)
    r = r.reshape(heads, 1, wr)
    return pl.pallas_call(
        functools.partial(_band_bias_kernel, n_valid=n_valid),
        grid=(heads,),
        in_specs=[pl.BlockSpec((None, 1, wr), lambda h: (h, 0, 0))],
        out_specs=pl.BlockSpec((None, tq, w), lambda h: (h, 0, 0)),
        out_shape=jax.ShapeDtypeStruct((heads, tq, w), F32),
        compiler_params=_params("arbitrary"),
        name="band_bias",
    )(r)


def _band_prompt_kernel(q_ref, k_ref, v_ref, bias_ref, o_ref, kp_ref, vp_ref,
                        *, scale, tq):
    seq, wd = k_ref.shape
    d = HEAD_DIM_AB
    nh = wd // d
    w = bias_ref.shape[1]
    hist = w - tq
    kp_ref[0:hist] = jnp.zeros((hist, wd), BF16)
    vp_ref[0:hist] = jnp.zeros((hist, wd), BF16)
    kp_ref[hist:hist + seq] = k_ref[...].astype(BF16)
    vp_ref[hist:hist + seq] = v_ref[...].astype(BF16)
    col = lax.broadcasted_iota(jnp.int32, (tq, w), 1)

    def body(i, c):
        start = pl.multiple_of(i * tq, tq)
        in_seq = col >= hist - start
        ss = []
        for h in range(nh):
            s = lax.dot_general(q_ref[pl.ds(start, tq), _head(h)],
                                kp_ref[pl.ds(start, w), _head(h)], _DN_T,
                                preferred_element_type=F32)
            s = s * scale + bias_ref[h * tq:(h + 1) * tq, :]
            ss.append(jnp.where(in_seq, s, NEG))
        ps, ls = _softmax_stage(ss)
        for h in range(nh):
            o = jnp.dot(ps[h], vp_ref[pl.ds(start, w), _head(h)],
                        preferred_element_type=F32) / ls[h]
            o_ref[pl.ds(start, tq), _head(h)] = o.astype(o_ref.dtype)
        return c

    lax.fori_loop(0, seq // tq, body, 0)


def _band_prompt(q, k, v, bias, batch, seq):
    m, wd = q.shape
    d = HEAD_DIM_AB
    nh = BAND_HEADS
    heads, tq, w = bias.shape
    bias = bias.reshape(heads // nh, nh * tq, w)
    blk = pl.BlockSpec((seq, nh * d), lambda b, g: (b, g))
    return pl.pallas_call(
        functools.partial(_band_prompt_kernel, scale=d ** -0.5, tq=tq),
        grid=(batch, heads // nh),
        in_specs=[blk, blk, blk,
                  pl.BlockSpec((None, nh * tq, w), lambda b, g: (g, 0, 0))],
        out_specs=blk,
        out_shape=jax.ShapeDtypeStruct((m, wd), BF16),
        scratch_shapes=[pltpu.VMEM((BAND_HIST + seq, nh * d), BF16),
                        pltpu.VMEM((BAND_HIST + seq, nh * d), BF16)],
        compiler_params=_params("arbitrary", "arbitrary"),
        name="band_prompt",
    )(q, k, v, bias)


def _roll_cache(new_ref, cache, new_rows_ref, t):
    past, nh, d = cache.shape
    new_ref[0:past - t] = cache[t:]
    for h in range(nh):
        new_ref[past - t:past, h, :] = new_rows_ref[:, _head(h, d)]


def _band_step_kernel(q_ref, kn_ref, vn_ref, kc_ref, vc_ref, bias_ref, o_ref,
                      nk_ref, nv_ref, kb_ref, vb_ref, *, scale):
    t = q_ref.shape[0]
    past, nh, d = kc_ref.shape
    w = bias_ref.shape[1]
    kc, vc = kc_ref[...], vc_ref[...]
    _roll_cache(nk_ref, kc, kn_ref, t)
    _roll_cache(nv_ref, vc, vn_ref, t)
    kb_ref[:, 0:past, :] = _heads_major(kc).astype(BF16)
    vb_ref[:, 0:past, :] = _heads_major(vc).astype(BF16)
    for h in range(nh):
        kb_ref[h, past:past + t] = kn_ref[:, _head(h)].astype(BF16)
        vb_ref[h, past:past + t] = vn_ref[:, _head(h)].astype(BF16)
        kb_ref[h, past + t:w] = jnp.zeros((w - past - t, d), BF16)
        vb_ref[h, past + t:w] = jnp.zeros((w - past - t, d), BF16)
    ss = [lax.dot_general(q_ref[:, _head(h)], kb_ref[h], _DN_T,
                          preferred_element_type=F32) * scale
          + bias_ref[h * t:(h + 1) * t, :] for h in range(nh)]
    ps, ls = _softmax_stage(ss)
    for h in range(nh):
        o = jnp.dot(ps[h], vb_ref[h], preferred_element_type=F32) / ls[h]
        o_ref[:, _head(h)] = o.astype(o_ref.dtype)


def _band_step(q, k_new, v_new, k_cache, v_cache, bias, batch):
    m, wd = q.shape
    _, past, heads, d = k_cache.shape
    t = m // batch
    nh = STEP_HEADS
    w = bias.shape[2]
    assert past == BAND_HIST and past + t <= w
    bias = bias.reshape(heads // nh, nh * t, w)
    row = pl.BlockSpec((t, nh * d), lambda b, g: (b, g))
    cache = pl.BlockSpec((None, past, nh, d), lambda b, g: (b, 0, g, 0))
    return pl.pallas_call(
        functools.partial(_band_step_kernel, scale=d ** -0.5),
        grid=(batch, heads // nh),
        in_specs=[row, row, row, cache, cache,
                  pl.BlockSpec((None, nh * t, w), lambda b, g: (g, 0, 0))],
        out_specs=[row, cache, cache],
        out_shape=[jax.ShapeDtypeStruct((m, wd), BF16),
                   jax.ShapeDtypeStruct(k_cache.shape, F32),
                   jax.ShapeDtypeStruct(v_cache.shape, F32)],
        scratch_shapes=[pltpu.VMEM((nh, w, d), BF16),
                        pltpu.VMEM((nh, w, d), BF16)],
        compiler_params=_params("arbitrary", "arbitrary"),
        name="band_step",
    )(q, k_new, v_new, k_cache, v_cache, bias)


def _split_pair(blk):
    rolled = pltpu.roll(blk, HEAD_DIM_C, 1)
    lane = lax.broadcasted_iota(jnp.int32, blk.shape, 1)
    low = lane < HEAD_DIM_C
    zero = jnp.zeros_like(blk)
    return [jnp.where(low, blk, zero), jnp.where(low, zero, rolled),
            jnp.where(low, rolled, zero), jnp.where(low, zero, blk)]


def _swa_stage(qs, kls, khs, vls, vhs, valid, sinks, scale):
    ss = []
    for q, kl, kh in zip(qs, kls, khs):
        for kx in (kl, kh):
            s = lax.dot_general(q, kx, _DN_T, preferred_element_type=F32)
            ss.append(jnp.where(valid, s * scale, NEG))
    ps, ls = _softmax_stage(ss, sinks)
    return [jnp.dot(ps[2 * i], vls[i], preferred_element_type=F32) / ls[2 * i]
            + jnp.dot(ps[2 * i + 1], vhs[i], preferred_element_type=F32)
            / ls[2 * i + 1] for i in range(len(qs))]


def _swa_prompt_kernel(sinks_ref, q_ref, k_ref, v_ref, o_ref, ks_ref, vs_ref,
                       *, scale, tq):
    hp = pl.program_id(1)
    seq = k_ref.shape[0]
    band = WINDOW + tq
    npair = q_ref.shape[1] // LANES
    group = npair // 2

    for src, dst in ((k_ref, ks_ref), (v_ref, vs_ref)):
        for n, x in enumerate(_split_pair(src[...])):
            dst[n, 0:WINDOW] = jnp.zeros((WINDOW, LANES), BF16)
            dst[n, WINDOW:WINDOW + seq] = x.astype(BF16)

    row = lax.broadcasted_iota(jnp.int32, (tq, band), 0)
    col = lax.broadcasted_iota(jnp.int32, (tq, band), 1)
    qc = _chunk_of(row)
    kc = _chunk_of(col)
    in_band = (kc >= qc) & (kc <= qc + SWA_PREV_CHUNKS)
    sinks = [sinks_ref[hp * 2 * npair + n] for n in range(2 * npair)]

    def body(i, c):
        start = pl.multiple_of(i * tq, tq)
        valid = in_band & (col >= WINDOW - start)
        kv = [[r[n, pl.ds(start, band), :] for n in range(4)]
              for r in (ks_ref, vs_ref)]
        qs = [q_ref[pl.ds(start, tq), _head(p, LANES)] for p in range(npair)]
        sel = [2 * (p // group) for p in range(npair)]
        outs = _swa_stage(qs, [kv[0][s] for s in sel], [kv[0][s + 1] for s in sel],
                          [kv[1][s] for s in sel], [kv[1][s + 1] for s in sel],
                          valid, sinks, scale)
        for p in range(npair):
            o_ref[pl.ds(start, tq), _head(p, LANES)] = outs[p].astype(o_ref.dtype)
        return c

    lax.fori_loop(0, seq // tq, body, 0)


def _swa_prompt(q, k, v, sinks, batch, seq):
    m, wq = q.shape
    wk = k.shape[1]
    npairs = wk // LANES
    qw = wq // npairs
    qo = pl.BlockSpec((seq, qw), lambda b, h, s: (b, h))
    kv = pl.BlockSpec((seq, LANES), lambda b, h, s: (b, h))
    grid_spec = pltpu.PrefetchScalarGridSpec(
        num_scalar_prefetch=1,
        grid=(batch, npairs),
        in_specs=[qo, kv, kv],
        out_specs=qo,
        scratch_shapes=[pltpu.VMEM((4, WINDOW + seq, LANES), BF16),
                        pltpu.VMEM((4, WINDOW + seq, LANES), BF16)],
    )
    return pl.pallas_call(
        functools.partial(_swa_prompt_kernel, scale=HEAD_DIM_C ** -0.5,
                          tq=SWA_TQ),
        grid_spec=grid_spec,
        out_shape=jax.ShapeDtypeStruct((m, wq), BF16),
        compiler_params=_params("arbitrary", "arbitrary"),
        name="swa_prompt",
    )(sinks, q, k, v)


def _swa_step_kernel(sinks_ref, q_ref, kn_ref, vn_ref, kc_ref, vc_ref, o_ref,
                     nk_ref, nv_ref, *, scale):
    t = q_ref.shape[0]
    past, wk = kc_ref.shape
    band = 2 * LANES
    nkv = wk // LANES
    npair = q_ref.shape[1] // LANES
    group = npair // (2 * nkv)
    kc, vc = kc_ref[...], vc_ref[...]
    kn, vn = kn_ref[...], vn_ref[...]
    nk_ref[0:past - t] = kc[t:]
    nv_ref[0:past - t] = vc[t:]
    nk_ref[past - t:past] = kn
    nv_ref[past - t:past] = vn
    pad = jnp.zeros((band - past - t, wk), F32)
    k_all = jnp.concatenate([kc, kn, pad], axis=0)
    v_all = jnp.concatenate([vc, vn, pad], axis=0)
    col = lax.broadcasted_iota(jnp.int32, (t, band), 1)
    valid = col < past + t
    kx = [[x.astype(BF16) for x in _split_pair(k_all[:, _head(g, LANES)])]
          for g in range(nkv)]
    vx = [[x.astype(BF16) for x in _split_pair(v_all[:, _head(g, LANES)])]
          for g in range(nkv)]
    qs = [q_ref[:, _head(p, LANES)] for p in range(npair)]
    where = [(p // (2 * group), 2 * ((p // group) % 2)) for p in range(npair)]
    sinks = [sinks_ref[n] for n in range(2 * npair)]
    outs = _swa_stage(qs, [kx[g][s] for g, s in where],
                      [kx[g][s + 1] for g, s in where],
                      [vx[g][s] for g, s in where],
                      [vx[g][s + 1] for g, s in where], valid, sinks, scale)
    for p in range(npair):
        o_ref[:, _head(p, LANES)] = outs[p].astype(o_ref.dtype)


def _swa_step(q, k_new, v_new, k_cache, v_cache, sinks, batch):
    m, wq = q.shape
    wk = k_new.shape[1]
    t = m // batch
    past = k_cache.shape[1]
    assert past + t <= 2 * LANES
    qrow = pl.BlockSpec((t, wq), lambda b, s: (b, 0))
    krow = pl.BlockSpec((t, wk), lambda b, s: (b, 0))
    cache = pl.BlockSpec((None, past, wk), lambda b, s: (b, 0, 0))
    grid_spec = pltpu.PrefetchScalarGridSpec(
        num_scalar_prefetch=1,
        grid=(batch,),
        in_specs=[qrow, krow, krow, cache, cache],
        out_specs=[qrow, cache, cache],
    )
    return pl.pallas_call(
        functools.partial(_swa_step_kernel, scale=HEAD_DIM_C ** -0.5),
        grid_spec=grid_spec,
        out_shape=[jax.ShapeDtypeStruct((m, wq), BF16),
                   jax.ShapeDtypeStruct(k_cache.shape, F32),
                   jax.ShapeDtypeStruct(v_cache.shape, F32)],
        compiler_params=_params("arbitrary"),
        name="swa_step",
    )(sinks, q, k_new, v_new, k_cache, v_cache)


def _rope_tables(pos):
    half = ROPE_DIM // 2
    inv = jnp.power(ROPE_THETA, -jnp.arange(half, dtype=F32) / half)
    ang = pos.astype(F32)[:, None] * inv[None, :]
    cos, sin = jnp.cos(ang), jnp.sin(ang)
    n = pos.shape[0]
    reps = LANES // HEAD_DIM_C
    rest = HEAD_DIM_C - ROPE_DIM
    ones = jnp.ones((n, rest), F32)
    zeros = jnp.zeros((n, rest), F32)
    zh = jnp.zeros((n, half), F32)
    c = jnp.tile(jnp.concatenate([cos, cos, ones], axis=1), (1, reps))
    s1 = jnp.tile(jnp.concatenate([zh, sin, zeros], axis=1), (1, reps))
    s2 = jnp.tile(jnp.concatenate([-sin, zh, zeros], axis=1), (1, reps))
    return c, s1, s2


def _ffn(hp, hs, norm_ffn3, w_gate, w_up, w_down, layer):
    f = w_gate.shape[-1]
    wg, wu = _Weight(w_gate, layer), _Weight(w_up, layer)
    act_p = _gate_up(_rmsnorm(hp, norm_ffn3, layer, BF16), wg, wu, f)
    act_s = _gate_up(_rmsnorm(hs, norm_ffn3, layer, BF16), wg, wu, f)
    for kb in range(2):
        hp, hs = _mm_res([(act_p, kb, _Weight(w_down, layer, kb))], f // 2, hp,
                         DENSE_TN, rider=([act_s], hs))
    return hp, hs


def kernel(x_prompt, x_sample, cache_sb_k, cache_sb_v, cache_band_k, cache_band_v, cache_swa_k, cache_swa_v, norm_mix, norm_ffn, norm_final, w_in_ab, w_out_ab, rel_bias, w_qkv_c, w_out_c, sinks, w_gate, w_up, w_down):
    bp, seq, dm = x_prompt.shape
    bs, t, _ = x_sample.shape
    depth = norm_mix.shape[0]
    past = cache_sb_k.shape[2]
    hp = x_prompt.reshape(bp * seq, dm)
    hs = x_sample.reshape(bs * t, dm)
    norm_mix3 = norm_mix.reshape(depth, 1, dm)
    norm_ffn3 = norm_ffn.reshape(depth, 1, dm)
    norm_final3 = norm_final.reshape(1, 1, dm)

    w_sb = cache_sb_k.shape[3] * cache_sb_k.shape[4]
    w_band = cache_band_k.shape[3] * cache_band_k.shape[4]
    wq_c = w_out_c.shape[1]
    wk_c = cache_swa_k.shape[3] * cache_swa_k.shape[4]
    p_band = cache_band_k.shape[2]
    p_swa = cache_swa_k.shape[2]

    rope_p = _rope_tables(jnp.arange(seq))
    rope_s = tuple(jnp.tile(x, (bs, 1)) for x in _rope_tables(past + jnp.arange(t)))

    outs = {name: [] for name in (
        "sbk_p", "sbv_p", "bk_p", "bv_p", "ck_p", "cv_p",
        "sbk_s", "sbv_s", "bk_s", "bv_s", "ck_s", "cv_s")}

    for l in range(depth):
        hn_p = _rmsnorm(hp, norm_mix3, l, BF16)
        hn_s = _rmsnorm(hs, norm_mix3, l, BF16)
        if l % 2 == 0:
            e = l // 2
            cols = (0, w_sb, 2 * w_sb, 3 * w_sb, 3 * w_sb + w_band, 3 * w_sb + 2 * w_band)
            bias_p = _band_bias(rel_bias[e], BAND_TQ, BAND_HIST + BAND_TQ)
            bias_s = _band_bias(rel_bias[e], t, p_band + t)

            widths = (w_sb, w_sb, w_sb, w_band, w_band, w_band)
            dtypes = (BF16, F32, F32, BF16, F32, F32)

            w_in = [_Weight(w_in_ab, e, 0, c) for c in cols]
            w_out = (_Weight(w_out_ab, e, 0), _Weight(w_out_ab, e, 1))

            (qa, qa_s), (ka, ka_s), (va, va_s), (qb, qb_s), (kb, kb_s), (vb, vb_s) = (
                _proj(hn_p, w, n, dt, rider=(hn_s, None))
                for w, n, dt in zip(w_in, widths, dtypes))
            oa = _sb_prompt(qa, ka, va, bp, seq)
            ob = _band_prompt(qb, kb, vb, bias_p, bp, seq)
            oa_s = _sb_step(qa_s, ka_s, va_s, cache_sb_k[e], cache_sb_v[e], bs)
            ob_s, nbk, nbv = _band_step(qb_s, kb_s, vb_s, cache_band_k[e],
                                        cache_band_v[e], bias_s, bs)
            hp, hs = _mm_res([(oa, 0, w_out[0]), (ob, 0, w_out[1])], w_sb, hp,
                             PROJ_TN, rider=([oa_s, ob_s], hs))
            nb = min(BAND_HIST, seq)
            outs["sbk_p"].append(ka.reshape(bp, seq, *cache_sb_k.shape[3:]))
            outs["sbv_p"].append(va.reshape(bp, seq, *cache_sb_k.shape[3:]))
            outs["bk_p"].append(kb.reshape(bp, seq, *cache_band_k.shape[3:])[:, seq - nb:])
            outs["bv_p"].append(vb.reshape(bp, seq, *cache_band_k.shape[3:])[:, seq - nb:])
            outs["sbk_s"].append(ka_s.reshape(bs, t, *cache_sb_k.shape[3:]))
            outs["sbv_s"].append(va_s.reshape(bs, t, *cache_sb_k.shape[3:]))
            outs["bk_s"].append(nbk)
            outs["bv_s"].append(nbv)
        else:
            o = l // 2

            widths = (wq_c, wk_c, wk_c)
            dtypes = (BF16, F32, F32)

            w_in = [_Weight(w_qkv_c, o, 0, c) for c in (0, wq_c, wq_c + wk_c)]
            w_out = _Weight(w_out_c, o, 0)

            (q, q_s), (k, k_s), (v, v_s) = (
                _proj(hn_p, w, n, dt, rope=rp, rider=(hn_s, rs))
                for w, n, dt, rp, rs in zip(w_in, widths, dtypes,
                                            (rope_p, rope_p, None),
                                            (rope_s, rope_s, None)))
            oc = _swa_prompt(q, k, v, sinks[o], bp, seq)
            oc_s, nck, ncv = _swa_step(
                q_s, k_s, v_s, cache_swa_k[o].reshape(bs, p_swa, wk_c),
                cache_swa_v[o].reshape(bs, p_swa, wk_c), sinks[o], bs)
            hp, hs = _mm_res([(oc, 0, w_out)], wq_c, hp, PROJ_TN,
                             rider=([oc_s], hs))
            nw = min(WINDOW, seq)
            outs["ck_p"].append(k.reshape(bp, seq, *cache_swa_k.shape[3:])[:, seq - nw:])
            outs["cv_p"].append(v.reshape(bp, seq, *cache_swa_k.shape[3:])[:, seq - nw:])
            outs["ck_s"].append(nck.reshape(cache_swa_k.shape[1:]))
            outs["cv_s"].append(ncv.reshape(cache_swa_v.shape[1:]))
        hp, hs = _ffn(hp, hs, norm_ffn3, w_gate, w_up, w_down, l)

    y_prompt = _rmsnorm(hp, norm_final3, 0, F32).reshape(bp, seq, dm)
    y_sample = _rmsnorm(hs, norm_final3, 0, F32).reshape(bs, t, dm)
    st = lambda name: jnp.stack(outs[name])
    return (y_prompt, y_sample,
            st("sbk_p"), st("sbv_p"), st("bk_p"), st("bv_p"), st("ck_p"), st("cv_p"),
            st("sbk_s"), st("sbv_s"), st("bk_s"), st("bv_s"), st("ck_s"), st("cv_s"))
```

```python
import functools
import math

import numpy as np
import jax
import jax.numpy as jnp
from jax import lax
from jax.experimental import pallas as pl
from jax.experimental.pallas import tpu as pltpu

F32 = jnp.float32
BF16 = jnp.bfloat16

CHUNK = 64
EPS = 1e-6
HEAD_DIM_AB = 128
BAND_PREV_CHUNKS = 8
BAND_HIST = BAND_PREV_CHUNKS * CHUNK
REL_CLIP = 2 * CHUNK
HEAD_DIM_C = 64
N_KV_C = 8
SWA_PREV_CHUNKS = 2
WINDOW = SWA_PREV_CHUNKS * CHUNK
ROPE_THETA = 500000.0
ROPE_DIM = HEAD_DIM_C // 4

LANES = 128
V7X_VMEM_LIMIT_BYTES = 60 * 1024 * 1024

DENSE_TM = 1024
GATE_UP_TM = 2048
DENSE_TN = 256
PROJ_TN = 512

NEG = -1e30

SB_TILE = 256
SB_HEADS = 4
SB_STEP_KEYS = 128
BAND_TQ = 128
BAND_HEADS = 4
STEP_HEADS = 8
SWA_TQ = 128


def _params(*sem):
    return pltpu.CompilerParams(dimension_semantics=sem,
                                vmem_limit_bytes=V7X_VMEM_LIMIT_BYTES)


def _chunk_of(pos):
    return lax.shift_right_logical(pos, int(math.log2(CHUNK)))


def _pick_tm(m):
    return DENSE_TM if m % DENSE_TM == 0 else m


def _rmsnorm_kernel(x_ref, g_ref, o_ref):
    x = x_ref[...]
    ms = jnp.mean(x * x, axis=-1, keepdims=True)
    y = x * lax.rsqrt(ms + EPS)
    o_ref[...] = (y * g_ref[...]).astype(o_ref.dtype)


def _rmsnorm(x, g3, layer, out_dtype):
    m, d = x.shape
    tm = 256
    return pl.pallas_call(
        _rmsnorm_kernel,
        grid=(m // tm,),
        in_specs=[pl.BlockSpec((tm, d), lambda i: (i, 0)),
                  pl.BlockSpec((None, 1, d), lambda i: (layer, 0, 0))],
        out_specs=pl.BlockSpec((tm, d), lambda i: (i, 0)),
        out_shape=jax.ShapeDtypeStruct((m, d), out_dtype),
        compiler_params=_params("arbitrary"),
        name="rmsnorm",
    )(x, g3)


class _Weight:
    def __init__(self, src, layer, kblk=0, col0=0):
        self.src, self.layer, self.kblk, self.col0 = src, layer, kblk, col0

    def spec(self, tk, tn):
        layer, kb, jb = self.layer, self.kblk, self.col0 // tn
        return pl.BlockSpec((None, tk, tn), lambda i, j: (layer, kb, jb + j))


def _rope_tile(x, c, s1, s2):
    return (x * c + pltpu.roll(x, 8, 1) * s1
            + pltpu.roll(x, LANES - 8, 1) * s2)


def _rider_in(x):
    return pl.BlockSpec(x.shape, lambda i, j: (0, 0), pipeline_mode=pl.Buffered(1))


def _rider_out(rows, tn, ni):
    return pl.BlockSpec((rows, tn),
                        lambda i, j: (0, jnp.where(i == ni - 1, j, 0)))


def _on_last_row_block(fn):
    pl.when(pl.program_id(0) == pl.num_programs(0) - 1)(fn)


def _lane_chunks(n):
    return [slice(c * LANES, (c + 1) * LANES) for c in range(n // LANES)]


def _weight_tile(w_ref, g_ref):
    if g_ref is None:
        return w_ref[...].astype(BF16)
    g = g_ref[...]
    return jnp.concatenate(
        [(w_ref[:, sl] * g).astype(BF16) for sl in _lane_chunks(w_ref.shape[1])],
        axis=1)


def _proj_kernel(*refs, rope, norm):
    w_ref = refs[1]
    n_tab = 3 if rope else 0
    a_ref, tabs = refs[0], refs[2:2 + n_tab]
    sa_ref, stabs = refs[2 + n_tab], refs[3 + n_tab:3 + 2 * n_tab]
    g_ref, r_ref, sr_ref = refs[3 + 2 * n_tab:6 + 2 * n_tab] if norm else (None,) * 3
    o_ref, so_ref = refs[-2:]

    def run(a_ref, tabs, r_ref, o_ref):
        acc = jnp.dot(a_ref[...], _weight_tile(w_ref, g_ref),
                      preferred_element_type=F32)
        for sl in _lane_chunks(acc.shape[1]):
            x = acc[:, sl]
            if norm:
                x = x * r_ref[...]
            if rope:
                x = _rope_tile(x, *(t[...] for t in tabs))
            o_ref[:, sl] = x.astype(o_ref.dtype)

    run(a_ref, tabs, r_ref, o_ref)
    _on_last_row_block(lambda: run(sa_ref, stabs, sr_ref, so_ref))


def _proj(a, a_s, w, ncols, out_dtype, rope=None, rope_s=None, norm=None):
    m, k = a.shape
    ms = a_s.shape[0]
    tm = _pick_tm(m)
    tn = min(PROJ_TN, ncols)
    ni = m // tm
    in_specs = [pl.BlockSpec((tm, k), lambda i, j: (i, 0)), w.spec(k, tn)]
    args = [a, w.src]
    if rope is not None:
        nrep = rope[0].shape[0] // tm
        for t in rope:
            in_specs.append(pl.BlockSpec((tm, LANES), lambda i, j: (i % nrep, 0)))
            args.append(t)
    for x in (a_s,) + tuple(rope_s or ()):
        in_specs.append(_rider_in(x))
        args.append(x)
    if norm is not None:
        g, r, r_s = norm
        in_specs += [_rider_in(g), pl.BlockSpec((tm, LANES), lambda i, j: (i, 0)),
                     _rider_in(r_s)]
        args += [g, r, r_s]
    return pl.pallas_call(
        functools.partial(_proj_kernel, rope=rope is not None,
                          norm=norm is not None),
        grid=(ni, ncols // tn),
        in_specs=in_specs,
        out_specs=[pl.BlockSpec((tm, tn), lambda i, j: (i, j)),
                   _rider_out(ms, tn, ni)],
        out_shape=[jax.ShapeDtypeStruct((m, ncols), out_dtype),
                   jax.ShapeDtypeStruct((ms, ncols), out_dtype)],
        compiler_params=_params("arbitrary", "arbitrary"),
        name="proj",
    )(*args)


def _gate_up_kernel(*refs, norm):
    a_ref, wg_ref, wu_ref = refs[:3]
    g_ref, r_ref = refs[3:5] if norm else (None, None)
    o_ref = refs[-1]
    a = a_ref[...]
    g = jnp.dot(a, _weight_tile(wg_ref, g_ref), preferred_element_type=F32)
    u = jnp.dot(a, _weight_tile(wu_ref, g_ref), preferred_element_type=F32)
    for sl in _lane_chunks(g.shape[1]):
        gc, uc = g[:, sl], u[:, sl]
        if norm:
            gc, uc = gc * r_ref[...], uc * r_ref[...]
        o_ref[:, sl] = (gc * jax.nn.sigmoid(gc) * uc).astype(o_ref.dtype)


def _gate_up(a, wg, wu, f, norm=None):
    m, k = a.shape
    tm = GATE_UP_TM if m % GATE_UP_TM == 0 else m
    tn = DENSE_TN
    in_specs = [pl.BlockSpec((tm, k), lambda i, j: (i, 0)),
                wg.spec(k, tn), wu.spec(k, tn)]
    args = [a, wg.src, wu.src]
    if norm is not None:
        in_specs += [_rider_in(norm[0]),
                     pl.BlockSpec((tm, LANES), lambda i, j: (i, 0))]
        args += list(norm)
    return pl.pallas_call(
        functools.partial(_gate_up_kernel, norm=norm is not None),
        grid=(m // tm, f // tn),
        in_specs=in_specs,
        out_specs=pl.BlockSpec((tm, tn), lambda i, j: (i, j)),
        out_shape=jax.ShapeDtypeStruct((m, f), BF16),
        compiler_params=_params("arbitrary", "arbitrary"),
        name="gate_up",
    )(*args)


def _mm_res_kernel(*refs, nterms, stats):
    n = nterms
    w_refs = refs[n:2 * n]
    outs = refs[3 * n + 2:]

    def run(a_refs, res_ref, o_ref, b_ref, ss_ref):
        acc = res_ref[...]
        for a_ref, w_ref in zip(a_refs, w_refs):
            acc = acc + jnp.dot(a_ref[...], w_ref[...].astype(BF16),
                                preferred_element_type=F32)
        o_ref[...] = acc
        if not stats:
            return
        b_ref[...] = acc.astype(BF16)
        part = sum(acc[:, sl] * acc[:, sl] for sl in _lane_chunks(acc.shape[1]))
        j = pl.program_id(1)

        @pl.when(j == 0)
        def _():
            ss_ref[...] = part

        @pl.when(j != 0)
        def _():
            ss_ref[...] += part

    p_extra = (outs[2], outs[3]) if stats else (None, None)
    s_extra = (outs[4], outs[5]) if stats else (None, None)
    run(refs[:n], refs[2 * n], outs[0], *p_extra)
    _on_last_row_block(
        lambda: run(refs[2 * n + 1:3 * n + 1], refs[3 * n + 1], outs[1], *s_extra))


def _mm_res(terms, tk, res, res_s, tn, stats=False):
    m, n = res.shape
    ms = res_s.shape[0]
    tm = _pick_tm(m)
    ni = m // tm
    in_specs, args = [], []
    for a, _, ka, _ in terms:
        in_specs.append(pl.BlockSpec((tm, tk), lambda i, j, ka=ka: (i, ka)))
        args.append(a)
    for _, _, _, w in terms:
        in_specs.append(w.spec(tk, tn))
        args.append(w.src)
    tile = pl.BlockSpec((tm, tn), lambda i, j: (i, j))
    tile_s = _rider_out(ms, tn, ni)
    in_specs.append(tile)
    args.append(res)
    for _, a_s, ka, _ in terms:
        in_specs.append(pl.BlockSpec((ms, tk), lambda i, j, ka=ka: (0, ka),
                                     pipeline_mode=pl.Buffered(1)))
        args.append(a_s)
    in_specs.append(tile_s)
    args.append(res_s)
    out_specs = [tile, tile_s]
    out_shape = [jax.ShapeDtypeStruct((m, n), F32),
                 jax.ShapeDtypeStruct((ms, n), F32)]
    if stats:
        out_specs += [tile, pl.BlockSpec((tm, LANES), lambda i, j: (i, 0)),
                      tile_s, pl.BlockSpec((ms, LANES), lambda i, j: (0, 0))]
        out_shape += [jax.ShapeDtypeStruct((m, n), BF16),
                      jax.ShapeDtypeStruct((m, LANES), F32),
                      jax.ShapeDtypeStruct((ms, n), BF16),
                      jax.ShapeDtypeStruct((ms, LANES), F32)]
    return pl.pallas_call(
        functools.partial(_mm_res_kernel, nterms=len(terms), stats=stats),
        grid=(ni, n // tn),
        in_specs=in_specs,
        out_specs=out_specs,
        out_shape=out_shape,
        compiler_params=_params("arbitrary", "arbitrary"),
        name="mm_res",
    )(*args)


def _row_scale_kernel(ss_ref, o_ref, *, d):
    tot = jnp.sum(ss_ref[...], axis=1, keepdims=True)
    o_ref[...] = jnp.broadcast_to(lax.rsqrt(tot / d + EPS), o_ref.shape)


def _row_scale(ss, d):
    m = ss.shape[0]
    tm = _pick_tm(m)
    blk = pl.BlockSpec((tm, LANES), lambda i: (i, 0))
    return pl.pallas_call(
        functools.partial(_row_scale_kernel, d=d),
        grid=(m // tm,),
        in_specs=[blk],
        out_specs=blk,
        out_shape=jax.ShapeDtypeStruct(ss.shape, F32),
        compiler_params=_params("arbitrary"),
        name="row_scale",
    )(ss)


_DN_T = (((1,), (1,)), ((), ()))


def _head(h, d=HEAD_DIM_AB):
    return slice(h * d, (h + 1) * d)


def _heads_major(x):
    return pltpu.einshape("phd->hpd", x)


def _softmax_stage(ss, sinks=None):
    ms = [jnp.max(s, axis=1, keepdims=True) for s in ss]
    if sinks is not None:
        ms = [jnp.maximum(m, sk) for m, sk in zip(ms, sinks)]
    ps = [jnp.exp(s - m) for s, m in zip(ss, ms)]
    ls = [jnp.sum(p, axis=1, keepdims=True) for p in ps]
    if sinks is not None:
        ls = [l + jnp.exp(sk - m) for l, sk, m in zip(ls, sinks, ms)]
    return [p.astype(BF16) for p in ps], ls


def _sb_logs(z, valid):
    lb = jnp.minimum(z, 0.0) - jnp.log(1.0 + jnp.exp(-jnp.abs(z)))
    l1m = lb - z
    if valid is not None:
        l1m = jnp.where(valid, l1m, 0.0)
    return lb, l1m


def _split_hi_lo(x):
    hi = x.astype(BF16)
    lo = (x - hi.astype(F32)).astype(BF16)
    return jnp.concatenate([hi, lo], axis=1)


def _suffix_matrix(n):
    u = np.tril(np.ones((n, n), np.float32), -1)
    return jnp.asarray(np.concatenate([u, u], axis=0), BF16)


def _sb_prompt_kernel(q_ref, k_ref, v_ref, uu_ref, o_ref, kb_ref, vb_ref,
                      oacc_ref, cacc_ref, *, scale):
    qi = pl.program_id(2)
    tq = q_ref.shape[0]
    nh = oacc_ref.shape[0]

    @pl.when(qi == 0)
    def _():
        kb_ref[...] = k_ref[...].astype(BF16)
        vb_ref[...] = v_ref[...].astype(BF16)

    oacc_ref[...] = jnp.zeros(oacc_ref.shape, F32)
    cacc_ref[...] = jnp.zeros(cacc_ref.shape, F32)

    def tile(j, masked):
        start = pl.multiple_of(j * tq, tq)
        valid = None
        if masked:
            row = lax.broadcasted_iota(jnp.int32, (tq, tq), 0)
            col = lax.broadcasted_iota(jnp.int32, (tq, tq), 1)
            valid = col < row
        zs = [lax.dot_general(q_ref[:, _head(h)],
                              kb_ref[pl.ds(start, tq), _head(h)], _DN_T,
                              preferred_element_type=F32) * scale
              for h in range(nh)]
        logs = [_sb_logs(z, valid) for z in zs]
        sufs = [jnp.dot(_split_hi_lo(l1m), uu_ref[...],
                        preferred_element_type=F32) for _, l1m in logs]
        ws = [jnp.exp(lb + suf + cacc_ref[h])
              for h, ((lb, _), suf) in enumerate(zip(logs, sufs))]
        if masked:
            ws = [jnp.where(valid, w, 0.0) for w in ws]
        for h in range(nh):
            oacc_ref[h] += jnp.dot(ws[h].astype(BF16),
                                   vb_ref[pl.ds(start, tq), _head(h)],
                                   preferred_element_type=F32)
            cacc_ref[h] += jnp.sum(logs[h][1], axis=1, keepdims=True)

    tile(qi, True)

    def body(t, c):
        tile(qi - 1 - t, False)
        return c

    lax.fori_loop(0, qi, body, 0)
    for h in range(nh):
        o_ref[:, _head(h)] = oacc_ref[h].astype(o_ref.dtype)


def _sb_prompt(q, k, v, batch, seq):
    m, w = q.shape
    d = HEAD_DIM_AB
    nh = SB_HEADS
    tq = SB_TILE
    nq = seq // tq
    qo = pl.BlockSpec((tq, nh * d), lambda b, g, i: (b * nq + i, g))
    kv = pl.BlockSpec((seq, nh * d), lambda b, g, i: (b, g))
    return pl.pallas_call(
        functools.partial(_sb_prompt_kernel, scale=d ** -0.5),
        grid=(batch, w // (nh * d), nq),
        in_specs=[qo, kv, kv, pl.BlockSpec((2 * tq, tq), lambda b, g, i: (0, 0))],
        out_specs=qo,
        out_shape=jax.ShapeDtypeStruct((m, w), BF16),
        scratch_shapes=[pltpu.VMEM((seq, nh * d), BF16),
                        pltpu.VMEM((seq, nh * d), BF16),
                        pltpu.VMEM((nh, tq, d), F32),
                        pltpu.VMEM((nh, tq, 1), F32)],
        compiler_params=_params("arbitrary", "arbitrary", "arbitrary"),
        name="sb_prompt",
    )(q, k, v, _suffix_matrix(tq))


def _sb_step_kernel(q_ref, kn_ref, vn_ref, kc_ref, vc_ref, uu_ref, o_ref,
                    kb_ref, vb_ref, *, scale):
    t = q_ref.shape[0]
    past, nh, d = kc_ref.shape
    tk = uu_ref.shape[1]
    n = past + tk
    kb_ref[:, 0:past, :] = _heads_major(kc_ref[...]).astype(BF16)
    vb_ref[:, 0:past, :] = _heads_major(vc_ref[...]).astype(BF16)
    for h in range(nh):
        kb_ref[h, past:past + t] = kn_ref[:, _head(h)].astype(BF16)
        vb_ref[h, past:past + t] = vn_ref[:, _head(h)].astype(BF16)
        kb_ref[h, past + t:n] = jnp.zeros((tk - t, d), BF16)
        vb_ref[h, past + t:n] = jnp.zeros((tk - t, d), BF16)

    z = jnp.concatenate(
        [lax.dot_general(q_ref[:, _head(h)], kb_ref[h], _DN_T,
                         preferred_element_type=F32) for h in range(nh)],
        axis=0) * scale
    row = lax.broadcasted_iota(jnp.int32, z.shape, 0) & (t - 1)
    col = lax.broadcasted_iota(jnp.int32, z.shape, 1)
    valid = col < past + row
    lb, l1m = _sb_logs(z, valid)
    ntiles = n // tk
    sufs = [jnp.dot(_split_hi_lo(l1m[:, j * tk:(j + 1) * tk]), uu_ref[...],
                    preferred_element_type=F32) for j in range(ntiles)]
    tots = [jnp.sum(l1m[:, j * tk:(j + 1) * tk], axis=1, keepdims=True)
            for j in range(ntiles)]
    carries = [None] * ntiles
    carry = jnp.zeros_like(tots[0])
    for j in reversed(range(ntiles)):
        carries[j] = carry
        carry = carry + tots[j]
    after = jnp.concatenate([sufs[j] + carries[j] for j in range(ntiles)],
                            axis=1)
    w = jnp.where(valid, jnp.exp(lb + after), 0.0).astype(BF16)
    for h in range(nh):
        o = jnp.dot(w[h * t:(h + 1) * t], vb_ref[h], preferred_element_type=F32)
        o_ref[:, _head(h)] = o.astype(o_ref.dtype)


def _sb_step(q, k_new, v_new, k_cache, v_cache, batch):
    m, w = q.shape
    _, past, heads, d = k_cache.shape
    t = m // batch
    tk = SB_STEP_KEYS
    nh = STEP_HEADS
    assert past % tk == 0 and t <= tk and t & (t - 1) == 0
    row = pl.BlockSpec((t, nh * d), lambda b, g: (b, g))
    cache = pl.BlockSpec((None, past, nh, d), lambda b, g: (b, 0, g, 0))
    return pl.pallas_call(
        functools.partial(_sb_step_kernel, scale=d ** -0.5),
        grid=(batch, heads // nh),
        in_specs=[row, row, row, cache, cache,
                  pl.BlockSpec((2 * tk, tk), lambda b, g: (0, 0))],
        out_specs=row,
        out_shape=jax.ShapeDtypeStruct((m, w), BF16),
        scratch_shapes=[pltpu.VMEM((nh, past + tk, d), BF16),
                        pltpu.VMEM((nh, past + tk, d), BF16)],
        compiler_params=_params("arbitrary", "arbitrary"),
        name="sb_step",
    )(q, k_new, v_new, k_cache, v_cache, _suffix_matrix(tk))


def _band_bias_kernel(r_ref, o_ref, *, n_valid):
    tq, w = o_ref.shape
    wr = r_ref.shape[1]
    t = jnp.broadcast_to(r_ref[...], (tq, wr))
    t = pltpu.roll(t, 0, 1, stride=1, stride_axis=0)
    row = lax.broadcasted_iota(jnp.int32, (tq, w), 0)
    col = lax.broadcasted_iota(jnp.int32, (tq, w), 1)
    qc = _chunk_of(row)
    kc = _chunk_of(col)
    valid = (kc >= qc) & (kc <= qc + BAND_PREV_CHUNKS) & (col < n_valid)
    o_ref[...] = jnp.where(valid, t[:, :w], NEG)


def _band_bias(rel, tq, n_valid):
    heads = rel.shape[0]
    w = BAND_HIST + BAND_TQ
    wr = w + LANES
    p = np.arange(wr)
    delta = np.where(p < w, p, p - wr)
    idx = np.clip(BAND_HIST - delta, -REL_CLIP, REL_CLIP) + REL_CLIP
    r = jnp.take(rel.astype(F32), jnp.asarray(idx, jnp.int32), axis=1)
    r = r.reshape(heads, 1, wr)
    return pl.pallas_call(
        functools.partial(_band_bias_kernel, n_valid=n_valid),
        grid=(heads,),
        in_specs=[pl.BlockSpec((None, 1, wr), lambda h: (h, 0, 0))],
        out_specs=pl.BlockSpec((None, tq, w), lambda h: (h, 0, 0)),
        out_shape=jax.ShapeDtypeStruct((heads, tq, w), F32),
        compiler_params=_params("arbitrary"),
        name="band_bias",
    )(r)


def _band_prompt_kernel(q_ref, k_ref, v_ref, bias_ref, o_ref, kp_ref, vp_ref,
                        *, scale, tq):
    seq, wd = k_ref.shape
    d = HEAD_DIM_AB
    nh = wd // d
    w = bias_ref.shape[1]
    hist = w - tq
    kp_ref[0:hist] = jnp.zeros((hist, wd), BF16)
    vp_ref[0:hist] = jnp.zeros((hist, wd), BF16)
    kp_ref[hist:hist + seq] = k_ref[...].astype(BF16)
    vp_ref[hist:hist + seq] = v_ref[...].astype(BF16)
    col = lax.broadcasted_iota(jnp.int32, (tq, w), 1)

    def body(i, c):
        start = pl.multiple_of(i * tq, tq)
        in_seq = col >= hist - start
        ss = []
        for h in range(nh):
            s = lax.dot_general(q_ref[pl.ds(start, tq), _head(h)],
                                kp_ref[pl.ds(start, w), _head(h)], _DN_T,
                                preferred_element_type=F32)
            s = s * scale + bias_ref[h * tq:(h + 1) * tq, :]
            ss.append(jnp.where(in_seq, s, NEG))
        ps, ls = _softmax_stage(ss)
        for h in range(nh):
            o = jnp.dot(ps[h], vp_ref[pl.ds(start, w), _head(h)],
                        preferred_element_type=F32) / ls[h]
            o_ref[pl.ds(start, tq), _head(h)] = o.astype(o_ref.dtype)
        return c

    lax.fori_loop(0, seq // tq, body, 0)


def _band_prompt(q, k, v, bias, batch, seq):
    m, wd = q.shape
    d = HEAD_DIM_AB
    nh = BAND_HEADS
    heads, tq, w = bias.shape
    bias = bias.reshape(heads // nh, nh * tq, w)
    blk = pl.BlockSpec((seq, nh * d), lambda b, g: (b, g))
    return pl.pallas_call(
        functools.partial(_band_prompt_kernel, scale=d ** -0.5, tq=tq),
        grid=(batch, heads // nh),
        in_specs=[blk, blk, blk,
                  pl.BlockSpec((None, nh * tq, w), lambda b, g: (g, 0, 0))],
        out_specs=blk,
        out_shape=jax.ShapeDtypeStruct((m, wd), BF16),
        scratch_shapes=[pltpu.VMEM((BAND_HIST + seq, nh * d), BF16),
                        pltpu.VMEM((BAND_HIST + seq, nh * d), BF16)],
        compiler_params=_params("arbitrary", "arbitrary"),
        name="band_prompt",
    )(q, k, v, bias)


def _roll_cache(new_ref, cache, new_rows_ref, t):
    past, nh, d = cache.shape
    new_ref[0:past - t] = cache[t:]
    for h in range(nh):
        new_ref[past - t:past, h, :] = new_rows_ref[:, _head(h, d)]


def _band_step_kernel(q_ref, kn_ref, vn_ref, kc_ref, vc_ref, bias_ref, o_ref,
                      nk_ref, nv_ref, kb_ref, vb_ref, *, scale):
    t = q_ref.shape[0]
    past, nh, d = kc_ref.shape
    w = bias_ref.shape[1]
    kc, vc = kc_ref[...], vc_ref[...]
    _roll_cache(nk_ref, kc, kn_ref, t)
    _roll_cache(nv_ref, vc, vn_ref, t)
    kb_ref[:, 0:past, :] = _heads_major(kc).astype(BF16)
    vb_ref[:, 0:past, :] = _heads_major(vc).astype(BF16)
    for h in range(nh):
        kb_ref[h, past:past + t] = kn_ref[:, _head(h)].astype(BF16)
        vb_ref[h, past:past + t] = vn_ref[:, _head(h)].astype(BF16)
        kb_ref[h, past + t:w] = jnp.zeros((w - past - t, d), BF16)
        vb_ref[h, past + t:w] = jnp.zeros((w - past - t, d), BF16)
    ss = [lax.dot_general(q_ref[:, _head(h)], kb_ref[h], _DN_T,
                          preferred_element_type=F32) * scale
          + bias_ref[h * t:(h + 1) * t, :] for h in range(nh)]
    ps, ls = _softmax_stage(ss)
    for h in range(nh):
        o = jnp.dot(ps[h], vb_ref[h], preferred_element_type=F32) / ls[h]
        o_ref[:, _head(h)] = o.astype(o_ref.dtype)


def _band_step(q, k_new, v_new, k_cache, v_cache, bias, batch):
    m, wd = q.shape
    _, past, heads, d = k_cache.shape
    t = m // batch
    nh = STEP_HEADS
    w = bias.shape[2]
    assert past == BAND_HIST and past + t <= w
    bias = bias.reshape(heads // nh, nh * t, w)
    row = pl.BlockSpec((t, nh * d), lambda b, g: (b, g))
    cache = pl.BlockSpec((None, past, nh, d), lambda b, g: (b, 0, g, 0))
    return pl.pallas_call(
        functools.partial(_band_step_kernel, scale=d ** -0.5),
        grid=(batch, heads // nh),
        in_specs=[row, row, row, cache, cache,
                  pl.BlockSpec((None, nh * t, w), lambda b, g: (g, 0, 0))],
        out_specs=[row, cache, cache],
        out_shape=[jax.ShapeDtypeStruct((m, wd), BF16),
                   jax.ShapeDtypeStruct(k_cache.shape, F32),
                   jax.ShapeDtypeStruct(v_cache.shape, F32)],
        scratch_shapes=[pltpu.VMEM((nh, w, d), BF16),
                        pltpu.VMEM((nh, w, d), BF16)],
        compiler_params=_params("arbitrary", "arbitrary"),
        name="band_step",
    )(q, k_new, v_new, k_cache, v_cache, bias)


def _split_pair(blk):
    rolled = pltpu.roll(blk, HEAD_DIM_C, 1)
    lane = lax.broadcasted_iota(jnp.int32, blk.shape, 1)
    low = lane < HEAD_DIM_C
    zero = jnp.zeros_like(blk)
    return [jnp.where(low, blk, zero), jnp.where(low, zero, rolled),
            jnp.where(low, rolled, zero), jnp.where(low, zero, blk)]


def _swa_stage(qs, kls, khs, vls, vhs, valid, sinks, scale):
    ss = []
    for q, kl, kh in zip(qs, kls, khs):
        for kx in (kl, kh):
            s = lax.dot_general(q, kx, _DN_T, preferred_element_type=F32)
            ss.append(jnp.where(valid, s * scale, NEG))
    ps, ls = _softmax_stage(ss, sinks)
    return [jnp.dot(ps[2 * i], vls[i], preferred_element_type=F32) / ls[2 * i]
            + jnp.dot(ps[2 * i + 1], vhs[i], preferred_element_type=F32)
            / ls[2 * i + 1] for i in range(len(qs))]


def _swa_prompt_kernel(sinks_ref, q_ref, k_ref, v_ref, o_ref, ks_ref, vs_ref,
                       *, scale, tq):
    hp = pl.program_id(1)
    seq = k_ref.shape[0]
    band = WINDOW + tq
    npair = q_ref.shape[1] // LANES
    group = npair // 2

    for src, dst in ((k_ref, ks_ref), (v_ref, vs_ref)):
        for n, x in enumerate(_split_pair(src[...])):
            dst[n, 0:WINDOW] = jnp.zeros((WINDOW, LANES), BF16)
            dst[n, WINDOW:WINDOW + seq] = x.astype(BF16)

    row = lax.broadcasted_iota(jnp.int32, (tq, band), 0)
    col = lax.broadcasted_iota(jnp.int32, (tq, band), 1)
    qc = _chunk_of(row)
    kc = _chunk_of(col)
    in_band = (kc >= qc) & (kc <= qc + SWA_PREV_CHUNKS)
    sinks = [sinks_ref[hp * 2 * npair + n] for n in range(2 * npair)]

    def body(i, c):
        start = pl.multiple_of(i * tq, tq)
        valid = in_band & (col >= WINDOW - start)
        kv = [[r[n, pl.ds(start, band), :] for n in range(4)]
              for r in (ks_ref, vs_ref)]
        qs = [q_ref[pl.ds(start, tq), _head(p, LANES)] for p in range(npair)]
        sel = [2 * (p // group) for p in range(npair)]
        outs = _swa_stage(qs, [kv[0][s] for s in sel], [kv[0][s + 1] for s in sel],
                          [kv[1][s] for s in sel], [kv[1][s + 1] for s in sel],
                          valid, sinks, scale)
        for p in range(npair):
            o_ref[pl.ds(start, tq), _head(p, LANES)] = outs[p].astype(o_ref.dtype)
        return c

    lax.fori_loop(0, seq // tq, body, 0)


def _swa_prompt(q, k, v, sinks, batch, seq):
    m, wq = q.shape
    wk = k.shape[1]
    npairs = wk // LANES
    qw = wq // npairs
    qo = pl.BlockSpec((seq, qw), lambda b, h, s: (b, h))
    kv = pl.BlockSpec((seq, LANES), lambda b, h, s: (b, h))
    grid_spec = pltpu.PrefetchScalarGridSpec(
        num_scalar_prefetch=1,
        grid=(batch, npairs),
        in_specs=[qo, kv, kv],
        out_specs=qo,
        scratch_shapes=[pltpu.VMEM((4, WINDOW + seq, LANES), BF16),
                        pltpu.VMEM((4, WINDOW + seq, LANES), BF16)],
    )
    return pl.pallas_call(
        functools.partial(_swa_prompt_kernel, scale=HEAD_DIM_C ** -0.5,
                          tq=SWA_TQ),
        grid_spec=grid_spec,
        out_shape=jax.ShapeDtypeStruct((m, wq), BF16),
        compiler_params=_params("arbitrary", "arbitrary"),
        name="swa_prompt",
    )(sinks, q, k, v)


def _swa_step_kernel(sinks_ref, q_ref, kn_ref, vn_ref, kc_ref, vc_ref, o_ref,
                     nk_ref, nv_ref, *, scale):
    t = q_ref.shape[0]
    past, wk = kc_ref.shape
    band = 2 * LANES
    nkv = wk // LANES
    npair = q_ref.shape[1] // LANES
    group = npair // (2 * nkv)
    kc, vc = kc_ref[...], vc_ref[...]
    kn, vn = kn_ref[...], vn_ref[...]
    nk_ref[0:past - t] = kc[t:]
    nv_ref[0:past - t] = vc[t:]
    nk_ref[past - t:past] = kn
    nv_ref[past - t:past] = vn
    pad = jnp.zeros((band - past - t, wk), F32)
    k_all = jnp.concatenate([kc, kn, pad], axis=0)
    v_all = jnp.concatenate([vc, vn, pad], axis=0)
    col = lax.broadcasted_iota(jnp.int32, (t, band), 1)
    valid = col < past + t
    kx = [[x.astype(BF16) for x in _split_pair(k_all[:, _head(g, LANES)])]
          for g in range(nkv)]
    vx = [[x.astype(BF16) for x in _split_pair(v_all[:, _head(g, LANES)])]
          for g in range(nkv)]
    qs = [q_ref[:, _head(p, LANES)] for p in range(npair)]
    where = [(p // (2 * group), 2 * ((p // group) % 2)) for p in range(npair)]
    sinks = [sinks_ref[n] for n in range(2 * npair)]
    outs = _swa_stage(qs, [kx[g][s] for g, s in where],
                      [kx[g][s + 1] for g, s in where],
                      [vx[g][s] for g, s in where],
                      [vx[g][s + 1] for g, s in where], valid, sinks, scale)
    for p in range(npair):
        o_ref[:, _head(p, LANES)] = outs[p].astype(o_ref.dtype)


def _swa_step(q, k_new, v_new, k_cache, v_cache, sinks, batch):
    m, wq = q.shape
    wk = k_new.shape[1]
    t = m // batch
    past = k_cache.shape[1]
    assert past + t <= 2 * LANES
    qrow = pl.BlockSpec((t, wq), lambda b, s: (b, 0))
    krow = pl.BlockSpec((t, wk), lambda b, s: (b, 0))
    cache = pl.BlockSpec((None, past, wk), lambda b, s: (b, 0, 0))
    grid_spec = pltpu.PrefetchScalarGridSpec(
        num_scalar_prefetch=1,
        grid=(batch,),
        in_specs=[qrow, krow, krow, cache, cache],
        out_specs=[qrow, cache, cache],
    )
    return pl.pallas_call(
        functools.partial(_swa_step_kernel, scale=HEAD_DIM_C ** -0.5),
        grid_spec=grid_spec,
        out_shape=[jax.ShapeDtypeStruct((m, wq), BF16),
                   jax.ShapeDtypeStruct(k_cache.shape, F32),
                   jax.ShapeDtypeStruct(v_cache.shape, F32)],
        compiler_params=_params("arbitrary"),
        name="swa_step",
    )(sinks, q, k_new, v_new, k_cache, v_cache)


def _rope_tables(pos):
    half = ROPE_DIM // 2
    inv = jnp.power(ROPE_THETA, -jnp.arange(half, dtype=F32) / half)
    ang = pos.astype(F32)[:, None] * inv[None, :]
    cos, sin = jnp.cos(ang), jnp.sin(ang)
    n = pos.shape[0]
    reps = LANES // HEAD_DIM_C
    rest = HEAD_DIM_C - ROPE_DIM
    ones = jnp.ones((n, rest), F32)
    zeros = jnp.zeros((n, rest), F32)
    zh = jnp.zeros((n, half), F32)
    c = jnp.tile(jnp.concatenate([cos, cos, ones], axis=1), (1, reps))
    s1 = jnp.tile(jnp.concatenate([zh, sin, zeros], axis=1), (1, reps))
    s2 = jnp.tile(jnp.concatenate([-sin, zh, zeros], axis=1), (1, reps))
    return c, s1, s2


def _gain_lanes(g):
    return jnp.broadcast_to(g.astype(F32)[:, None], (g.shape[0], LANES))


def _ffn(hp, hs, pre, gain, w_gate, w_up, w_down, layer, stats):
    f = w_gate.shape[-1]
    d = hp.shape[1]
    wg, wu = _Weight(w_gate, layer), _Weight(w_up, layer)
    bp_, ss_p, bs_, ss_s = pre
    g = _gain_lanes(gain)
    act_p = _gate_up(bp_, wg, wu, f, norm=(g, _row_scale(ss_p, d)))
    act_s = _gate_up(bs_, wg, wu, f, norm=(g, _row_scale(ss_s, d)))
    hp, hs = _mm_res([(act_p, act_s, 0, _Weight(w_down, layer, 0))], f // 2,
                     hp, hs, DENSE_TN)
    return _mm_res([(act_p, act_s, 1, _Weight(w_down, layer, 1))], f // 2,
                   hp, hs, DENSE_TN, stats=stats)


def kernel(x_prompt, x_sample, cache_sb_k, cache_sb_v, cache_band_k, cache_band_v, cache_swa_k, cache_swa_v, norm_mix, norm_ffn, norm_final, w_in_ab, w_out_ab, rel_bias, w_qkv_c, w_out_c, sinks, w_gate, w_up, w_down):
    bp, seq, dm = x_prompt.shape
    bs, t, _ = x_sample.shape
    depth = norm_mix.shape[0]
    past = cache_sb_k.shape[2]
    hp = x_prompt.reshape(bp * seq, dm)
    hs = x_sample.reshape(bs * t, dm)
    norm_mix3 = norm_mix.reshape(depth, 1, dm)
    norm_ffn3 = norm_ffn.reshape(depth, 1, dm)
    norm_final3 = norm_final.reshape(1, 1, dm)

    w_sb = cache_sb_k.shape[3] * cache_sb_k.shape[4]
    w_band = cache_band_k.shape[3] * cache_band_k.shape[4]
    wq_c = w_out_c.shape[1]
    wk_c = cache_swa_k.shape[3] * cache_swa_k.shape[4]
    p_band = cache_band_k.shape[2]
    p_swa = cache_swa_k.shape[2]

    rope_p = _rope_tables(jnp.arange(seq))
    rope_s = tuple(jnp.tile(x, (bs, 1)) for x in _rope_tables(past + jnp.arange(t)))

    outs = {name: [] for name in (
        "sbk_p", "sbv_p", "bk_p", "bv_p", "ck_p", "cv_p",
        "sbk_s", "sbv_s", "bk_s", "bv_s", "ck_s", "cv_s")}

    pre = None
    for l in range(depth):
        if pre is None:
            a_p = _rmsnorm(hp, norm_mix3, l, BF16)
            a_s = _rmsnorm(hs, norm_mix3, l, BF16)
            norm = None
        else:
            a_p, ss_p, a_s, ss_s = pre
            norm = (_gain_lanes(norm_mix[l]), _row_scale(ss_p, dm),
                    _row_scale(ss_s, dm))
        if l % 2 == 0:
            e = l // 2
            cols = (0, w_sb, 2 * w_sb, 3 * w_sb, 3 * w_sb + w_band, 3 * w_sb + 2 * w_band)
            bias_p = _band_bias(rel_bias[e], BAND_TQ, BAND_HIST + BAND_TQ)
            bias_s = _band_bias(rel_bias[e], t, p_band + t)

            widths = (w_sb, w_sb, w_sb, w_band, w_band, w_band)
            dtypes = (BF16, F32, F32, BF16, F32, F32)

            w_in = [_Weight(w_in_ab, e, 0, c) for c in cols]
            w_out = (_Weight(w_out_ab, e, 0), _Weight(w_out_ab, e, 1))

            (qa, qa_s), (ka, ka_s), (va, va_s), (qb, qb_s), (kb, kb_s), (vb, vb_s) = (
                _proj(a_p, a_s, w, n, dt, norm=norm)
                for w, n, dt in zip(w_in, widths, dtypes))
            oa = _sb_prompt(qa, ka, va, bp, seq)
            ob = _band_prompt(qb, kb, vb, bias_p, bp, seq)
            oa_s = _sb_step(qa_s, ka_s, va_s, cache_sb_k[e], cache_sb_v[e], bs)
            ob_s, nbk, nbv = _band_step(qb_s, kb_s, vb_s, cache_band_k[e],
                                        cache_band_v[e], bias_s, bs)
            hp, hs, *pre = _mm_res(
                [(oa, oa_s, 0, w_out[0]), (ob, ob_s, 0, w_out[1])], w_sb,
                hp, hs, PROJ_TN, stats=True)
            nb = min(BAND_HIST, seq)
            outs["sbk_p"].append(ka.reshape(bp, seq, *cache_sb_k.shape[3:]))
            outs["sbv_p"].append(va.reshape(bp, seq, *cache_sb_k.shape[3:]))
            outs["bk_p"].append(kb.reshape(bp, seq, *cache_band_k.shape[3:])[:, seq - nb:])
            outs["bv_p"].append(vb.reshape(bp, seq, *cache_band_k.shape[3:])[:, seq - nb:])
            outs["sbk_s"].append(ka_s.reshape(bs, t, *cache_sb_k.shape[3:]))
            outs["sbv_s"].append(va_s.reshape(bs, t, *cache_sb_k.shape[3:]))
            outs["bk_s"].append(nbk)
            outs["bv_s"].append(nbv)
        else:
            o = l // 2

            widths = (wq_c, wk_c, wk_c)
            dtypes = (BF16, F32, F32)

            w_in = [_Weight(w_qkv_c, o, 0, c) for c in (0, wq_c, wq_c + wk_c)]
            w_out = _Weight(w_out_c, o, 0)

            (q, q_s), (k, k_s), (v, v_s) = (
                _proj(a_p, a_s, w, n, dt, rope=rp, rope_s=rs, norm=norm)
                for w, n, dt, rp, rs in zip(w_in, widths, dtypes,
                                            (rope_p, rope_p, None),
                                            (rope_s, rope_s, None)))
            oc = _swa_prompt(q, k, v, sinks[o], bp, seq)
            oc_s, nck, ncv = _swa_step(
                q_s, k_s, v_s, cache_swa_k[o].reshape(bs, p_swa, wk_c),
                cache_swa_v[o].reshape(bs, p_swa, wk_c), sinks[o], bs)
            hp, hs, *pre = _mm_res([(oc, oc_s, 0, w_out)], wq_c, hp, hs,
                                   PROJ_TN, stats=True)
            nw = min(WINDOW, seq)
            outs["ck_p"].append(k.reshape(bp, seq, *cache_swa_k.shape[3:])[:, seq - nw:])
            outs["cv_p"].append(v.reshape(bp, seq, *cache_swa_k.shape[3:])[:, seq - nw:])
            outs["ck_s"].append(nck.reshape(cache_swa_k.shape[1:]))
            outs["cv_s"].append(ncv.reshape(cache_swa_v.shape[1:]))
        hp, hs, *pre = _ffn(hp, hs, pre, norm_ffn[l], w_gate, w_up, w_down, l,
                            stats=l + 1 < depth)

    y_prompt = _rmsnorm(hp, norm_final3, 0, F32).reshape(bp, seq, dm)
    y_sample = _rmsnorm(hs, norm_final3, 0, F32).reshape(bs, t, dm)
    st = lambda name: jnp.stack(outs[name])
    return (y_prompt, y_sample,
            st("sbk_p"), st("sbv_p"), st("bk_p"), st("bv_p"), st("ck_p"), st("cv_p"),
            st("sbk_s"), st("sbv_s"), st("bk_s"), st("bv_s"), st("ck_s"), st("cv_s"))
```

```python
import functools
import math

import numpy as np
import jax
import jax.numpy as jnp
from jax import lax
from jax.experimental import pallas as pl
from jax.experimental.pallas import tpu as pltpu

F32 = jnp.float32
BF16 = jnp.bfloat16

CHUNK = 64
EPS = 1e-6
HEAD_DIM_AB = 128
BAND_PREV_CHUNKS = 8
BAND_HIST = BAND_PREV_CHUNKS * CHUNK
REL_CLIP = 2 * CHUNK
HEAD_DIM_C = 64
N_KV_C = 8
SWA_PREV_CHUNKS = 2
WINDOW = SWA_PREV_CHUNKS * CHUNK
ROPE_THETA = 500000.0
ROPE_DIM = HEAD_DIM_C // 4

LANES = 128
V7X_VMEM_LIMIT_BYTES = 60 * 1024 * 1024

DENSE_TM = 1024
GATE_UP_TM = 2048
DENSE_TN = 256
PROJ_TN = 512

NEG = -1e30

SB_TILE = 256
SB_HEADS = 4
SB_STEP_KEYS = 128
BAND_TQ = 128
BAND_HEADS = 4
STEP_HEADS = 8
SWA_TQ = 128


def _params(*sem):
    return pltpu.CompilerParams(dimension_semantics=sem,
                                vmem_limit_bytes=V7X_VMEM_LIMIT_BYTES)


def _chunk_of(pos):
    return lax.shift_right_logical(pos, int(math.log2(CHUNK)))


def _pick_tm(m):
    return DENSE_TM if m % DENSE_TM == 0 else m


def _rmsnorm_kernel(x_ref, g_ref, o_ref):
    x = x_ref[...]
    ms = jnp.mean(x * x, axis=-1, keepdims=True)
    y = x * lax.rsqrt(ms + EPS)
    o_ref[...] = (y * g_ref[...]).astype(o_ref.dtype)


def _rmsnorm(x, g3, layer, out_dtype):
    m, d = x.shape
    tm = 256
    return pl.pallas_call(
        _rmsnorm_kernel,
        grid=(m // tm,),
        in_specs=[pl.BlockSpec((tm, d), lambda i: (i, 0)),
                  pl.BlockSpec((None, 1, d), lambda i: (layer, 0, 0))],
        out_specs=pl.BlockSpec((tm, d), lambda i: (i, 0)),
        out_shape=jax.ShapeDtypeStruct((m, d), out_dtype),
        compiler_params=_params("arbitrary"),
        name="rmsnorm",
    )(x, g3)


class _Weight:
    def __init__(self, src, layer, kblk=0, col0=0):
        self.src, self.layer, self.kblk, self.col0 = src, layer, kblk, col0

    def spec(self, tk, tn):
        layer, kb, jb = self.layer, self.kblk, self.col0 // tn
        return pl.BlockSpec((None, tk, tn), lambda i, j: (layer, kb, jb + j))


def _rope_tile(x, c, s1, s2):
    return (x * c + pltpu.roll(x, 8, 1) * s1
            + pltpu.roll(x, LANES - 8, 1) * s2)


def _rider_in(x):
    return pl.BlockSpec(x.shape, lambda i, j: (0, 0), pipeline_mode=pl.Buffered(1))


def _rider_out(rows, tn, ni):
    return pl.BlockSpec((rows, tn),
                        lambda i, j: (0, jnp.where(i == ni - 1, j, 0)))


def _on_last_row_block(fn):
    pl.when(pl.program_id(0) == pl.num_programs(0) - 1)(fn)


def _lane_chunks(n):
    return [slice(c * LANES, (c + 1) * LANES) for c in range(n // LANES)]


def _weight_tile(w_ref, g_ref):
    if g_ref is None:
        return w_ref[...].astype(BF16)
    g = g_ref[...]
    return jnp.concatenate(
        [(w_ref[:, sl] * g).astype(BF16) for sl in _lane_chunks(w_ref.shape[1])],
        axis=1)


def _proj_kernel(*refs, rope, norm):
    w_ref = refs[1]
    n_tab = 3 if rope else 0
    a_ref, tabs = refs[0], refs[2:2 + n_tab]
    sa_ref, stabs = refs[2 + n_tab], refs[3 + n_tab:3 + 2 * n_tab]
    g_ref, r_ref, sr_ref = refs[3 + 2 * n_tab:6 + 2 * n_tab] if norm else (None,) * 3
    o_ref, so_ref = refs[-2:]

    def run(a_ref, tabs, r_ref, o_ref):
        acc = jnp.dot(a_ref[...], _weight_tile(w_ref, g_ref),
                      preferred_element_type=F32)
        for sl in _lane_chunks(acc.shape[1]):
            x = acc[:, sl]
            if norm:
                x = x * r_ref[...]
            if rope:
                x = _rope_tile(x, *(t[...] for t in tabs))
            o_ref[:, sl] = x.astype(o_ref.dtype)

    run(a_ref, tabs, r_ref, o_ref)
    _on_last_row_block(lambda: run(sa_ref, stabs, sr_ref, so_ref))


def _proj(a, a_s, w, ncols, out_dtype, rope=None, rope_s=None, norm=None):
    m, k = a.shape
    ms = a_s.shape[0]
    tm = _pick_tm(m)
    tn = min(PROJ_TN, ncols)
    ni = m // tm
    in_specs = [pl.BlockSpec((tm, k), lambda i, j: (i, 0)), w.spec(k, tn)]
    args = [a, w.src]
    if rope is not None:
        nrep = rope[0].shape[0] // tm
        for t in rope:
            in_specs.append(pl.BlockSpec((tm, LANES), lambda i, j: (i % nrep, 0)))
            args.append(t)
    for x in (a_s,) + tuple(rope_s or ()):
        in_specs.append(_rider_in(x))
        args.append(x)
    if norm is not None:
        g, r, r_s = norm
        in_specs += [_rider_in(g), pl.BlockSpec((tm, LANES), lambda i, j: (i, 0)),
                     _rider_in(r_s)]
        args += [g, r, r_s]
    return pl.pallas_call(
        functools.partial(_proj_kernel, rope=rope is not None,
                          norm=norm is not None),
        grid=(ni, ncols // tn),
        in_specs=in_specs,
        out_specs=[pl.BlockSpec((tm, tn), lambda i, j: (i, j)),
                   _rider_out(ms, tn, ni)],
        out_shape=[jax.ShapeDtypeStruct((m, ncols), out_dtype),
                   jax.ShapeDtypeStruct((ms, ncols), out_dtype)],
        compiler_params=_params("arbitrary", "arbitrary"),
        name="proj",
    )(*args)


def _gate_up_kernel(*refs, norm):
    a_ref, wg_ref, wu_ref = refs[:3]
    g_ref, r_ref = refs[3:5] if norm else (None, None)
    o_ref = refs[-1]
    a = a_ref[...]
    g = jnp.dot(a, _weight_tile(wg_ref, g_ref), preferred_element_type=F32)
    u = jnp.dot(a, _weight_tile(wu_ref, g_ref), preferred_element_type=F32)
    for sl in _lane_chunks(g.shape[1]):
        gc, uc = g[:, sl], u[:, sl]
        if norm:
            gc, uc = gc * r_ref[...], uc * r_ref[...]
        o_ref[:, sl] = (gc * jax.nn.sigmoid(gc) * uc).astype(o_ref.dtype)


def _gate_up(a, wg, wu, f, norm=None):
    m, k = a.shape
    tm = GATE_UP_TM if m % GATE_UP_TM == 0 else m
    tn = DENSE_TN
    in_specs = [pl.BlockSpec((tm, k), lambda i, j: (i, 0)),
                wg.spec(k, tn), wu.spec(k, tn)]
    args = [a, wg.src, wu.src]
    if norm is not None:
        in_specs += [_rider_in(norm[0]),
                     pl.BlockSpec((tm, LANES), lambda i, j: (i, 0))]
        args += list(norm)
    return pl.pallas_call(
        functools.partial(_gate_up_kernel, norm=norm is not None),
        grid=(m // tm, f // tn),
        in_specs=in_specs,
        out_specs=pl.BlockSpec((tm, tn), lambda i, j: (i, j)),
        out_shape=jax.ShapeDtypeStruct((m, f), BF16),
        compiler_params=_params("arbitrary", "arbitrary"),
        name="gate_up",
    )(*args)


def _mm_res_kernel(*refs, nterms, stats):
    n = nterms
    w_refs = refs[n:2 * n]
    outs = refs[3 * n + 2:]

    def run(a_refs, res_ref, o_ref, b_ref, ss_ref):
        acc = res_ref[...]
        for a_ref, w_ref in zip(a_refs, w_refs):
            acc = acc + jnp.dot(a_ref[...], w_ref[...].astype(BF16),
                                preferred_element_type=F32)
        o_ref[...] = acc
        if stats:
            _row_stats(acc, b_ref, ss_ref)

    p_extra = (outs[2], outs[3]) if stats else (None, None)
    s_extra = (outs[4], outs[5]) if stats else (None, None)
    run(refs[:n], refs[2 * n], outs[0], *p_extra)
    _on_last_row_block(
        lambda: run(refs[2 * n + 1:3 * n + 1], refs[3 * n + 1], outs[1], *s_extra))


def _mm_res(terms, tk, res, res_s, tn, stats=False):
    m, n = res.shape
    ms = res_s.shape[0]
    tm = _pick_tm(m)
    ni = m // tm
    in_specs, args = [], []
    for a, _, ka, _ in terms:
        in_specs.append(pl.BlockSpec((tm, tk), lambda i, j, ka=ka: (i, ka)))
        args.append(a)
    for _, _, _, w in terms:
        in_specs.append(w.spec(tk, tn))
        args.append(w.src)
    tile = pl.BlockSpec((tm, tn), lambda i, j: (i, j))
    tile_s = _rider_out(ms, tn, ni)
    in_specs.append(tile)
    args.append(res)
    for _, a_s, ka, _ in terms:
        in_specs.append(pl.BlockSpec((ms, tk), lambda i, j, ka=ka: (0, ka),
                                     pipeline_mode=pl.Buffered(1)))
        args.append(a_s)
    in_specs.append(tile_s)
    args.append(res_s)
    out_specs = [tile, tile_s]
    out_shape = [jax.ShapeDtypeStruct((m, n), F32),
                 jax.ShapeDtypeStruct((ms, n), F32)]
    if stats:
        out_specs += [tile, pl.BlockSpec((tm, LANES), lambda i, j: (i, 0)),
                      tile_s, pl.BlockSpec((ms, LANES), lambda i, j: (0, 0))]
        out_shape += [jax.ShapeDtypeStruct((m, n), BF16),
                      jax.ShapeDtypeStruct((m, LANES), F32),
                      jax.ShapeDtypeStruct((ms, n), BF16),
                      jax.ShapeDtypeStruct((ms, LANES), F32)]
    return pl.pallas_call(
        functools.partial(_mm_res_kernel, nterms=len(terms), stats=stats),
        grid=(ni, n // tn),
        in_specs=in_specs,
        out_specs=out_specs,
        out_shape=out_shape,
        compiler_params=_params("arbitrary", "arbitrary"),
        name="mm_res",
    )(*args)


def _row_stats(acc, b_ref, ss_ref):
    b_ref[...] = acc.astype(BF16)
    part = sum(acc[:, sl] * acc[:, sl] for sl in _lane_chunks(acc.shape[1]))
    j = pl.program_id(1)

    @pl.when(j == 0)
    def _():
        ss_ref[...] = part

    @pl.when(j != 0)
    def _():
        ss_ref[...] += part


def _down_kernel(*refs, stats):
    a_refs, w_ref, res_ref = refs[0:2], refs[2], refs[3]
    sa_refs, sres_ref = refs[4:6], refs[6]
    outs = refs[7:]
    k = pl.program_id(2)

    def run(x_refs, r_ref, o_ref, b_ref, ss_ref):
        @pl.when(k == 0)
        def _():
            o_ref[...] = r_ref[...] + jnp.dot(
                x_refs[0][...], w_ref[...].astype(BF16),
                preferred_element_type=F32)

        @pl.when(k == 1)
        def _():
            acc = o_ref[...] + jnp.dot(
                x_refs[1][...], w_ref[...].astype(BF16),
                preferred_element_type=F32)
            o_ref[...] = acc
            if stats:
                _row_stats(acc, b_ref, ss_ref)

    p_extra = (outs[2], outs[3]) if stats else (None, None)
    s_extra = (outs[4], outs[5]) if stats else (None, None)
    run(a_refs, res_ref, outs[0], *p_extra)
    _on_last_row_block(lambda: run(sa_refs, sres_ref, outs[1], *s_extra))


def _down(act, act_s, w, layer, res, res_s, stats=False):
    m, n = res.shape
    ms = res_s.shape[0]
    tk = act.shape[1] // 2
    tm, tn = _pick_tm(m), DENSE_TN
    ni = m // tm
    one = dict(pipeline_mode=pl.Buffered(1))
    last = ni - 1
    tile = pl.BlockSpec((tm, tn), lambda i, j, k: (i, j))
    tile_s = pl.BlockSpec((ms, tn),
                          lambda i, j, k: (0, jnp.where(i == last, j, 0)))
    in_specs = (
        [pl.BlockSpec((tm, tk), lambda i, j, k, h=h: (i, h), **one) for h in (0, 1)]
        + [pl.BlockSpec((None, tk, tn), lambda i, j, k: (layer, k, j)), tile]
        + [pl.BlockSpec((ms, tk), lambda i, j, k, h=h: (0, h), **one) for h in (0, 1)]
        + [tile_s])
    out_specs = [tile, tile_s]
    out_shape = [jax.ShapeDtypeStruct((m, n), F32),
                 jax.ShapeDtypeStruct((ms, n), F32)]
    if stats:
        out_specs += [tile, pl.BlockSpec((tm, LANES), lambda i, j, k: (i, 0)),
                      tile_s, pl.BlockSpec((ms, LANES), lambda i, j, k: (0, 0))]
        out_shape += [jax.ShapeDtypeStruct((m, n), BF16),
                      jax.ShapeDtypeStruct((m, LANES), F32),
                      jax.ShapeDtypeStruct((ms, n), BF16),
                      jax.ShapeDtypeStruct((ms, LANES), F32)]
    return pl.pallas_call(
        functools.partial(_down_kernel, stats=stats),
        grid=(ni, n // tn, 2),
        in_specs=in_specs,
        out_specs=out_specs,
        out_shape=out_shape,
        compiler_params=_params("arbitrary", "arbitrary", "arbitrary"),
        name="down",
    )(act, act, w, res, act_s, act_s, res_s)


def _row_scale_kernel(ss_ref, o_ref, *, d):
    tot = jnp.sum(ss_ref[...], axis=1, keepdims=True)
    o_ref[...] = jnp.broadcast_to(lax.rsqrt(tot / d + EPS), o_ref.shape)


def _row_scale(ss, d):
    m = ss.shape[0]
    tm = _pick_tm(m)
    blk = pl.BlockSpec((tm, LANES), lambda i: (i, 0))
    return pl.pallas_call(
        functools.partial(_row_scale_kernel, d=d),
        grid=(m // tm,),
        in_specs=[blk],
        out_specs=blk,
        out_shape=jax.ShapeDtypeStruct(ss.shape, F32),
        compiler_params=_params("arbitrary"),
        name="row_scale",
    )(ss)


_DN_T = (((1,), (1,)), ((), ()))


def _head(h, d=HEAD_DIM_AB):
    return slice(h * d, (h + 1) * d)


def _heads_major(x):
    return pltpu.einshape("phd->hpd", x)


def _softmax_stage(ss, sinks=None):
    ms = [jnp.max(s, axis=1, keepdims=True) for s in ss]
    if sinks is not None:
        ms = [jnp.maximum(m, sk) for m, sk in zip(ms, sinks)]
    ps = [jnp.exp(s - m) for s, m in zip(ss, ms)]
    ls = [jnp.sum(p, axis=1, keepdims=True) for p in ps]
    if sinks is not None:
        ls = [l + jnp.exp(sk - m) for l, sk, m in zip(ls, sinks, ms)]
    return [p.astype(BF16) for p in ps], ls


def _sb_logs(z, valid):
    lb = jnp.minimum(z, 0.0) - jnp.log(1.0 + jnp.exp(-jnp.abs(z)))
    l1m = lb - z
    if valid is not None:
        l1m = jnp.where(valid, l1m, 0.0)
    return lb, l1m


def _split_hi_lo(x):
    hi = x.astype(BF16)
    lo = (x - hi.astype(F32)).astype(BF16)
    return jnp.concatenate([hi, lo], axis=1)


def _suffix_matrix(n):
    u = np.tril(np.ones((n, n), np.float32), -1)
    return jnp.asarray(np.concatenate([u, u], axis=0), BF16)


def _sb_prompt_kernel(q_ref, k_ref, v_ref, uu_ref, o_ref, kb_ref, vb_ref,
                      oacc_ref, cacc_ref, *, scale):
    qi = pl.program_id(2)
    tq = q_ref.shape[0]
    nh = oacc_ref.shape[0]

    @pl.when(qi == 0)
    def _():
        kb_ref[...] = k_ref[...].astype(BF16)
        vb_ref[...] = v_ref[...].astype(BF16)

    oacc_ref[...] = jnp.zeros(oacc_ref.shape, F32)
    cacc_ref[...] = jnp.zeros(cacc_ref.shape, F32)

    def tile(j, masked):
        start = pl.multiple_of(j * tq, tq)
        valid = None
        if masked:
            row = lax.broadcasted_iota(jnp.int32, (tq, tq), 0)
            col = lax.broadcasted_iota(jnp.int32, (tq, tq), 1)
            valid = col < row
        zs = [lax.dot_general(q_ref[:, _head(h)],
                              kb_ref[pl.ds(start, tq), _head(h)], _DN_T,
                              preferred_element_type=F32) * scale
              for h in range(nh)]
        logs = [_sb_logs(z, valid) for z in zs]
        sufs = [jnp.dot(_split_hi_lo(l1m), uu_ref[...],
                        preferred_element_type=F32) for _, l1m in logs]
        ws = [jnp.exp(lb + suf + cacc_ref[h])
              for h, ((lb, _), suf) in enumerate(zip(logs, sufs))]
        if masked:
            ws = [jnp.where(valid, w, 0.0) for w in ws]
        for h in range(nh):
            oacc_ref[h] += jnp.dot(ws[h].astype(BF16),
                                   vb_ref[pl.ds(start, tq), _head(h)],
                                   preferred_element_type=F32)
            cacc_ref[h] += jnp.sum(logs[h][1], axis=1, keepdims=True)

    tile(qi, True)

    def body(t, c):
        tile(qi - 1 - t, False)
        return c

    lax.fori_loop(0, qi, body, 0)
    for h in range(nh):
        o_ref[:, _head(h)] = oacc_ref[h].astype(o_ref.dtype)


def _sb_prompt(q, k, v, batch, seq):
    m, w = q.shape
    d = HEAD_DIM_AB
    nh = SB_HEADS
    tq = SB_TILE
    nq = seq // tq
    qo = pl.BlockSpec((tq, nh * d), lambda b, g, i: (b * nq + i, g))
    kv = pl.BlockSpec((seq, nh * d), lambda b, g, i: (b, g))
    return pl.pallas_call(
        functools.partial(_sb_prompt_kernel, scale=d ** -0.5),
        grid=(batch, w // (nh * d), nq),
        in_specs=[qo, kv, kv, pl.BlockSpec((2 * tq, tq), lambda b, g, i: (0, 0))],
        out_specs=qo,
        out_shape=jax.ShapeDtypeStruct((m, w), BF16),
        scratch_shapes=[pltpu.VMEM((seq, nh * d), BF16),
                        pltpu.VMEM((seq, nh * d), BF16),
                        pltpu.VMEM((nh, tq, d), F32),
                        pltpu.VMEM((nh, tq, 1), F32)],
        compiler_params=_params("arbitrary", "arbitrary", "arbitrary"),
        name="sb_prompt",
    )(q, k, v, _suffix_matrix(tq))


def _sb_step_kernel(q_ref, kn_ref, vn_ref, kc_ref, vc_ref, uu_ref, o_ref,
                    kb_ref, vb_ref, *, scale):
    t = q_ref.shape[0]
    past, nh, d = kc_ref.shape
    tk = uu_ref.shape[1]
    n = past + tk
    kb_ref[:, 0:past, :] = _heads_major(kc_ref[...]).astype(BF16)
    vb_ref[:, 0:past, :] = _heads_major(vc_ref[...]).astype(BF16)
    for h in range(nh):
        kb_ref[h, past:past + t] = kn_ref[:, _head(h)].astype(BF16)
        vb_ref[h, past:past + t] = vn_ref[:, _head(h)].astype(BF16)
        kb_ref[h, past + t:n] = jnp.zeros((tk - t, d), BF16)
        vb_ref[h, past + t:n] = jnp.zeros((tk - t, d), BF16)

    z = jnp.concatenate(
        [lax.dot_general(q_ref[:, _head(h)], kb_ref[h], _DN_T,
                         preferred_element_type=F32) for h in range(nh)],
        axis=0) * scale
    row = lax.broadcasted_iota(jnp.int32, z.shape, 0) & (t - 1)
    col = lax.broadcasted_iota(jnp.int32, z.shape, 1)
    valid = col < past + row
    lb, l1m = _sb_logs(z, valid)
    ntiles = n // tk
    sufs = [jnp.dot(_split_hi_lo(l1m[:, j * tk:(j + 1) * tk]), uu_ref[...],
                    preferred_element_type=F32) for j in range(ntiles)]
    tots = [jnp.sum(l1m[:, j * tk:(j + 1) * tk], axis=1, keepdims=True)
            for j in range(ntiles)]
    carries = [None] * ntiles
    carry = jnp.zeros_like(tots[0])
    for j in reversed(range(ntiles)):
        carries[j] = carry
        carry = carry + tots[j]
    after = jnp.concatenate([sufs[j] + carries[j] for j in range(ntiles)],
                            axis=1)
    w = jnp.where(valid, jnp.exp(lb + after), 0.0).astype(BF16)
    for h in range(nh):
        o = jnp.dot(w[h * t:(h + 1) * t], vb_ref[h], preferred_element_type=F32)
        o_ref[:, _head(h)] = o.astype(o_ref.dtype)


def _sb_step(q, k_new, v_new, k_cache, v_cache, batch):
    m, w = q.shape
    _, past, heads, d = k_cache.shape
    t = m // batch
    tk = SB_STEP_KEYS
    nh = STEP_HEADS
    assert past % tk == 0 and t <= tk and t & (t - 1) == 0
    row = pl.BlockSpec((t, nh * d), lambda b, g: (b, g))
    cache = pl.BlockSpec((None, past, nh, d), lambda b, g: (b, 0, g, 0))
    return pl.pallas_call(
        functools.partial(_sb_step_kernel, scale=d ** -0.5),
        grid=(batch, heads // nh),
        in_specs=[row, row, row, cache, cache,
                  pl.BlockSpec((2 * tk, tk), lambda b, g: (0, 0))],
        out_specs=row,
        out_shape=jax.ShapeDtypeStruct((m, w), BF16),
        scratch_shapes=[pltpu.VMEM((nh, past + tk, d), BF16),
                        pltpu.VMEM((nh, past + tk, d), BF16)],
        compiler_params=_params("arbitrary", "arbitrary"),
        name="sb_step",
    )(q, k_new, v_new, k_cache, v_cache, _suffix_matrix(tk))


def _band_bias_kernel(r_ref, o_ref, *, n_valid):
    tq, w = o_ref.shape
    wr = r_ref.shape[1]
    t = jnp.broadcast_to(r_ref[...], (tq, wr))
    t = pltpu.roll(t, 0, 1, stride=1, stride_axis=0)
    row = lax.broadcasted_iota(jnp.int32, (tq, w), 0)
    col = lax.broadcasted_iota(jnp.int32, (tq, w), 1)
    qc = _chunk_of(row)
    kc = _chunk_of(col)
    valid = (kc >= qc) & (kc <= qc + BAND_PREV_CHUNKS) & (col < n_valid)
    o_ref[...] = jnp.where(valid, t[:, :w], NEG)


def _band_bias(rel, tq, n_valid):
    heads = rel.shape[0]
    w = BAND_HIST + BAND_TQ
    wr = w + LANES
    p = np.arange(wr)
    delta = np.where(p < w, p, p - wr)
    idx = np.clip(BAND_HIST - delta, -REL_CLIP, REL_CLIP) + REL_CLIP
    r = jnp.take(rel.astype(F32), jnp.asarray(idx, jnp.int32), axis=1)
    r = r.reshape(heads, 1, wr)
    return pl.pallas_call(
        functools.partial(_band_bias_kernel, n_valid=n_valid),
        grid=(heads,),
        in_specs=[pl.BlockSpec((None, 1, wr), lambda h: (h, 0, 0))],
        out_specs=pl.BlockSpec((None, tq, w), lambda h: (h, 0, 0)),
        out_shape=jax.ShapeDtypeStruct((heads, tq, w), F32),
        compiler_params=_params("arbitrary"),
        name="band_bias",
    )(r)


def _band_prompt_kernel(q_ref, k_ref, v_ref, bias_ref, o_ref, kp_ref, vp_ref,
                        *, scale, tq):
    seq, wd = k_ref.shape
    d = HEAD_DIM_AB
    nh = wd // d
    w = bias_ref.shape[1]
    hist = w - tq
    kp_ref[0:hist] = jnp.zeros((hist, wd), BF16)
    vp_ref[0:hist] = jnp.zeros((hist, wd), BF16)
    kp_ref[hist:hist + seq] = k_ref[...].astype(BF16)
    vp_ref[hist:hist + seq] = v_ref[...].astype(BF16)
    col = lax.broadcasted_iota(jnp.int32, (tq, w), 1)

    def body(i, c):
        start = pl.multiple_of(i * tq, tq)
        in_seq = col >= hist - start
        ss = []
        for h in range(nh):
            s = lax.dot_general(q_ref[pl.ds(start, tq), _head(h)],
                                kp_ref[pl.ds(start, w), _head(h)], _DN_T,
                                preferred_element_type=F32)
            s = s * scale + bias_ref[h * tq:(h + 1) * tq, :]
            ss.append(jnp.where(in_seq, s, NEG))
        ps, ls = _softmax_stage(ss)
        for h in range(nh):
            o = jnp.dot(ps[h], vp_ref[pl.ds(start, w), _head(h)],
                        preferred_element_type=F32) / ls[h]
            o_ref[pl.ds(start, tq), _head(h)] = o.astype(o_ref.dtype)
        return c

    lax.fori_loop(0, seq // tq, body, 0)


def _band_prompt(q, k, v, bias, batch, seq):
    m, wd = q.shape
    d = HEAD_DIM_AB
    nh = BAND_HEADS
    heads, tq, w = bias.shape
    bias = bias.reshape(heads // nh, nh * tq, w)
    blk = pl.BlockSpec((seq, nh * d), lambda b, g: (b, g))
    return pl.pallas_call(
        functools.partial(_band_prompt_kernel, scale=d ** -0.5, tq=tq),
        grid=(batch, heads // nh),
        in_specs=[blk, blk, blk,
                  pl.BlockSpec((None, nh * tq, w), lambda b, g: (g, 0, 0))],
        out_specs=blk,
        out_shape=jax.ShapeDtypeStruct((m, wd), BF16),
        scratch_shapes=[pltpu.VMEM((BAND_HIST + seq, nh * d), BF16),
                        pltpu.VMEM((BAND_HIST + seq, nh * d), BF16)],
        compiler_params=_params("arbitrary", "arbitrary"),
        name="band_prompt",
    )(q, k, v, bias)


def _roll_cache(new_ref, cache, new_rows_ref, t):
    past, nh, d = cache.shape
    new_ref[0:past - t] = cache[t:]
    for h in range(nh):
        new_ref[past - t:past, h, :] = new_rows_ref[:, _head(h, d)]


def _band_step_kernel(q_ref, kn_ref, vn_ref, kc_ref, vc_ref, bias_ref, o_ref,
                      nk_ref, nv_ref, kb_ref, vb_ref, *, scale):
    t = q_ref.shape[0]
    past, nh, d = kc_ref.shape
    w = bias_ref.shape[1]
    kc, vc = kc_ref[...], vc_ref[...]
    _roll_cache(nk_ref, kc, kn_ref, t)
    _roll_cache(nv_ref, vc, vn_ref, t)
    kb_ref[:, 0:past, :] = _heads_major(kc).astype(BF16)
    vb_ref[:, 0:past, :] = _heads_major(vc).astype(BF16)
    for h in range(nh):
        kb_ref[h, past:past + t] = kn_ref[:, _head(h)].astype(BF16)
        vb_ref[h, past:past + t] = vn_ref[:, _head(h)].astype(BF16)
        kb_ref[h, past + t:w] = jnp.zeros((w - past - t, d), BF16)
        vb_ref[h, past + t:w] = jnp.zeros((w - past - t, d), BF16)
    ss = [lax.dot_general(q_ref[:, _head(h)], kb_ref[h], _DN_T,
                          preferred_element_type=F32) * scale
          + bias_ref[h * t:(h + 1) * t, :] for h in range(nh)]
    ps, ls = _softmax_stage(ss)
    for h in range(nh):
        o = jnp.dot(ps[h], vb_ref[h], preferred_element_type=F32) / ls[h]
        o_ref[:, _head(h)] = o.astype(o_ref.dtype)


def _band_step(q, k_new, v_new, k_cache, v_cache, bias, batch):
    m, wd = q.shape
    _, past, heads, d = k_cache.shape
    t = m // batch
    nh = STEP_HEADS
    w = bias.shape[2]
    assert past == BAND_HIST and past + t <= w
    bias = bias.reshape(heads // nh, nh * t, w)
    row = pl.BlockSpec((t, nh * d), lambda b, g: (b, g))
    cache = pl.BlockSpec((None, past, nh, d), lambda b, g: (b, 0, g, 0))
    return pl.pallas_call(
        functools.partial(_band_step_kernel, scale=d ** -0.5),
        grid=(batch, heads // nh),
        in_specs=[row, row, row, cache, cache,
                  pl.BlockSpec((None, nh * t, w), lambda b, g: (g, 0, 0))],
        out_specs=[row, cache, cache],
        out_shape=[jax.ShapeDtypeStruct((m, wd), BF16),
                   jax.ShapeDtypeStruct(k_cache.shape, F32),
                   jax.ShapeDtypeStruct(v_cache.shape, F32)],
        scratch_shapes=[pltpu.VMEM((nh, w, d), BF16),
                        pltpu.VMEM((nh, w, d), BF16)],
        compiler_params=_params("arbitrary", "arbitrary"),
        name="band_step",
    )(q, k_new, v_new, k_cache, v_cache, bias)


def _split_pair(blk):
    rolled = pltpu.roll(blk, HEAD_DIM_C, 1)
    lane = lax.broadcasted_iota(jnp.int32, blk.shape, 1)
    low = lane < HEAD_DIM_C
    zero = jnp.zeros_like(blk)
    return [jnp.where(low, blk, zero), jnp.where(low, zero, rolled),
            jnp.where(low, rolled, zero), jnp.where(low, zero, blk)]


def _swa_stage(qs, kls, khs, vls, vhs, valid, sinks, scale):
    ss = []
    for q, kl, kh in zip(qs, kls, khs):
        for kx in (kl, kh):
            s = lax.dot_general(q, kx, _DN_T, preferred_element_type=F32)
            ss.append(jnp.where(valid, s * scale, NEG))
    ps, ls = _softmax_stage(ss, sinks)
    return [jnp.dot(ps[2 * i], vls[i], preferred_element_type=F32) / ls[2 * i]
            + jnp.dot(ps[2 * i + 1], vhs[i], preferred_element_type=F32)
            / ls[2 * i + 1] for i in range(len(qs))]


def _swa_prompt_kernel(sinks_ref, q_ref, k_ref, v_ref, o_ref, ks_ref, vs_ref,
                       *, scale, tq):
    hp = pl.program_id(1)
    seq = k_ref.shape[0]
    band = WINDOW + tq
    npair = q_ref.shape[1] // LANES
    group = npair // 2

    for src, dst in ((k_ref, ks_ref), (v_ref, vs_ref)):
        for n, x in enumerate(_split_pair(src[...])):
            dst[n, 0:WINDOW] = jnp.zeros((WINDOW, LANES), BF16)
            dst[n, WINDOW:WINDOW + seq] = x.astype(BF16)

    row = lax.broadcasted_iota(jnp.int32, (tq, band), 0)
    col = lax.broadcasted_iota(jnp.int32, (tq, band), 1)
    qc = _chunk_of(row)
    kc = _chunk_of(col)
    in_band = (kc >= qc) & (kc <= qc + SWA_PREV_CHUNKS)
    sinks = [sinks_ref[hp * 2 * npair + n] for n in range(2 * npair)]

    def body(i, c):
        start = pl.multiple_of(i * tq, tq)
        valid = in_band & (col >= WINDOW - start)
        kv = [[r[n, pl.ds(start, band), :] for n in range(4)]
              for r in (ks_ref, vs_ref)]
        qs = [q_ref[pl.ds(start, tq), _head(p, LANES)] for p in range(npair)]
        sel = [2 * (p // group) for p in range(npair)]
        outs = _swa_stage(qs, [kv[0][s] for s in sel], [kv[0][s + 1] for s in sel],
                          [kv[1][s] for s in sel], [kv[1][s + 1] for s in sel],
                          valid, sinks, scale)
        for p in range(npair):
            o_ref[pl.ds(start, tq), _head(p, LANES)] = outs[p].astype(o_ref.dtype)
        return c

    lax.fori_loop(0, seq // tq, body, 0)


def _swa_prompt(q, k, v, sinks, batch, seq):
    m, wq = q.shape
    wk = k.shape[1]
    npairs = wk // LANES
    qw = wq // npairs
    qo = pl.BlockSpec((seq, qw), lambda b, h, s: (b, h))
    kv = pl.BlockSpec((seq, LANES), lambda b, h, s: (b, h))
    grid_spec = pltpu.PrefetchScalarGridSpec(
        num_scalar_prefetch=1,
        grid=(batch, npairs),
        in_specs=[qo, kv, kv],
        out_specs=qo,
        scratch_shapes=[pltpu.VMEM((4, WINDOW + seq, LANES), BF16),
                        pltpu.VMEM((4, WINDOW + seq, LANES), BF16)],
    )
    return pl.pallas_call(
        functools.partial(_swa_prompt_kernel, scale=HEAD_DIM_C ** -0.5,
                          tq=SWA_TQ),
        grid_spec=grid_spec,
        out_shape=jax.ShapeDtypeStruct((m, wq), BF16),
        compiler_params=_params("arbitrary", "arbitrary"),
        name="swa_prompt",
    )(sinks, q, k, v)


def _swa_step_kernel(sinks_ref, q_ref, kn_ref, vn_ref, kc_ref, vc_ref, o_ref,
                     nk_ref, nv_ref, *, scale):
    t = q_ref.shape[0]
    past, wk = kc_ref.shape
    band = 2 * LANES
    nkv = wk // LANES
    npair = q_ref.shape[1] // LANES
    group = npair // (2 * nkv)
    kc, vc = kc_ref[...], vc_ref[...]
    kn, vn = kn_ref[...], vn_ref[...]
    nk_ref[0:past - t] = kc[t:]
    nv_ref[0:past - t] = vc[t:]
    nk_ref[past - t:past] = kn
    nv_ref[past - t:past] = vn
    pad = jnp.zeros((band - past - t, wk), F32)
    k_all = jnp.concatenate([kc, kn, pad], axis=0)
    v_all = jnp.concatenate([vc, vn, pad], axis=0)
    col = lax.broadcasted_iota(jnp.int32, (t, band), 1)
    valid = col < past + t
    kx = [[x.astype(BF16) for x in _split_pair(k_all[:, _head(g, LANES)])]
          for g in range(nkv)]
    vx = [[x.astype(BF16) for x in _split_pair(v_all[:, _head(g, LANES)])]
          for g in range(nkv)]
    qs = [q_ref[:, _head(p, LANES)] for p in range(npair)]
    where = [(p // (2 * group), 2 * ((p // group) % 2)) for p in range(npair)]
    sinks = [sinks_ref[n] for n in range(2 * npair)]
    outs = _swa_stage(qs, [kx[g][s] for g, s in where],
                      [kx[g][s + 1] for g, s in where],
                      [vx[g][s] for g, s in where],
                      [vx[g][s + 1] for g, s in where], valid, sinks, scale)
    for p in range(npair):
        o_ref[:, _head(p, LANES)] = outs[p].astype(o_ref.dtype)


def _swa_step(q, k_new, v_new, k_cache, v_cache, sinks, batch):
    m, wq = q.shape
    wk = k_new.shape[1]
    t = m // batch
    past = k_cache.shape[1]
    assert past + t <= 2 * LANES
    qrow = pl.BlockSpec((t, wq), lambda b, s: (b, 0))
    krow = pl.BlockSpec((t, wk), lambda b, s: (b, 0))
    cache = pl.BlockSpec((None, past, wk), lambda b, s: (b, 0, 0))
    grid_spec = pltpu.PrefetchScalarGridSpec(
        num_scalar_prefetch=1,
        grid=(batch,),
        in_specs=[qrow, krow, krow, cache, cache],
        out_specs=[qrow, cache, cache],
    )
    return pl.pallas_call(
        functools.partial(_swa_step_kernel, scale=HEAD_DIM_C ** -0.5),
        grid_spec=grid_spec,
        out_shape=[jax.ShapeDtypeStruct((m, wq), BF16),
                   jax.ShapeDtypeStruct(k_cache.shape, F32),
                   jax.ShapeDtypeStruct(v_cache.shape, F32)],
        compiler_params=_params("arbitrary"),
        name="swa_step",
    )(sinks, q, k_new, v_new, k_cache, v_cache)


def _rope_tables(pos):
    half = ROPE_DIM // 2
    inv = jnp.power(ROPE_THETA, -jnp.arange(half, dtype=F32) / half)
    ang = pos.astype(F32)[:, None] * inv[None, :]
    cos, sin = jnp.cos(ang), jnp.sin(ang)
    n = pos.shape[0]
    reps = LANES // HEAD_DIM_C
    rest = HEAD_DIM_C - ROPE_DIM
    ones = jnp.ones((n, rest), F32)
    zeros = jnp.zeros((n, rest), F32)
    zh = jnp.zeros((n, half), F32)
    c = jnp.tile(jnp.concatenate([cos, cos, ones], axis=1), (1, reps))
    s1 = jnp.tile(jnp.concatenate([zh, sin, zeros], axis=1), (1, reps))
    s2 = jnp.tile(jnp.concatenate([-sin, zh, zeros], axis=1), (1, reps))
    return c, s1, s2


def _gain_lanes(g):
    return jnp.broadcast_to(g.astype(F32)[:, None], (g.shape[0], LANES))


def _ffn(hp, hs, pre, gain, w_gate, w_up, w_down, layer, stats):
    f = w_gate.shape[-1]
    d = hp.shape[1]
    wg, wu = _Weight(w_gate, layer), _Weight(w_up, layer)
    bp_, ss_p, bs_, ss_s = pre
    g = _gain_lanes(gain)
    act_p = _gate_up(bp_, wg, wu, f, norm=(g, _row_scale(ss_p, d)))
    act_s = _gate_up(bs_, wg, wu, f, norm=(g, _row_scale(ss_s, d)))
    return _down(act_p, act_s, w_down, layer, hp, hs, stats=stats)


def kernel(x_prompt, x_sample, cache_sb_k, cache_sb_v, cache_band_k, cache_band_v, cache_swa_k, cache_swa_v, norm_mix, norm_ffn, norm_final, w_in_ab, w_out_ab, rel_bias, w_qkv_c, w_out_c, sinks, w_gate, w_up, w_down):
    bp, seq, dm = x_prompt.shape
    bs, t, _ = x_sample.shape
    depth = norm_mix.shape[0]
    past = cache_sb_k.shape[2]
    hp = x_prompt.reshape(bp * seq, dm)
    hs = x_sample.reshape(bs * t, dm)
    norm_mix3 = norm_mix.reshape(depth, 1, dm)
    norm_ffn3 = norm_ffn.reshape(depth, 1, dm)
    norm_final3 = norm_final.reshape(1, 1, dm)

    w_sb = cache_sb_k.shape[3] * cache_sb_k.shape[4]
    w_band = cache_band_k.shape[3] * cache_band_k.shape[4]
    wq_c = w_out_c.shape[1]
    wk_c = cache_swa_k.shape[3] * cache_swa_k.shape[4]
    p_band = cache_band_k.shape[2]
    p_swa = cache_swa_k.shape[2]

    rope_p = _rope_tables(jnp.arange(seq))
    rope_s = tuple(jnp.tile(x, (bs, 1)) for x in _rope_tables(past + jnp.arange(t)))

    outs = {name: [] for name in (
        "sbk_p", "sbv_p", "bk_p", "bv_p", "ck_p", "cv_p",
        "sbk_s", "sbv_s", "bk_s", "bv_s", "ck_s", "cv_s")}

    pre = None
    for l in range(depth):
        if pre is None:
            a_p = _rmsnorm(hp, norm_mix3, l, BF16)
            a_s = _rmsnorm(hs, norm_mix3, l, BF16)
            norm = None
        else:
            a_p, ss_p, a_s, ss_s = pre
            norm = (_gain_lanes(norm_mix[l]), _row_scale(ss_p, dm),
                    _row_scale(ss_s, dm))
        if l % 2 == 0:
            e = l // 2
            cols = (0, w_sb, 2 * w_sb, 3 * w_sb, 3 * w_sb + w_band, 3 * w_sb + 2 * w_band)
            bias_p = _band_bias(rel_bias[e], BAND_TQ, BAND_HIST + BAND_TQ)
            bias_s = _band_bias(rel_bias[e], t, p_band + t)

            widths = (w_sb, w_sb, w_sb, w_band, w_band, w_band)
            dtypes = (BF16, F32, F32, BF16, F32, F32)

            w_in = [_Weight(w_in_ab, e, 0, c) for c in cols]
            w_out = (_Weight(w_out_ab, e, 0), _Weight(w_out_ab, e, 1))

            (qa, qa_s), (ka, ka_s), (va, va_s), (qb, qb_s), (kb, kb_s), (vb, vb_s) = (
                _proj(a_p, a_s, w, n, dt, norm=norm)
                for w, n, dt in zip(w_in, widths, dtypes))
            oa = _sb_prompt(qa, ka, va, bp, seq)
            ob = _band_prompt(qb, kb, vb, bias_p, bp, seq)
            oa_s = _sb_step(qa_s, ka_s, va_s, cache_sb_k[e], cache_sb_v[e], bs)
            ob_s, nbk, nbv = _band_step(qb_s, kb_s, vb_s, cache_band_k[e],
                                        cache_band_v[e], bias_s, bs)
            hp, hs, *pre = _mm_res(
                [(oa, oa_s, 0, w_out[0]), (ob, ob_s, 0, w_out[1])], w_sb,
                hp, hs, PROJ_TN, stats=True)
            nb = min(BAND_HIST, seq)
            outs["sbk_p"].append(ka.reshape(bp, seq, *cache_sb_k.shape[3:]))
            outs["sbv_p"].append(va.reshape(bp, seq, *cache_sb_k.shape[3:]))
            outs["bk_p"].append(kb.reshape(bp, seq, *cache_band_k.shape[3:])[:, seq - nb:])
            outs["bv_p"].append(vb.reshape(bp, seq, *cache_band_k.shape[3:])[:, seq - nb:])
            outs["sbk_s"].append(ka_s.reshape(bs, t, *cache_sb_k.shape[3:]))
            outs["sbv_s"].append(va_s.reshape(bs, t, *cache_sb_k.shape[3:]))
            outs["bk_s"].append(nbk)
            outs["bv_s"].append(nbv)
        else:
            o = l // 2

            widths = (wq_c, wk_c, wk_c)
            dtypes = (BF16, F32, F32)

            w_in = [_Weight(w_qkv_c, o, 0, c) for c in (0, wq_c, wq_c + wk_c)]
            w_out = _Weight(w_out_c, o, 0)

            (q, q_s), (k, k_s), (v, v_s) = (
                _proj(a_p, a_s, w, n, dt, rope=rp, rope_s=rs, norm=norm)
                for w, n, dt, rp, rs in zip(w_in, widths, dtypes,
                                            (rope_p, rope_p, None),
                                            (rope_s, rope_s, None)))
            oc = _swa_prompt(q, k, v, sinks[o], bp, seq)
            oc_s, nck, ncv = _swa_step(
                q_s, k_s, v_s, cache_swa_k[o].reshape(bs, p_swa, wk_c),
                cache_swa_v[o].reshape(bs, p_swa, wk_c), sinks[o], bs)
            hp, hs, *pre = _mm_res([(oc, oc_s, 0, w_out)], wq_c, hp, hs,
                                   PROJ_TN, stats=True)
            nw = min(WINDOW, seq)
            outs["ck_p"].append(k.reshape(bp, seq, *cache_swa_k.shape[3:])[:, seq - nw:])
            outs["cv_p"].append(v.reshape(bp, seq, *cache_swa_k.shape[3:])[:, seq - nw:])
            outs["ck_s"].append(nck.reshape(cache_swa_k.shape[1:]))
            outs["cv_s"].append(ncv.reshape(cache_swa_v.shape[1:]))
        hp, hs, *pre = _ffn(hp, hs, pre, norm_ffn[l], w_gate, w_up, w_down, l,
                            stats=l + 1 < depth)

    y_prompt = _rmsnorm(hp, norm_final3, 0, F32).reshape(bp, seq, dm)
    y_sample = _rmsnorm(hs, norm_final3, 0, F32).reshape(bs, t, dm)
    st = lambda name: jnp.stack(outs[name])
    return (y_prompt, y_sample,
            st("sbk_p"), st("sbv_p"), st("bk_p"), st("bv_p"), st("ck_p"), st("cv_p"),
            st("sbk_s"), st("sbv_s"), st("bk_s"), st("bv_s"), st("ck_s"), st("cv_s"))
```

```python
import functools
import math

import numpy as np
import jax
import jax.numpy as jnp
from jax import lax
from jax.experimental import pallas as pl
from jax.experimental.pallas import tpu as pltpu

F32 = jnp.float32
BF16 = jnp.bfloat16

CHUNK = 64
EPS = 1e-6
HEAD_DIM_AB = 128
BAND_PREV_CHUNKS = 8
BAND_HIST = BAND_PREV_CHUNKS * CHUNK
REL_CLIP = 2 * CHUNK
HEAD_DIM_C = 64
N_KV_C = 8
SWA_PREV_CHUNKS = 2
WINDOW = SWA_PREV_CHUNKS * CHUNK
ROPE_THETA = 500000.0
ROPE_DIM = HEAD_DIM_C // 4

LANES = 128
V7X_VMEM_LIMIT_BYTES = 60 * 1024 * 1024

DENSE_TM = 1024
GATE_UP_TM = 2048
DENSE_TN = 256
PROJ_TN = 512

NEG = -1e30

SB_TILE = 256
SB_HEADS = 4
SB_STEP_KEYS = 128
BAND_TQ = 128
BAND_HEADS = 4
STEP_HEADS = 8
SWA_TQ = 128


def _params(*sem):
    return pltpu.CompilerParams(dimension_semantics=sem,
                                vmem_limit_bytes=V7X_VMEM_LIMIT_BYTES)


def _chunk_of(pos):
    return lax.shift_right_logical(pos, int(math.log2(CHUNK)))


def _pick_tm(m):
    return DENSE_TM if m % DENSE_TM == 0 else m


def _rmsnorm_kernel(x_ref, g_ref, o_ref):
    x = x_ref[...]
    ms = jnp.mean(x * x, axis=-1, keepdims=True)
    y = x * lax.rsqrt(ms + EPS)
    o_ref[...] = (y * g_ref[...]).astype(o_ref.dtype)


def _rmsnorm(x, g3, layer, out_dtype):
    m, d = x.shape
    tm = 256
    return pl.pallas_call(
        _rmsnorm_kernel,
        grid=(m // tm,),
        in_specs=[pl.BlockSpec((tm, d), lambda i: (i, 0)),
                  pl.BlockSpec((None, 1, d), lambda i: (layer, 0, 0))],
        out_specs=pl.BlockSpec((tm, d), lambda i: (i, 0)),
        out_shape=jax.ShapeDtypeStruct((m, d), out_dtype),
        compiler_params=_params("arbitrary"),
        name="rmsnorm",
    )(x, g3)


class _Weight:
    def __init__(self, src, layer, kblk=0, col0=0):
        self.src, self.layer, self.kblk, self.col0 = src, layer, kblk, col0

    def spec(self, tk, tn):
        layer, kb, jb = self.layer, self.kblk, self.col0 // tn
        return pl.BlockSpec((None, tk, tn), lambda i, j: (layer, kb, jb + j))


def _rope_tile(x, c, s1, s2):
    return (x * c + pltpu.roll(x, 8, 1) * s1
            + pltpu.roll(x, LANES - 8, 1) * s2)


def _rider_in(x):
    return pl.BlockSpec(x.shape, lambda i, j: (0, 0), pipeline_mode=pl.Buffered(1))


def _rider_out(rows, tn, ni):
    return pl.BlockSpec((rows, tn),
                        lambda i, j: (0, jnp.where(i == ni - 1, j, 0)))


def _on_last_row_block(fn):
    pl.when(pl.program_id(0) == pl.num_programs(0) - 1)(fn)


def _lane_chunks(n):
    return [slice(c * LANES, (c + 1) * LANES) for c in range(n // LANES)]


def _proj_kernel(*refs, rope, norm):
    w_ref = refs[1]
    n_tab = 3 if rope else 0
    a_ref, tabs = refs[0], refs[2:2 + n_tab]
    sa_ref, stabs = refs[2 + n_tab], refs[3 + n_tab:3 + 2 * n_tab]
    r_ref, sr_ref = refs[3 + 2 * n_tab:5 + 2 * n_tab] if norm else (None, None)
    o_ref, so_ref = refs[-2:]

    def run(a_ref, tabs, r_ref, o_ref):
        acc = jnp.dot(a_ref[...], w_ref[...].astype(BF16),
                      preferred_element_type=F32)
        for sl in _lane_chunks(acc.shape[1]):
            x = acc[:, sl]
            if norm:
                x = x * r_ref[...]
            if rope:
                x = _rope_tile(x, *(t[...] for t in tabs))
            o_ref[:, sl] = x.astype(o_ref.dtype)

    run(a_ref, tabs, r_ref, o_ref)
    _on_last_row_block(lambda: run(sa_ref, stabs, sr_ref, so_ref))


def _proj(a, a_s, w, ncols, out_dtype, rope=None, rope_s=None, norm=None):
    m, k = a.shape
    ms = a_s.shape[0]
    tm = _pick_tm(m)
    tn = min(PROJ_TN, ncols)
    ni = m // tm
    in_specs = [pl.BlockSpec((tm, k), lambda i, j: (i, 0)), w.spec(k, tn)]
    args = [a, w.src]
    if rope is not None:
        nrep = rope[0].shape[0] // tm
        for t in rope:
            in_specs.append(pl.BlockSpec((tm, LANES), lambda i, j: (i % nrep, 0)))
            args.append(t)
    for x in (a_s,) + tuple(rope_s or ()):
        in_specs.append(_rider_in(x))
        args.append(x)
    if norm is not None:
        r, r_s = norm
        in_specs += [pl.BlockSpec((tm, LANES), lambda i, j: (i, 0)), _rider_in(r_s)]
        args += [r, r_s]
    return pl.pallas_call(
        functools.partial(_proj_kernel, rope=rope is not None,
                          norm=norm is not None),
        grid=(ni, ncols // tn),
        in_specs=in_specs,
        out_specs=[pl.BlockSpec((tm, tn), lambda i, j: (i, j)),
                   _rider_out(ms, tn, ni)],
        out_shape=[jax.ShapeDtypeStruct((m, ncols), out_dtype),
                   jax.ShapeDtypeStruct((ms, ncols), out_dtype)],
        compiler_params=_params("arbitrary", "arbitrary"),
        name="proj",
    )(*args)


def _gate_up_kernel(*refs, norm):
    a_ref, wg_ref, wu_ref = refs[:3]
    r_ref = refs[3] if norm else None
    o_ref = refs[-1]
    a = a_ref[...]
    g = jnp.dot(a, wg_ref[...].astype(BF16), preferred_element_type=F32)
    u = jnp.dot(a, wu_ref[...].astype(BF16), preferred_element_type=F32)
    for sl in _lane_chunks(g.shape[1]):
        gc, uc = g[:, sl], u[:, sl]
        if norm:
            gc, uc = gc * r_ref[...], uc * r_ref[...]
        o_ref[:, sl] = (gc * jax.nn.sigmoid(gc) * uc).astype(o_ref.dtype)


def _gate_up(a, wg, wu, f, norm=None):
    m, k = a.shape
    tm = GATE_UP_TM if m % GATE_UP_TM == 0 else m
    tn = DENSE_TN
    in_specs = [pl.BlockSpec((tm, k), lambda i, j: (i, 0)),
                wg.spec(k, tn), wu.spec(k, tn)]
    args = [a, wg.src, wu.src]
    if norm is not None:
        in_specs.append(pl.BlockSpec((tm, LANES), lambda i, j: (i, 0)))
        args.append(norm)
    return pl.pallas_call(
        functools.partial(_gate_up_kernel, norm=norm is not None),
        grid=(m // tm, f // tn),
        in_specs=in_specs,
        out_specs=pl.BlockSpec((tm, tn), lambda i, j: (i, j)),
        out_shape=jax.ShapeDtypeStruct((m, f), BF16),
        compiler_params=_params("arbitrary", "arbitrary"),
        name="gate_up",
    )(*args)


def _mm_res_kernel(*refs, nterms, stats):
    n = nterms
    w_refs = refs[n:2 * n]
    gain_ref = refs[3 * n + 2] if stats else None
    outs = refs[3 * n + 2 + bool(stats):]

    def run(a_refs, res_ref, o_ref, b_ref, ss_ref):
        acc = res_ref[...]
        for a_ref, w_ref in zip(a_refs, w_refs):
            acc = acc + jnp.dot(a_ref[...], w_ref[...].astype(BF16),
                                preferred_element_type=F32)
        o_ref[...] = acc
        if stats:
            _row_stats(acc, gain_ref, b_ref, ss_ref)

    p_extra = (outs[2], outs[3]) if stats else (None, None)
    s_extra = (outs[4], outs[5]) if stats else (None, None)
    run(refs[:n], refs[2 * n], outs[0], *p_extra)
    _on_last_row_block(
        lambda: run(refs[2 * n + 1:3 * n + 1], refs[3 * n + 1], outs[1], *s_extra))


def _mm_res(terms, tk, res, res_s, tn, gain=None):
    stats = gain is not None
    m, n = res.shape
    ms = res_s.shape[0]
    tm = _pick_tm(m)
    ni = m // tm
    in_specs, args = [], []
    for a, _, ka, _ in terms:
        in_specs.append(pl.BlockSpec((tm, tk), lambda i, j, ka=ka: (i, ka)))
        args.append(a)
    for _, _, _, w in terms:
        in_specs.append(w.spec(tk, tn))
        args.append(w.src)
    tile = pl.BlockSpec((tm, tn), lambda i, j: (i, j))
    tile_s = _rider_out(ms, tn, ni)
    in_specs.append(tile)
    args.append(res)
    for _, a_s, ka, _ in terms:
        in_specs.append(pl.BlockSpec((ms, tk), lambda i, j, ka=ka: (0, ka),
                                     pipeline_mode=pl.Buffered(1)))
        args.append(a_s)
    in_specs.append(tile_s)
    args.append(res_s)
    out_specs = [tile, tile_s]
    out_shape = [jax.ShapeDtypeStruct((m, n), F32),
                 jax.ShapeDtypeStruct((ms, n), F32)]
    if stats:
        in_specs.append(pl.BlockSpec((1, tn), lambda i, j: (0, j)))
        args.append(gain.astype(F32).reshape(1, n))
        out_specs += [tile, pl.BlockSpec((tm, LANES), lambda i, j: (i, 0)),
                      tile_s, pl.BlockSpec((ms, LANES), lambda i, j: (0, 0))]
        out_shape += [jax.ShapeDtypeStruct((m, n), BF16),
                      jax.ShapeDtypeStruct((m, LANES), F32),
                      jax.ShapeDtypeStruct((ms, n), BF16),
                      jax.ShapeDtypeStruct((ms, LANES), F32)]
    return pl.pallas_call(
        functools.partial(_mm_res_kernel, nterms=len(terms), stats=stats),
        grid=(ni, n // tn),
        in_specs=in_specs,
        out_specs=out_specs,
        out_shape=out_shape,
        compiler_params=_params("arbitrary", "arbitrary"),
        name="mm_res",
    )(*args)


def _row_stats(acc, gain_ref, b_ref, ss_ref):
    b_ref[...] = (acc * gain_ref[...]).astype(BF16)
    part = sum(acc[:, sl] * acc[:, sl] for sl in _lane_chunks(acc.shape[1]))
    j = pl.program_id(1)

    @pl.when(j == 0)
    def _():
        ss_ref[...] = part

    @pl.when(j != 0)
    def _():
        ss_ref[...] += part


def _down_kernel(*refs, stats):
    a_refs, w_ref, res_ref = refs[0:2], refs[2], refs[3]
    sa_refs, sres_ref = refs[4:6], refs[6]
    gain_ref = refs[7] if stats else None
    outs = refs[7 + bool(stats):]
    k = pl.program_id(2)

    def run(x_refs, r_ref, o_ref, b_ref, ss_ref):
        @pl.when(k == 0)
        def _():
            o_ref[...] = r_ref[...] + jnp.dot(
                x_refs[0][...], w_ref[...].astype(BF16),
                preferred_element_type=F32)

        @pl.when(k == 1)
        def _():
            acc = o_ref[...] + jnp.dot(
                x_refs[1][...], w_ref[...].astype(BF16),
                preferred_element_type=F32)
            o_ref[...] = acc
            if stats:
                _row_stats(acc, gain_ref, b_ref, ss_ref)

    p_extra = (outs[2], outs[3]) if stats else (None, None)
    s_extra = (outs[4], outs[5]) if stats else (None, None)
    run(a_refs, res_ref, outs[0], *p_extra)
    _on_last_row_block(lambda: run(sa_refs, sres_ref, outs[1], *s_extra))


def _down(act, act_s, w, layer, res, res_s, gain=None):
    stats = gain is not None
    m, n = res.shape
    ms = res_s.shape[0]
    tk = act.shape[1] // 2
    tm, tn = _pick_tm(m), DENSE_TN
    ni = m // tm
    one = dict(pipeline_mode=pl.Buffered(1))
    last = ni - 1
    tile = pl.BlockSpec((tm, tn), lambda i, j, k: (i, j))
    tile_s = pl.BlockSpec((ms, tn),
                          lambda i, j, k: (0, jnp.where(i == last, j, 0)))
    in_specs = (
        [pl.BlockSpec((tm, tk), lambda i, j, k, h=h: (i, h), **one) for h in (0, 1)]
        + [pl.BlockSpec((None, tk, tn), lambda i, j, k: (layer, k, j)), tile]
        + [pl.BlockSpec((ms, tk), lambda i, j, k, h=h: (0, h), **one) for h in (0, 1)]
        + [tile_s])
    args = [act, act, w, res, act_s, act_s, res_s]
    out_specs = [tile, tile_s]
    out_shape = [jax.ShapeDtypeStruct((m, n), F32),
                 jax.ShapeDtypeStruct((ms, n), F32)]
    if stats:
        in_specs.append(pl.BlockSpec((1, tn), lambda i, j, k: (0, j)))
        args.append(gain.astype(F32).reshape(1, n))
        out_specs += [tile, pl.BlockSpec((tm, LANES), lambda i, j, k: (i, 0)),
                      tile_s, pl.BlockSpec((ms, LANES), lambda i, j, k: (0, 0))]
        out_shape += [jax.ShapeDtypeStruct((m, n), BF16),
                      jax.ShapeDtypeStruct((m, LANES), F32),
                      jax.ShapeDtypeStruct((ms, n), BF16),
                      jax.ShapeDtypeStruct((ms, LANES), F32)]
    return pl.pallas_call(
        functools.partial(_down_kernel, stats=stats),
        grid=(ni, n // tn, 2),
        in_specs=in_specs,
        out_specs=out_specs,
        out_shape=out_shape,
        compiler_params=_params("arbitrary", "arbitrary", "arbitrary"),
        name="down",
    )(*args)


def _row_scale_kernel(ss_ref, o_ref, *, d):
    tot = jnp.sum(ss_ref[...], axis=1, keepdims=True)
    o_ref[...] = jnp.broadcast_to(lax.rsqrt(tot / d + EPS), o_ref.shape)


def _row_scale(ss, d):
    m = ss.shape[0]
    tm = _pick_tm(m)
    blk = pl.BlockSpec((tm, LANES), lambda i: (i, 0))
    return pl.pallas_call(
        functools.partial(_row_scale_kernel, d=d),
        grid=(m // tm,),
        in_specs=[blk],
        out_specs=blk,
        out_shape=jax.ShapeDtypeStruct(ss.shape, F32),
        compiler_params=_params("arbitrary"),
        name="row_scale",
    )(ss)


_DN_T = (((1,), (1,)), ((), ()))


def _head(h, d=HEAD_DIM_AB):
    return slice(h * d, (h + 1) * d)


def _heads_major(x):
    return pltpu.einshape("phd->hpd", x)


def _softmax_stage(ss, sinks=None):
    ms = [jnp.max(s, axis=1, keepdims=True) for s in ss]
    if sinks is not None:
        ms = [jnp.maximum(m, sk) for m, sk in zip(ms, sinks)]
    ps = [jnp.exp(s - m) for s, m in zip(ss, ms)]
    ls = [jnp.sum(p, axis=1, keepdims=True) for p in ps]
    if sinks is not None:
        ls = [l + jnp.exp(sk - m) for l, sk, m in zip(ls, sinks, ms)]
    return [p.astype(BF16) for p in ps], ls


def _sb_logs(z, valid):
    lb = jnp.minimum(z, 0.0) - jnp.log(1.0 + jnp.exp(-jnp.abs(z)))
    l1m = lb - z
    if valid is not None:
        l1m = jnp.where(valid, l1m, 0.0)
    return lb, l1m


def _split_hi_lo(x):
    hi = x.astype(BF16)
    lo = (x - hi.astype(F32)).astype(BF16)
    return jnp.concatenate([hi, lo], axis=1)


def _suffix_matrix(n):
    u = np.tril(np.ones((n, n), np.float32), -1)
    return jnp.asarray(np.concatenate([u, u], axis=0), BF16)


def _sb_prompt_kernel(q_ref, k_ref, v_ref, uu_ref, o_ref, kb_ref, vb_ref,
                      oacc_ref, cacc_ref, *, scale):
    qi = pl.program_id(2)
    tq = q_ref.shape[0]
    nh = oacc_ref.shape[0]

    @pl.when(qi == 0)
    def _():
        kb_ref[...] = k_ref[...].astype(BF16)
        vb_ref[...] = v_ref[...].astype(BF16)

    oacc_ref[...] = jnp.zeros(oacc_ref.shape, F32)
    cacc_ref[...] = jnp.zeros(cacc_ref.shape, F32)

    def tile(j, masked):
        start = pl.multiple_of(j * tq, tq)
        valid = None
        if masked:
            row = lax.broadcasted_iota(jnp.int32, (tq, tq), 0)
            col = lax.broadcasted_iota(jnp.int32, (tq, tq), 1)
            valid = col < row
        zs = [lax.dot_general(q_ref[:, _head(h)],
                              kb_ref[pl.ds(start, tq), _head(h)], _DN_T,
                              preferred_element_type=F32) * scale
              for h in range(nh)]
        logs = [_sb_logs(z, valid) for z in zs]
        sufs = [jnp.dot(_split_hi_lo(l1m), uu_ref[...],
                        preferred_element_type=F32) for _, l1m in logs]
        ws = [jnp.exp(lb + suf + cacc_ref[h])
              for h, ((lb, _), suf) in enumerate(zip(logs, sufs))]
        if masked:
            ws = [jnp.where(valid, w, 0.0) for w in ws]
        for h in range(nh):
            oacc_ref[h] += jnp.dot(ws[h].astype(BF16),
                                   vb_ref[pl.ds(start, tq), _head(h)],
                                   preferred_element_type=F32)
            cacc_ref[h] += jnp.sum(logs[h][1], axis=1, keepdims=True)

    tile(qi, True)

    def body(t, c):
        tile(qi - 1 - t, False)
        return c

    lax.fori_loop(0, qi, body, 0)
    for h in range(nh):
        o_ref[:, _head(h)] = oacc_ref[h].astype(o_ref.dtype)


def _sb_prompt(q, k, v, batch, seq):
    m, w = q.shape
    d = HEAD_DIM_AB
    nh = SB_HEADS
    tq = SB_TILE
    nq = seq // tq
    qo = pl.BlockSpec((tq, nh * d), lambda b, g, i: (b * nq + i, g))
    kv = pl.BlockSpec((seq, nh * d), lambda b, g, i: (b, g))
    return pl.pallas_call(
        functools.partial(_sb_prompt_kernel, scale=d ** -0.5),
        grid=(batch, w // (nh * d), nq),
        in_specs=[qo, kv, kv, pl.BlockSpec((2 * tq, tq), lambda b, g, i: (0, 0))],
        out_specs=qo,
        out_shape=jax.ShapeDtypeStruct((m, w), BF16),
        scratch_shapes=[pltpu.VMEM((seq, nh * d), BF16),
                        pltpu.VMEM((seq, nh * d), BF16),
                        pltpu.VMEM((nh, tq, d), F32),
                        pltpu.VMEM((nh, tq, 1), F32)],
        compiler_params=_params("arbitrary", "arbitrary", "arbitrary"),
        name="sb_prompt",
    )(q, k, v, _suffix_matrix(tq))


def _sb_step_kernel(q_ref, kn_ref, vn_ref, kc_ref, vc_ref, uu_ref, o_ref,
                    kb_ref, vb_ref, *, scale):
    t = q_ref.shape[0]
    past, nh, d = kc_ref.shape
    tk = uu_ref.shape[1]
    n = past + tk
    kb_ref[:, 0:past, :] = _heads_major(kc_ref[...]).astype(BF16)
    vb_ref[:, 0:past, :] = _heads_major(vc_ref[...]).astype(BF16)
    for h in range(nh):
        kb_ref[h, past:past + t] = kn_ref[:, _head(h)].astype(BF16)
        vb_ref[h, past:past + t] = vn_ref[:, _head(h)].astype(BF16)
        kb_ref[h, past + t:n] = jnp.zeros((tk - t, d), BF16)
        vb_ref[h, past + t:n] = jnp.zeros((tk - t, d), BF16)

    z = jnp.concatenate(
        [lax.dot_general(q_ref[:, _head(h)], kb_ref[h], _DN_T,
                         preferred_element_type=F32) for h in range(nh)],
        axis=0) * scale
    row = lax.broadcasted_iota(jnp.int32, z.shape, 0) & (t - 1)
    col = lax.broadcasted_iota(jnp.int32, z.shape, 1)
    valid = col < past + row
    lb, l1m = _sb_logs(z, valid)
    ntiles = n // tk
    sufs = [jnp.dot(_split_hi_lo(l1m[:, j * tk:(j + 1) * tk]), uu_ref[...],
                    preferred_element_type=F32) for j in range(ntiles)]
    tots = [jnp.sum(l1m[:, j * tk:(j + 1) * tk], axis=1, keepdims=True)
            for j in range(ntiles)]
    carries = [None] * ntiles
    carry = jnp.zeros_like(tots[0])
    for j in reversed(range(ntiles)):
        carries[j] = carry
        carry = carry + tots[j]
    after = jnp.concatenate([sufs[j] + carries[j] for j in range(ntiles)],
                            axis=1)
    w = jnp.where(valid, jnp.exp(lb + after), 0.0).astype(BF16)
    for h in range(nh):
        o = jnp.dot(w[h * t:(h + 1) * t], vb_ref[h], preferred_element_type=F32)
        o_ref[:, _head(h)] = o.astype(o_ref.dtype)


def _sb_step(q, k_new, v_new, k_cache, v_cache, batch):
    m, w = q.shape
    _, past, heads, d = k_cache.shape
    t = m // batch
    tk = SB_STEP_KEYS
    nh = STEP_HEADS
    assert past % tk == 0 and t <= tk and t & (t - 1) == 0
    row = pl.BlockSpec((t, nh * d), lambda b, g: (b, g))
    cache = pl.BlockSpec((None, past, nh, d), lambda b, g: (b, 0, g, 0))
    return pl.pallas_call(
        functools.partial(_sb_step_kernel, scale=d ** -0.5),
        grid=(batch, heads // nh),
        in_specs=[row, row, row, cache, cache,
                  pl.BlockSpec((2 * tk, tk), lambda b, g: (0, 0))],
        out_specs=row,
        out_shape=jax.ShapeDtypeStruct((m, w), BF16),
        scratch_shapes=[pltpu.VMEM((nh, past + tk, d), BF16),
                        pltpu.VMEM((nh, past + tk, d), BF16)],
        compiler_params=_params("arbitrary", "arbitrary"),
        name="sb_step",
    )(q, k_new, v_new, k_cache, v_cache, _suffix_matrix(tk))


def _band_bias_kernel(r_ref, o_ref, *, n_valid):
    tq, w = o_ref.shape
    wr = r_ref.shape[1]
    t = jnp.broadcast_to(r_ref[...], (tq, wr))
    t = pltpu.roll(t, 0, 1, stride=1, stride_axis=0)
    row = lax.broadcasted_iota(jnp.int32, (tq, w), 0)
    col = lax.broadcasted_iota(jnp.int32, (tq, w), 1)
    qc = _chunk_of(row)
    kc = _chunk_of(col)
    valid = (kc >= qc) & (kc <= qc + BAND_PREV_CHUNKS) & (col < n_valid)
    o_ref[...] = jnp.where(valid, t[:, :w], NEG)


def _band_bias(rel, tq, n_valid):
    heads = rel.shape[0]
    w = BAND_HIST + BAND_TQ
    wr = w + LANES
    p = np.arange(wr)
    delta = np.where(p < w, p, p - wr)
    idx = np.clip(BAND_HIST - delta, -REL_CLIP, REL_CLIP) + REL_CLIP
    r = jnp.take(rel.astype(F32), jnp.asarray(idx, jnp.int32), axis=1)
    r = r.reshape(heads, 1, wr)
    return pl.pallas_call(
        functools.partial(_band_bias_kernel, n_valid=n_valid),
        grid=(heads,),
        in_specs=[pl.BlockSpec((None, 1, wr), lambda h: (h, 0, 0))],
        out_specs=pl.BlockSpec((None, tq, w), lambda h: (h, 0, 0)),
        out_shape=jax.ShapeDtypeStruct((heads, tq, w), F32),
        compiler_params=_params("arbitrary"),
        name="band_bias",
    )(r)


def _band_prompt_kernel(q_ref, k_ref, v_ref, bias_ref, o_ref, kp_ref, vp_ref,
                        *, scale, tq):
    seq, wd = k_ref.shape
    d = HEAD_DIM_AB
    nh = wd // d
    w = bias_ref.shape[1]
    hist = w - tq
    kp_ref[0:hist] = jnp.zeros((hist, wd), BF16)
    vp_ref[0:hist] = jnp.zeros((hist, wd), BF16)
    kp_ref[hist:hist + seq] = k_ref[...].astype(BF16)
    vp_ref[hist:hist + seq] = v_ref[...].astype(BF16)
    col = lax.broadcasted_iota(jnp.int32, (tq, w), 1)

    def body(i, c):
        start = pl.multiple_of(i * tq, tq)
        in_seq = col >= hist - start
        ss = []
        for h in range(nh):
            s = lax.dot_general(q_ref[pl.ds(start, tq), _head(h)],
                                kp_ref[pl.ds(start, w), _head(h)], _DN_T,
                                preferred_element_type=F32)
            s = s * scale + bias_ref[h * tq:(h + 1) * tq, :]
            ss.append(jnp.where(in_seq, s, NEG))
        ps, ls = _softmax_stage(ss)
        for h in range(nh):
            o = jnp.dot(ps[h], vp_ref[pl.ds(start, w), _head(h)],
                        preferred_element_type=F32) / ls[h]
            o_ref[pl.ds(start, tq), _head(h)] = o.astype(o_ref.dtype)
        return c

    lax.fori_loop(0, seq // tq, body, 0)


def _band_prompt(q, k, v, bias, batch, seq):
    m, wd = q.shape
    d = HEAD_DIM_AB
    nh = BAND_HEADS
    heads, tq, w = bias.shape
    bias = bias.reshape(heads // nh, nh * tq, w)
    blk = pl.BlockSpec((seq, nh * d), lambda b, g: (b, g))
    return pl.pallas_call(
        functools.partial(_band_prompt_kernel, scale=d ** -0.5, tq=tq),
        grid=(batch, heads // nh),
        in_specs=[blk, blk, blk,
                  pl.BlockSpec((None, nh * tq, w), lambda b, g: (g, 0, 0))],
        out_specs=blk,
        out_shape=jax.ShapeDtypeStruct((m, wd), BF16),
        scratch_shapes=[pltpu.VMEM((BAND_HIST + seq, nh * d), BF16),
                        pltpu.VMEM((BAND_HIST + seq, nh * d), BF16)],
        compiler_params=_params("arbitrary", "arbitrary"),
        name="band_prompt",
    )(q, k, v, bias)


def _roll_cache(new_ref, cache, new_rows_ref, t):
    past, nh, d = cache.shape
    new_ref[0:past - t] = cache[t:]
    for h in range(nh):
        new_ref[past - t:past, h, :] = new_rows_ref[:, _head(h, d)]


def _band_step_kernel(q_ref, kn_ref, vn_ref, kc_ref, vc_ref, bias_ref, o_ref,
                      nk_ref, nv_ref, kb_ref, vb_ref, *, scale):
    t = q_ref.shape[0]
    past, nh, d = kc_ref.shape
    w = bias_ref.shape[1]
    kc, vc = kc_ref[...], vc_ref[...]
    _roll_cache(nk_ref, kc, kn_ref, t)
    _roll_cache(nv_ref, vc, vn_ref, t)
    kb_ref[:, 0:past, :] = _heads_major(kc).astype(BF16)
    vb_ref[:, 0:past, :] = _heads_major(vc).astype(BF16)
    for h in range(nh):
        kb_ref[h, past:past + t] = kn_ref[:, _head(h)].astype(BF16)
        vb_ref[h, past:past + t] = vn_ref[:, _head(h)].astype(BF16)
        kb_ref[h, past + t:w] = jnp.zeros((w - past - t, d), BF16)
        vb_ref[h, past + t:w] = jnp.zeros((w - past - t, d), BF16)
    ss = [lax.dot_general(q_ref[:, _head(h)], kb_ref[h], _DN_T,
                          preferred_element_type=F32) * scale
          + bias_ref[h * t:(h + 1) * t, :] for h in range(nh)]
    ps, ls = _softmax_stage(ss)
    for h in range(nh):
        o = jnp.dot(ps[h], vb_ref[h], preferred_element_type=F32) / ls[h]
        o_ref[:, _head(h)] = o.astype(o_ref.dtype)


def _band_step(q, k_new, v_new, k_cache, v_cache, bias, batch):
    m, wd = q.shape
    _, past, heads, d = k_cache.shape
    t = m // batch
    nh = STEP_HEADS
    w = bias.shape[2]
    assert past == BAND_HIST and past + t <= w
    bias = bias.reshape(heads // nh, nh * t, w)
    row = pl.BlockSpec((t, nh * d), lambda b, g: (b, g))
    cache = pl.BlockSpec((None, past, nh, d), lambda b, g: (b, 0, g, 0))
    return pl.pallas_call(
        functools.partial(_band_step_kernel, scale=d ** -0.5),
        grid=(batch, heads // nh),
        in_specs=[row, row, row, cache, cache,
                  pl.BlockSpec((None, nh * t, w), lambda b, g: (g, 0, 0))],
        out_specs=[row, cache, cache],
        out_shape=[jax.ShapeDtypeStruct((m, wd), BF16),
                   jax.ShapeDtypeStruct(k_cache.shape, F32),
                   jax.ShapeDtypeStruct(v_cache.shape, F32)],
        scratch_shapes=[pltpu.VMEM((nh, w, d), BF16),
                        pltpu.VMEM((nh, w, d), BF16)],
        compiler_params=_params("arbitrary", "arbitrary"),
        name="band_step",
    )(q, k_new, v_new, k_cache, v_cache, bias)


def _split_pair(blk):
    rolled = pltpu.roll(blk, HEAD_DIM_C, 1)
    lane = lax.broadcasted_iota(jnp.int32, blk.shape, 1)
    low = lane < HEAD_DIM_C
    zero = jnp.zeros_like(blk)
    return [jnp.where(low, blk, zero), jnp.where(low, zero, rolled),
            jnp.where(low, rolled, zero), jnp.where(low, zero, blk)]


def _swa_stage(qs, kls, khs, vls, vhs, valid, sinks, scale):
    ss = []
    for q, kl, kh in zip(qs, kls, khs):
        for kx in (kl, kh):
            s = lax.dot_general(q, kx, _DN_T, preferred_element_type=F32)
            ss.append(jnp.where(valid, s * scale, NEG))
    ps, ls = _softmax_stage(ss, sinks)
    return [jnp.dot(ps[2 * i], vls[i], preferred_element_type=F32) / ls[2 * i]
            + jnp.dot(ps[2 * i + 1], vhs[i], preferred_element_type=F32)
            / ls[2 * i + 1] for i in range(len(qs))]


def _swa_prompt_kernel(sinks_ref, q_ref, k_ref, v_ref, o_ref, ks_ref, vs_ref,
                       *, scale, tq):
    hp = pl.program_id(1)
    seq = k_ref.shape[0]
    band = WINDOW + tq
    npair = q_ref.shape[1] // LANES
    group = npair // 2

    for src, dst in ((k_ref, ks_ref), (v_ref, vs_ref)):
        for n, x in enumerate(_split_pair(src[...])):
            dst[n, 0:WINDOW] = jnp.zeros((WINDOW, LANES), BF16)
            dst[n, WINDOW:WINDOW + seq] = x.astype(BF16)

    row = lax.broadcasted_iota(jnp.int32, (tq, band), 0)
    col = lax.broadcasted_iota(jnp.int32, (tq, band), 1)
    qc = _chunk_of(row)
    kc = _chunk_of(col)
    in_band = (kc >= qc) & (kc <= qc + SWA_PREV_CHUNKS)
    sinks = [sinks_ref[hp * 2 * npair + n] for n in range(2 * npair)]

    def body(i, c):
        start = pl.multiple_of(i * tq, tq)
        valid = in_band & (col >= WINDOW - start)
        kv = [[r[n, pl.ds(start, band), :] for n in range(4)]
              for r in (ks_ref, vs_ref)]
        qs = [q_ref[pl.ds(start, tq), _head(p, LANES)] for p in range(npair)]
        sel = [2 * (p // group) for p in range(npair)]
        outs = _swa_stage(qs, [kv[0][s] for s in sel], [kv[0][s + 1] for s in sel],
                          [kv[1][s] for s in sel], [kv[1][s + 1] for s in sel],
                          valid, sinks, scale)
        for p in range(npair):
            o_ref[pl.ds(start, tq), _head(p, LANES)] = outs[p].astype(o_ref.dtype)
        return c

    lax.fori_loop(0, seq // tq, body, 0)


def _swa_prompt(q, k, v, sinks, batch, seq):
    m, wq = q.shape
    wk = k.shape[1]
    npairs = wk // LANES
    qw = wq // npairs
    qo = pl.BlockSpec((seq, qw), lambda b, h, s: (b, h))
    kv = pl.BlockSpec((seq, LANES), lambda b, h, s: (b, h))
    grid_spec = pltpu.PrefetchScalarGridSpec(
        num_scalar_prefetch=1,
        grid=(batch, npairs),
        in_specs=[qo, kv, kv],
        out_specs=qo,
        scratch_shapes=[pltpu.VMEM((4, WINDOW + seq, LANES), BF16),
                        pltpu.VMEM((4, WINDOW + seq, LANES), BF16)],
    )
    return pl.pallas_call(
        functools.partial(_swa_prompt_kernel, scale=HEAD_DIM_C ** -0.5,
                          tq=SWA_TQ),
        grid_spec=grid_spec,
        out_shape=jax.ShapeDtypeStruct((m, wq), BF16),
        compiler_params=_params("arbitrary", "arbitrary"),
        name="swa_prompt",
    )(sinks, q, k, v)


def _swa_step_kernel(sinks_ref, q_ref, kn_ref, vn_ref, kc_ref, vc_ref, o_ref,
                     nk_ref, nv_ref, *, scale):
    t = q_ref.shape[0]
    past, wk = kc_ref.shape
    band = 2 * LANES
    nkv = wk // LANES
    npair = q_ref.shape[1] // LANES
    group = npair // (2 * nkv)
    kc, vc = kc_ref[...], vc_ref[...]
    kn, vn = kn_ref[...], vn_ref[...]
    nk_ref[0:past - t] = kc[t:]
    nv_ref[0:past - t] = vc[t:]
    nk_ref[past - t:past] = kn
    nv_ref[past - t:past] = vn
    pad = jnp.zeros((band - past - t, wk), F32)
    k_all = jnp.concatenate([kc, kn, pad], axis=0)
    v_all = jnp.concatenate([vc, vn, pad], axis=0)
    col = lax.broadcasted_iota(jnp.int32, (t, band), 1)
    valid = col < past + t
    kx = [[x.astype(BF16) for x in _split_pair(k_all[:, _head(g, LANES)])]
          for g in range(nkv)]
    vx = [[x.astype(BF16) for x in _split_pair(v_all[:, _head(g, LANES)])]
          for g in range(nkv)]
    qs = [q_ref[:, _head(p, LANES)] for p in range(npair)]
    where = [(p // (2 * group), 2 * ((p // group) % 2)) for p in range(npair)]
    sinks = [sinks_ref[n] for n in range(2 * npair)]
    outs = _swa_stage(qs, [kx[g][s] for g, s in where],
                      [kx[g][s + 1] for g, s in where],
                      [vx[g][s] for g, s in where],
                      [vx[g][s + 1] for g, s in where], valid, sinks, scale)
    for p in range(npair):
        o_ref[:, _head(p, LANES)] = outs[p].astype(o_ref.dtype)


def _swa_step(q, k_new, v_new, k_cache, v_cache, sinks, batch):
    m, wq = q.shape
    wk = k_new.shape[1]
    t = m // batch
    past = k_cache.shape[1]
    assert past + t <= 2 * LANES
    qrow = pl.BlockSpec((t, wq), lambda b, s: (b, 0))
    krow = pl.BlockSpec((t, wk), lambda b, s: (b, 0))
    cache = pl.BlockSpec((None, past, wk), lambda b, s: (b, 0, 0))
    grid_spec = pltpu.PrefetchScalarGridSpec(
        num_scalar_prefetch=1,
        grid=(batch,),
        in_specs=[qrow, krow, krow, cache, cache],
        out_specs=[qrow, cache, cache],
    )
    return pl.pallas_call(
        functools.partial(_swa_step_kernel, scale=HEAD_DIM_C ** -0.5),
        grid_spec=grid_spec,
        out_shape=[jax.ShapeDtypeStruct((m, wq), BF16),
                   jax.ShapeDtypeStruct(k_cache.shape, F32),
                   jax.ShapeDtypeStruct(v_cache.shape, F32)],
        compiler_params=_params("arbitrary"),
        name="swa_step",
    )(sinks, q, k_new, v_new, k_cache, v_cache)


def _rope_tables(pos):
    half = ROPE_DIM // 2
    inv = jnp.power(ROPE_THETA, -jnp.arange(half, dtype=F32) / half)
    ang = pos.astype(F32)[:, None] * inv[None, :]
    cos, sin = jnp.cos(ang), jnp.sin(ang)
    n = pos.shape[0]
    reps = LANES // HEAD_DIM_C
    rest = HEAD_DIM_C - ROPE_DIM
    ones = jnp.ones((n, rest), F32)
    zeros = jnp.zeros((n, rest), F32)
    zh = jnp.zeros((n, half), F32)
    c = jnp.tile(jnp.concatenate([cos, cos, ones], axis=1), (1, reps))
    s1 = jnp.tile(jnp.concatenate([zh, sin, zeros], axis=1), (1, reps))
    s2 = jnp.tile(jnp.concatenate([-sin, zh, zeros], axis=1), (1, reps))
    return c, s1, s2


def _ffn(hp, hs, pre, w_gate, w_up, w_down, layer, next_gain):
    f = w_gate.shape[-1]
    d = hp.shape[1]
    wg, wu = _Weight(w_gate, layer), _Weight(w_up, layer)
    a_p, ss_p, a_s, ss_s = pre
    act_p = _gate_up(a_p, wg, wu, f, norm=_row_scale(ss_p, d))
    act_s = _gate_up(a_s, wg, wu, f, norm=_row_scale(ss_s, d))
    return _down(act_p, act_s, w_down, layer, hp, hs, gain=next_gain)


def kernel(x_prompt, x_sample, cache_sb_k, cache_sb_v, cache_band_k, cache_band_v, cache_swa_k, cache_swa_v, norm_mix, norm_ffn, norm_final, w_in_ab, w_out_ab, rel_bias, w_qkv_c, w_out_c, sinks, w_gate, w_up, w_down):
    bp, seq, dm = x_prompt.shape
    bs, t, _ = x_sample.shape
    depth = norm_mix.shape[0]
    past = cache_sb_k.shape[2]
    hp = x_prompt.reshape(bp * seq, dm)
    hs = x_sample.reshape(bs * t, dm)
    norm_mix3 = norm_mix.reshape(depth, 1, dm)
    norm_final3 = norm_final.reshape(1, 1, dm)

    w_sb = cache_sb_k.shape[3] * cache_sb_k.shape[4]
    w_band = cache_band_k.shape[3] * cache_band_k.shape[4]
    wq_c = w_out_c.shape[1]
    wk_c = cache_swa_k.shape[3] * cache_swa_k.shape[4]
    p_band = cache_band_k.shape[2]
    p_swa = cache_swa_k.shape[2]

    rope_p = _rope_tables(jnp.arange(seq))
    rope_s = tuple(jnp.tile(x, (bs, 1)) for x in _rope_tables(past + jnp.arange(t)))

    outs = {name: [] for name in (
        "sbk_p", "sbv_p", "bk_p", "bv_p", "ck_p", "cv_p",
        "sbk_s", "sbv_s", "bk_s", "bv_s", "ck_s", "cv_s")}

    pre = None
    for l in range(depth):
        if pre is None:
            a_p = _rmsnorm(hp, norm_mix3, l, BF16)
            a_s = _rmsnorm(hs, norm_mix3, l, BF16)
            norm = None
        else:
            a_p, ss_p, a_s, ss_s = pre
            norm = (_row_scale(ss_p, dm), _row_scale(ss_s, dm))
        if l % 2 == 0:
            e = l // 2
            cols = (0, w_sb, 2 * w_sb, 3 * w_sb, 3 * w_sb + w_band, 3 * w_sb + 2 * w_band)
            bias_p = _band_bias(rel_bias[e], BAND_TQ, BAND_HIST + BAND_TQ)
            bias_s = _band_bias(rel_bias[e], t, p_band + t)

            widths = (w_sb, w_sb, w_sb, w_band, w_band, w_band)
            dtypes = (BF16, F32, F32, BF16, F32, F32)

            w_in = [_Weight(w_in_ab, e, 0, c) for c in cols]
            w_out = (_Weight(w_out_ab, e, 0), _Weight(w_out_ab, e, 1))

            (qa, qa_s), (ka, ka_s), (va, va_s), (qb, qb_s), (kb, kb_s), (vb, vb_s) = (
                _proj(a_p, a_s, w, n, dt, norm=norm)
                for w, n, dt in zip(w_in, widths, dtypes))
            oa = _sb_prompt(qa, ka, va, bp, seq)
            ob = _band_prompt(qb, kb, vb, bias_p, bp, seq)
            oa_s = _sb_step(qa_s, ka_s, va_s, cache_sb_k[e], cache_sb_v[e], bs)
            ob_s, nbk, nbv = _band_step(qb_s, kb_s, vb_s, cache_band_k[e],
                                        cache_band_v[e], bias_s, bs)
            hp, hs, *pre = _mm_res(
                [(oa, oa_s, 0, w_out[0]), (ob, ob_s, 0, w_out[1])], w_sb,
                hp, hs, PROJ_TN, gain=norm_ffn[l])
            nb = min(BAND_HIST, seq)
            outs["sbk_p"].append(ka.reshape(bp, seq, *cache_sb_k.shape[3:]))
            outs["sbv_p"].append(va.reshape(bp, seq, *cache_sb_k.shape[3:]))
            outs["bk_p"].append(kb.reshape(bp, seq, *cache_band_k.shape[3:])[:, seq - nb:])
            outs["bv_p"].append(vb.reshape(bp, seq, *cache_band_k.shape[3:])[:, seq - nb:])
            outs["sbk_s"].append(ka_s.reshape(bs, t, *cache_sb_k.shape[3:]))
            outs["sbv_s"].append(va_s.reshape(bs, t, *cache_sb_k.shape[3:]))
            outs["bk_s"].append(nbk)
            outs["bv_s"].append(nbv)
        else:
            o = l // 2

            widths = (wq_c, wk_c, wk_c)
            dtypes = (BF16, F32, F32)

            w_in = [_Weight(w_qkv_c, o, 0, c) for c in (0, wq_c, wq_c + wk_c)]
            w_out = _Weight(w_out_c, o, 0)

            (q, q_s), (k, k_s), (v, v_s) = (
                _proj(a_p, a_s, w, n, dt, rope=rp, rope_s=rs, norm=norm)
                for w, n, dt, rp, rs in zip(w_in, widths, dtypes,
                                            (rope_p, rope_p, None),
                                            (rope_s, rope_s, None)))
            oc = _swa_prompt(q, k, v, sinks[o], bp, seq)
            oc_s, nck, ncv = _swa_step(
                q_s, k_s, v_s, cache_swa_k[o].reshape(bs, p_swa, wk_c),
                cache_swa_v[o].reshape(bs, p_swa, wk_c), sinks[o], bs)
            hp, hs, *pre = _mm_res([(oc, oc_s, 0, w_out)], wq_c, hp, hs,
                                   PROJ_TN, gain=norm_ffn[l])
            nw = min(WINDOW, seq)
            outs["ck_p"].append(k.reshape(bp, seq, *cache_swa_k.shape[3:])[:, seq - nw:])
            outs["cv_p"].append(v.reshape(bp, seq, *cache_swa_k.shape[3:])[:, seq - nw:])
            outs["ck_s"].append(nck.reshape(cache_swa_k.shape[1:]))
            outs["cv_s"].append(ncv.reshape(cache_swa_v.shape[1:]))
        next_gain = norm_mix[l + 1] if l + 1 < depth else None
        hp, hs, *pre = _ffn(hp, hs, pre, w_gate, w_up, w_down, l, next_gain)

    y_prompt = _rmsnorm(hp, norm_final3, 0, F32).reshape(bp, seq, dm)
    y_sample = _rmsnorm(hs, norm_final3, 0, F32).reshape(bs, t, dm)
    st = lambda name: jnp.stack(outs[name])
    return (y_prompt, y_sample,
            st("sbk_p"), st("sbv_p"), st("bk_p"), st("bv_p"), st("ck_p"), st("cv_p"),
            st("sbk_s"), st("sbv_s"), st("bk_s"), st("bv_s"), st("ck_s"), st("cv_s"))
```

```python
import functools
import math

import numpy as np
import jax
import jax.numpy as jnp
from jax import lax
from jax.experimental import pallas as pl
from jax.experimental.pallas import tpu as pltpu

F32 = jnp.float32
BF16 = jnp.bfloat16

CHUNK = 64
EPS = 1e-6
HEAD_DIM_AB = 128
BAND_PREV_CHUNKS = 8
BAND_HIST = BAND_PREV_CHUNKS * CHUNK
REL_CLIP = 2 * CHUNK
HEAD_DIM_C = 64
N_KV_C = 8
SWA_PREV_CHUNKS = 2
WINDOW = SWA_PREV_CHUNKS * CHUNK
ROPE_THETA = 500000.0
ROPE_DIM = HEAD_DIM_C // 4

LANES = 128
V7X_VMEM_LIMIT_BYTES = 60 * 1024 * 1024

DENSE_TM = 1024
GATE_UP_TM = 2048
DENSE_TN = 256
PROJ_TN = 512

NEG = -1e30

SB_TILE = 256
SB_HEADS = 4
SB_STEP_KEYS = 128
BAND_TQ = 128
BAND_HEADS = 4
STEP_HEADS = 8
SWA_TQ = 128


def _params(*sem):
    return pltpu.CompilerParams(dimension_semantics=sem,
                                vmem_limit_bytes=V7X_VMEM_LIMIT_BYTES)


def _chunk_of(pos):
    return lax.shift_right_logical(pos, int(math.log2(CHUNK)))


def _pick_tm(m):
    return DENSE_TM if m % DENSE_TM == 0 else m


def _rmsnorm_kernel(x_ref, g_ref, o_ref):
    x = x_ref[...]
    ms = jnp.mean(x * x, axis=-1, keepdims=True)
    y = x * lax.rsqrt(ms + EPS)
    o_ref[...] = (y * g_ref[...]).astype(o_ref.dtype)


def _rmsnorm(x, g3, layer, out_dtype):
    m, d = x.shape
    tm = 256
    return pl.pallas_call(
        _rmsnorm_kernel,
        grid=(m // tm,),
        in_specs=[pl.BlockSpec((tm, d), lambda i: (i, 0)),
                  pl.BlockSpec((None, 1, d), lambda i: (layer, 0, 0))],
        out_specs=pl.BlockSpec((tm, d), lambda i: (i, 0)),
        out_shape=jax.ShapeDtypeStruct((m, d), out_dtype),
        compiler_params=_params("arbitrary"),
        name="rmsnorm",
    )(x, g3)


class _Weight:
    def __init__(self, src, layer, kblk=0, col0=0):
        self.src, self.layer, self.kblk, self.col0 = src, layer, kblk, col0

    def spec(self, tk, tn):
        layer, kb, jb = self.layer, self.kblk, self.col0 // tn
        return pl.BlockSpec((None, tk, tn), lambda i, j: (layer, kb, jb + j))


def _rope_tile(x, c, s1, s2):
    return (x * c + pltpu.roll(x, 8, 1) * s1
            + pltpu.roll(x, LANES - 8, 1) * s2)


def _rider_in(x):
    return pl.BlockSpec(x.shape, lambda i, j: (0, 0), pipeline_mode=pl.Buffered(1))


def _rider_out(rows, tn, ni):
    return pl.BlockSpec((rows, tn),
                        lambda i, j: (0, jnp.where(i == ni - 1, j, 0)))


def _on_last_row_block(fn):
    pl.when(pl.program_id(0) == pl.num_programs(0) - 1)(fn)


def _lane_chunks(n):
    return [slice(c * LANES, (c + 1) * LANES) for c in range(n // LANES)]


def _proj_kernel(*refs, rope, norm):
    w_ref = refs[1]
    n_tab = 3 if rope else 0
    a_ref, tabs = refs[0], refs[2:2 + n_tab]
    sa_ref, stabs = refs[2 + n_tab], refs[3 + n_tab:3 + 2 * n_tab]
    r_ref, sr_ref = refs[3 + 2 * n_tab:5 + 2 * n_tab] if norm else (None, None)
    o_ref, so_ref = refs[-2:]

    def run(a_ref, tabs, r_ref, o_ref):
        acc = jnp.dot(a_ref[...], w_ref[...].astype(BF16),
                      preferred_element_type=F32)
        for sl in _lane_chunks(acc.shape[1]):
            x = acc[:, sl]
            if norm:
                x = x * r_ref[...]
            if rope:
                x = _rope_tile(x, *(t[...] for t in tabs))
            o_ref[:, sl] = x.astype(o_ref.dtype)

    run(a_ref, tabs, r_ref, o_ref)
    _on_last_row_block(lambda: run(sa_ref, stabs, sr_ref, so_ref))


def _proj(a, a_s, w, ncols, out_dtype, rope=None, rope_s=None, norm=None):
    m, k = a.shape
    ms = a_s.shape[0]
    tm = _pick_tm(m)
    tn = min(PROJ_TN, ncols)
    ni = m // tm
    in_specs = [pl.BlockSpec((tm, k), lambda i, j: (i, 0)), w.spec(k, tn)]
    args = [a, w.src]
    if rope is not None:
        nrep = rope[0].shape[0] // tm
        for t in rope:
            in_specs.append(pl.BlockSpec((tm, LANES), lambda i, j: (i % nrep, 0)))
            args.append(t)
    for x in (a_s,) + tuple(rope_s or ()):
        in_specs.append(_rider_in(x))
        args.append(x)
    if norm is not None:
        r, r_s = norm
        in_specs += [pl.BlockSpec((tm, LANES), lambda i, j: (i, 0)), _rider_in(r_s)]
        args += [r, r_s]
    return pl.pallas_call(
        functools.partial(_proj_kernel, rope=rope is not None,
                          norm=norm is not None),
        grid=(ni, ncols // tn),
        in_specs=in_specs,
        out_specs=[pl.BlockSpec((tm, tn), lambda i, j: (i, j)),
                   _rider_out(ms, tn, ni)],
        out_shape=[jax.ShapeDtypeStruct((m, ncols), out_dtype),
                   jax.ShapeDtypeStruct((ms, ncols), out_dtype)],
        compiler_params=_params("arbitrary", "arbitrary"),
        name="proj",
    )(*args)


def _gate_up_kernel(*refs, norm, cast):
    a_ref, wg_ref, wu_ref = refs[:3]
    r_ref = refs[3] if norm else None
    o_ref = refs[-2] if cast else refs[-1]
    a = a_ref[...]
    g = jnp.dot(a, wg_ref[...].astype(BF16), preferred_element_type=F32)
    u = jnp.dot(a, wu_ref[...].astype(BF16), preferred_element_type=F32)
    for sl in _lane_chunks(g.shape[1]):
        gc, uc = g[:, sl], u[:, sl]
        if norm:
            gc, uc = gc * r_ref[...], uc * r_ref[...]
        o_ref[:, sl] = (gc * jax.nn.sigmoid(gc) * uc).astype(o_ref.dtype)
    if cast:
        refs[-1][...] = refs[-3][...].astype(BF16)


def _gate_up(a, wg, wu, f, norm=None, cast=None):
    m, k = a.shape
    tm = GATE_UP_TM if m % GATE_UP_TM == 0 else m
    tn = DENSE_TN
    ni, nj = m // tm, f // tn
    in_specs = [pl.BlockSpec((tm, k), lambda i, j: (i, 0)),
                wg.spec(k, tn), wu.spec(k, tn)]
    args = [a, wg.src, wu.src]
    out_specs = [pl.BlockSpec((tm, tn), lambda i, j: (i, j))]
    out_shape = [jax.ShapeDtypeStruct((m, f), BF16)]
    if norm is not None:
        in_specs.append(pl.BlockSpec((tm, LANES), lambda i, j: (i, 0),
                                     pipeline_mode=pl.Buffered(1)))
        args.append(norm)
    if cast is not None:
        w3, layer = cast
        rows, n = w3.shape[1] // (ni * nj), w3.shape[2]
        assert rows * ni * nj == w3.shape[1] and rows % 16 == 0
        in_specs.append(pl.BlockSpec((None, rows, n),
                                     lambda i, j: (layer, i * nj + j, 0)))
        args.append(w3)
        out_specs.append(pl.BlockSpec((rows, n), lambda i, j: (i * nj + j, 0)))
        out_shape.append(jax.ShapeDtypeStruct(w3.shape[1:], BF16))
    out = pl.pallas_call(
        functools.partial(_gate_up_kernel, norm=norm is not None,
                          cast=cast is not None),
        grid=(ni, nj),
        in_specs=in_specs,
        out_specs=out_specs,
        out_shape=out_shape,
        compiler_params=_params("arbitrary", "arbitrary"),
        name="gate_up",
    )(*args)
    return out if cast is not None else out[0]


def _mm_res_kernel(*refs, nterms, stats):
    n = nterms
    w_refs = refs[n:2 * n]
    gain_ref = refs[3 * n + 2] if stats else None
    outs = refs[3 * n + 2 + bool(stats):]

    def run(a_refs, res_ref, o_ref, b_ref, ss_ref):
        acc = res_ref[...]
        for a_ref, w_ref in zip(a_refs, w_refs):
            acc = acc + jnp.dot(a_ref[...], w_ref[...].astype(BF16),
                                preferred_element_type=F32)
        o_ref[...] = acc
        if stats:
            _row_stats(acc, gain_ref, b_ref, ss_ref)

    p_extra = (outs[2], outs[3]) if stats else (None, None)
    s_extra = (outs[4], outs[5]) if stats else (None, None)
    run(refs[:n], refs[2 * n], outs[0], *p_extra)
    _on_last_row_block(
        lambda: run(refs[2 * n + 1:3 * n + 1], refs[3 * n + 1], outs[1], *s_extra))


def _mm_res(terms, tk, res, res_s, tn, gain=None):
    stats = gain is not None
    m, n = res.shape
    ms = res_s.shape[0]
    tm = _pick_tm(m)
    ni = m // tm
    in_specs, args = [], []
    for a, _, ka, _ in terms:
        in_specs.append(pl.BlockSpec((tm, tk), lambda i, j, ka=ka: (i, ka)))
        args.append(a)
    for _, _, _, w in terms:
        in_specs.append(w.spec(tk, tn))
        args.append(w.src)
    tile = pl.BlockSpec((tm, tn), lambda i, j: (i, j))
    tile_s = _rider_out(ms, tn, ni)
    in_specs.append(tile)
    args.append(res)
    for _, a_s, ka, _ in terms:
        in_specs.append(pl.BlockSpec((ms, tk), lambda i, j, ka=ka: (0, ka),
                                     pipeline_mode=pl.Buffered(1)))
        args.append(a_s)
    in_specs.append(tile_s)
    args.append(res_s)
    out_specs = [tile, tile_s]
    out_shape = [jax.ShapeDtypeStruct((m, n), F32),
                 jax.ShapeDtypeStruct((ms, n), F32)]
    if stats:
        in_specs.append(pl.BlockSpec((1, tn), lambda i, j: (0, j)))
        args.append(gain.astype(F32).reshape(1, n))
        out_specs += [tile, pl.BlockSpec((tm, LANES), lambda i, j: (i, 0)),
                      tile_s, pl.BlockSpec((ms, LANES), lambda i, j: (0, 0))]
        out_shape += [jax.ShapeDtypeStruct((m, n), BF16),
                      jax.ShapeDtypeStruct((m, LANES), F32),
                      jax.ShapeDtypeStruct((ms, n), BF16),
                      jax.ShapeDtypeStruct((ms, LANES), F32)]
    return pl.pallas_call(
        functools.partial(_mm_res_kernel, nterms=len(terms), stats=stats),
        grid=(ni, n // tn),
        in_specs=in_specs,
        out_specs=out_specs,
        out_shape=out_shape,
        compiler_params=_params("arbitrary", "arbitrary"),
        name="mm_res",
    )(*args)


def _row_stats(acc, gain_ref, b_ref, ss_ref):
    b_ref[...] = (acc * gain_ref[...]).astype(BF16)
    part = sum(acc[:, sl] * acc[:, sl] for sl in _lane_chunks(acc.shape[1]))
    j = pl.program_id(1)

    @pl.when(j == 0)
    def _():
        ss_ref[...] = part

    @pl.when(j != 0)
    def _():
        ss_ref[...] += part


def _down_kernel(*refs, stats):
    a_refs, w_ref, res_ref = refs[0:2], refs[2], refs[3]
    sa_refs, sres_ref = refs[4:6], refs[6]
    gain_ref = refs[7] if stats else None
    outs = refs[7 + bool(stats):]
    k = pl.program_id(2)

    def run(x_refs, r_ref, o_ref, b_ref, ss_ref):
        @pl.when(k == 0)
        def _():
            o_ref[...] = r_ref[...] + jnp.dot(
                x_refs[0][...], w_ref[...].astype(BF16),
                preferred_element_type=F32)

        @pl.when(k == 1)
        def _():
            acc = o_ref[...] + jnp.dot(
                x_refs[1][...], w_ref[...].astype(BF16),
                preferred_element_type=F32)
            o_ref[...] = acc
            if stats:
                _row_stats(acc, gain_ref, b_ref, ss_ref)

    p_extra = (outs[2], outs[3]) if stats else (None, None)
    s_extra = (outs[4], outs[5]) if stats else (None, None)
    run(a_refs, res_ref, outs[0], *p_extra)
    _on_last_row_block(lambda: run(sa_refs, sres_ref, outs[1], *s_extra))


def _down(act, act_s, w, res, res_s, gain=None):
    stats = gain is not None
    m, n = res.shape
    ms = res_s.shape[0]
    tk = act.shape[1] // 2
    tm, tn = _pick_tm(m), DENSE_TN
    ni = m // tm
    one = dict(pipeline_mode=pl.Buffered(1))
    last = ni - 1
    tile = pl.BlockSpec((tm, tn), lambda i, j, k: (i, j))
    tile_s = pl.BlockSpec((ms, tn),
                          lambda i, j, k: (0, jnp.where(i == last, j, 0)))
    in_specs = (
        [pl.BlockSpec((tm, tk), lambda i, j, k, h=h: (i, h), **one) for h in (0, 1)]
        + [pl.BlockSpec((tk, tn), lambda i, j, k: (k, j)), tile]
        + [pl.BlockSpec((ms, tk), lambda i, j, k, h=h: (0, h), **one) for h in (0, 1)]
        + [tile_s])
    args = [act, act, w, res, act_s, act_s, res_s]
    out_specs = [tile, tile_s]
    out_shape = [jax.ShapeDtypeStruct((m, n), F32),
                 jax.ShapeDtypeStruct((ms, n), F32)]
    if stats:
        in_specs.append(pl.BlockSpec((1, tn), lambda i, j, k: (0, j)))
        args.append(gain.astype(F32).reshape(1, n))
        out_specs += [tile, pl.BlockSpec((tm, LANES), lambda i, j, k: (i, 0)),
                      tile_s, pl.BlockSpec((ms, LANES), lambda i, j, k: (0, 0))]
        out_shape += [jax.ShapeDtypeStruct((m, n), BF16),
                      jax.ShapeDtypeStruct((m, LANES), F32),
                      jax.ShapeDtypeStruct((ms, n), BF16),
                      jax.ShapeDtypeStruct((ms, LANES), F32)]
    return pl.pallas_call(
        functools.partial(_down_kernel, stats=stats),
        grid=(ni, n // tn, 2),
        in_specs=in_specs,
        out_specs=out_specs,
        out_shape=out_shape,
        compiler_params=_params("arbitrary", "arbitrary", "arbitrary"),
        name="down",
    )(*args)


def _row_scale_kernel(ss_ref, o_ref, *, d):
    tot = jnp.sum(ss_ref[...], axis=1, keepdims=True)
    o_ref[...] = jnp.broadcast_to(lax.rsqrt(tot / d + EPS), o_ref.shape)


def _row_scale(ss, d):
    m = ss.shape[0]
    tm = _pick_tm(m)
    blk = pl.BlockSpec((tm, LANES), lambda i: (i, 0))
    return pl.pallas_call(
        functools.partial(_row_scale_kernel, d=d),
        grid=(m // tm,),
        in_specs=[blk],
        out_specs=blk,
        out_shape=jax.ShapeDtypeStruct(ss.shape, F32),
        compiler_params=_params("arbitrary"),
        name="row_scale",
    )(ss)


_DN_T = (((1,), (1,)), ((), ()))


def _head(h, d=HEAD_DIM_AB):
    return slice(h * d, (h + 1) * d)


def _heads_major(x):
    return pltpu.einshape("phd->hpd", x)


def _softmax_stage(ss, sinks=None):
    ms = [jnp.max(s, axis=1, keepdims=True) for s in ss]
    if sinks is not None:
        ms = [jnp.maximum(m, sk) for m, sk in zip(ms, sinks)]
    ps = [jnp.exp(s - m) for s, m in zip(ss, ms)]
    ls = [jnp.sum(p, axis=1, keepdims=True) for p in ps]
    if sinks is not None:
        ls = [l + jnp.exp(sk - m) for l, sk, m in zip(ls, sinks, ms)]
    return [p.astype(BF16) for p in ps], ls


def _sb_logs(z, valid):
    lb = jnp.minimum(z, 0.0) - jnp.log(1.0 + jnp.exp(-jnp.abs(z)))
    l1m = lb - z
    if valid is not None:
        l1m = jnp.where(valid, l1m, 0.0)
    return lb, l1m


def _split_hi_lo(x):
    hi = x.astype(BF16)
    lo = (x - hi.astype(F32)).astype(BF16)
    return jnp.concatenate([hi, lo], axis=1)


def _suffix_matrix(n):
    u = np.tril(np.ones((n, n), np.float32), -1)
    return jnp.asarray(np.concatenate([u, u], axis=0), BF16)


def _sb_prompt_kernel(q_ref, k_ref, v_ref, uu_ref, o_ref, kb_ref, vb_ref,
                      oacc_ref, cacc_ref, *, scale):
    qi = pl.program_id(2)
    tq = q_ref.shape[0]
    nh = oacc_ref.shape[0]

    @pl.when(qi == 0)
    def _():
        kb_ref[...] = k_ref[...].astype(BF16)
        vb_ref[...] = v_ref[...].astype(BF16)

    oacc_ref[...] = jnp.zeros(oacc_ref.shape, F32)
    cacc_ref[...] = jnp.zeros(cacc_ref.shape, F32)

    def tile(j, masked):
        start = pl.multiple_of(j * tq, tq)
        valid = None
        if masked:
            row = lax.broadcasted_iota(jnp.int32, (tq, tq), 0)
            col = lax.broadcasted_iota(jnp.int32, (tq, tq), 1)
            valid = col < row
        zs = [lax.dot_general(q_ref[:, _head(h)],
                              kb_ref[pl.ds(start, tq), _head(h)], _DN_T,
                              preferred_element_type=F32) * scale
              for h in range(nh)]
        logs = [_sb_logs(z, valid) for z in zs]
        sufs = [jnp.dot(_split_hi_lo(l1m), uu_ref[...],
                        preferred_element_type=F32) for _, l1m in logs]
        ws = [jnp.exp(lb + suf + cacc_ref[h])
              for h, ((lb, _), suf) in enumerate(zip(logs, sufs))]
        if masked:
            ws = [jnp.where(valid, w, 0.0) for w in ws]
        for h in range(nh):
            oacc_ref[h] += jnp.dot(ws[h].astype(BF16),
                                   vb_ref[pl.ds(start, tq), _head(h)],
                                   preferred_element_type=F32)
            cacc_ref[h] += jnp.sum(logs[h][1], axis=1, keepdims=True)

    tile(qi, True)

    def body(t, c):
        tile(qi - 1 - t, False)
        return c

    lax.fori_loop(0, qi, body, 0)
    for h in range(nh):
        o_ref[:, _head(h)] = oacc_ref[h].astype(o_ref.dtype)


def _sb_prompt(q, k, v, batch, seq):
    m, w = q.shape
    d = HEAD_DIM_AB
    nh = SB_HEADS
    tq = SB_TILE
    nq = seq // tq
    qo = pl.BlockSpec((tq, nh * d), lambda b, g, i: (b * nq + i, g))
    kv = pl.BlockSpec((seq, nh * d), lambda b, g, i: (b, g))
    return pl.pallas_call(
        functools.partial(_sb_prompt_kernel, scale=d ** -0.5),
        grid=(batch, w // (nh * d), nq),
        in_specs=[qo, kv, kv, pl.BlockSpec((2 * tq, tq), lambda b, g, i: (0, 0))],
        out_specs=qo,
        out_shape=jax.ShapeDtypeStruct((m, w), BF16),
        scratch_shapes=[pltpu.VMEM((seq, nh * d), BF16),
                        pltpu.VMEM((seq, nh * d), BF16),
                        pltpu.VMEM((nh, tq, d), F32),
                        pltpu.VMEM((nh, tq, 1), F32)],
        compiler_params=_params("arbitrary", "arbitrary", "arbitrary"),
        name="sb_prompt",
    )(q, k, v, _suffix_matrix(tq))


def _sb_step_kernel(q_ref, kn_ref, vn_ref, kc_ref, vc_ref, uu_ref, o_ref,
                    kb_ref, vb_ref, *, scale):
    t = q_ref.shape[0]
    past, nh, d = kc_ref.shape
    tk = uu_ref.shape[1]
    n = past + tk
    kb_ref[:, 0:past, :] = _heads_major(kc_ref[...]).astype(BF16)
    vb_ref[:, 0:past, :] = _heads_major(vc_ref[...]).astype(BF16)
    for h in range(nh):
        kb_ref[h, past:past + t] = kn_ref[:, _head(h)].astype(BF16)
        vb_ref[h, past:past + t] = vn_ref[:, _head(h)].astype(BF16)
        kb_ref[h, past + t:n] = jnp.zeros((tk - t, d), BF16)
        vb_ref[h, past + t:n] = jnp.zeros((tk - t, d), BF16)

    z = jnp.concatenate(
        [lax.dot_general(q_ref[:, _head(h)], kb_ref[h], _DN_T,
                         preferred_element_type=F32) for h in range(nh)],
        axis=0) * scale
    row = lax.broadcasted_iota(jnp.int32, z.shape, 0) & (t - 1)
    col = lax.broadcasted_iota(jnp.int32, z.shape, 1)
    valid = col < past + row
    lb, l1m = _sb_logs(z, valid)
    ntiles = n // tk
    sufs = [jnp.dot(_split_hi_lo(l1m[:, j * tk:(j + 1) * tk]), uu_ref[...],
                    preferred_element_type=F32) for j in range(ntiles)]
    tots = [jnp.sum(l1m[:, j * tk:(j + 1) * tk], axis=1, keepdims=True)
            for j in range(ntiles)]
    carries = [None] * ntiles
    carry = jnp.zeros_like(tots[0])
    for j in reversed(range(ntiles)):
        carries[j] = carry
        carry = carry + tots[j]
    after = jnp.concatenate([sufs[j] + carries[j] for j in range(ntiles)],
                            axis=1)
    w = jnp.where(valid, jnp.exp(lb + after), 0.0).astype(BF16)
    for h in range(nh):
        o = jnp.dot(w[h * t:(h + 1) * t], vb_ref[h], preferred_element_type=F32)
        o_ref[:, _head(h)] = o.astype(o_ref.dtype)


def _sb_step(q, k_new, v_new, k_cache, v_cache, batch):
    m, w = q.shape
    _, past, heads, d = k_cache.shape
    t = m // batch
    tk = SB_STEP_KEYS
    nh = STEP_HEADS
    assert past % tk == 0 and t <= tk and t & (t - 1) == 0
    row = pl.BlockSpec((t, nh * d), lambda b, g: (b, g))
    cache = pl.BlockSpec((None, past, nh, d), lambda b, g: (b, 0, g, 0))
    return pl.pallas_call(
        functools.partial(_sb_step_kernel, scale=d ** -0.5),
        grid=(batch, heads // nh),
        in_specs=[row, row, row, cache, cache,
                  pl.BlockSpec((2 * tk, tk), lambda b, g: (0, 0))],
        out_specs=row,
        out_shape=jax.ShapeDtypeStruct((m, w), BF16),
        scratch_shapes=[pltpu.VMEM((nh, past + tk, d), BF16),
                        pltpu.VMEM((nh, past + tk, d), BF16)],
        compiler_params=_params("arbitrary", "arbitrary"),
        name="sb_step",
    )(q, k_new, v_new, k_cache, v_cache, _suffix_matrix(tk))


def _band_bias_kernel(r_ref, o_ref, *, n_valid):
    tq, w = o_ref.shape
    wr = r_ref.shape[1]
    t = jnp.broadcast_to(r_ref[...], (tq, wr))
    t = pltpu.roll(t, 0, 1, stride=1, stride_axis=0)
    row = lax.broadcasted_iota(jnp.int32, (tq, w), 0)
    col = lax.broadcasted_iota(jnp.int32, (tq, w), 1)
    qc = _chunk_of(row)
    kc = _chunk_of(col)
    valid = (kc >= qc) & (kc <= qc + BAND_PREV_CHUNKS) & (col < n_valid)
    o_ref[...] = jnp.where(valid, t[:, :w], NEG)


def _band_bias(rel, tq, n_valid):
    heads = rel.shape[0]
    w = BAND_HIST + BAND_TQ
    wr = w + LANES
    p = np.arange(wr)
    delta = np.where(p < w, p, p - wr)
    idx = np.clip(BAND_HIST - delta, -REL_CLIP, REL_CLIP) + REL_CLIP
    r = jnp.take(rel.astype(F32), jnp.asarray(idx, jnp.int32), axis=1)
    r = r.reshape(heads, 1, wr)
    return pl.pallas_call(
        functools.partial(_band_bias_kernel, n_valid=n_valid),
        grid=(heads,),
        in_specs=[pl.BlockSpec((None, 1, wr), lambda h: (h, 0, 0))],
        out_specs=pl.BlockSpec((None, tq, w), lambda h: (h, 0, 0)),
        out_shape=jax.ShapeDtypeStruct((heads, tq, w), F32),
        compiler_params=_params("arbitrary"),
        name="band_bias",
    )(r)


def _band_prompt_kernel(q_ref, k_ref, v_ref, bias_ref, o_ref, kp_ref, vp_ref,
                        *, scale, tq):
    seq, wd = k_ref.shape
    d = HEAD_DIM_AB
    nh = wd // d
    w = bias_ref.shape[1]
    hist = w - tq
    kp_ref[0:hist] = jnp.zeros((hist, wd), BF16)
    vp_ref[0:hist] = jnp.zeros((hist, wd), BF16)
    kp_ref[hist:hist + seq] = k_ref[...].astype(BF16)
    vp_ref[hist:hist + seq] = v_ref[...].astype(BF16)
    col = lax.broadcasted_iota(jnp.int32, (tq, w), 1)

    def body(i, c):
        start = pl.multiple_of(i * tq, tq)
        in_seq = col >= hist - start
        ss = []
        for h in range(nh):
            s = lax.dot_general(q_ref[pl.ds(start, tq), _head(h)],
                                kp_ref[pl.ds(start, w), _head(h)], _DN_T,
                                preferred_element_type=F32)
            s = s * scale + bias_ref[h * tq:(h + 1) * tq, :]
            ss.append(jnp.where(in_seq, s, NEG))
        ps, ls = _softmax_stage(ss)
        for h in range(nh):
            o = jnp.dot(ps[h], vp_ref[pl.ds(start, w), _head(h)],
                        preferred_element_type=F32) / ls[h]
            o_ref[pl.ds(start, tq), _head(h)] = o.astype(o_ref.dtype)
        return c

    lax.fori_loop(0, seq // tq, body, 0)


def _band_prompt(q, k, v, bias, batch, seq):
    m, wd = q.shape
    d = HEAD_DIM_AB
    nh = BAND_HEADS
    heads, tq, w = bias.shape
    bias = bias.reshape(heads // nh, nh * tq, w)
    blk = pl.BlockSpec((seq, nh * d), lambda b, g: (b, g))
    return pl.pallas_call(
        functools.partial(_band_prompt_kernel, scale=d ** -0.5, tq=tq),
        grid=(batch, heads // nh),
        in_specs=[blk, blk, blk,
                  pl.BlockSpec((None, nh * tq, w), lambda b, g: (g, 0, 0))],
        out_specs=blk,
        out_shape=jax.ShapeDtypeStruct((m, wd), BF16),
        scratch_shapes=[pltpu.VMEM((BAND_HIST + seq, nh * d), BF16),
                        pltpu.VMEM((BAND_HIST + seq, nh * d), BF16)],
        compiler_params=_params("arbitrary", "arbitrary"),
        name="band_prompt",
    )(q, k, v, bias)


def _roll_cache(new_ref, cache, new_rows_ref, t):
    past, nh, d = cache.shape
    new_ref[0:past - t] = cache[t:]
    for h in range(nh):
        new_ref[past - t:past, h, :] = new_rows_ref[:, _head(h, d)]


def _band_step_kernel(q_ref, kn_ref, vn_ref, kc_ref, vc_ref, bias_ref, o_ref,
                      nk_ref, nv_ref, kb_ref, vb_ref, *, scale):
    t = q_ref.shape[0]
    past, nh, d = kc_ref.shape
    w = bias_ref.shape[1]
    kc, vc = kc_ref[...], vc_ref[...]
    _roll_cache(nk_ref, kc, kn_ref, t)
    _roll_cache(nv_ref, vc, vn_ref, t)
    kb_ref[:, 0:past, :] = _heads_major(kc).astype(BF16)
    vb_ref[:, 0:past, :] = _heads_major(vc).astype(BF16)
    for h in range(nh):
        kb_ref[h, past:past + t] = kn_ref[:, _head(h)].astype(BF16)
        vb_ref[h, past:past + t] = vn_ref[:, _head(h)].astype(BF16)
        kb_ref[h, past + t:w] = jnp.zeros((w - past - t, d), BF16)
        vb_ref[h, past + t:w] = jnp.zeros((w - past - t, d), BF16)
    ss = [lax.dot_general(q_ref[:, _head(h)], kb_ref[h], _DN_T,
                          preferred_element_type=F32) * scale
          + bias_ref[h * t:(h + 1) * t, :] for h in range(nh)]
    ps, ls = _softmax_stage(ss)
    for h in range(nh):
        o = jnp.dot(ps[h], vb_ref[h], preferred_element_type=F32) / ls[h]
        o_ref[:, _head(h)] = o.astype(o_ref.dtype)


def _band_step(q, k_new, v_new, k_cache, v_cache, bias, batch):
    m, wd = q.shape
    _, past, heads, d = k_cache.shape
    t = m // batch
    nh = STEP_HEADS
    w = bias.shape[2]
    assert past == BAND_HIST and past + t <= w
    bias = bias.reshape(heads // nh, nh * t, w)
    row = pl.BlockSpec((t, nh * d), lambda b, g: (b, g))
    cache = pl.BlockSpec((None, past, nh, d), lambda b, g: (b, 0, g, 0))
    return pl.pallas_call(
        functools.partial(_band_step_kernel, scale=d ** -0.5),
        grid=(batch, heads // nh),
        in_specs=[row, row, row, cache, cache,
                  pl.BlockSpec((None, nh * t, w), lambda b, g: (g, 0, 0))],
        out_specs=[row, cache, cache],
        out_shape=[jax.ShapeDtypeStruct((m, wd), BF16),
                   jax.ShapeDtypeStruct(k_cache.shape, F32),
                   jax.ShapeDtypeStruct(v_cache.shape, F32)],
        scratch_shapes=[pltpu.VMEM((nh, w, d), BF16),
                        pltpu.VMEM((nh, w, d), BF16)],
        compiler_params=_params("arbitrary", "arbitrary"),
        name="band_step",
    )(q, k_new, v_new, k_cache, v_cache, bias)


def _split_pair(blk):
    rolled = pltpu.roll(blk, HEAD_DIM_C, 1)
    lane = lax.broadcasted_iota(jnp.int32, blk.shape, 1)
    low = lane < HEAD_DIM_C
    zero = jnp.zeros_like(blk)
    return [jnp.where(low, blk, zero), jnp.where(low, zero, rolled),
            jnp.where(low, rolled, zero), jnp.where(low, zero, blk)]


def _swa_stage(qs, kls, khs, vls, vhs, valid, sinks, scale):
    ss = []
    for q, kl, kh in zip(qs, kls, khs):
        for kx in (kl, kh):
            s = lax.dot_general(q, kx, _DN_T, preferred_element_type=F32)
            ss.append(jnp.where(valid, s * scale, NEG))
    ps, ls = _softmax_stage(ss, sinks)
    return [jnp.dot(ps[2 * i], vls[i], preferred_element_type=F32) / ls[2 * i]
            + jnp.dot(ps[2 * i + 1], vhs[i], preferred_element_type=F32)
            / ls[2 * i + 1] for i in range(len(qs))]


def _swa_prompt_kernel(sinks_ref, q_ref, k_ref, v_ref, o_ref, ks_ref, vs_ref,
                       *, scale, tq):
    hp = pl.program_id(1)
    seq = k_ref.shape[0]
    band = WINDOW + tq
    npair = q_ref.shape[1] // LANES
    group = npair // 2

    for src, dst in ((k_ref, ks_ref), (v_ref, vs_ref)):
        for n, x in enumerate(_split_pair(src[...])):
            dst[n, 0:WINDOW] = jnp.zeros((WINDOW, LANES), BF16)
            dst[n, WINDOW:WINDOW + seq] = x.astype(BF16)

    row = lax.broadcasted_iota(jnp.int32, (tq, band), 0)
    col = lax.broadcasted_iota(jnp.int32, (tq, band), 1)
    qc = _chunk_of(row)
    kc = _chunk_of(col)
    in_band = (kc >= qc) & (kc <= qc + SWA_PREV_CHUNKS)
    sinks = [sinks_ref[hp * 2 * npair + n] for n in range(2 * npair)]

    def body(i, c):
        start = pl.multiple_of(i * tq, tq)
        valid = in_band & (col >= WINDOW - start)
        kv = [[r[n, pl.ds(start, band), :] for n in range(4)]
              for r in (ks_ref, vs_ref)]
        qs = [q_ref[pl.ds(start, tq), _head(p, LANES)] for p in range(npair)]
        sel = [2 * (p // group) for p in range(npair)]
        outs = _swa_stage(qs, [kv[0][s] for s in sel], [kv[0][s + 1] for s in sel],
                          [kv[1][s] for s in sel], [kv[1][s + 1] for s in sel],
                          valid, sinks, scale)
        for p in range(npair):
            o_ref[pl.ds(start, tq), _head(p, LANES)] = outs[p].astype(o_ref.dtype)
        return c

    lax.fori_loop(0, seq // tq, body, 0)


def _swa_prompt(q, k, v, sinks, batch, seq):
    m, wq = q.shape
    wk = k.shape[1]
    npairs = wk // LANES
    qw = wq // npairs
    qo = pl.BlockSpec((seq, qw), lambda b, h, s: (b, h))
    kv = pl.BlockSpec((seq, LANES), lambda b, h, s: (b, h))
    grid_spec = pltpu.PrefetchScalarGridSpec(
        num_scalar_prefetch=1,
        grid=(batch, npairs),
        in_specs=[qo, kv, kv],
        out_specs=qo,
        scratch_shapes=[pltpu.VMEM((4, WINDOW + seq, LANES), BF16),
                        pltpu.VMEM((4, WINDOW + seq, LANES), BF16)],
    )
    return pl.pallas_call(
        functools.partial(_swa_prompt_kernel, scale=HEAD_DIM_C ** -0.5,
                          tq=SWA_TQ),
        grid_spec=grid_spec,
        out_shape=jax.ShapeDtypeStruct((m, wq), BF16),
        compiler_params=_params("arbitrary", "arbitrary"),
        name="swa_prompt",
    )(sinks, q, k, v)


def _swa_step_kernel(sinks_ref, q_ref, kn_ref, vn_ref, kc_ref, vc_ref, o_ref,
                     nk_ref, nv_ref, *, scale):
    t = q_ref.shape[0]
    past, wk = kc_ref.shape
    band = 2 * LANES
    nkv = wk // LANES
    npair = q_ref.shape[1] // LANES
    group = npair // (2 * nkv)
    kc, vc = kc_ref[...], vc_ref[...]
    kn, vn = kn_ref[...], vn_ref[...]
    nk_ref[0:past - t] = kc[t:]
    nv_ref[0:past - t] = vc[t:]
    nk_ref[past - t:past] = kn
    nv_ref[past - t:past] = vn
    pad = jnp.zeros((band - past - t, wk), F32)
    k_all = jnp.concatenate([kc, kn, pad], axis=0)
    v_all = jnp.concatenate([vc, vn, pad], axis=0)
    col = lax.broadcasted_iota(jnp.int32, (t, band), 1)
    valid = col < past + t
    kx = [[x.astype(BF16) for x in _split_pair(k_all[:, _head(g, LANES)])]
          for g in range(nkv)]
    vx = [[x.astype(BF16) for x in _split_pair(v_all[:, _head(g, LANES)])]
          for g in range(nkv)]
    qs = [q_ref[:, _head(p, LANES)] for p in range(npair)]
    where = [(p // (2 * group), 2 * ((p // group) % 2)) for p in range(npair)]
    sinks = [sinks_ref[n] for n in range(2 * npair)]
    outs = _swa_stage(qs, [kx[g][s] for g, s in where],
                      [kx[g][s + 1] for g, s in where],
                      [vx[g][s] for g, s in where],
                      [vx[g][s + 1] for g, s in where], valid, sinks, scale)
    for p in range(npair):
        o_ref[:, _head(p, LANES)] = outs[p].astype(o_ref.dtype)


def _swa_step(q, k_new, v_new, k_cache, v_cache, sinks, batch):
    m, wq = q.shape
    wk = k_new.shape[1]
    t = m // batch
    past = k_cache.shape[1]
    assert past + t <= 2 * LANES
    qrow = pl.BlockSpec((t, wq), lambda b, s: (b, 0))
    krow = pl.BlockSpec((t, wk), lambda b, s: (b, 0))
    cache = pl.BlockSpec((None, past, wk), lambda b, s: (b, 0, 0))
    grid_spec = pltpu.PrefetchScalarGridSpec(
        num_scalar_prefetch=1,
        grid=(batch,),
        in_specs=[qrow, krow, krow, cache, cache],
        out_specs=[qrow, cache, cache],
    )
    return pl.pallas_call(
        functools.partial(_swa_step_kernel, scale=HEAD_DIM_C ** -0.5),
        grid_spec=grid_spec,
        out_shape=[jax.ShapeDtypeStruct((m, wq), BF16),
                   jax.ShapeDtypeStruct(k_cache.shape, F32),
                   jax.ShapeDtypeStruct(v_cache.shape, F32)],
        compiler_params=_params("arbitrary"),
        name="swa_step",
    )(sinks, q, k_new, v_new, k_cache, v_cache)


def _rope_tables(pos):
    half = ROPE_DIM // 2
    inv = jnp.power(ROPE_THETA, -jnp.arange(half, dtype=F32) / half)
    ang = pos.astype(F32)[:, None] * inv[None, :]
    cos, sin = jnp.cos(ang), jnp.sin(ang)
    n = pos.shape[0]
    reps = LANES // HEAD_DIM_C
    rest = HEAD_DIM_C - ROPE_DIM
    ones = jnp.ones((n, rest), F32)
    zeros = jnp.zeros((n, rest), F32)
    zh = jnp.zeros((n, half), F32)
    c = jnp.tile(jnp.concatenate([cos, cos, ones], axis=1), (1, reps))
    s1 = jnp.tile(jnp.concatenate([zh, sin, zeros], axis=1), (1, reps))
    s2 = jnp.tile(jnp.concatenate([-sin, zh, zeros], axis=1), (1, reps))
    return c, s1, s2


def _ffn(hp, hs, pre, w_gate, w_up, w_down, layer, next_gain):
    f = w_gate.shape[-1]
    d = hp.shape[1]
    wg, wu = _Weight(w_gate, layer), _Weight(w_up, layer)
    a_p, ss_p, a_s, ss_s = pre
    act_p, wd = _gate_up(a_p, wg, wu, f, norm=_row_scale(ss_p, d),
                         cast=(w_down, layer))
    act_s = _gate_up(a_s, wg, wu, f, norm=_row_scale(ss_s, d))
    return _down(act_p, act_s, wd, hp, hs, gain=next_gain)


def kernel(x_prompt, x_sample, cache_sb_k, cache_sb_v, cache_band_k, cache_band_v, cache_swa_k, cache_swa_v, norm_mix, norm_ffn, norm_final, w_in_ab, w_out_ab, rel_bias, w_qkv_c, w_out_c, sinks, w_gate, w_up, w_down):
    bp, seq, dm = x_prompt.shape
    bs, t, _ = x_sample.shape
    depth = norm_mix.shape[0]
    past = cache_sb_k.shape[2]
    hp = x_prompt.reshape(bp * seq, dm)
    hs = x_sample.reshape(bs * t, dm)
    norm_mix3 = norm_mix.reshape(depth, 1, dm)
    norm_final3 = norm_final.reshape(1, 1, dm)

    w_sb = cache_sb_k.shape[3] * cache_sb_k.shape[4]
    w_band = cache_band_k.shape[3] * cache_band_k.shape[4]
    wq_c = w_out_c.shape[1]
    wk_c = cache_swa_k.shape[3] * cache_swa_k.shape[4]
    p_band = cache_band_k.shape[2]
    p_swa = cache_swa_k.shape[2]

    rope_p = _rope_tables(jnp.arange(seq))
    rope_s = tuple(jnp.tile(x, (bs, 1)) for x in _rope_tables(past + jnp.arange(t)))

    outs = {name: [] for name in (
        "sbk_p", "sbv_p", "bk_p", "bv_p", "ck_p", "cv_p",
        "sbk_s", "sbv_s", "bk_s", "bv_s", "ck_s", "cv_s")}

    pre = None
    for l in range(depth):
        if pre is None:
            a_p = _rmsnorm(hp, norm_mix3, l, BF16)
            a_s = _rmsnorm(hs, norm_mix3, l, BF16)
            norm = None
        else:
            a_p, ss_p, a_s, ss_s = pre
            norm = (_row_scale(ss_p, dm), _row_scale(ss_s, dm))
        if l % 2 == 0:
            e = l // 2
            cols = (0, w_sb, 2 * w_sb, 3 * w_sb, 3 * w_sb + w_band, 3 * w_sb + 2 * w_band)
            bias_p = _band_bias(rel_bias[e], BAND_TQ, BAND_HIST + BAND_TQ)
            bias_s = _band_bias(rel_bias[e], t, p_band + t)

            widths = (w_sb, w_sb, w_sb, w_band, w_band, w_band)
            dtypes = (BF16, F32, F32, BF16, F32, F32)

            w_in = [_Weight(w_in_ab, e, 0, c) for c in cols]
            w_out = (_Weight(w_out_ab, e, 0), _Weight(w_out_ab, e, 1))

            (qa, qa_s), (ka, ka_s), (va, va_s), (qb, qb_s), (kb, kb_s), (vb, vb_s) = (
                _proj(a_p, a_s, w, n, dt, norm=norm)
                for w, n, dt in zip(w_in, widths, dtypes))
            oa = _sb_prompt(qa, ka, va, bp, seq)
            ob = _band_prompt(qb, kb, vb, bias_p, bp, seq)
            oa_s = _sb_step(qa_s, ka_s, va_s, cache_sb_k[e], cache_sb_v[e], bs)
            ob_s, nbk, nbv = _band_step(qb_s, kb_s, vb_s, cache_band_k[e],
                                        cache_band_v[e], bias_s, bs)
            hp, hs, *pre = _mm_res(
                [(oa, oa_s, 0, w_out[0]), (ob, ob_s, 0, w_out[1])], w_sb,
                hp, hs, PROJ_TN, gain=norm_ffn[l])
            nb = min(BAND_HIST, seq)
            outs["sbk_p"].append(ka.reshape(bp, seq, *cache_sb_k.shape[3:]))
            outs["sbv_p"].append(va.reshape(bp, seq, *cache_sb_k.shape[3:]))
            outs["bk_p"].append(kb.reshape(bp, seq, *cache_band_k.shape[3:])[:, seq - nb:])
            outs["bv_p"].append(vb.reshape(bp, seq, *cache_band_k.shape[3:])[:, seq - nb:])
            outs["sbk_s"].append(ka_s.reshape(bs, t, *cache_sb_k.shape[3:]))
            outs["sbv_s"].append(va_s.reshape(bs, t, *cache_sb_k.shape[3:]))
            outs["bk_s"].append(nbk)
            outs["bv_s"].append(nbv)
        else:
            o = l // 2

            widths = (wq_c, wk_c, wk_c)
            dtypes = (BF16, F32, F32)

            w_in = [_Weight(w_qkv_c, o, 0, c) for c in (0, wq_c, wq_c + wk_c)]
            w_out = _Weight(w_out_c, o, 0)

            (q, q_s), (k, k_s), (v, v_s) = (
                _proj(a_p, a_s, w, n, dt, rope=rp, rope_s=rs, norm=norm)
                for w, n, dt, rp, rs in zip(w_in, widths, dtypes,
                                            (rope_p, rope_p, None),
                                            (rope_s, rope_s, None)))
            oc = _swa_prompt(q, k, v, sinks[o], bp, seq)
            oc_s, nck, ncv = _swa_step(
                q_s, k_s, v_s, cache_swa_k[o].reshape(bs, p_swa, wk_c),
                cache_swa_v[o].reshape(bs, p_swa, wk_c), sinks[o], bs)
            hp, hs, *pre = _mm_res([(oc, oc_s, 0, w_out)], wq_c, hp, hs,
                                   PROJ_TN, gain=norm_ffn[l])
            nw = min(WINDOW, seq)
            outs["ck_p"].append(k.reshape(bp, seq, *cache_swa_k.shape[3:])[:, seq - nw:])
            outs["cv_p"].append(v.reshape(bp, seq, *cache_swa_k.shape[3:])[:, seq - nw:])
            outs["ck_s"].append(nck.reshape(cache_swa_k.shape[1:]))
            outs["cv_s"].append(ncv.reshape(cache_swa_v.shape[1:]))
        next_gain = norm_mix[l + 1] if l + 1 < depth else None
        hp, hs, *pre = _ffn(hp, hs, pre, w_gate, w_up, w_down, l, next_gain)

    y_prompt = _rmsnorm(hp, norm_final3, 0, F32).reshape(bp, seq, dm)
    y_sample = _rmsnorm(hs, norm_final3, 0, F32).reshape(bs, t, dm)
    st = lambda name: jnp.stack(outs[name])
    return (y_prompt, y_sample,
            st("sbk_p"), st("sbv_p"), st("bk_p"), st("bv_p"), st("ck_p"), st("cv_p"),
            st("sbk_s"), st("sbv_s"), st("bk_s"), st("bv_s"), st("ck_s"), st("cv_s"))
```

```python
import functools
import math

import numpy as np
import jax
import jax.numpy as jnp
from jax import lax
from jax.experimental import pallas as pl
from jax.experimental.pallas import tpu as pltpu

F32 = jnp.float32
BF16 = jnp.bfloat16

CHUNK = 64
EPS = 1e-6
HEAD_DIM_AB = 128
BAND_PREV_CHUNKS = 8
BAND_HIST = BAND_PREV_CHUNKS * CHUNK
REL_CLIP = 2 * CHUNK
HEAD_DIM_C = 64
N_KV_C = 8
SWA_PREV_CHUNKS = 2
WINDOW = SWA_PREV_CHUNKS * CHUNK
ROPE_THETA = 500000.0
ROPE_DIM = HEAD_DIM_C // 4

LANES = 128
V7X_VMEM_LIMIT_BYTES = 60 * 1024 * 1024

DENSE_TM = 1024
GATE_UP_TM = 2048
DENSE_TN = 256
PROJ_TN = 512

NEG = -1e30

SB_TILE = 256
SB_HEADS = 4
SB_STEP_KEYS = 128
BAND_TQ = 128
BAND_HEADS = 4
STEP_HEADS = 8
SWA_TQ = 128


def _params(*sem):
    return pltpu.CompilerParams(dimension_semantics=sem,
                                vmem_limit_bytes=V7X_VMEM_LIMIT_BYTES)


def _chunk_of(pos):
    return lax.shift_right_logical(pos, int(math.log2(CHUNK)))


def _pick_tm(m):
    return DENSE_TM if m % DENSE_TM == 0 else m


def _rmsnorm_kernel(x_ref, g_ref, o_ref):
    x = x_ref[...]
    ms = jnp.mean(x * x, axis=-1, keepdims=True)
    y = x * lax.rsqrt(ms + EPS)
    o_ref[...] = (y * g_ref[...]).astype(o_ref.dtype)


def _rmsnorm(x, g3, layer, out_dtype):
    m, d = x.shape
    tm = 256
    return pl.pallas_call(
        _rmsnorm_kernel,
        grid=(m // tm,),
        in_specs=[pl.BlockSpec((tm, d), lambda i: (i, 0)),
                  pl.BlockSpec((None, 1, d), lambda i: (layer, 0, 0))],
        out_specs=pl.BlockSpec((tm, d), lambda i: (i, 0)),
        out_shape=jax.ShapeDtypeStruct((m, d), out_dtype),
        compiler_params=_params("arbitrary"),
        name="rmsnorm",
    )(x, g3)


class _Weight:
    def __init__(self, src, layer, kblk=0, col0=0):
        self.src, self.layer, self.kblk, self.col0 = src, layer, kblk, col0

    def spec(self, tk, tn):
        layer, kb, jb = self.layer, self.kblk, self.col0 // tn
        return pl.BlockSpec((None, tk, tn), lambda i, j: (layer, kb, jb + j))


def _rope_tile(x, c, s1, s2):
    return (x * c + pltpu.roll(x, 8, 1) * s1
            + pltpu.roll(x, LANES - 8, 1) * s2)


def _rider_in(x):
    return pl.BlockSpec(x.shape, lambda i, j: (0, 0), pipeline_mode=pl.Buffered(1))


def _rider_out(rows, tn, ni):
    return pl.BlockSpec((rows, tn),
                        lambda i, j: (0, jnp.where(i == ni - 1, j, 0)))


def _on_last_row_block(fn):
    pl.when(pl.program_id(0) == pl.num_programs(0) - 1)(fn)


def _lane_chunks(n):
    return [slice(c * LANES, (c + 1) * LANES) for c in range(n // LANES)]


def _proj_kernel(*refs, rope, norm):
    w_ref = refs[1]
    n_tab = 3 if rope else 0
    a_ref, tabs = refs[0], refs[2:2 + n_tab]
    sa_ref, stabs = refs[2 + n_tab], refs[3 + n_tab:3 + 2 * n_tab]
    r_ref, sr_ref = refs[3 + 2 * n_tab:5 + 2 * n_tab] if norm else (None, None)
    o_ref, so_ref = refs[-2:]

    def run(a_ref, tabs, r_ref, o_ref):
        acc = jnp.dot(a_ref[...], w_ref[...].astype(BF16),
                      preferred_element_type=F32)
        for sl in _lane_chunks(acc.shape[1]):
            x = acc[:, sl]
            if norm:
                x = x * r_ref[...]
            if rope:
                x = _rope_tile(x, *(t[...] for t in tabs))
            o_ref[:, sl] = x.astype(o_ref.dtype)

    run(a_ref, tabs, r_ref, o_ref)
    _on_last_row_block(lambda: run(sa_ref, stabs, sr_ref, so_ref))


def _proj(a, a_s, w, ncols, out_dtype, rope=None, rope_s=None, norm=None):
    m, k = a.shape
    ms = a_s.shape[0]
    tm = _pick_tm(m)
    tn = min(PROJ_TN, ncols)
    ni = m // tm
    in_specs = [pl.BlockSpec((tm, k), lambda i, j: (i, 0)), w.spec(k, tn)]
    args = [a, w.src]
    if rope is not None:
        nrep = rope[0].shape[0] // tm
        for t in rope:
            in_specs.append(pl.BlockSpec((tm, LANES), lambda i, j: (i % nrep, 0)))
            args.append(t)
    for x in (a_s,) + tuple(rope_s or ()):
        in_specs.append(_rider_in(x))
        args.append(x)
    if norm is not None:
        r, r_s = norm
        in_specs += [pl.BlockSpec((tm, LANES), lambda i, j: (i, 0)), _rider_in(r_s)]
        args += [r, r_s]
    return pl.pallas_call(
        functools.partial(_proj_kernel, rope=rope is not None,
                          norm=norm is not None),
        grid=(ni, ncols // tn),
        in_specs=in_specs,
        out_specs=[pl.BlockSpec((tm, tn), lambda i, j: (i, j)),
                   _rider_out(ms, tn, ni)],
        out_shape=[jax.ShapeDtypeStruct((m, ncols), out_dtype),
                   jax.ShapeDtypeStruct((ms, ncols), out_dtype)],
        compiler_params=_params("arbitrary", "arbitrary"),
        name="proj",
    )(*args)


def _gate_up_kernel(*refs, norm, cast):
    a_ref, wg_ref, wu_ref = refs[:3]
    r_ref = refs[3] if norm else None
    o_ref = refs[-2] if cast else refs[-1]
    a = a_ref[...]
    g = jnp.dot(a, wg_ref[...].astype(BF16), preferred_element_type=F32)
    u = jnp.dot(a, wu_ref[...].astype(BF16), preferred_element_type=F32)
    for sl in _lane_chunks(g.shape[1]):
        gc, uc = g[:, sl], u[:, sl]
        if norm:
            gc, uc = gc * r_ref[...], uc * r_ref[...]
        o_ref[:, sl] = (gc * jax.nn.sigmoid(gc) * uc).astype(o_ref.dtype)
    if cast:
        refs[-1][...] = refs[-3][...].astype(BF16)


def _gate_up(a, wg, wu, f, norm=None, cast=None):
    m, k = a.shape
    tm = GATE_UP_TM if m % GATE_UP_TM == 0 else m
    tn = DENSE_TN
    ni, nj = m // tm, f // tn
    in_specs = [pl.BlockSpec((tm, k), lambda i, j: (i, 0)),
                wg.spec(k, tn), wu.spec(k, tn)]
    args = [a, wg.src, wu.src]
    out_specs = [pl.BlockSpec((tm, tn), lambda i, j: (i, j))]
    out_shape = [jax.ShapeDtypeStruct((m, f), BF16)]
    if norm is not None:
        in_specs.append(pl.BlockSpec((tm, LANES), lambda i, j: (i, 0),
                                     pipeline_mode=pl.Buffered(1)))
        args.append(norm)
    if cast is not None:
        w3, layer = cast
        rows, n = w3.shape[1] // (ni * nj), w3.shape[2]
        assert rows * ni * nj == w3.shape[1] and rows % 16 == 0
        in_specs.append(pl.BlockSpec((None, rows, n),
                                     lambda i, j: (layer, i * nj + j, 0)))
        args.append(w3)
        out_specs.append(pl.BlockSpec((rows, n), lambda i, j: (i * nj + j, 0)))
        out_shape.append(jax.ShapeDtypeStruct(w3.shape[1:], BF16))
    out = pl.pallas_call(
        functools.partial(_gate_up_kernel, norm=norm is not None,
                          cast=cast is not None),
        grid=(ni, nj),
        in_specs=in_specs,
        out_specs=out_specs,
        out_shape=out_shape,
        compiler_params=_params("arbitrary", "arbitrary"),
        name="gate_up",
    )(*args)
    return out if cast is not None else out[0]


def _mm_res_kernel(*refs, nterms, stats):
    n = nterms
    w_refs = refs[n:2 * n]
    gain_ref = refs[3 * n + 2] if stats else None
    outs = refs[3 * n + 2 + bool(stats):]

    def run(a_refs, res_ref, o_ref, b_ref, ss_ref):
        acc = res_ref[...]
        for a_ref, w_ref in zip(a_refs, w_refs):
            acc = acc + jnp.dot(a_ref[...], w_ref[...].astype(BF16),
                                preferred_element_type=F32)
        o_ref[...] = acc
        if stats:
            _row_stats(acc, gain_ref, b_ref, ss_ref)

    p_extra = (outs[2], outs[3]) if stats else (None, None)
    s_extra = (outs[4], outs[5]) if stats else (None, None)
    run(refs[:n], refs[2 * n], outs[0], *p_extra)
    _on_last_row_block(
        lambda: run(refs[2 * n + 1:3 * n + 1], refs[3 * n + 1], outs[1], *s_extra))


def _mm_res(terms, tk, res, res_s, tn, gain=None):
    stats = gain is not None
    m, n = res.shape
    ms = res_s.shape[0]
    tm = _pick_tm(m)
    ni = m // tm
    in_specs, args = [], []
    for a, _, ka, _ in terms:
        in_specs.append(pl.BlockSpec((tm, tk), lambda i, j, ka=ka: (i, ka)))
        args.append(a)
    for _, _, _, w in terms:
        in_specs.append(w.spec(tk, tn))
        args.append(w.src)
    tile = pl.BlockSpec((tm, tn), lambda i, j: (i, j))
    tile_s = _rider_out(ms, tn, ni)
    in_specs.append(tile)
    args.append(res)
    for _, a_s, ka, _ in terms:
        in_specs.append(pl.BlockSpec((ms, tk), lambda i, j, ka=ka: (0, ka),
                                     pipeline_mode=pl.Buffered(1)))
        args.append(a_s)
    in_specs.append(tile_s)
    args.append(res_s)
    out_specs = [tile, tile_s]
    out_shape = [jax.ShapeDtypeStruct((m, n), F32),
                 jax.ShapeDtypeStruct((ms, n), F32)]
    if stats:
        in_specs.append(pl.BlockSpec((1, tn), lambda i, j: (0, j)))
        args.append(gain.astype(F32).reshape(1, n))
        out_specs += [tile, pl.BlockSpec((tm, LANES), lambda i, j: (i, 0)),
                      tile_s, pl.BlockSpec((ms, LANES), lambda i, j: (0, 0))]
        out_shape += [jax.ShapeDtypeStruct((m, n), BF16),
                      jax.ShapeDtypeStruct((m, LANES), F32),
                      jax.ShapeDtypeStruct((ms, n), BF16),
                      jax.ShapeDtypeStruct((ms, LANES), F32)]
    return pl.pallas_call(
        functools.partial(_mm_res_kernel, nterms=len(terms), stats=stats),
        grid=(ni, n // tn),
        in_specs=in_specs,
        out_specs=out_specs,
        out_shape=out_shape,
        compiler_params=_params("arbitrary", "arbitrary"),
        name="mm_res",
    )(*args)


def _row_stats(acc, gain_ref, b_ref, ss_ref):
    b_ref[...] = (acc * gain_ref[...]).astype(BF16)
    part = sum(acc[:, sl] * acc[:, sl] for sl in _lane_chunks(acc.shape[1]))
    j = pl.program_id(1)

    @pl.when(j == 0)
    def _():
        ss_ref[...] = part

    @pl.when(j != 0)
    def _():
        ss_ref[...] += part


def _down_kernel(*refs, stats):
    a_refs, w_ref, res_ref = refs[0:2], refs[2], refs[3]
    sa_refs, sres_ref = refs[4:6], refs[6]
    gain_ref = refs[7] if stats else None
    outs = refs[7 + bool(stats):]
    k = pl.program_id(2)

    def run(x_refs, r_ref, o_ref, b_ref, ss_ref):
        @pl.when(k == 0)
        def _():
            o_ref[...] = r_ref[...] + jnp.dot(
                x_refs[0][...], w_ref[...].astype(BF16),
                preferred_element_type=F32)

        @pl.when(k == 1)
        def _():
            acc = o_ref[...] + jnp.dot(
                x_refs[1][...], w_ref[...].astype(BF16),
                preferred_element_type=F32)
            o_ref[...] = acc
            if stats:
                _row_stats(acc, gain_ref, b_ref, ss_ref)

    p_extra = (outs[2], outs[3]) if stats else (None, None)
    s_extra = (outs[4], outs[5]) if stats else (None, None)
    run(a_refs, res_ref, outs[0], *p_extra)
    _on_last_row_block(lambda: run(sa_refs, sres_ref, outs[1], *s_extra))


def _down(act, act_s, w, res, res_s, gain=None):
    stats = gain is not None
    m, n = res.shape
    ms = res_s.shape[0]
    tk = act.shape[1] // 2
    tm, tn = _pick_tm(m), PROJ_TN
    ni = m // tm
    one = dict(pipeline_mode=pl.Buffered(1))
    last = ni - 1
    tile = pl.BlockSpec((tm, tn), lambda i, j, k: (i, j))
    tile_s = pl.BlockSpec((ms, tn),
                          lambda i, j, k: (0, jnp.where(i == last, j, 0)))
    in_specs = (
        [pl.BlockSpec((tm, tk), lambda i, j, k, h=h: (i, h), **one) for h in (0, 1)]
        + [pl.BlockSpec((tk, tn), lambda i, j, k: (k, j)), tile]
        + [pl.BlockSpec((ms, tk), lambda i, j, k, h=h: (0, h), **one) for h in (0, 1)]
        + [tile_s])
    args = [act, act, w, res, act_s, act_s, res_s]
    out_specs = [tile, tile_s]
    out_shape = [jax.ShapeDtypeStruct((m, n), F32),
                 jax.ShapeDtypeStruct((ms, n), F32)]
    if stats:
        in_specs.append(pl.BlockSpec((1, tn), lambda i, j, k: (0, j)))
        args.append(gain.astype(F32).reshape(1, n))
        out_specs += [tile, pl.BlockSpec((tm, LANES), lambda i, j, k: (i, 0)),
                      tile_s, pl.BlockSpec((ms, LANES), lambda i, j, k: (0, 0))]
        out_shape += [jax.ShapeDtypeStruct((m, n), BF16),
                      jax.ShapeDtypeStruct((m, LANES), F32),
                      jax.ShapeDtypeStruct((ms, n), BF16),
                      jax.ShapeDtypeStruct((ms, LANES), F32)]
    return pl.pallas_call(
        functools.partial(_down_kernel, stats=stats),
        grid=(ni, n // tn, 2),
        in_specs=in_specs,
        out_specs=out_specs,
        out_shape=out_shape,
        compiler_params=_params("arbitrary", "arbitrary", "arbitrary"),
        name="down",
    )(*args)


def _row_scale_kernel(ss_ref, o_ref, *, d):
    tot = jnp.sum(ss_ref[...], axis=1, keepdims=True)
    o_ref[...] = jnp.broadcast_to(lax.rsqrt(tot / d + EPS), o_ref.shape)


def _row_scale(ss, d):
    m = ss.shape[0]
    tm = _pick_tm(m)
    blk = pl.BlockSpec((tm, LANES), lambda i: (i, 0))
    return pl.pallas_call(
        functools.partial(_row_scale_kernel, d=d),
        grid=(m // tm,),
        in_specs=[blk],
        out_specs=blk,
        out_shape=jax.ShapeDtypeStruct(ss.shape, F32),
        compiler_params=_params("arbitrary"),
        name="row_scale",
    )(ss)


_DN_T = (((1,), (1,)), ((), ()))


def _head(h, d=HEAD_DIM_AB):
    return slice(h * d, (h + 1) * d)


def _heads_major(x):
    return pltpu.einshape("phd->hpd", x)


def _softmax_stage(ss, sinks=None):
    ms = [jnp.max(s, axis=1, keepdims=True) for s in ss]
    if sinks is not None:
        ms = [jnp.maximum(m, sk) for m, sk in zip(ms, sinks)]
    ps = [jnp.exp(s - m) for s, m in zip(ss, ms)]
    ls = [jnp.sum(p, axis=1, keepdims=True) for p in ps]
    if sinks is not None:
        ls = [l + jnp.exp(sk - m) for l, sk, m in zip(ls, sinks, ms)]
    return [p.astype(BF16) for p in ps], ls


def _sb_logs(z, valid):
    lb = jnp.minimum(z, 0.0) - jnp.log(1.0 + jnp.exp(-jnp.abs(z)))
    l1m = lb - z
    if valid is not None:
        l1m = jnp.where(valid, l1m, 0.0)
    return lb, l1m


def _split_hi_lo(x):
    hi = x.astype(BF16)
    lo = (x - hi.astype(F32)).astype(BF16)
    return jnp.concatenate([hi, lo], axis=1)


def _suffix_matrix(n):
    u = np.tril(np.ones((n, n), np.float32), -1)
    return jnp.asarray(np.concatenate([u, u], axis=0), BF16)


def _sb_prompt_kernel(q_ref, k_ref, v_ref, uu_ref, o_ref, kb_ref, vb_ref,
                      oacc_ref, cacc_ref, *, scale):
    qi = pl.program_id(2)
    tq = q_ref.shape[0]
    nh = oacc_ref.shape[0]

    @pl.when(qi == 0)
    def _():
        kb_ref[...] = k_ref[...].astype(BF16)
        vb_ref[...] = v_ref[...].astype(BF16)

    oacc_ref[...] = jnp.zeros(oacc_ref.shape, F32)
    cacc_ref[...] = jnp.zeros(cacc_ref.shape, F32)

    def tile(j, masked):
        start = pl.multiple_of(j * tq, tq)
        valid = None
        if masked:
            row = lax.broadcasted_iota(jnp.int32, (tq, tq), 0)
            col = lax.broadcasted_iota(jnp.int32, (tq, tq), 1)
            valid = col < row
        zs = [lax.dot_general(q_ref[:, _head(h)],
                              kb_ref[pl.ds(start, tq), _head(h)], _DN_T,
                              preferred_element_type=F32) * scale
              for h in range(nh)]
        logs = [_sb_logs(z, valid) for z in zs]
        sufs = [jnp.dot(_split_hi_lo(l1m), uu_ref[...],
                        preferred_element_type=F32) for _, l1m in logs]
        ws = [jnp.exp(lb + suf + cacc_ref[h])
              for h, ((lb, _), suf) in enumerate(zip(logs, sufs))]
        if masked:
            ws = [jnp.where(valid, w, 0.0) for w in ws]
        for h in range(nh):
            oacc_ref[h] += jnp.dot(ws[h].astype(BF16),
                                   vb_ref[pl.ds(start, tq), _head(h)],
                                   preferred_element_type=F32)
            cacc_ref[h] += jnp.sum(logs[h][1], axis=1, keepdims=True)

    tile(qi, True)

    def body(t, c):
        tile(qi - 1 - t, False)
        return c

    lax.fori_loop(0, qi, body, 0)
    for h in range(nh):
        o_ref[:, _head(h)] = oacc_ref[h].astype(o_ref.dtype)


def _sb_prompt(q, k, v, batch, seq):
    m, w = q.shape
    d = HEAD_DIM_AB
    nh = SB_HEADS
    tq = SB_TILE
    nq = seq // tq
    qo = pl.BlockSpec((tq, nh * d), lambda b, g, i: (b * nq + i, g))
    kv = pl.BlockSpec((seq, nh * d), lambda b, g, i: (b, g))
    return pl.pallas_call(
        functools.partial(_sb_prompt_kernel, scale=d ** -0.5),
        grid=(batch, w // (nh * d), nq),
        in_specs=[qo, kv, kv, pl.BlockSpec((2 * tq, tq), lambda b, g, i: (0, 0))],
        out_specs=qo,
        out_shape=jax.ShapeDtypeStruct((m, w), BF16),
        scratch_shapes=[pltpu.VMEM((seq, nh * d), BF16),
                        pltpu.VMEM((seq, nh * d), BF16),
                        pltpu.VMEM((nh, tq, d), F32),
                        pltpu.VMEM((nh, tq, 1), F32)],
        compiler_params=_params("arbitrary", "arbitrary", "arbitrary"),
        name="sb_prompt",
    )(q, k, v, _suffix_matrix(tq))


def _sb_step_kernel(q_ref, kn_ref, vn_ref, kc_ref, vc_ref, uu_ref, o_ref,
                    kb_ref, vb_ref, *, scale):
    t = q_ref.shape[0]
    past, nh, d = kc_ref.shape
    tk = uu_ref.shape[1]
    n = past + tk
    kb_ref[:, 0:past, :] = _heads_major(kc_ref[...]).astype(BF16)
    vb_ref[:, 0:past, :] = _heads_major(vc_ref[...]).astype(BF16)
    for h in range(nh):
        kb_ref[h, past:past + t] = kn_ref[:, _head(h)].astype(BF16)
        vb_ref[h, past:past + t] = vn_ref[:, _head(h)].astype(BF16)
        kb_ref[h, past + t:n] = jnp.zeros((tk - t, d), BF16)
        vb_ref[h, past + t:n] = jnp.zeros((tk - t, d), BF16)

    z = jnp.concatenate(
        [lax.dot_general(q_ref[:, _head(h)], kb_ref[h], _DN_T,
                         preferred_element_type=F32) for h in range(nh)],
        axis=0) * scale
    row = lax.broadcasted_iota(jnp.int32, z.shape, 0) & (t - 1)
    col = lax.broadcasted_iota(jnp.int32, z.shape, 1)
    valid = col < past + row
    lb, l1m = _sb_logs(z, valid)
    ntiles = n // tk
    sufs = [jnp.dot(_split_hi_lo(l1m[:, j * tk:(j + 1) * tk]), uu_ref[...],
                    preferred_element_type=F32) for j in range(ntiles)]
    tots = [jnp.sum(l1m[:, j * tk:(j + 1) * tk], axis=1, keepdims=True)
            for j in range(ntiles)]
    carries = [None] * ntiles
    carry = jnp.zeros_like(tots[0])
    for j in reversed(range(ntiles)):
        carries[j] = carry
        carry = carry + tots[j]
    after = jnp.concatenate([sufs[j] + carries[j] for j in range(ntiles)],
                            axis=1)
    w = jnp.where(valid, jnp.exp(lb + after), 0.0).astype(BF16)
    for h in range(nh):
        o = jnp.dot(w[h * t:(h + 1) * t], vb_ref[h], preferred_element_type=F32)
        o_ref[:, _head(h)] = o.astype(o_ref.dtype)


def _sb_step(q, k_new, v_new, k_cache, v_cache, batch):
    m, w = q.shape
    _, past, heads, d = k_cache.shape
    t = m // batch
    tk = SB_STEP_KEYS
    nh = STEP_HEADS
    assert past % tk == 0 and t <= tk and t & (t - 1) == 0
    row = pl.BlockSpec((t, nh * d), lambda b, g: (b, g))
    cache = pl.BlockSpec((None, past, nh, d), lambda b, g: (b, 0, g, 0))
    return pl.pallas_call(
        functools.partial(_sb_step_kernel, scale=d ** -0.5),
        grid=(batch, heads // nh),
        in_specs=[row, row, row, cache, cache,
                  pl.BlockSpec((2 * tk, tk), lambda b, g: (0, 0))],
        out_specs=row,
        out_shape=jax.ShapeDtypeStruct((m, w), BF16),
        scratch_shapes=[pltpu.VMEM((nh, past + tk, d), BF16),
                        pltpu.VMEM((nh, past + tk, d), BF16)],
        compiler_params=_params("arbitrary", "arbitrary"),
        name="sb_step",
    )(q, k_new, v_new, k_cache, v_cache, _suffix_matrix(tk))


def _band_bias_kernel(r_ref, o_ref, *, n_valid):
    tq, w = o_ref.shape
    wr = r_ref.shape[1]
    t = jnp.broadcast_to(r_ref[...], (tq, wr))
    t = pltpu.roll(t, 0, 1, stride=1, stride_axis=0)
    row = lax.broadcasted_iota(jnp.int32, (tq, w), 0)
    col = lax.broadcasted_iota(jnp.int32, (tq, w), 1)
    qc = _chunk_of(row)
    kc = _chunk_of(col)
    valid = (kc >= qc) & (kc <= qc + BAND_PREV_CHUNKS) & (col < n_valid)
    o_ref[...] = jnp.where(valid, t[:, :w], NEG)


def _band_bias(rel, tq, n_valid):
    heads = rel.shape[0]
    w = BAND_HIST + BAND_TQ
    wr = w + LANES
    p = np.arange(wr)
    delta = np.where(p < w, p, p - wr)
    idx = np.clip(BAND_HIST - delta, -REL_CLIP, REL_CLIP) + REL_CLIP
    r = jnp.take(rel.astype(F32), jnp.asarray(idx, jnp.int32), axis=1)
    r = r.reshape(heads, 1, wr)
    return pl.pallas_call(
        functools.partial(_band_bias_kernel, n_valid=n_valid),
        grid=(heads,),
        in_specs=[pl.BlockSpec((None, 1, wr), lambda h: (h, 0, 0))],
        out_specs=pl.BlockSpec((None, tq, w), lambda h: (h, 0, 0)),
        out_shape=jax.ShapeDtypeStruct((heads, tq, w), F32),
        compiler_params=_params("arbitrary"),
        name="band_bias",
    )(r)


def _band_prompt_kernel(q_ref, k_ref, v_ref, bias_ref, o_ref, kp_ref, vp_ref,
                        *, scale, tq):
    seq, wd = k_ref.shape
    d = HEAD_DIM_AB
    nh = wd // d
    w = bias_ref.shape[1]
    hist = w - tq
    kp_ref[0:hist] = jnp.zeros((hist, wd), BF16)
    vp_ref[0:hist] = jnp.zeros((hist, wd), BF16)
    kp_ref[hist:hist + seq] = k_ref[...].astype(BF16)
    vp_ref[hist:hist + seq] = v_ref[...].astype(BF16)
    col = lax.broadcasted_iota(jnp.int32, (tq, w), 1)

    def body(i, c):
        start = pl.multiple_of(i * tq, tq)
        in_seq = col >= hist - start
        ss = []
        for h in range(nh):
            s = lax.dot_general(q_ref[pl.ds(start, tq), _head(h)],
                                kp_ref[pl.ds(start, w), _head(h)], _DN_T,
                                preferred_element_type=F32)
            s = s * scale + bias_ref[h * tq:(h + 1) * tq, :]
            ss.append(jnp.where(in_seq, s, NEG))
        ps, ls = _softmax_stage(ss)
        for h in range(nh):
            o = jnp.dot(ps[h], vp_ref[pl.ds(start, w), _head(h)],
                        preferred_element_type=F32) / ls[h]
            o_ref[pl.ds(start, tq), _head(h)] = o.astype(o_ref.dtype)
        return c

    lax.fori_loop(0, seq // tq, body, 0)


def _band_prompt(q, k, v, bias, batch, seq):
    m, wd = q.shape
    d = HEAD_DIM_AB
    nh = BAND_HEADS
    heads, tq, w = bias.shape
    bias = bias.reshape(heads // nh, nh * tq, w)
    blk = pl.BlockSpec((seq, nh * d), lambda b, g: (b, g))
    return pl.pallas_call(
        functools.partial(_band_prompt_kernel, scale=d ** -0.5, tq=tq),
        grid=(batch, heads // nh),
        in_specs=[blk, blk, blk,
                  pl.BlockSpec((None, nh * tq, w), lambda b, g: (g, 0, 0))],
        out_specs=blk,
        out_shape=jax.ShapeDtypeStruct((m, wd), BF16),
        scratch_shapes=[pltpu.VMEM((BAND_HIST + seq, nh * d), BF16),
                        pltpu.VMEM((BAND_HIST + seq, nh * d), BF16)],
        compiler_params=_params("arbitrary", "arbitrary"),
        name="band_prompt",
    )(q, k, v, bias)


def _roll_cache(new_ref, cache, new_rows_ref, t):
    past, nh, d = cache.shape
    new_ref[0:past - t] = cache[t:]
    for h in range(nh):
        new_ref[past - t:past, h, :] = new_rows_ref[:, _head(h, d)]


def _band_step_kernel(q_ref, kn_ref, vn_ref, kc_ref, vc_ref, bias_ref, o_ref,
                      nk_ref, nv_ref, kb_ref, vb_ref, *, scale):
    t = q_ref.shape[0]
    past, nh, d = kc_ref.shape
    w = bias_ref.shape[1]
    kc, vc = kc_ref[...], vc_ref[...]
    _roll_cache(nk_ref, kc, kn_ref, t)
    _roll_cache(nv_ref, vc, vn_ref, t)
    kb_ref[:, 0:past, :] = _heads_major(kc).astype(BF16)
    vb_ref[:, 0:past, :] = _heads_major(vc).astype(BF16)
    for h in range(nh):
        kb_ref[h, past:past + t] = kn_ref[:, _head(h)].astype(BF16)
        vb_ref[h, past:past + t] = vn_ref[:, _head(h)].astype(BF16)
        kb_ref[h, past + t:w] = jnp.zeros((w - past - t, d), BF16)
        vb_ref[h, past + t:w] = jnp.zeros((w - past - t, d), BF16)
    ss = [lax.dot_general(q_ref[:, _head(h)], kb_ref[h], _DN_T,
                          preferred_element_type=F32) * scale
          + bias_ref[h * t:(h + 1) * t, :] for h in range(nh)]
    ps, ls = _softmax_stage(ss)
    for h in range(nh):
        o = jnp.dot(ps[h], vb_ref[h], preferred_element_type=F32) / ls[h]
        o_ref[:, _head(h)] = o.astype(o_ref.dtype)


def _band_step(q, k_new, v_new, k_cache, v_cache, bias, batch):
    m, wd = q.shape
    _, past, heads, d = k_cache.shape
    t = m // batch
    nh = STEP_HEADS
    w = bias.shape[2]
    assert past == BAND_HIST and past + t <= w
    bias = bias.reshape(heads // nh, nh * t, w)
    row = pl.BlockSpec((t, nh * d), lambda b, g: (b, g))
    cache = pl.BlockSpec((None, past, nh, d), lambda b, g: (b, 0, g, 0))
    return pl.pallas_call(
        functools.partial(_band_step_kernel, scale=d ** -0.5),
        grid=(batch, heads // nh),
        in_specs=[row, row, row, cache, cache,
                  pl.BlockSpec((None, nh * t, w), lambda b, g: (g, 0, 0))],
        out_specs=[row, cache, cache],
        out_shape=[jax.ShapeDtypeStruct((m, wd), BF16),
                   jax.ShapeDtypeStruct(k_cache.shape, F32),
                   jax.ShapeDtypeStruct(v_cache.shape, F32)],
        scratch_shapes=[pltpu.VMEM((nh, w, d), BF16),
                        pltpu.VMEM((nh, w, d), BF16)],
        compiler_params=_params("arbitrary", "arbitrary"),
        name="band_step",
    )(q, k_new, v_new, k_cache, v_cache, bias)


def _split_pair(blk):
    rolled = pltpu.roll(blk, HEAD_DIM_C, 1)
    lane = lax.broadcasted_iota(jnp.int32, blk.shape, 1)
    low = lane < HEAD_DIM_C
    zero = jnp.zeros_like(blk)
    return [jnp.where(low, blk, zero), jnp.where(low, zero, rolled),
            jnp.where(low, rolled, zero), jnp.where(low, zero, blk)]


def _swa_stage(qs, kls, khs, vls, vhs, valid, sinks, scale):
    ss = []
    for q, kl, kh in zip(qs, kls, khs):
        for kx in (kl, kh):
            s = lax.dot_general(q, kx, _DN_T, preferred_element_type=F32)
            ss.append(jnp.where(valid, s * scale, NEG))
    ps, ls = _softmax_stage(ss, sinks)
    return [jnp.dot(ps[2 * i], vls[i], preferred_element_type=F32) / ls[2 * i]
            + jnp.dot(ps[2 * i + 1], vhs[i], preferred_element_type=F32)
            / ls[2 * i + 1] for i in range(len(qs))]


def _swa_prompt_kernel(sinks_ref, q_ref, k_ref, v_ref, o_ref, ks_ref, vs_ref,
                       *, scale, tq):
    hp = pl.program_id(1)
    seq = k_ref.shape[0]
    band = WINDOW + tq
    npair = q_ref.shape[1] // LANES
    group = npair // 2

    for src, dst in ((k_ref, ks_ref), (v_ref, vs_ref)):
        for n, x in enumerate(_split_pair(src[...])):
            dst[n, 0:WINDOW] = jnp.zeros((WINDOW, LANES), BF16)
            dst[n, WINDOW:WINDOW + seq] = x.astype(BF16)

    row = lax.broadcasted_iota(jnp.int32, (tq, band), 0)
    col = lax.broadcasted_iota(jnp.int32, (tq, band), 1)
    qc = _chunk_of(row)
    kc = _chunk_of(col)
    in_band = (kc >= qc) & (kc <= qc + SWA_PREV_CHUNKS)
    sinks = [sinks_ref[hp * 2 * npair + n] for n in range(2 * npair)]

    def body(i, c):
        start = pl.multiple_of(i * tq, tq)
        valid = in_band & (col >= WINDOW - start)
        kv = [[r[n, pl.ds(start, band), :] for n in range(4)]
              for r in (ks_ref, vs_ref)]
        qs = [q_ref[pl.ds(start, tq), _head(p, LANES)] for p in range(npair)]
        sel = [2 * (p // group) for p in range(npair)]
        outs = _swa_stage(qs, [kv[0][s] for s in sel], [kv[0][s + 1] for s in sel],
                          [kv[1][s] for s in sel], [kv[1][s + 1] for s in sel],
                          valid, sinks, scale)
        for p in range(npair):
            o_ref[pl.ds(start, tq), _head(p, LANES)] = outs[p].astype(o_ref.dtype)
        return c

    lax.fori_loop(0, seq // tq, body, 0)


def _swa_prompt(q, k, v, sinks, batch, seq):
    m, wq = q.shape
    wk = k.shape[1]
    npairs = wk // LANES
    qw = wq // npairs
    qo = pl.BlockSpec((seq, qw), lambda b, h, s: (b, h))
    kv = pl.BlockSpec((seq, LANES), lambda b, h, s: (b, h))
    grid_spec = pltpu.PrefetchScalarGridSpec(
        num_scalar_prefetch=1,
        grid=(batch, npairs),
        in_specs=[qo, kv, kv],
        out_specs=qo,
        scratch_shapes=[pltpu.VMEM((4, WINDOW + seq, LANES), BF16),
                        pltpu.VMEM((4, WINDOW + seq, LANES), BF16)],
    )
    return pl.pallas_call(
        functools.partial(_swa_prompt_kernel, scale=HEAD_DIM_C ** -0.5,
                          tq=SWA_TQ),
        grid_spec=grid_spec,
        out_shape=jax.ShapeDtypeStruct((m, wq), BF16),
        compiler_params=_params("arbitrary", "arbitrary"),
        name="swa_prompt",
    )(sinks, q, k, v)


def _swa_step_kernel(sinks_ref, q_ref, kn_ref, vn_ref, kc_ref, vc_ref, o_ref,
                     nk_ref, nv_ref, *, scale):
    t = q_ref.shape[0]
    past, wk = kc_ref.shape
    band = 2 * LANES
    nkv = wk // LANES
    npair = q_ref.shape[1] // LANES
    group = npair // (2 * nkv)
    kc, vc = kc_ref[...], vc_ref[...]
    kn, vn = kn_ref[...], vn_ref[...]
    nk_ref[0:past - t] = kc[t:]
    nv_ref[0:past - t] = vc[t:]
    nk_ref[past - t:past] = kn
    nv_ref[past - t:past] = vn
    pad = jnp.zeros((band - past - t, wk), F32)
    k_all = jnp.concatenate([kc, kn, pad], axis=0)
    v_all = jnp.concatenate([vc, vn, pad], axis=0)
    col = lax.broadcasted_iota(jnp.int32, (t, band), 1)
    valid = col < past + t
    kx = [[x.astype(BF16) for x in _split_pair(k_all[:, _head(g, LANES)])]
          for g in range(nkv)]
    vx = [[x.astype(BF16) for x in _split_pair(v_all[:, _head(g, LANES)])]
          for g in range(nkv)]
    qs = [q_ref[:, _head(p, LANES)] for p in range(npair)]
    where = [(p // (2 * group), 2 * ((p // group) % 2)) for p in range(npair)]
    sinks = [sinks_ref[n] for n in range(2 * npair)]
    outs = _swa_stage(qs, [kx[g][s] for g, s in where],
                      [kx[g][s + 1] for g, s in where],
                      [vx[g][s] for g, s in where],
                      [vx[g][s + 1] for g, s in where], valid, sinks, scale)
    for p in range(npair):
        o_ref[:, _head(p, LANES)] = outs[p].astype(o_ref.dtype)


def _swa_step(q, k_new, v_new, k_cache, v_cache, sinks, batch):
    m, wq = q.shape
    wk = k_new.shape[1]
    t = m // batch
    past = k_cache.shape[1]
    assert past + t <= 2 * LANES
    qrow = pl.BlockSpec((t, wq), lambda b, s: (b, 0))
    krow = pl.BlockSpec((t, wk), lambda b, s: (b, 0))
    cache = pl.BlockSpec((None, past, wk), lambda b, s: (b, 0, 0))
    grid_spec = pltpu.PrefetchScalarGridSpec(
        num_scalar_prefetch=1,
        grid=(batch,),
        in_specs=[qrow, krow, krow, cache, cache],
        out_specs=[qrow, cache, cache],
    )
    return pl.pallas_call(
        functools.partial(_swa_step_kernel, scale=HEAD_DIM_C ** -0.5),
        grid_spec=grid_spec,
        out_shape=[jax.ShapeDtypeStruct((m, wq), BF16),
                   jax.ShapeDtypeStruct(k_cache.shape, F32),
                   jax.ShapeDtypeStruct(v_cache.shape, F32)],
        compiler_params=_params("arbitrary"),
        name="swa_step",
    )(sinks, q, k_new, v_new, k_cache, v_cache)


def _rope_tables(pos):
    half = ROPE_DIM // 2
    inv = jnp.power(ROPE_THETA, -jnp.arange(half, dtype=F32) / half)
    ang = pos.astype(F32)[:, None] * inv[None, :]
    cos, sin = jnp.cos(ang), jnp.sin(ang)
    n = pos.shape[0]
    reps = LANES // HEAD_DIM_C
    rest = HEAD_DIM_C - ROPE_DIM
    ones = jnp.ones((n, rest), F32)
    zeros = jnp.zeros((n, rest), F32)
    zh = jnp.zeros((n, half), F32)
    c = jnp.tile(jnp.concatenate([cos, cos, ones], axis=1), (1, reps))
    s1 = jnp.tile(jnp.concatenate([zh, sin, zeros], axis=1), (1, reps))
    s2 = jnp.tile(jnp.concatenate([-sin, zh, zeros], axis=1), (1, reps))
    return c, s1, s2


def _ffn(hp, hs, pre, w_gate, w_up, w_down, layer, next_gain):
    f = w_gate.shape[-1]
    d = hp.shape[1]
    wg, wu = _Weight(w_gate, layer), _Weight(w_up, layer)
    a_p, ss_p, a_s, ss_s = pre
    act_p, wd = _gate_up(a_p, wg, wu, f, norm=_row_scale(ss_p, d),
                         cast=(w_down, layer))
    act_s = _gate_up(a_s, wg, wu, f, norm=_row_scale(ss_s, d))
    return _down(act_p, act_s, wd, hp, hs, gain=next_gain)


def kernel(x_prompt, x_sample, cache_sb_k, cache_sb_v, cache_band_k, cache_band_v, cache_swa_k, cache_swa_v, norm_mix, norm_ffn, norm_final, w_in_ab, w_out_ab, rel_bias, w_qkv_c, w_out_c, sinks, w_gate, w_up, w_down):
    bp, seq, dm = x_prompt.shape
    bs, t, _ = x_sample.shape
    depth = norm_mix.shape[0]
    past = cache_sb_k.shape[2]
    hp = x_prompt.reshape(bp * seq, dm)
    hs = x_sample.reshape(bs * t, dm)
    norm_mix3 = norm_mix.reshape(depth, 1, dm)
    norm_final3 = norm_final.reshape(1, 1, dm)

    w_sb = cache_sb_k.shape[3] * cache_sb_k.shape[4]
    w_band = cache_band_k.shape[3] * cache_band_k.shape[4]
    wq_c = w_out_c.shape[1]
    wk_c = cache_swa_k.shape[3] * cache_swa_k.shape[4]
    p_band = cache_band_k.shape[2]
    p_swa = cache_swa_k.shape[2]

    rope_p = _rope_tables(jnp.arange(seq))
    rope_s = tuple(jnp.tile(x, (bs, 1)) for x in _rope_tables(past + jnp.arange(t)))

    outs = {name: [] for name in (
        "sbk_p", "sbv_p", "bk_p", "bv_p", "ck_p", "cv_p",
        "sbk_s", "sbv_s", "bk_s", "bv_s", "ck_s", "cv_s")}

    pre = None
    for l in range(depth):
        if pre is None:
            a_p = _rmsnorm(hp, norm_mix3, l, BF16)
            a_s = _rmsnorm(hs, norm_mix3, l, BF16)
            norm = None
        else:
            a_p, ss_p, a_s, ss_s = pre
            norm = (_row_scale(ss_p, dm), _row_scale(ss_s, dm))
        if l % 2 == 0:
            e = l // 2
            cols = (0, w_sb, 2 * w_sb, 3 * w_sb, 3 * w_sb + w_band, 3 * w_sb + 2 * w_band)
            bias_p = _band_bias(rel_bias[e], BAND_TQ, BAND_HIST + BAND_TQ)
            bias_s = _band_bias(rel_bias[e], t, p_band + t)

            widths = (w_sb, w_sb, w_sb, w_band, w_band, w_band)
            dtypes = (BF16, F32, F32, BF16, F32, F32)

            w_in = [_Weight(w_in_ab, e, 0, c) for c in cols]
            w_out = (_Weight(w_out_ab, e, 0), _Weight(w_out_ab, e, 1))

            (qa, qa_s), (ka, ka_s), (va, va_s), (qb, qb_s), (kb, kb_s), (vb, vb_s) = (
                _proj(a_p, a_s, w, n, dt, norm=norm)
                for w, n, dt in zip(w_in, widths, dtypes))
            oa = _sb_prompt(qa, ka, va, bp, seq)
            ob = _band_prompt(qb, kb, vb, bias_p, bp, seq)
            oa_s = _sb_step(qa_s, ka_s, va_s, cache_sb_k[e], cache_sb_v[e], bs)
            ob_s, nbk, nbv = _band_step(qb_s, kb_s, vb_s, cache_band_k[e],
                                        cache_band_v[e], bias_s, bs)
            hp, hs, *pre = _mm_res(
                [(oa, oa_s, 0, w_out[0]), (ob, ob_s, 0, w_out[1])], w_sb,
                hp, hs, PROJ_TN, gain=norm_ffn[l])
            nb = min(BAND_HIST, seq)
            outs["sbk_p"].append(ka.reshape(bp, seq, *cache_sb_k.shape[3:]))
            outs["sbv_p"].append(va.reshape(bp, seq, *cache_sb_k.shape[3:]))
            outs["bk_p"].append(kb.reshape(bp, seq, *cache_band_k.shape[3:])[:, seq - nb:])
            outs["bv_p"].append(vb.reshape(bp, seq, *cache_band_k.shape[3:])[:, seq - nb:])
            outs["sbk_s"].append(ka_s.reshape(bs, t, *cache_sb_k.shape[3:]))
            outs["sbv_s"].append(va_s.reshape(bs, t, *cache_sb_k.shape[3:]))
            outs["bk_s"].append(nbk)
            outs["bv_s"].append(nbv)
        else:
            o = l // 2

            widths = (wq_c, wk_c, wk_c)
            dtypes = (BF16, F32, F32)

            w_in = [_Weight(w_qkv_c, o, 0, c) for c in (0, wq_c, wq_c + wk_c)]
            w_out = _Weight(w_out_c, o, 0)

            (q, q_s), (k, k_s), (v, v_s) = (
                _proj(a_p, a_s, w, n, dt, rope=rp, rope_s=rs, norm=norm)
                for w, n, dt, rp, rs in zip(w_in, widths, dtypes,
                                            (rope_p, rope_p, None),
                                            (rope_s, rope_s, None)))
            oc = _swa_prompt(q, k, v, sinks[o], bp, seq)
            oc_s, nck, ncv = _swa_step(
                q_s, k_s, v_s, cache_swa_k[o].reshape(bs, p_swa, wk_c),
                cache_swa_v[o].reshape(bs, p_swa, wk_c), sinks[o], bs)
            hp, hs, *pre = _mm_res([(oc, oc_s, 0, w_out)], wq_c, hp, hs,
                                   PROJ_TN, gain=norm_ffn[l])
            nw = min(WINDOW, seq)
            outs["ck_p"].append(k.reshape(bp, seq, *cache_swa_k.shape[3:])[:, seq - nw:])
            outs["cv_p"].append(v.reshape(bp, seq, *cache_swa_k.shape[3:])[:, seq - nw:])
            outs["ck_s"].append(nck.reshape(cache_swa_k.shape[1:]))
            outs["cv_s"].append(ncv.reshape(cache_swa_v.shape[1:]))
        next_gain = norm_mix[l + 1] if l + 1 < depth else None
        hp, hs, *pre = _ffn(hp, hs, pre, w_gate, w_up, w_down, l, next_gain)

    y_prompt = _rmsnorm(hp, norm_final3, 0, F32).reshape(bp, seq, dm)
    y_sample = _rmsnorm(hs, norm_final3, 0, F32).reshape(bs, t, dm)
    st = lambda name: jnp.stack(outs[name])
    return (y_prompt, y_sample,
            st("sbk_p"), st("sbv_p"), st("bk_p"), st("bv_p"), st("ck_p"), st("cv_p"),
            st("sbk_s"), st("sbv_s"), st("bk_s"), st("bv_s"), st("ck_s"), st("cv_s"))
```

```python
import functools
import math

import numpy as np
import jax
import jax.numpy as jnp
from jax import lax
from jax.experimental import pallas as pl
from jax.experimental.pallas import tpu as pltpu

F32 = jnp.float32
BF16 = jnp.bfloat16

CHUNK = 64
EPS = 1e-6
HEAD_DIM_AB = 128
BAND_PREV_CHUNKS = 8
BAND_HIST = BAND_PREV_CHUNKS * CHUNK
REL_CLIP = 2 * CHUNK
HEAD_DIM_C = 64
SWA_PREV_CHUNKS = 2
WINDOW = SWA_PREV_CHUNKS * CHUNK
ROPE_THETA = 500000.0
ROPE_DIM = HEAD_DIM_C // 4

LANES = 128
V7X_VMEM_LIMIT_BYTES = 60 * 1024 * 1024

DENSE_TM = 1024
GATE_UP_TM = 2048
GATE_UP_TN = 256
DENSE_TN = 512

NEG = -1e30

SB_TILE = 256
SB_HEADS = 4
SB_STEP_KEYS = 128
BAND_TQ = 128
BAND_HEADS = 4
STEP_HEADS = 8
SWA_TQ = 128


def _params(*sem):
    return pltpu.CompilerParams(dimension_semantics=sem,
                                vmem_limit_bytes=V7X_VMEM_LIMIT_BYTES)


def _chunk_of(pos):
    return lax.shift_right_logical(pos, int(math.log2(CHUNK)))


def _pick_tm(m):
    return DENSE_TM if m % DENSE_TM == 0 else m


def _rmsnorm_kernel(x_ref, g_ref, o_ref):
    x = x_ref[...]
    ms = jnp.mean(x * x, axis=-1, keepdims=True)
    y = x * lax.rsqrt(ms + EPS)
    o_ref[...] = (y * g_ref[...]).astype(o_ref.dtype)


def _rmsnorm(x, g3, layer, out_dtype):
    m, d = x.shape
    tm = 256
    return pl.pallas_call(
        _rmsnorm_kernel,
        grid=(m // tm,),
        in_specs=[pl.BlockSpec((tm, d), lambda i: (i, 0)),
                  pl.BlockSpec((None, 1, d), lambda i: (layer, 0, 0))],
        out_specs=pl.BlockSpec((tm, d), lambda i: (i, 0)),
        out_shape=jax.ShapeDtypeStruct((m, d), out_dtype),
        compiler_params=_params("arbitrary"),
        name="rmsnorm",
    )(x, g3)


class _Weight:
    def __init__(self, src, layer, kblk=0, col0=0):
        self.src, self.layer, self.kblk, self.col0 = src, layer, kblk, col0

    def spec(self, tk, tn):
        layer, kb, jb = self.layer, self.kblk, self.col0 // tn
        return pl.BlockSpec((None, tk, tn), lambda i, j: (layer, kb, jb + j))


def _rope_tile(x, c, s1, s2):
    return (x * c + pltpu.roll(x, 8, 1) * s1
            + pltpu.roll(x, LANES - 8, 1) * s2)


def _rider_in(x):
    return pl.BlockSpec(x.shape, lambda i, j: (0, 0), pipeline_mode=pl.Buffered(1))


def _rider_out(rows, tn, ni):
    return pl.BlockSpec((rows, tn),
                        lambda i, j: (0, jnp.where(i == ni - 1, j, 0)))


def _on_last_row_block(fn):
    pl.when(pl.program_id(0) == pl.num_programs(0) - 1)(fn)


def _lane_chunks(n):
    return [slice(c * LANES, (c + 1) * LANES) for c in range(n // LANES)]


def _proj_kernel(*refs, rope, norm):
    w_ref = refs[1]
    n_tab = 3 if rope else 0
    a_ref, tabs = refs[0], refs[2:2 + n_tab]
    sa_ref, stabs = refs[2 + n_tab], refs[3 + n_tab:3 + 2 * n_tab]
    r_ref, sr_ref = refs[3 + 2 * n_tab:5 + 2 * n_tab] if norm else (None, None)
    o_ref, so_ref = refs[-2:]

    def run(a_ref, tabs, r_ref, o_ref):
        acc = jnp.dot(a_ref[...], w_ref[...].astype(BF16),
                      preferred_element_type=F32)
        for sl in _lane_chunks(acc.shape[1]):
            x = acc[:, sl]
            if norm:
                x = x * r_ref[...]
            if rope:
                x = _rope_tile(x, *(t[...] for t in tabs))
            o_ref[:, sl] = x.astype(o_ref.dtype)

    run(a_ref, tabs, r_ref, o_ref)
    _on_last_row_block(lambda: run(sa_ref, stabs, sr_ref, so_ref))


def _proj(a, a_s, w, ncols, out_dtype, rope=None, rope_s=None, norm=None):
    m, k = a.shape
    ms = a_s.shape[0]
    tm = _pick_tm(m)
    tn = min(DENSE_TN, ncols)
    ni = m // tm
    in_specs = [pl.BlockSpec((tm, k), lambda i, j: (i, 0)), w.spec(k, tn)]
    args = [a, w.src]
    if rope is not None:
        nrep = rope[0].shape[0] // tm
        for t in rope:
            in_specs.append(pl.BlockSpec((tm, LANES), lambda i, j: (i % nrep, 0)))
            args.append(t)
    for x in (a_s,) + tuple(rope_s or ()):
        in_specs.append(_rider_in(x))
        args.append(x)
    if norm is not None:
        r, r_s = norm
        in_specs += [pl.BlockSpec((tm, LANES), lambda i, j: (i, 0)), _rider_in(r_s)]
        args += [r, r_s]
    return pl.pallas_call(
        functools.partial(_proj_kernel, rope=rope is not None,
                          norm=norm is not None),
        grid=(ni, ncols // tn),
        in_specs=in_specs,
        out_specs=[pl.BlockSpec((tm, tn), lambda i, j: (i, j)),
                   _rider_out(ms, tn, ni)],
        out_shape=[jax.ShapeDtypeStruct((m, ncols), out_dtype),
                   jax.ShapeDtypeStruct((ms, ncols), out_dtype)],
        compiler_params=_params("arbitrary", "arbitrary"),
        name="proj",
    )(*args)


def _gate_up_kernel(*refs, norm, cast):
    a_ref, wg_ref, wu_ref = refs[:3]
    r_ref = refs[3] if norm else None
    o_ref = refs[-2] if cast else refs[-1]
    a = a_ref[...]
    g = jnp.dot(a, wg_ref[...].astype(BF16), preferred_element_type=F32)
    u = jnp.dot(a, wu_ref[...].astype(BF16), preferred_element_type=F32)
    for sl in _lane_chunks(g.shape[1]):
        gc, uc = g[:, sl], u[:, sl]
        if norm:
            gc, uc = gc * r_ref[...], uc * r_ref[...]
        o_ref[:, sl] = (gc * jax.nn.sigmoid(gc) * uc).astype(o_ref.dtype)
    if cast:
        refs[-1][...] = refs[-3][...].astype(BF16)


def _gate_up(a, wg, wu, f, norm=None, cast=None):
    m, k = a.shape
    tm = GATE_UP_TM if m % GATE_UP_TM == 0 else m
    tn = GATE_UP_TN
    ni, nj = m // tm, f // tn
    in_specs = [pl.BlockSpec((tm, k), lambda i, j: (i, 0)),
                wg.spec(k, tn), wu.spec(k, tn)]
    args = [a, wg.src, wu.src]
    out_specs = [pl.BlockSpec((tm, tn), lambda i, j: (i, j))]
    out_shape = [jax.ShapeDtypeStruct((m, f), BF16)]
    if norm is not None:
        in_specs.append(pl.BlockSpec((tm, LANES), lambda i, j: (i, 0),
                                     pipeline_mode=pl.Buffered(1)))
        args.append(norm)
    if cast is not None:
        w3, layer = cast
        rows, n = w3.shape[1] // (ni * nj), w3.shape[2]
        assert rows * ni * nj == w3.shape[1] and rows % 16 == 0
        in_specs.append(pl.BlockSpec((None, rows, n),
                                     lambda i, j: (layer, i * nj + j, 0)))
        args.append(w3)
        out_specs.append(pl.BlockSpec((rows, n), lambda i, j: (i * nj + j, 0)))
        out_shape.append(jax.ShapeDtypeStruct(w3.shape[1:], BF16))
    out = pl.pallas_call(
        functools.partial(_gate_up_kernel, norm=norm is not None,
                          cast=cast is not None),
        grid=(ni, nj),
        in_specs=in_specs,
        out_specs=out_specs,
        out_shape=out_shape,
        compiler_params=_params("arbitrary", "arbitrary"),
        name="gate_up",
    )(*args)
    return out if cast is not None else out[0]


def _mm_res_kernel(*refs, nterms, stats):
    n = nterms
    w_refs = refs[n:2 * n]
    gain_ref = refs[3 * n + 2] if stats else None
    outs = refs[3 * n + 2 + bool(stats):]

    def run(a_refs, res_ref, o_ref, b_ref, ss_ref):
        acc = res_ref[...]
        for a_ref, w_ref in zip(a_refs, w_refs):
            acc = acc + jnp.dot(a_ref[...], w_ref[...].astype(BF16),
                                preferred_element_type=F32)
        o_ref[...] = acc
        if stats:
            _row_stats(acc, gain_ref, b_ref, ss_ref)

    p_extra = (outs[2], outs[3]) if stats else (None, None)
    s_extra = (outs[4], outs[5]) if stats else (None, None)
    run(refs[:n], refs[2 * n], outs[0], *p_extra)
    _on_last_row_block(
        lambda: run(refs[2 * n + 1:3 * n + 1], refs[3 * n + 1], outs[1], *s_extra))


def _mm_res(terms, tk, res, res_s, tn, gain=None):
    stats = gain is not None
    m, n = res.shape
    ms = res_s.shape[0]
    tm = _pick_tm(m)
    ni = m // tm
    in_specs, args = [], []
    for a, _, ka, _ in terms:
        in_specs.append(pl.BlockSpec((tm, tk), lambda i, j, ka=ka: (i, ka)))
        args.append(a)
    for _, _, _, w in terms:
        in_specs.append(w.spec(tk, tn))
        args.append(w.src)
    tile = pl.BlockSpec((tm, tn), lambda i, j: (i, j))
    tile_s = _rider_out(ms, tn, ni)
    in_specs.append(tile)
    args.append(res)
    for _, a_s, ka, _ in terms:
        in_specs.append(pl.BlockSpec((ms, tk), lambda i, j, ka=ka: (0, ka),
                                     pipeline_mode=pl.Buffered(1)))
        args.append(a_s)
    in_specs.append(tile_s)
    args.append(res_s)
    out_specs = [tile, tile_s]
    out_shape = [jax.ShapeDtypeStruct((m, n), F32),
                 jax.ShapeDtypeStruct((ms, n), F32)]
    if stats:
        in_specs.append(pl.BlockSpec((1, tn), lambda i, j: (0, j)))
        args.append(gain.astype(F32).reshape(1, n))
        out_specs += [tile, pl.BlockSpec((tm, LANES), lambda i, j: (i, 0)),
                      tile_s, pl.BlockSpec((ms, LANES), lambda i, j: (0, 0))]
        out_shape += [jax.ShapeDtypeStruct((m, n), BF16),
                      jax.ShapeDtypeStruct((m, LANES), F32),
                      jax.ShapeDtypeStruct((ms, n), BF16),
                      jax.ShapeDtypeStruct((ms, LANES), F32)]
    return pl.pallas_call(
        functools.partial(_mm_res_kernel, nterms=len(terms), stats=stats),
        grid=(ni, n // tn),
        in_specs=in_specs,
        out_specs=out_specs,
        out_shape=out_shape,
        compiler_params=_params("arbitrary", "arbitrary"),
        name="mm_res",
    )(*args)


def _row_stats(acc, gain_ref, b_ref, ss_ref):
    b_ref[...] = (acc * gain_ref[...]).astype(BF16)
    part = sum(acc[:, sl] * acc[:, sl] for sl in _lane_chunks(acc.shape[1]))
    j = pl.program_id(1)

    @pl.when(j == 0)
    def _():
        ss_ref[...] = part

    @pl.when(j != 0)
    def _():
        ss_ref[...] += part


def _down_kernel(*refs, stats):
    a_refs, w_ref, res_ref = refs[0:2], refs[2], refs[3]
    sa_refs, sres_ref = refs[4:6], refs[6]
    gain_ref = refs[7] if stats else None
    outs = refs[7 + bool(stats):]
    k = pl.program_id(2)

    def run(x_refs, r_ref, o_ref, b_ref, ss_ref):
        @pl.when(k == 0)
        def _():
            o_ref[...] = r_ref[...] + jnp.dot(
                x_refs[0][...], w_ref[...].astype(BF16),
                preferred_element_type=F32)

        @pl.when(k == 1)
        def _():
            acc = o_ref[...] + jnp.dot(
                x_refs[1][...], w_ref[...].astype(BF16),
                preferred_element_type=F32)
            o_ref[...] = acc
            if stats:
                _row_stats(acc, gain_ref, b_ref, ss_ref)

    p_extra = (outs[2], outs[3]) if stats else (None, None)
    s_extra = (outs[4], outs[5]) if stats else (None, None)
    run(a_refs, res_ref, outs[0], *p_extra)
    _on_last_row_block(lambda: run(sa_refs, sres_ref, outs[1], *s_extra))


def _down(act, act_s, w, res, res_s, gain=None):
    stats = gain is not None
    m, n = res.shape
    ms = res_s.shape[0]
    tk = act.shape[1] // 2
    tm, tn = _pick_tm(m), DENSE_TN
    ni = m // tm
    one = dict(pipeline_mode=pl.Buffered(1))
    last = ni - 1
    tile = pl.BlockSpec((tm, tn), lambda i, j, k: (i, j))
    tile_s = pl.BlockSpec((ms, tn),
                          lambda i, j, k: (0, jnp.where(i == last, j, 0)))
    in_specs = (
        [pl.BlockSpec((tm, tk), lambda i, j, k, h=h: (i, h), **one) for h in (0, 1)]
        + [pl.BlockSpec((tk, tn), lambda i, j, k: (k, j)), tile]
        + [pl.BlockSpec((ms, tk), lambda i, j, k, h=h: (0, h), **one) for h in (0, 1)]
        + [tile_s])
    args = [act, act, w, res, act_s, act_s, res_s]
    out_specs = [tile, tile_s]
    out_shape = [jax.ShapeDtypeStruct((m, n), F32),
                 jax.ShapeDtypeStruct((ms, n), F32)]
    if stats:
        in_specs.append(pl.BlockSpec((1, tn), lambda i, j, k: (0, j)))
        args.append(gain.astype(F32).reshape(1, n))
        out_specs += [tile, pl.BlockSpec((tm, LANES), lambda i, j, k: (i, 0)),
                      tile_s, pl.BlockSpec((ms, LANES), lambda i, j, k: (0, 0))]
        out_shape += [jax.ShapeDtypeStruct((m, n), BF16),
                      jax.ShapeDtypeStruct((m, LANES), F32),
                      jax.ShapeDtypeStruct((ms, n), BF16),
                      jax.ShapeDtypeStruct((ms, LANES), F32)]
    return pl.pallas_call(
        functools.partial(_down_kernel, stats=stats),
        grid=(ni, n // tn, 2),
        in_specs=in_specs,
        out_specs=out_specs,
        out_shape=out_shape,
        compiler_params=_params("arbitrary", "arbitrary", "arbitrary"),
        name="down",
    )(*args)


def _row_scale_kernel(ss_ref, o_ref, *, d):
    tot = jnp.sum(ss_ref[...], axis=1, keepdims=True)
    o_ref[...] = jnp.broadcast_to(lax.rsqrt(tot / d + EPS), o_ref.shape)


def _row_scale(ss, d):
    m = ss.shape[0]
    tm = _pick_tm(m)
    blk = pl.BlockSpec((tm, LANES), lambda i: (i, 0))
    return pl.pallas_call(
        functools.partial(_row_scale_kernel, d=d),
        grid=(m // tm,),
        in_specs=[blk],
        out_specs=blk,
        out_shape=jax.ShapeDtypeStruct(ss.shape, F32),
        compiler_params=_params("arbitrary"),
        name="row_scale",
    )(ss)


_DN_T = (((1,), (1,)), ((), ()))


def _head(h, d=HEAD_DIM_AB):
    return slice(h * d, (h + 1) * d)


def _heads_major(x):
    return pltpu.einshape("phd->hpd", x)


def _softmax_stage(ss, sinks=None):
    ms = [jnp.max(s, axis=1, keepdims=True) for s in ss]
    if sinks is not None:
        ms = [jnp.maximum(m, sk) for m, sk in zip(ms, sinks)]
    ps = [jnp.exp(s - m) for s, m in zip(ss, ms)]
    ls = [jnp.sum(p, axis=1, keepdims=True) for p in ps]
    if sinks is not None:
        ls = [l + jnp.exp(sk - m) for l, sk, m in zip(ls, sinks, ms)]
    return [p.astype(BF16) for p in ps], ls


def _sb_logs(z, valid):
    lb = jnp.minimum(z, 0.0) - jnp.log(1.0 + jnp.exp(-jnp.abs(z)))
    l1m = lb - z
    if valid is not None:
        l1m = jnp.where(valid, l1m, 0.0)
    return lb, l1m


def _split_hi_lo(x):
    hi = x.astype(BF16)
    lo = (x - hi.astype(F32)).astype(BF16)
    return jnp.concatenate([hi, lo], axis=1)


def _suffix_matrix(n):
    u = np.tril(np.ones((n, n), np.float32), -1)
    return jnp.asarray(np.concatenate([u, u], axis=0), BF16)


def _sb_prompt_kernel(q_ref, k_ref, v_ref, uu_ref, o_ref, kb_ref, vb_ref,
                      oacc_ref, cacc_ref, *, scale):
    qi = pl.program_id(2)
    tq = q_ref.shape[0]
    nh = oacc_ref.shape[0]

    @pl.when(qi == 0)
    def _():
        kb_ref[...] = k_ref[...].astype(BF16)
        vb_ref[...] = v_ref[...].astype(BF16)

    oacc_ref[...] = jnp.zeros(oacc_ref.shape, F32)
    cacc_ref[...] = jnp.zeros(cacc_ref.shape, F32)

    def tile(j, masked):
        start = pl.multiple_of(j * tq, tq)
        valid = None
        if masked:
            row = lax.broadcasted_iota(jnp.int32, (tq, tq), 0)
            col = lax.broadcasted_iota(jnp.int32, (tq, tq), 1)
            valid = col < row
        zs = [lax.dot_general(q_ref[:, _head(h)],
                              kb_ref[pl.ds(start, tq), _head(h)], _DN_T,
                              preferred_element_type=F32) * scale
              for h in range(nh)]
        logs = [_sb_logs(z, valid) for z in zs]
        sufs = [jnp.dot(_split_hi_lo(l1m), uu_ref[...],
                        preferred_element_type=F32) for _, l1m in logs]
        ws = [jnp.exp(lb + suf + cacc_ref[h])
              for h, ((lb, _), suf) in enumerate(zip(logs, sufs))]
        if masked:
            ws = [jnp.where(valid, w, 0.0) for w in ws]
        for h in range(nh):
            oacc_ref[h] += jnp.dot(ws[h].astype(BF16),
                                   vb_ref[pl.ds(start, tq), _head(h)],
                                   preferred_element_type=F32)
            cacc_ref[h] += jnp.sum(logs[h][1], axis=1, keepdims=True)

    tile(qi, True)

    def body(t, c):
        tile(qi - 1 - t, False)
        return c

    lax.fori_loop(0, qi, body, 0)
    for h in range(nh):
        o_ref[:, _head(h)] = oacc_ref[h].astype(o_ref.dtype)


def _sb_prompt(q, k, v, batch, seq):
    m, w = q.shape
    d = HEAD_DIM_AB
    nh = SB_HEADS
    tq = SB_TILE
    nq = seq // tq
    qo = pl.BlockSpec((tq, nh * d), lambda b, g, i: (b * nq + i, g))
    kv = pl.BlockSpec((seq, nh * d), lambda b, g, i: (b, g))
    return pl.pallas_call(
        functools.partial(_sb_prompt_kernel, scale=d ** -0.5),
        grid=(batch, w // (nh * d), nq),
        in_specs=[qo, kv, kv, pl.BlockSpec((2 * tq, tq), lambda b, g, i: (0, 0))],
        out_specs=qo,
        out_shape=jax.ShapeDtypeStruct((m, w), BF16),
        scratch_shapes=[pltpu.VMEM((seq, nh * d), BF16),
                        pltpu.VMEM((seq, nh * d), BF16),
                        pltpu.VMEM((nh, tq, d), F32),
                        pltpu.VMEM((nh, tq, 1), F32)],
        compiler_params=_params("arbitrary", "arbitrary", "arbitrary"),
        name="sb_prompt",
    )(q, k, v, _suffix_matrix(tq))


def _sb_step_kernel(q_ref, kn_ref, vn_ref, kc_ref, vc_ref, uu_ref, o_ref,
                    kb_ref, vb_ref, *, scale):
    t = q_ref.shape[0]
    past, nh, d = kc_ref.shape
    tk = uu_ref.shape[1]
    n = past + tk
    kb_ref[:, 0:past, :] = _heads_major(kc_ref[...]).astype(BF16)
    vb_ref[:, 0:past, :] = _heads_major(vc_ref[...]).astype(BF16)
    for h in range(nh):
        kb_ref[h, past:past + t] = kn_ref[:, _head(h)].astype(BF16)
        vb_ref[h, past:past + t] = vn_ref[:, _head(h)].astype(BF16)
        kb_ref[h, past + t:n] = jnp.zeros((tk - t, d), BF16)
        vb_ref[h, past + t:n] = jnp.zeros((tk - t, d), BF16)

    z = jnp.concatenate(
        [lax.dot_general(q_ref[:, _head(h)], kb_ref[h], _DN_T,
                         preferred_element_type=F32) for h in range(nh)],
        axis=0) * scale
    row = lax.broadcasted_iota(jnp.int32, z.shape, 0) & (t - 1)
    col = lax.broadcasted_iota(jnp.int32, z.shape, 1)
    valid = col < past + row
    lb, l1m = _sb_logs(z, valid)
    ntiles = n // tk
    sufs = [jnp.dot(_split_hi_lo(l1m[:, j * tk:(j + 1) * tk]), uu_ref[...],
                    preferred_element_type=F32) for j in range(ntiles)]
    tots = [jnp.sum(l1m[:, j * tk:(j + 1) * tk], axis=1, keepdims=True)
            for j in range(ntiles)]
    carries = [None] * ntiles
    carry = jnp.zeros_like(tots[0])
    for j in reversed(range(ntiles)):
        carries[j] = carry
        carry = carry + tots[j]
    after = jnp.concatenate([sufs[j] + carries[j] for j in range(ntiles)],
                            axis=1)
    w = jnp.where(valid, jnp.exp(lb + after), 0.0).astype(BF16)
    for h in range(nh):
        o = jnp.dot(w[h * t:(h + 1) * t], vb_ref[h], preferred_element_type=F32)
        o_ref[:, _head(h)] = o.astype(o_ref.dtype)


def _sb_step(q, k_new, v_new, k_cache, v_cache, batch):
    m, w = q.shape
    _, past, heads, d = k_cache.shape
    t = m // batch
    tk = SB_STEP_KEYS
    nh = STEP_HEADS
    assert past % tk == 0 and t <= tk and t & (t - 1) == 0
    row = pl.BlockSpec((t, nh * d), lambda b, g: (b, g))
    cache = pl.BlockSpec((None, past, nh, d), lambda b, g: (b, 0, g, 0))
    return pl.pallas_call(
        functools.partial(_sb_step_kernel, scale=d ** -0.5),
        grid=(batch, heads // nh),
        in_specs=[row, row, row, cache, cache,
                  pl.BlockSpec((2 * tk, tk), lambda b, g: (0, 0))],
        out_specs=row,
        out_shape=jax.ShapeDtypeStruct((m, w), BF16),
        scratch_shapes=[pltpu.VMEM((nh, past + tk, d), BF16),
                        pltpu.VMEM((nh, past + tk, d), BF16)],
        compiler_params=_params("arbitrary", "arbitrary"),
        name="sb_step",
    )(q, k_new, v_new, k_cache, v_cache, _suffix_matrix(tk))


def _band_bias_kernel(r_ref, o_ref, *, n_valid):
    tq, w = o_ref.shape
    wr = r_ref.shape[1]
    t = jnp.broadcast_to(r_ref[...], (tq, wr))
    t = pltpu.roll(t, 0, 1, stride=1, stride_axis=0)
    row = lax.broadcasted_iota(jnp.int32, (tq, w), 0)
    col = lax.broadcasted_iota(jnp.int32, (tq, w), 1)
    qc = _chunk_of(row)
    kc = _chunk_of(col)
    valid = (kc >= qc) & (kc <= qc + BAND_PREV_CHUNKS) & (col < n_valid)
    o_ref[...] = jnp.where(valid, t[:, :w], NEG)


def _band_bias(rel, tq, n_valid):
    heads = rel.shape[0]
    w = BAND_HIST + BAND_TQ
    wr = w + LANES
    p = np.arange(wr)
    delta = np.where(p < w, p, p - wr)
    idx = np.clip(BAND_HIST - delta, -REL_CLIP, REL_CLIP) + REL_CLIP
    r = jnp.take(rel.astype(F32), jnp.asarray(idx, jnp.int32), axis=1)
    r = r.reshape(heads, 1, wr)
    return pl.pallas_call(
        functools.partial(_band_bias_kernel, n_valid=n_valid),
        grid=(heads,),
        in_specs=[pl.BlockSpec((None, 1, wr), lambda h: (h, 0, 0))],
        out_specs=pl.BlockSpec((None, tq, w), lambda h: (h, 0, 0)),
        out_shape=jax.ShapeDtypeStruct((heads, tq, w), F32),
        compiler_params=_params("arbitrary"),
        name="band_bias",
    )(r)


def _band_prompt_kernel(q_ref, k_ref, v_ref, bias_ref, o_ref, kp_ref, vp_ref,
                        *, scale, tq):
    seq, wd = k_ref.shape
    d = HEAD_DIM_AB
    nh = wd // d
    w = bias_ref.shape[1]
    hist = w - tq
    kp_ref[0:hist] = jnp.zeros((hist, wd), BF16)
    vp_ref[0:hist] = jnp.zeros((hist, wd), BF16)
    kp_ref[hist:hist + seq] = k_ref[...].astype(BF16)
    vp_ref[hist:hist + seq] = v_ref[...].astype(BF16)
    col = lax.broadcasted_iota(jnp.int32, (tq, w), 1)

    def tile(i, at_start):
        start = pl.multiple_of(i * tq, tq)
        ss = []
        for h in range(nh):
            s = lax.dot_general(q_ref[pl.ds(start, tq), _head(h)],
                                kp_ref[pl.ds(start, w), _head(h)], _DN_T,
                                preferred_element_type=F32)
            s = s * scale + bias_ref[h * tq:(h + 1) * tq, :]
            if at_start:
                s = jnp.where(col >= hist - start, s, NEG)
            ss.append(s)
        ps, ls = _softmax_stage(ss)
        for h in range(nh):
            o = jnp.dot(ps[h], vp_ref[pl.ds(start, w), _head(h)],
                        preferred_element_type=F32) / ls[h]
            o_ref[pl.ds(start, tq), _head(h)] = o.astype(o_ref.dtype)

    def loop(lo, hi, at_start):
        def body(i, c):
            tile(i, at_start)
            return c
        lax.fori_loop(lo, hi, body, 0)

    n_tiles = seq // tq
    n_start = min(hist // tq, n_tiles)
    loop(0, n_start, True)
    loop(n_start, n_tiles, False)


def _band_prompt(q, k, v, bias, batch, seq):
    m, wd = q.shape
    d = HEAD_DIM_AB
    nh = BAND_HEADS
    heads, tq, w = bias.shape
    bias = bias.reshape(heads // nh, nh * tq, w)
    blk = pl.BlockSpec((seq, nh * d), lambda b, g: (b, g))
    return pl.pallas_call(
        functools.partial(_band_prompt_kernel, scale=d ** -0.5, tq=tq),
        grid=(batch, heads // nh),
        in_specs=[blk, blk, blk,
                  pl.BlockSpec((None, nh * tq, w), lambda b, g: (g, 0, 0))],
        out_specs=blk,
        out_shape=jax.ShapeDtypeStruct((m, wd), BF16),
        scratch_shapes=[pltpu.VMEM((BAND_HIST + seq, nh * d), BF16),
                        pltpu.VMEM((BAND_HIST + seq, nh * d), BF16)],
        compiler_params=_params("arbitrary", "arbitrary"),
        name="band_prompt",
    )(q, k, v, bias)


def _roll_cache(new_ref, cache, new_rows_ref, t):
    past, nh, d = cache.shape
    new_ref[0:past - t] = cache[t:]
    for h in range(nh):
        new_ref[past - t:past, h, :] = new_rows_ref[:, _head(h, d)]


def _band_step_kernel(q_ref, kn_ref, vn_ref, kc_ref, vc_ref, bias_ref, o_ref,
                      nk_ref, nv_ref, kb_ref, vb_ref, *, scale):
    t = q_ref.shape[0]
    past, nh, d = kc_ref.shape
    w = bias_ref.shape[1]
    kc, vc = kc_ref[...], vc_ref[...]
    _roll_cache(nk_ref, kc, kn_ref, t)
    _roll_cache(nv_ref, vc, vn_ref, t)
    kb_ref[:, 0:past, :] = _heads_major(kc).astype(BF16)
    vb_ref[:, 0:past, :] = _heads_major(vc).astype(BF16)
    for h in range(nh):
        kb_ref[h, past:past + t] = kn_ref[:, _head(h)].astype(BF16)
        vb_ref[h, past:past + t] = vn_ref[:, _head(h)].astype(BF16)
        kb_ref[h, past + t:w] = jnp.zeros((w - past - t, d), BF16)
        vb_ref[h, past + t:w] = jnp.zeros((w - past - t, d), BF16)
    ss = [lax.dot_general(q_ref[:, _head(h)], kb_ref[h], _DN_T,
                          preferred_element_type=F32) * scale
          + bias_ref[h * t:(h + 1) * t, :] for h in range(nh)]
    ps, ls = _softmax_stage(ss)
    for h in range(nh):
        o = jnp.dot(ps[h], vb_ref[h], preferred_element_type=F32) / ls[h]
        o_ref[:, _head(h)] = o.astype(o_ref.dtype)


def _band_step(q, k_new, v_new, k_cache, v_cache, bias, batch):
    m, wd = q.shape
    _, past, heads, d = k_cache.shape
    t = m // batch
    nh = STEP_HEADS
    w = bias.shape[2]
    assert past == BAND_HIST and past + t <= w
    bias = bias.reshape(heads // nh, nh * t, w)
    row = pl.BlockSpec((t, nh * d), lambda b, g: (b, g))
    cache = pl.BlockSpec((None, past, nh, d), lambda b, g: (b, 0, g, 0))
    return pl.pallas_call(
        functools.partial(_band_step_kernel, scale=d ** -0.5),
        grid=(batch, heads // nh),
        in_specs=[row, row, row, cache, cache,
                  pl.BlockSpec((None, nh * t, w), lambda b, g: (g, 0, 0))],
        out_specs=[row, cache, cache],
        out_shape=[jax.ShapeDtypeStruct((m, wd), BF16),
                   jax.ShapeDtypeStruct(k_cache.shape, F32),
                   jax.ShapeDtypeStruct(v_cache.shape, F32)],
        scratch_shapes=[pltpu.VMEM((nh, w, d), BF16),
                        pltpu.VMEM((nh, w, d), BF16)],
        compiler_params=_params("arbitrary", "arbitrary"),
        name="band_step",
    )(q, k_new, v_new, k_cache, v_cache, bias)


def _split_pair(blk):
    rolled = pltpu.roll(blk, HEAD_DIM_C, 1)
    lane = lax.broadcasted_iota(jnp.int32, blk.shape, 1)
    low = lane < HEAD_DIM_C
    zero = jnp.zeros_like(blk)
    return [jnp.where(low, blk, zero), jnp.where(low, zero, rolled),
            jnp.where(low, rolled, zero), jnp.where(low, zero, blk)]


def _swa_stage(qs, kls, khs, vls, vhs, valid, sinks, scale):
    ss = []
    for q, kl, kh in zip(qs, kls, khs):
        for kx in (kl, kh):
            s = lax.dot_general(q, kx, _DN_T, preferred_element_type=F32)
            ss.append(jnp.where(valid, s * scale, NEG))
    ps, ls = _softmax_stage(ss, sinks)
    return [jnp.dot(ps[2 * i], vls[i], preferred_element_type=F32) / ls[2 * i]
            + jnp.dot(ps[2 * i + 1], vhs[i], preferred_element_type=F32)
            / ls[2 * i + 1] for i in range(len(qs))]


def _swa_prompt_kernel(sinks_ref, q_ref, k_ref, v_ref, o_ref, ks_ref, vs_ref,
                       *, scale, tq):
    hp = pl.program_id(1)
    seq = k_ref.shape[0]
    band = WINDOW + tq
    npair = q_ref.shape[1] // LANES
    group = npair // 2

    for src, dst in ((k_ref, ks_ref), (v_ref, vs_ref)):
        for n, x in enumerate(_split_pair(src[...])):
            dst[n, 0:WINDOW] = jnp.zeros((WINDOW, LANES), BF16)
            dst[n, WINDOW:WINDOW + seq] = x.astype(BF16)

    row = lax.broadcasted_iota(jnp.int32, (tq, band), 0)
    col = lax.broadcasted_iota(jnp.int32, (tq, band), 1)
    qc = _chunk_of(row)
    kc = _chunk_of(col)
    in_band = (kc >= qc) & (kc <= qc + SWA_PREV_CHUNKS)
    sinks = [sinks_ref[hp * 2 * npair + n] for n in range(2 * npair)]

    def body(i, c):
        start = pl.multiple_of(i * tq, tq)
        valid = in_band & (col >= WINDOW - start)
        kv = [[r[n, pl.ds(start, band), :] for n in range(4)]
              for r in (ks_ref, vs_ref)]
        qs = [q_ref[pl.ds(start, tq), _head(p, LANES)] for p in range(npair)]
        sel = [2 * (p // group) for p in range(npair)]
        outs = _swa_stage(qs, [kv[0][s] for s in sel], [kv[0][s + 1] for s in sel],
                          [kv[1][s] for s in sel], [kv[1][s + 1] for s in sel],
                          valid, sinks, scale)
        for p in range(npair):
            o_ref[pl.ds(start, tq), _head(p, LANES)] = outs[p].astype(o_ref.dtype)
        return c

    lax.fori_loop(0, seq // tq, body, 0)


def _swa_prompt(q, k, v, sinks, batch, seq):
    m, wq = q.shape
    wk = k.shape[1]
    npairs = wk // LANES
    qw = wq // npairs
    qo = pl.BlockSpec((seq, qw), lambda b, h, s: (b, h))
    kv = pl.BlockSpec((seq, LANES), lambda b, h, s: (b, h))
    grid_spec = pltpu.PrefetchScalarGridSpec(
        num_scalar_prefetch=1,
        grid=(batch, npairs),
        in_specs=[qo, kv, kv],
        out_specs=qo,
        scratch_shapes=[pltpu.VMEM((4, WINDOW + seq, LANES), BF16),
                        pltpu.VMEM((4, WINDOW + seq, LANES), BF16)],
    )
    return pl.pallas_call(
        functools.partial(_swa_prompt_kernel, scale=HEAD_DIM_C ** -0.5,
                          tq=SWA_TQ),
        grid_spec=grid_spec,
        out_shape=jax.ShapeDtypeStruct((m, wq), BF16),
        compiler_params=_params("arbitrary", "arbitrary"),
        name="swa_prompt",
    )(sinks, q, k, v)


def _swa_step_kernel(sinks_ref, q_ref, kn_ref, vn_ref, kc_ref, vc_ref, o_ref,
                     nk_ref, nv_ref, *, scale):
    t = q_ref.shape[0]
    past, wk = kc_ref.shape
    band = 2 * LANES
    nkv = wk // LANES
    npair = q_ref.shape[1] // LANES
    group = npair // (2 * nkv)
    kc, vc = kc_ref[...], vc_ref[...]
    kn, vn = kn_ref[...], vn_ref[...]
    nk_ref[0:past - t] = kc[t:]
    nv_ref[0:past - t] = vc[t:]
    nk_ref[past - t:past] = kn
    nv_ref[past - t:past] = vn
    pad = jnp.zeros((band - past - t, wk), F32)
    k_all = jnp.concatenate([kc, kn, pad], axis=0)
    v_all = jnp.concatenate([vc, vn, pad], axis=0)
    col = lax.broadcasted_iota(jnp.int32, (t, band), 1)
    valid = col < past + t
    kx = [[x.astype(BF16) for x in _split_pair(k_all[:, _head(g, LANES)])]
          for g in range(nkv)]
    vx = [[x.astype(BF16) for x in _split_pair(v_all[:, _head(g, LANES)])]
          for g in range(nkv)]
    qs = [q_ref[:, _head(p, LANES)] for p in range(npair)]
    where = [(p // (2 * group), 2 * ((p // group) % 2)) for p in range(npair)]
    sinks = [sinks_ref[n] for n in range(2 * npair)]
    outs = _swa_stage(qs, [kx[g][s] for g, s in where],
                      [kx[g][s + 1] for g, s in where],
                      [vx[g][s] for g, s in where],
                      [vx[g][s + 1] for g, s in where], valid, sinks, scale)
    for p in range(npair):
        o_ref[:, _head(p, LANES)] = outs[p].astype(o_ref.dtype)


def _swa_step(q, k_new, v_new, k_cache, v_cache, sinks, batch):
    m, wq = q.shape
    wk = k_new.shape[1]
    t = m // batch
    past = k_cache.shape[1]
    assert past + t <= 2 * LANES
    qrow = pl.BlockSpec((t, wq), lambda b, s: (b, 0))
    krow = pl.BlockSpec((t, wk), lambda b, s: (b, 0))
    cache = pl.BlockSpec((None, past, wk), lambda b, s: (b, 0, 0))
    grid_spec = pltpu.PrefetchScalarGridSpec(
        num_scalar_prefetch=1,
        grid=(batch,),
        in_specs=[qrow, krow, krow, cache, cache],
        out_specs=[qrow, cache, cache],
    )
    return pl.pallas_call(
        functools.partial(_swa_step_kernel, scale=HEAD_DIM_C ** -0.5),
        grid_spec=grid_spec,
        out_shape=[jax.ShapeDtypeStruct((m, wq), BF16),
                   jax.ShapeDtypeStruct(k_cache.shape, F32),
                   jax.ShapeDtypeStruct(v_cache.shape, F32)],
        compiler_params=_params("arbitrary"),
        name="swa_step",
    )(sinks, q, k_new, v_new, k_cache, v_cache)


def _rope_tables(pos):
    half = ROPE_DIM // 2
    inv = jnp.power(ROPE_THETA, -jnp.arange(half, dtype=F32) / half)
    ang = pos.astype(F32)[:, None] * inv[None, :]
    cos, sin = jnp.cos(ang), jnp.sin(ang)
    n = pos.shape[0]
    reps = LANES // HEAD_DIM_C
    rest = HEAD_DIM_C - ROPE_DIM
    ones = jnp.ones((n, rest), F32)
    zeros = jnp.zeros((n, rest), F32)
    zh = jnp.zeros((n, half), F32)
    c = jnp.tile(jnp.concatenate([cos, cos, ones], axis=1), (1, reps))
    s1 = jnp.tile(jnp.concatenate([zh, sin, zeros], axis=1), (1, reps))
    s2 = jnp.tile(jnp.concatenate([-sin, zh, zeros], axis=1), (1, reps))
    return c, s1, s2


def _ffn(hp, hs, pre, w_gate, w_up, w_down, layer, next_gain):
    f = w_gate.shape[-1]
    d = hp.shape[1]
    wg, wu = _Weight(w_gate, layer), _Weight(w_up, layer)
    a_p, ss_p, a_s, ss_s = pre
    act_p, wd = _gate_up(a_p, wg, wu, f, norm=_row_scale(ss_p, d),
                         cast=(w_down, layer))
    act_s = _gate_up(a_s, wg, wu, f, norm=_row_scale(ss_s, d))
    return _down(act_p, act_s, wd, hp, hs, gain=next_gain)


def kernel(x_prompt, x_sample, cache_sb_k, cache_sb_v, cache_band_k, cache_band_v, cache_swa_k, cache_swa_v, norm_mix, norm_ffn, norm_final, w_in_ab, w_out_ab, rel_bias, w_qkv_c, w_out_c, sinks, w_gate, w_up, w_down):
    bp, seq, dm = x_prompt.shape
    bs, t, _ = x_sample.shape
    depth = norm_mix.shape[0]
    past = cache_sb_k.shape[2]
    hp = x_prompt.reshape(bp * seq, dm)
    hs = x_sample.reshape(bs * t, dm)
    norm_mix3 = norm_mix.reshape(depth, 1, dm)
    norm_final3 = norm_final.reshape(1, 1, dm)

    w_sb = cache_sb_k.shape[3] * cache_sb_k.shape[4]
    w_band = cache_band_k.shape[3] * cache_band_k.shape[4]
    wq_c = w_out_c.shape[1]
    wk_c = cache_swa_k.shape[3] * cache_swa_k.shape[4]
    p_band = cache_band_k.shape[2]
    p_swa = cache_swa_k.shape[2]

    rope_p = _rope_tables(jnp.arange(seq))
    rope_s = tuple(jnp.tile(x, (bs, 1)) for x in _rope_tables(past + jnp.arange(t)))

    outs = {name: [] for name in (
        "sbk_p", "sbv_p", "bk_p", "bv_p", "ck_p", "cv_p",
        "sbk_s", "sbv_s", "bk_s", "bv_s", "ck_s", "cv_s")}

    pre = None
    for l in range(depth):
        if pre is None:
            a_p = _rmsnorm(hp, norm_mix3, l, BF16)
            a_s = _rmsnorm(hs, norm_mix3, l, BF16)
            norm = None
        else:
            a_p, ss_p, a_s, ss_s = pre
            norm = (_row_scale(ss_p, dm), _row_scale(ss_s, dm))
        if l % 2 == 0:
            e = l // 2
            cols = (0, w_sb, 2 * w_sb, 3 * w_sb, 3 * w_sb + w_band, 3 * w_sb + 2 * w_band)
            bias_p = _band_bias(rel_bias[e], BAND_TQ, BAND_HIST + BAND_TQ)
            bias_s = _band_bias(rel_bias[e], t, p_band + t)

            widths = (w_sb, w_sb, w_sb, w_band, w_band, w_band)
            dtypes = (BF16, F32, F32, BF16, F32, F32)

            w_in = [_Weight(w_in_ab, e, 0, c) for c in cols]
            w_out = (_Weight(w_out_ab, e, 0), _Weight(w_out_ab, e, 1))

            (qa, qa_s), (ka, ka_s), (va, va_s), (qb, qb_s), (kb, kb_s), (vb, vb_s) = (
                _proj(a_p, a_s, w, n, dt, norm=norm)
                for w, n, dt in zip(w_in, widths, dtypes))
            oa = _sb_prompt(qa, ka, va, bp, seq)
            ob = _band_prompt(qb, kb, vb, bias_p, bp, seq)
            oa_s = _sb_step(qa_s, ka_s, va_s, cache_sb_k[e], cache_sb_v[e], bs)
            ob_s, nbk, nbv = _band_step(qb_s, kb_s, vb_s, cache_band_k[e],
                                        cache_band_v[e], bias_s, bs)
            hp, hs, *pre = _mm_res(
                [(oa, oa_s, 0, w_out[0]), (ob, ob_s, 0, w_out[1])], w_sb,
                hp, hs, DENSE_TN, gain=norm_ffn[l])
            nb = min(BAND_HIST, seq)
            outs["sbk_p"].append(ka.reshape(bp, seq, *cache_sb_k.shape[3:]))
            outs["sbv_p"].append(va.reshape(bp, seq, *cache_sb_k.shape[3:]))
            outs["bk_p"].append(kb.reshape(bp, seq, *cache_band_k.shape[3:])[:, seq - nb:])
            outs["bv_p"].append(vb.reshape(bp, seq, *cache_band_k.shape[3:])[:, seq - nb:])
            outs["sbk_s"].append(ka_s.reshape(bs, t, *cache_sb_k.shape[3:]))
            outs["sbv_s"].append(va_s.reshape(bs, t, *cache_sb_k.shape[3:]))
            outs["bk_s"].append(nbk)
            outs["bv_s"].append(nbv)
        else:
            o = l // 2

            widths = (wq_c, wk_c, wk_c)
            dtypes = (BF16, F32, F32)

            w_in = [_Weight(w_qkv_c, o, 0, c) for c in (0, wq_c, wq_c + wk_c)]
            w_out = _Weight(w_out_c, o, 0)

            (q, q_s), (k, k_s), (v, v_s) = (
                _proj(a_p, a_s, w, n, dt, rope=rp, rope_s=rs, norm=norm)
                for w, n, dt, rp, rs in zip(w_in, widths, dtypes,
                                            (rope_p, rope_p, None),
                                            (rope_s, rope_s, None)))
            oc = _swa_prompt(q, k, v, sinks[o], bp, seq)
            oc_s, nck, ncv = _swa_step(
                q_s, k_s, v_s, cache_swa_k[o].reshape(bs, p_swa, wk_c),
                cache_swa_v[o].reshape(bs, p_swa, wk_c), sinks[o], bs)
            hp, hs, *pre = _mm_res([(oc, oc_s, 0, w_out)], wq_c, hp, hs,
                                   DENSE_TN, gain=norm_ffn[l])
            nw = min(WINDOW, seq)
            outs["ck_p"].append(k.reshape(bp, seq, *cache_swa_k.shape[3:])[:, seq - nw:])
            outs["cv_p"].append(v.reshape(bp, seq, *cache_swa_k.shape[3:])[:, seq - nw:])
            outs["ck_s"].append(nck.reshape(cache_swa_k.shape[1:]))
            outs["cv_s"].append(ncv.reshape(cache_swa_v.shape[1:]))
        next_gain = norm_mix[l + 1] if l + 1 < depth else None
        hp, hs, *pre = _ffn(hp, hs, pre, w_gate, w_up, w_down, l, next_gain)

    y_prompt = _rmsnorm(hp, norm_final3, 0, F32).reshape(bp, seq, dm)
    y_sample = _rmsnorm(hs, norm_final3, 0, F32).reshape(bs, t, dm)
    st = lambda name: jnp.stack(outs[name])
    return (y_prompt, y_sample,
            st("sbk_p"), st("sbv_p"), st("bk_p"), st("bv_p"), st("ck_p"), st("cv_p"),
            st("sbk_s"), st("sbv_s"), st("bk_s"), st("bv_s"), st("ck_s"), st("cv_s"))
```

```python
import functools
import math

import numpy as np
import jax
import jax.numpy as jnp
from jax import lax
from jax.experimental import pallas as pl
from jax.experimental.pallas import tpu as pltpu

F32 = jnp.float32
BF16 = jnp.bfloat16

CHUNK = 64
EPS = 1e-6
HEAD_DIM_AB = 128
BAND_PREV_CHUNKS = 8
BAND_HIST = BAND_PREV_CHUNKS * CHUNK
REL_CLIP = 2 * CHUNK
HEAD_DIM_C = 64
SWA_PREV_CHUNKS = 2
WINDOW = SWA_PREV_CHUNKS * CHUNK
ROPE_THETA = 500000.0
ROPE_DIM = HEAD_DIM_C // 4

LANES = 128
V7X_VMEM_LIMIT_BYTES = 60 * 1024 * 1024

DENSE_TM = 1024
GATE_UP_TM = 2048
GATE_UP_TN = 256
DENSE_TN = 512

NEG = -1e30

SB_TILE = 256
SB_HEADS = 4
SB_STEP_KEYS = 128
BAND_TQ = 128
BAND_HEADS = 4
STEP_HEADS = 8
SWA_TQ = 128


def _params(*sem):
    return pltpu.CompilerParams(dimension_semantics=sem,
                                vmem_limit_bytes=V7X_VMEM_LIMIT_BYTES)


def _chunk_of(pos):
    return lax.shift_right_logical(pos, int(math.log2(CHUNK)))


def _pick_tm(m):
    return DENSE_TM if m % DENSE_TM == 0 else m


def _rmsnorm_kernel(x_ref, g_ref, o_ref):
    x = x_ref[...]
    ms = jnp.mean(x * x, axis=-1, keepdims=True)
    y = x * lax.rsqrt(ms + EPS)
    o_ref[...] = (y * g_ref[...]).astype(o_ref.dtype)


def _rmsnorm(x, g3, layer, out_dtype):
    m, d = x.shape
    tm = 256
    return pl.pallas_call(
        _rmsnorm_kernel,
        grid=(m // tm,),
        in_specs=[pl.BlockSpec((tm, d), lambda i: (i, 0)),
                  pl.BlockSpec((None, 1, d), lambda i: (layer, 0, 0))],
        out_specs=pl.BlockSpec((tm, d), lambda i: (i, 0)),
        out_shape=jax.ShapeDtypeStruct((m, d), out_dtype),
        compiler_params=_params("arbitrary"),
        name="rmsnorm",
    )(x, g3)


class _Weight:
    def __init__(self, src, layer, kblk=0, col0=0):
        self.src, self.layer, self.kblk, self.col0 = src, layer, kblk, col0

    def spec(self, tk, tn):
        layer, kb, jb = self.layer, self.kblk, self.col0 // tn
        return pl.BlockSpec((None, tk, tn), lambda i, j: (layer, kb, jb + j))


def _rope_tile(x, c, s1, s2):
    return (x * c + pltpu.roll(x, 8, 1) * s1
            + pltpu.roll(x, LANES - 8, 1) * s2)


def _rider_in(x):
    return pl.BlockSpec(x.shape, lambda i, j: (0, 0), pipeline_mode=pl.Buffered(1))


def _rider_out(rows, tn, ni):
    return pl.BlockSpec((rows, tn),
                        lambda i, j: (0, jnp.where(i == ni - 1, j, 0)))


def _on_last_row_block(fn):
    pl.when(pl.program_id(0) == pl.num_programs(0) - 1)(fn)


def _lane_chunks(n):
    return [slice(c * LANES, (c + 1) * LANES) for c in range(n // LANES)]


def _proj_kernel(*refs, rope, norm):
    w_ref = refs[1]
    n_tab = 3 if rope else 0
    a_ref, tabs = refs[0], refs[2:2 + n_tab]
    sa_ref, stabs = refs[2 + n_tab], refs[3 + n_tab:3 + 2 * n_tab]
    r_ref, sr_ref = refs[3 + 2 * n_tab:5 + 2 * n_tab] if norm else (None, None)
    o_ref, so_ref = refs[-2:]

    def run(a_ref, tabs, r_ref, o_ref):
        acc = jnp.dot(a_ref[...], w_ref[...].astype(BF16),
                      preferred_element_type=F32)
        for sl in _lane_chunks(acc.shape[1]):
            x = acc[:, sl]
            if norm:
                x = x * r_ref[...]
            if rope:
                x = _rope_tile(x, *(t[...] for t in tabs))
            o_ref[:, sl] = x.astype(o_ref.dtype)

    run(a_ref, tabs, r_ref, o_ref)
    _on_last_row_block(lambda: run(sa_ref, stabs, sr_ref, so_ref))


def _proj(a, a_s, w, ncols, out_dtype, rope=None, rope_s=None, norm=None):
    m, k = a.shape
    ms = a_s.shape[0]
    tm = _pick_tm(m)
    tn = min(DENSE_TN, ncols)
    ni = m // tm
    in_specs = [pl.BlockSpec((tm, k), lambda i, j: (i, 0)), w.spec(k, tn)]
    args = [a, w.src]
    if rope is not None:
        nrep = rope[0].shape[0] // tm
        for t in rope:
            in_specs.append(pl.BlockSpec((tm, LANES), lambda i, j: (i % nrep, 0)))
            args.append(t)
    for x in (a_s,) + tuple(rope_s or ()):
        in_specs.append(_rider_in(x))
        args.append(x)
    if norm is not None:
        r, r_s = norm
        in_specs += [pl.BlockSpec((tm, LANES), lambda i, j: (i, 0)), _rider_in(r_s)]
        args += [r, r_s]
    return pl.pallas_call(
        functools.partial(_proj_kernel, rope=rope is not None,
                          norm=norm is not None),
        grid=(ni, ncols // tn),
        in_specs=in_specs,
        out_specs=[pl.BlockSpec((tm, tn), lambda i, j: (i, j)),
                   _rider_out(ms, tn, ni)],
        out_shape=[jax.ShapeDtypeStruct((m, ncols), out_dtype),
                   jax.ShapeDtypeStruct((ms, ncols), out_dtype)],
        compiler_params=_params("arbitrary", "arbitrary"),
        name="proj",
    )(*args)


def _gate_up_kernel(*refs, norm, cast):
    a_ref, wg_ref, wu_ref = refs[:3]
    r_ref = refs[3] if norm else None
    o_ref = refs[-2] if cast else refs[-1]
    a = a_ref[...]
    g = jnp.dot(a, wg_ref[...].astype(BF16), preferred_element_type=F32)
    u = jnp.dot(a, wu_ref[...].astype(BF16), preferred_element_type=F32)
    for sl in _lane_chunks(g.shape[1]):
        gc, uc = g[:, sl], u[:, sl]
        if norm:
            gc, uc = gc * r_ref[...], uc * r_ref[...]
        o_ref[:, sl] = (gc * jax.nn.sigmoid(gc) * uc).astype(o_ref.dtype)
    if cast:
        refs[-1][...] = refs[-3][...].astype(BF16)


def _gate_up(a, wg, wu, f, norm=None, cast=None):
    m, k = a.shape
    tm = GATE_UP_TM if m % GATE_UP_TM == 0 else m
    tn = GATE_UP_TN
    ni, nj = m // tm, f // tn
    in_specs = [pl.BlockSpec((tm, k), lambda i, j: (i, 0)),
                wg.spec(k, tn), wu.spec(k, tn)]
    args = [a, wg.src, wu.src]
    out_specs = [pl.BlockSpec((tm, tn), lambda i, j: (i, j))]
    out_shape = [jax.ShapeDtypeStruct((m, f), BF16)]
    if norm is not None:
        in_specs.append(pl.BlockSpec((tm, LANES), lambda i, j: (i, 0),
                                     pipeline_mode=pl.Buffered(1)))
        args.append(norm)
    if cast is not None:
        w3, layer = cast
        rows, n = w3.shape[1] // (ni * nj), w3.shape[2]
        assert rows * ni * nj == w3.shape[1] and rows % 16 == 0
        in_specs.append(pl.BlockSpec((None, rows, n),
                                     lambda i, j: (layer, i * nj + j, 0)))
        args.append(w3)
        out_specs.append(pl.BlockSpec((rows, n), lambda i, j: (i * nj + j, 0)))
        out_shape.append(jax.ShapeDtypeStruct(w3.shape[1:], BF16))
    out = pl.pallas_call(
        functools.partial(_gate_up_kernel, norm=norm is not None,
                          cast=cast is not None),
        grid=(ni, nj),
        in_specs=in_specs,
        out_specs=out_specs,
        out_shape=out_shape,
        compiler_params=_params("arbitrary", "arbitrary"),
        name="gate_up",
    )(*args)
    return out if cast is not None else out[0]


def _mm_res_kernel(*refs, nterms, stats):
    n = nterms
    w_refs = refs[n:2 * n]
    gain_ref = refs[3 * n + 2] if stats else None
    outs = refs[3 * n + 2 + bool(stats):]

    def run(a_refs, res_ref, o_ref, b_ref, ss_ref):
        acc = res_ref[...]
        for a_ref, w_ref in zip(a_refs, w_refs):
            acc = acc + jnp.dot(a_ref[...], w_ref[...].astype(BF16),
                                preferred_element_type=F32)
        o_ref[...] = acc
        if stats:
            _row_stats(acc, gain_ref, b_ref, ss_ref)

    p_extra = (outs[2], outs[3]) if stats else (None, None)
    s_extra = (outs[4], outs[5]) if stats else (None, None)
    run(refs[:n], refs[2 * n], outs[0], *p_extra)
    _on_last_row_block(
        lambda: run(refs[2 * n + 1:3 * n + 1], refs[3 * n + 1], outs[1], *s_extra))


def _mm_res(terms, tk, res, res_s, tn, gain=None):
    stats = gain is not None
    m, n = res.shape
    ms = res_s.shape[0]
    tm = _pick_tm(m)
    ni = m // tm
    in_specs, args = [], []
    for a, _, ka, _ in terms:
        in_specs.append(pl.BlockSpec((tm, tk), lambda i, j, ka=ka: (i, ka)))
        args.append(a)
    for _, _, _, w in terms:
        in_specs.append(w.spec(tk, tn))
        args.append(w.src)
    tile = pl.BlockSpec((tm, tn), lambda i, j: (i, j))
    tile_s = _rider_out(ms, tn, ni)
    in_specs.append(tile)
    args.append(res)
    for _, a_s, ka, _ in terms:
        in_specs.append(pl.BlockSpec((ms, tk), lambda i, j, ka=ka: (0, ka),
                                     pipeline_mode=pl.Buffered(1)))
        args.append(a_s)
    in_specs.append(tile_s)
    args.append(res_s)
    out_specs = [tile, tile_s]
    out_shape = [jax.ShapeDtypeStruct((m, n), F32),
                 jax.ShapeDtypeStruct((ms, n), F32)]
    if stats:
        in_specs.append(pl.BlockSpec((1, tn), lambda i, j: (0, j)))
        args.append(gain.astype(F32).reshape(1, n))
        out_specs += [tile, pl.BlockSpec((tm, LANES), lambda i, j: (i, 0)),
                      tile_s, pl.BlockSpec((ms, LANES), lambda i, j: (0, 0))]
        out_shape += [jax.ShapeDtypeStruct((m, n), BF16),
                      jax.ShapeDtypeStruct((m, LANES), F32),
                      jax.ShapeDtypeStruct((ms, n), BF16),
                      jax.ShapeDtypeStruct((ms, LANES), F32)]
    return pl.pallas_call(
        functools.partial(_mm_res_kernel, nterms=len(terms), stats=stats),
        grid=(ni, n // tn),
        in_specs=in_specs,
        out_specs=out_specs,
        out_shape=out_shape,
        compiler_params=_params("arbitrary", "arbitrary"),
        name="mm_res",
    )(*args)


def _row_stats(acc, gain_ref, b_ref, ss_ref):
    b_ref[...] = (acc * gain_ref[...]).astype(BF16)
    part = sum(acc[:, sl] * acc[:, sl] for sl in _lane_chunks(acc.shape[1]))
    j = pl.program_id(1)

    @pl.when(j == 0)
    def _():
        ss_ref[...] = part

    @pl.when(j != 0)
    def _():
        ss_ref[...] += part


def _down_kernel(*refs, stats):
    a_refs, w_ref, res_ref = refs[0:2], refs[2], refs[3]
    sa_refs, sres_ref = refs[4:6], refs[6]
    gain_ref = refs[7] if stats else None
    outs = refs[7 + bool(stats):]
    k = pl.program_id(2)

    def run(x_refs, r_ref, o_ref, b_ref, ss_ref):
        @pl.when(k == 0)
        def _():
            o_ref[...] = r_ref[...] + jnp.dot(
                x_refs[0][...], w_ref[...].astype(BF16),
                preferred_element_type=F32)

        @pl.when(k == 1)
        def _():
            acc = o_ref[...] + jnp.dot(
                x_refs[1][...], w_ref[...].astype(BF16),
                preferred_element_type=F32)
            o_ref[...] = acc
            if stats:
                _row_stats(acc, gain_ref, b_ref, ss_ref)

    p_extra = (outs[2], outs[3]) if stats else (None, None)
    s_extra = (outs[4], outs[5]) if stats else (None, None)
    run(a_refs, res_ref, outs[0], *p_extra)
    _on_last_row_block(lambda: run(sa_refs, sres_ref, outs[1], *s_extra))


def _down(act, act_s, w, res, res_s, gain=None):
    stats = gain is not None
    m, n = res.shape
    ms = res_s.shape[0]
    tk = act.shape[1] // 2
    tm, tn = _pick_tm(m), DENSE_TN
    ni = m // tm
    one = dict(pipeline_mode=pl.Buffered(1))
    last = ni - 1
    tile = pl.BlockSpec((tm, tn), lambda i, j, k: (i, j))
    tile_s = pl.BlockSpec((ms, tn),
                          lambda i, j, k: (0, jnp.where(i == last, j, 0)))
    in_specs = (
        [pl.BlockSpec((tm, tk), lambda i, j, k, h=h: (i, h), **one) for h in (0, 1)]
        + [pl.BlockSpec((tk, tn), lambda i, j, k: (k, j)), tile]
        + [pl.BlockSpec((ms, tk), lambda i, j, k, h=h: (0, h), **one) for h in (0, 1)]
        + [tile_s])
    args = [act, act, w, res, act_s, act_s, res_s]
    out_specs = [tile, tile_s]
    out_shape = [jax.ShapeDtypeStruct((m, n), F32),
                 jax.ShapeDtypeStruct((ms, n), F32)]
    if stats:
        in_specs.append(pl.BlockSpec((1, tn), lambda i, j, k: (0, j)))
        args.append(gain.astype(F32).reshape(1, n))
        out_specs += [tile, pl.BlockSpec((tm, LANES), lambda i, j, k: (i, 0)),
                      tile_s, pl.BlockSpec((ms, LANES), lambda i, j, k: (0, 0))]
        out_shape += [jax.ShapeDtypeStruct((m, n), BF16),
                      jax.ShapeDtypeStruct((m, LANES), F32),
                      jax.ShapeDtypeStruct((ms, n), BF16),
                      jax.ShapeDtypeStruct((ms, LANES), F32)]
    return pl.pallas_call(
        functools.partial(_down_kernel, stats=stats),
        grid=(ni, n // tn, 2),
        in_specs=in_specs,
        out_specs=out_specs,
        out_shape=out_shape,
        compiler_params=_params("arbitrary", "arbitrary", "arbitrary"),
        name="down",
    )(*args)


def _row_scale_kernel(ss_ref, o_ref, *, d):
    tot = jnp.sum(ss_ref[...], axis=1, keepdims=True)
    o_ref[...] = jnp.broadcast_to(lax.rsqrt(tot / d + EPS), o_ref.shape)


def _row_scale(ss, d):
    m = ss.shape[0]
    tm = _pick_tm(m)
    blk = pl.BlockSpec((tm, LANES), lambda i: (i, 0))
    return pl.pallas_call(
        functools.partial(_row_scale_kernel, d=d),
        grid=(m // tm,),
        in_specs=[blk],
        out_specs=blk,
        out_shape=jax.ShapeDtypeStruct(ss.shape, F32),
        compiler_params=_params("arbitrary"),
        name="row_scale",
    )(ss)


_DN_T = (((1,), (1,)), ((), ()))


def _head(h, d=HEAD_DIM_AB):
    return slice(h * d, (h + 1) * d)


def _heads_major(x):
    return pltpu.einshape("phd->hpd", x)


def _softmax_stage(ss, sinks=None):
    ms = [jnp.max(s, axis=1, keepdims=True) for s in ss]
    if sinks is not None:
        ms = [jnp.maximum(m, sk) for m, sk in zip(ms, sinks)]
    ps = [jnp.exp(s - m) for s, m in zip(ss, ms)]
    ls = [jnp.sum(p, axis=1, keepdims=True) for p in ps]
    if sinks is not None:
        ls = [l + jnp.exp(sk - m) for l, sk, m in zip(ls, sinks, ms)]
    return [p.astype(BF16) for p in ps], ls


def _sb_logs(z, valid):
    lb = jnp.minimum(z, 0.0) - jnp.log(1.0 + jnp.exp(-jnp.abs(z)))
    l1m = lb - z
    if valid is not None:
        l1m = jnp.where(valid, l1m, 0.0)
    return lb, l1m


def _split_hi_lo(x):
    hi = x.astype(BF16)
    lo = (x - hi.astype(F32)).astype(BF16)
    return jnp.concatenate([hi, lo], axis=1)


def _suffix_matrix(n):
    u = np.tril(np.ones((n, n), np.float32), -1)
    return jnp.asarray(np.concatenate([u, u], axis=0), BF16)


def _sb_prompt_kernel(q_ref, k_ref, v_ref, uu_ref, o_ref, kb_ref, vb_ref,
                      oacc_ref, cacc_ref, *, scale):
    qi = pl.program_id(2)
    tq = q_ref.shape[0]
    nh = oacc_ref.shape[0]

    @pl.when(qi == 0)
    def _():
        kb_ref[...] = k_ref[...].astype(BF16)
        vb_ref[...] = v_ref[...].astype(BF16)

    oacc_ref[...] = jnp.zeros(oacc_ref.shape, F32)
    cacc_ref[...] = jnp.zeros(cacc_ref.shape, F32)

    def tile(j, masked):
        start = pl.multiple_of(j * tq, tq)
        valid = None
        if masked:
            row = lax.broadcasted_iota(jnp.int32, (tq, tq), 0)
            col = lax.broadcasted_iota(jnp.int32, (tq, tq), 1)
            valid = col < row
        zs = [lax.dot_general(q_ref[:, _head(h)],
                              kb_ref[pl.ds(start, tq), _head(h)], _DN_T,
                              preferred_element_type=F32) * scale
              for h in range(nh)]
        logs = [_sb_logs(z, valid) for z in zs]
        sufs = [jnp.dot(_split_hi_lo(l1m), uu_ref[...],
                        preferred_element_type=F32) for _, l1m in logs]
        ws = [jnp.exp(lb + suf + cacc_ref[h])
              for h, ((lb, _), suf) in enumerate(zip(logs, sufs))]
        if masked:
            ws = [jnp.where(valid, w, 0.0) for w in ws]
        for h in range(nh):
            oacc_ref[h] += jnp.dot(ws[h].astype(BF16),
                                   vb_ref[pl.ds(start, tq), _head(h)],
                                   preferred_element_type=F32)
            cacc_ref[h] += jnp.sum(logs[h][1], axis=1, keepdims=True)

    tile(qi, True)

    def body(t, c):
        tile(qi - 1 - t, False)
        return c

    lax.fori_loop(0, qi, body, 0)
    for h in range(nh):
        o_ref[:, _head(h)] = oacc_ref[h].astype(o_ref.dtype)


def _sb_prompt(q, k, v, batch, seq):
    m, w = q.shape
    d = HEAD_DIM_AB
    nh = SB_HEADS
    tq = SB_TILE
    nq = seq // tq
    qo = pl.BlockSpec((tq, nh * d), lambda b, g, i: (b * nq + i, g))
    kv = pl.BlockSpec((seq, nh * d), lambda b, g, i: (b, g))
    return pl.pallas_call(
        functools.partial(_sb_prompt_kernel, scale=d ** -0.5),
        grid=(batch, w // (nh * d), nq),
        in_specs=[qo, kv, kv, pl.BlockSpec((2 * tq, tq), lambda b, g, i: (0, 0))],
        out_specs=qo,
        out_shape=jax.ShapeDtypeStruct((m, w), BF16),
        scratch_shapes=[pltpu.VMEM((seq, nh * d), BF16),
                        pltpu.VMEM((seq, nh * d), BF16),
                        pltpu.VMEM((nh, tq, d), F32),
                        pltpu.VMEM((nh, tq, 1), F32)],
        compiler_params=_params("arbitrary", "arbitrary", "arbitrary"),
        name="sb_prompt",
    )(q, k, v, _suffix_matrix(tq))


def _sb_step_kernel(q_ref, kn_ref, vn_ref, kc_ref, vc_ref, uu_ref, o_ref,
                    kb_ref, vb_ref, *, scale):
    t = q_ref.shape[0]
    past, nh, d = kc_ref.shape
    tk = uu_ref.shape[1]
    n = past + tk
    kb_ref[:, 0:past, :] = _heads_major(kc_ref[...]).astype(BF16)
    vb_ref[:, 0:past, :] = _heads_major(vc_ref[...]).astype(BF16)
    for h in range(nh):
        kb_ref[h, past:past + t] = kn_ref[:, _head(h)].astype(BF16)
        vb_ref[h, past:past + t] = vn_ref[:, _head(h)].astype(BF16)
        kb_ref[h, past + t:n] = jnp.zeros((tk - t, d), BF16)
        vb_ref[h, past + t:n] = jnp.zeros((tk - t, d), BF16)

    z = jnp.concatenate(
        [lax.dot_general(q_ref[:, _head(h)], kb_ref[h], _DN_T,
                         preferred_element_type=F32) for h in range(nh)],
        axis=0) * scale
    row = lax.broadcasted_iota(jnp.int32, z.shape, 0) & (t - 1)
    col = lax.broadcasted_iota(jnp.int32, z.shape, 1)
    valid = col < past + row
    lb, l1m = _sb_logs(z, valid)
    ntiles = n // tk
    sufs = [jnp.dot(_split_hi_lo(l1m[:, j * tk:(j + 1) * tk]), uu_ref[...],
                    preferred_element_type=F32) for j in range(ntiles)]
    tots = [jnp.sum(l1m[:, j * tk:(j + 1) * tk], axis=1, keepdims=True)
            for j in range(ntiles)]
    carries = [None] * ntiles
    carry = jnp.zeros_like(tots[0])
    for j in reversed(range(ntiles)):
        carries[j] = carry
        carry = carry + tots[j]
    after = jnp.concatenate([sufs[j] + carries[j] for j in range(ntiles)],
                            axis=1)
    w = jnp.where(valid, jnp.exp(lb + after), 0.0).astype(BF16)
    for h in range(nh):
        o = jnp.dot(w[h * t:(h + 1) * t], vb_ref[h], preferred_element_type=F32)
        o_ref[:, _head(h)] = o.astype(o_ref.dtype)


def _sb_step(q, k_new, v_new, k_cache, v_cache, batch):
    m, w = q.shape
    _, past, heads, d = k_cache.shape
    t = m // batch
    tk = SB_STEP_KEYS
    nh = STEP_HEADS
    assert past % tk == 0 and t <= tk and t & (t - 1) == 0
    row = pl.BlockSpec((t, nh * d), lambda b, g: (b, g))
    cache = pl.BlockSpec((None, past, nh, d), lambda b, g: (b, 0, g, 0))
    return pl.pallas_call(
        functools.partial(_sb_step_kernel, scale=d ** -0.5),
        grid=(batch, heads // nh),
        in_specs=[row, row, row, cache, cache,
                  pl.BlockSpec((2 * tk, tk), lambda b, g: (0, 0))],
        out_specs=row,
        out_shape=jax.ShapeDtypeStruct((m, w), BF16),
        scratch_shapes=[pltpu.VMEM((nh, past + tk, d), BF16),
                        pltpu.VMEM((nh, past + tk, d), BF16)],
        compiler_params=_params("arbitrary", "arbitrary"),
        name="sb_step",
    )(q, k_new, v_new, k_cache, v_cache, _suffix_matrix(tk))


def _band_bias_kernel(r_ref, o_ref, *, n_valid):
    tq, w = o_ref.shape
    wr = r_ref.shape[1]
    t = jnp.broadcast_to(r_ref[...], (tq, wr))
    t = pltpu.roll(t, 0, 1, stride=1, stride_axis=0)
    row = lax.broadcasted_iota(jnp.int32, (tq, w), 0)
    col = lax.broadcasted_iota(jnp.int32, (tq, w), 1)
    qc = _chunk_of(row)
    kc = _chunk_of(col)
    valid = (kc >= qc) & (kc <= qc + BAND_PREV_CHUNKS) & (col < n_valid)
    o_ref[...] = jnp.where(valid, t[:, :w], NEG)


def _band_bias(rel, tq, n_valid):
    heads = rel.shape[0]
    w = BAND_HIST + BAND_TQ
    wr = w + LANES
    p = np.arange(wr)
    delta = np.where(p < w, p, p - wr)
    idx = np.clip(BAND_HIST - delta, -REL_CLIP, REL_CLIP) + REL_CLIP
    r = jnp.take(rel.astype(F32), jnp.asarray(idx, jnp.int32), axis=1)
    r = r.reshape(heads, 1, wr)
    return pl.pallas_call(
        functools.partial(_band_bias_kernel, n_valid=n_valid),
        grid=(heads,),
        in_specs=[pl.BlockSpec((None, 1, wr), lambda h: (h, 0, 0))],
        out_specs=pl.BlockSpec((None, tq, w), lambda h: (h, 0, 0)),
        out_shape=jax.ShapeDtypeStruct((heads, tq, w), F32),
        compiler_params=_params("arbitrary"),
        name="band_bias",
    )(r)


def _band_prompt_kernel(q_ref, k_ref, v_ref, bias_ref, o_ref, kp_ref, vp_ref,
                        *, scale, tq):
    seq, wd = k_ref.shape
    d = HEAD_DIM_AB
    nh = wd // d
    w = bias_ref.shape[1]
    hist = w - tq
    kp_ref[0:hist] = jnp.zeros((hist, wd), BF16)
    vp_ref[0:hist] = jnp.zeros((hist, wd), BF16)
    kp_ref[hist:hist + seq] = k_ref[...].astype(BF16)
    vp_ref[hist:hist + seq] = v_ref[...].astype(BF16)
    col = lax.broadcasted_iota(jnp.int32, (tq, w), 1)

    def tile(i, at_start):
        start = pl.multiple_of(i * tq, tq)
        ss = []
        for h in range(nh):
            s = lax.dot_general(q_ref[pl.ds(start, tq), _head(h)],
                                kp_ref[pl.ds(start, w), _head(h)], _DN_T,
                                preferred_element_type=F32)
            s = s * scale + bias_ref[h * tq:(h + 1) * tq, :]
            if at_start:
                s = jnp.where(col >= hist - start, s, NEG)
            ss.append(s)
        ps, ls = _softmax_stage(ss)
        for h in range(nh):
            o = jnp.dot(ps[h], vp_ref[pl.ds(start, w), _head(h)],
                        preferred_element_type=F32) / ls[h]
            o_ref[pl.ds(start, tq), _head(h)] = o.astype(o_ref.dtype)

    def loop(lo, hi, at_start):
        def body(i, c):
            tile(i, at_start)
            return c
        lax.fori_loop(lo, hi, body, 0)

    n_tiles = seq // tq
    n_start = min(hist // tq, n_tiles)
    loop(0, n_start, True)
    loop(n_start, n_tiles, False)


def _band_prompt(q, k, v, bias, batch, seq):
    m, wd = q.shape
    d = HEAD_DIM_AB
    nh = BAND_HEADS
    heads, tq, w = bias.shape
    bias = bias.reshape(heads // nh, nh * tq, w)
    blk = pl.BlockSpec((seq, nh * d), lambda b, g: (b, g))
    return pl.pallas_call(
        functools.partial(_band_prompt_kernel, scale=d ** -0.5, tq=tq),
        grid=(batch, heads // nh),
        in_specs=[blk, blk, blk,
                  pl.BlockSpec((None, nh * tq, w), lambda b, g: (g, 0, 0))],
        out_specs=blk,
        out_shape=jax.ShapeDtypeStruct((m, wd), BF16),
        scratch_shapes=[pltpu.VMEM((BAND_HIST + seq, nh * d), BF16),
                        pltpu.VMEM((BAND_HIST + seq, nh * d), BF16)],
        compiler_params=_params("arbitrary", "arbitrary"),
        name="band_prompt",
    )(q, k, v, bias)


def _roll_cache(new_ref, cache, new_rows_ref, t):
    past, nh, d = cache.shape
    new_ref[0:past - t] = cache[t:]
    for h in range(nh):
        new_ref[past - t:past, h, :] = new_rows_ref[:, _head(h, d)]


def _band_step_kernel(q_ref, kn_ref, vn_ref, kc_ref, vc_ref, bias_ref, o_ref,
                      nk_ref, nv_ref, kb_ref, vb_ref, *, scale):
    t = q_ref.shape[0]
    past, nh, d = kc_ref.shape
    w = bias_ref.shape[1]
    kc, vc = kc_ref[...], vc_ref[...]
    _roll_cache(nk_ref, kc, kn_ref, t)
    _roll_cache(nv_ref, vc, vn_ref, t)
    kb_ref[:, 0:past, :] = _heads_major(kc).astype(BF16)
    vb_ref[:, 0:past, :] = _heads_major(vc).astype(BF16)
    for h in range(nh):
        kb_ref[h, past:past + t] = kn_ref[:, _head(h)].astype(BF16)
        vb_ref[h, past:past + t] = vn_ref[:, _head(h)].astype(BF16)
        kb_ref[h, past + t:w] = jnp.zeros((w - past - t, d), BF16)
        vb_ref[h, past + t:w] = jnp.zeros((w - past - t, d), BF16)
    ss = [lax.dot_general(q_ref[:, _head(h)], kb_ref[h], _DN_T,
                          preferred_element_type=F32) * scale
          + bias_ref[h * t:(h + 1) * t, :] for h in range(nh)]
    ps, ls = _softmax_stage(ss)
    for h in range(nh):
        o = jnp.dot(ps[h], vb_ref[h], preferred_element_type=F32) / ls[h]
        o_ref[:, _head(h)] = o.astype(o_ref.dtype)


def _band_step(q, k_new, v_new, k_cache, v_cache, bias, batch):
    m, wd = q.shape
    _, past, heads, d = k_cache.shape
    t = m // batch
    nh = STEP_HEADS
    w = bias.shape[2]
    assert past == BAND_HIST and past + t <= w
    bias = bias.reshape(heads // nh, nh * t, w)
    row = pl.BlockSpec((t, nh * d), lambda b, g: (b, g))
    cache = pl.BlockSpec((None, past, nh, d), lambda b, g: (b, 0, g, 0))
    return pl.pallas_call(
        functools.partial(_band_step_kernel, scale=d ** -0.5),
        grid=(batch, heads // nh),
        in_specs=[row, row, row, cache, cache,
                  pl.BlockSpec((None, nh * t, w), lambda b, g: (g, 0, 0))],
        out_specs=[row, cache, cache],
        out_shape=[jax.ShapeDtypeStruct((m, wd), BF16),
                   jax.ShapeDtypeStruct(k_cache.shape, F32),
                   jax.ShapeDtypeStruct(v_cache.shape, F32)],
        scratch_shapes=[pltpu.VMEM((nh, w, d), BF16),
                        pltpu.VMEM((nh, w, d), BF16)],
        compiler_params=_params("arbitrary", "arbitrary"),
        name="band_step",
    )(q, k_new, v_new, k_cache, v_cache, bias)


def _split_pair(blk):
    rolled = pltpu.roll(blk, HEAD_DIM_C, 1)
    lane = lax.broadcasted_iota(jnp.int32, blk.shape, 1)
    low = lane < HEAD_DIM_C
    zero = jnp.zeros_like(blk)
    return [jnp.where(low, blk, zero), jnp.where(low, zero, rolled),
            jnp.where(low, rolled, zero), jnp.where(low, zero, blk)]


def _swa_stage(qs, kls, khs, vls, vhs, valid, sinks, scale):
    ss = []
    for q, kl, kh in zip(qs, kls, khs):
        for kx in (kl, kh):
            s = lax.dot_general(q, kx, _DN_T, preferred_element_type=F32)
            ss.append(jnp.where(valid, s * scale, NEG))
    ps, ls = _softmax_stage(ss, sinks)
    return [jnp.dot(ps[2 * i], vls[i], preferred_element_type=F32) / ls[2 * i]
            + jnp.dot(ps[2 * i + 1], vhs[i], preferred_element_type=F32)
            / ls[2 * i + 1] for i in range(len(qs))]


def _swa_prompt_kernel(sinks_ref, q_ref, k_ref, v_ref, o_ref, ks_ref, vs_ref,
                       *, scale, tq):
    hp = pl.program_id(1)
    seq = k_ref.shape[0]
    band = WINDOW + tq
    npair = q_ref.shape[1] // LANES
    group = npair // 2

    for src, dst in ((k_ref, ks_ref), (v_ref, vs_ref)):
        for n, x in enumerate(_split_pair(src[...])):
            dst[n, 0:WINDOW] = jnp.zeros((WINDOW, LANES), BF16)
            dst[n, WINDOW:WINDOW + seq] = x.astype(BF16)

    row = lax.broadcasted_iota(jnp.int32, (tq, band), 0)
    col = lax.broadcasted_iota(jnp.int32, (tq, band), 1)
    qc = _chunk_of(row)
    kc = _chunk_of(col)
    in_band = (kc >= qc) & (kc <= qc + SWA_PREV_CHUNKS)
    sinks = [sinks_ref[hp * 2 * npair + n] for n in range(2 * npair)]

    def body(i, c):
        start = pl.multiple_of(i * tq, tq)
        valid = in_band & (col >= WINDOW - start)
        kv = [[r[n, pl.ds(start, band), :] for n in range(4)]
              for r in (ks_ref, vs_ref)]
        qs = [q_ref[pl.ds(start, tq), _head(p, LANES)] for p in range(npair)]
        sel = [2 * (p // group) for p in range(npair)]
        outs = _swa_stage(qs, [kv[0][s] for s in sel], [kv[0][s + 1] for s in sel],
                          [kv[1][s] for s in sel], [kv[1][s + 1] for s in sel],
                          valid, sinks, scale)
        for p in range(npair):
            o_ref[pl.ds(start, tq), _head(p, LANES)] = outs[p].astype(o_ref.dtype)
        return c

    lax.fori_loop(0, seq // tq, body, 0)


def _swa_prompt(q, k, v, sinks, batch, seq):
    m, wq = q.shape
    wk = k.shape[1]
    npairs = wk // LANES
    qw = wq // npairs
    qo = pl.BlockSpec((seq, qw), lambda b, h, s: (b, h))
    kv = pl.BlockSpec((seq, LANES), lambda b, h, s: (b, h))
    grid_spec = pltpu.PrefetchScalarGridSpec(
        num_scalar_prefetch=1,
        grid=(batch, npairs),
        in_specs=[qo, kv, kv],
        out_specs=qo,
        scratch_shapes=[pltpu.VMEM((4, WINDOW + seq, LANES), BF16),
                        pltpu.VMEM((4, WINDOW + seq, LANES), BF16)],
    )
    return pl.pallas_call(
        functools.partial(_swa_prompt_kernel, scale=HEAD_DIM_C ** -0.5,
                          tq=SWA_TQ),
        grid_spec=grid_spec,
        out_shape=jax.ShapeDtypeStruct((m, wq), BF16),
        compiler_params=_params("arbitrary", "arbitrary"),
        name="swa_prompt",
    )(sinks, q, k, v)


def _swa_step_kernel(sinks_ref, q_ref, kn_ref, vn_ref, kc_ref, vc_ref, o_ref,
                     nk_ref, nv_ref, *, scale):
    t = q_ref.shape[0]
    past, wk = kc_ref.shape
    band = 2 * LANES
    nkv = wk // LANES
    npair = q_ref.shape[1] // LANES
    group = npair // (2 * nkv)
    kc, vc = kc_ref[...], vc_ref[...]
    kn, vn = kn_ref[...], vn_ref[...]
    nk_ref[0:past - t] = kc[t:]
    nv_ref[0:past - t] = vc[t:]
    nk_ref[past - t:past] = kn
    nv_ref[past - t:past] = vn
    pad = jnp.zeros((band - past - t, wk), F32)
    k_all = jnp.concatenate([kc, kn, pad], axis=0)
    v_all = jnp.concatenate([vc, vn, pad], axis=0)
    col = lax.broadcasted_iota(jnp.int32, (t, band), 1)
    valid = col < past + t
    kx = [[x.astype(BF16) for x in _split_pair(k_all[:, _head(g, LANES)])]
          for g in range(nkv)]
    vx = [[x.astype(BF16) for x in _split_pair(v_all[:, _head(g, LANES)])]
          for g in range(nkv)]
    qs = [q_ref[:, _head(p, LANES)] for p in range(npair)]
    where = [(p // (2 * group), 2 * ((p // group) % 2)) for p in range(npair)]
    sinks = [sinks_ref[n] for n in range(2 * npair)]
    outs = _swa_stage(qs, [kx[g][s] for g, s in where],
                      [kx[g][s + 1] for g, s in where],
                      [vx[g][s] for g, s in where],
                      [vx[g][s + 1] for g, s in where], valid, sinks, scale)
    for p in range(npair):
        o_ref[:, _head(p, LANES)] = outs[p].astype(o_ref.dtype)


def _swa_step(q, k_new, v_new, k_cache, v_cache, sinks, batch):
    m, wq = q.shape
    wk = k_new.shape[1]
    t = m // batch
    past = k_cache.shape[1]
    assert past + t <= 2 * LANES
    qrow = pl.BlockSpec((t, wq), lambda b, s: (b, 0))
    krow = pl.BlockSpec((t, wk), lambda b, s: (b, 0))
    cache = pl.BlockSpec((None, past, wk), lambda b, s: (b, 0, 0))
    grid_spec = pltpu.PrefetchScalarGridSpec(
        num_scalar_prefetch=1,
        grid=(batch,),
        in_specs=[qrow, krow, krow, cache, cache],
        out_specs=[qrow, cache, cache],
    )
    return pl.pallas_call(
        functools.partial(_swa_step_kernel, scale=HEAD_DIM_C ** -0.5),
        grid_spec=grid_spec,
        out_shape=[jax.ShapeDtypeStruct((m, wq), BF16),
                   jax.ShapeDtypeStruct(k_cache.shape, F32),
                   jax.ShapeDtypeStruct(v_cache.shape, F32)],
        compiler_params=_params("arbitrary"),
        name="swa_step",
    )(sinks, q, k_new, v_new, k_cache, v_cache)


def _rope_tables(pos):
    half = ROPE_DIM // 2
    inv = jnp.power(ROPE_THETA, -jnp.arange(half, dtype=F32) / half)
    ang = pos.astype(F32)[:, None] * inv[None, :]
    cos, sin = jnp.cos(ang), jnp.sin(ang)
    n = pos.shape[0]
    reps = LANES // HEAD_DIM_C
    rest = HEAD_DIM_C - ROPE_DIM
    ones = jnp.ones((n, rest), F32)
    zeros = jnp.zeros((n, rest), F32)
    zh = jnp.zeros((n, half), F32)
    c = jnp.tile(jnp.concatenate([cos, cos, ones], axis=1), (1, reps))
    s1 = jnp.tile(jnp.concatenate([zh, sin, zeros], axis=1), (1, reps))
    s2 = jnp.tile(jnp.concatenate([-sin, zh, zeros], axis=1), (1, reps))
    return c, s1, s2


def _ffn(hp, hs, pre, w_gate, w_up, w_down, layer, next_gain):
    f = w_gate.shape[-1]
    d = hp.shape[1]
    wg, wu = _Weight(w_gate, layer), _Weight(w_up, layer)
    a_p, ss_p, a_s, ss_s = pre
    act_p, wd = _gate_up(a_p, wg, wu, f, norm=_row_scale(ss_p, d),
                         cast=(w_down, layer))
    act_s = _gate_up(a_s, wg, wu, f, norm=_row_scale(ss_s, d))
    return _down(act_p, act_s, wd, hp, hs, gain=next_gain)


def kernel(x_prompt, x_sample, cache_sb_k, cache_sb_v, cache_band_k, cache_band_v, cache_swa_k, cache_swa_v, norm_mix, norm_ffn, norm_final, w_in_ab, w_out_ab, rel_bias, w_qkv_c, w_out_c, sinks, w_gate, w_up, w_down):
    bp, seq, dm = x_prompt.shape
    bs, t, _ = x_sample.shape
    depth = norm_mix.shape[0]
    past = cache_sb_k.shape[2]
    hp = x_prompt.reshape(bp * seq, dm)
    hs = x_sample.reshape(bs * t, dm)
    norm_mix3 = norm_mix.reshape(depth, 1, dm)
    norm_final3 = norm_final.reshape(1, 1, dm)

    w_sb = cache_sb_k.shape[3] * cache_sb_k.shape[4]
    w_band = cache_band_k.shape[3] * cache_band_k.shape[4]
    wq_c = w_out_c.shape[1]
    wk_c = cache_swa_k.shape[3] * cache_swa_k.shape[4]
    p_band = cache_band_k.shape[2]
    p_swa = cache_swa_k.shape[2]

    rope_p = _rope_tables(jnp.arange(seq))
    rope_s = tuple(jnp.tile(x, (bs, 1)) for x in _rope_tables(past + jnp.arange(t)))

    outs = {name: [] for name in (
        "sbk_p", "sbv_p", "bk_p", "bv_p", "ck_p", "cv_p",
        "sbk_s", "sbv_s", "bk_s", "bv_s", "ck_s", "cv_s")}

    pre = None
    for l in range(depth):
        if pre is None:
            a_p = _rmsnorm(hp, norm_mix3, l, BF16)
            a_s = _rmsnorm(hs, norm_mix3, l, BF16)
            norm = None
        else:
            a_p, ss_p, a_s, ss_s = pre
            norm = (_row_scale(ss_p, dm), _row_scale(ss_s, dm))
        if l % 2 == 0:
            e = l // 2
            cols = (0, w_sb, 2 * w_sb, 3 * w_sb, 3 * w_sb + w_band, 3 * w_sb + 2 * w_band)
            bias_p = _band_bias(rel_bias[e], BAND_TQ, BAND_HIST + BAND_TQ)
            bias_s = _band_bias(rel_bias[e], t, p_band + t)

            widths = (w_sb, w_sb, w_sb, w_band, w_band, w_band)
            dtypes = (BF16, F32, F32, BF16, F32, F32)

            w_in = [_Weight(w_in_ab, e, 0, c) for c in cols]
            w_out = (_Weight(w_out_ab, e, 0), _Weight(w_out_ab, e, 1))

            (qa, qa_s), (ka, ka_s), (va, va_s), (qb, qb_s), (kb, kb_s), (vb, vb_s) = (
                _proj(a_p, a_s, w, n, dt, norm=norm)
                for w, n, dt in zip(w_in, widths, dtypes))
            oa = _sb_prompt(qa, ka, va, bp, seq)
            ob = _band_prompt(qb, kb, vb, bias_p, bp, seq)
            oa_s = _sb_step(qa_s, ka_s, va_s, cache_sb_k[e], cache_sb_v[e], bs)
            ob_s, nbk, nbv = _band_step(qb_s, kb_s, vb_s, cache_band_k[e],
                                        cache_band_v[e], bias_s, bs)
            hp, hs, *pre = _mm_res(
                [(oa, oa_s, 0, w_out[0]), (ob, ob_s, 0, w_out[1])], w_sb,
                hp, hs, DENSE_TN, gain=norm_ffn[l])
            nb = min(BAND_HIST, seq)
            outs["sbk_p"].append(ka.reshape(bp, seq, *cache_sb_k.shape[3:]))
            outs["sbv_p"].append(va.reshape(bp, seq, *cache_sb_k.shape[3:]))
            outs["bk_p"].append(kb.reshape(bp, seq, *cache_band_k.shape[3:])[:, seq - nb:])
            outs["bv_p"].append(vb.reshape(bp, seq, *cache_band_k.shape[3:])[:, seq - nb:])
            outs["sbk_s"].append(ka_s.reshape(bs, t, *cache_sb_k.shape[3:]))
            outs["sbv_s"].append(va_s.reshape(bs, t, *cache_sb_k.shape[3:]))
            outs["bk_s"].append(nbk)
            outs["bv_s"].append(nbv)
        else:
            o = l // 2

            widths = (wq_c, wk_c, wk_c)
            dtypes = (BF16, F32, F32)

            w_in = [_Weight(w_qkv_c, o, 0, c) for c in (0, wq_c, wq_c + wk_c)]
            w_out = _Weight(w_out_c, o, 0)

            (v, v_s), (k, k_s), (q, q_s) = (
                _proj(a_p, a_s, w, n, dt, rope=rp, rope_s=rs, norm=norm)
                for w, n, dt, rp, rs in zip(w_in[::-1], widths[::-1], dtypes[::-1],
                                            (None, rope_p, rope_p),
                                            (None, rope_s, rope_s)))
            oc = _swa_prompt(q, k, v, sinks[o], bp, seq)
            oc_s, nck, ncv = _swa_step(
                q_s, k_s, v_s, cache_swa_k[o].reshape(bs, p_swa, wk_c),
                cache_swa_v[o].reshape(bs, p_swa, wk_c), sinks[o], bs)
            hp, hs, *pre = _mm_res([(oc, oc_s, 0, w_out)], wq_c, hp, hs,
                                   DENSE_TN, gain=norm_ffn[l])
            nw = min(WINDOW, seq)
            outs["ck_p"].append(k.reshape(bp, seq, *cache_swa_k.shape[3:])[:, seq - nw:])
            outs["cv_p"].append(v.reshape(bp, seq, *cache_swa_k.shape[3:])[:, seq - nw:])
            outs["ck_s"].append(nck.reshape(cache_swa_k.shape[1:]))
            outs["cv_s"].append(ncv.reshape(cache_swa_v.shape[1:]))
        next_gain = norm_mix[l + 1] if l + 1 < depth else None
        hp, hs, *pre = _ffn(hp, hs, pre, w_gate, w_up, w_down, l, next_gain)

    y_prompt = _rmsnorm(hp, norm_final3, 0, F32).reshape(bp, seq, dm)
    y_sample = _rmsnorm(hs, norm_final3, 0, F32).reshape(bs, t, dm)
    st = lambda name: jnp.stack(outs[name])
    return (y_prompt, y_sample,
            st("sbk_p"), st("sbv_p"), st("bk_p"), st("bv_p"), st("ck_p"), st("cv_p"),
            st("sbk_s"), st("sbv_s"), st("bk_s"), st("bv_s"), st("ck_s"), st("cv_s"))
```

```python
import functools
import math

import numpy as np
import jax
import jax.numpy as jnp
from jax import lax
from jax.experimental import pallas as pl
from jax.experimental.pallas import tpu as pltpu

F32 = jnp.float32
BF16 = jnp.bfloat16

CHUNK = 64
EPS = 1e-6
HEAD_DIM_AB = 128
BAND_PREV_CHUNKS = 8
BAND_HIST = BAND_PREV_CHUNKS * CHUNK
REL_CLIP = 2 * CHUNK
HEAD_DIM_C = 64
SWA_PREV_CHUNKS = 2
WINDOW = SWA_PREV_CHUNKS * CHUNK
ROPE_THETA = 500000.0
ROPE_DIM = HEAD_DIM_C // 4

LANES = 128
V7X_VMEM_LIMIT_BYTES = 60 * 1024 * 1024

DENSE_TM = 1024
GATE_UP_TM = 2048
GATE_UP_TN = 256
DENSE_TN = 512

NEG = -1e30

SB_TILE = 256
SB_HEADS = 8
SB_STEP_KEYS = 128
BAND_TQ = 128
BAND_HEADS = 4
STEP_HEADS = 8
SWA_TQ = 128


def _params(*sem):
    return pltpu.CompilerParams(dimension_semantics=sem,
                                vmem_limit_bytes=V7X_VMEM_LIMIT_BYTES)


def _chunk_of(pos):
    return lax.shift_right_logical(pos, int(math.log2(CHUNK)))


def _pick_tm(m):
    return DENSE_TM if m % DENSE_TM == 0 else m


def _rmsnorm_kernel(x_ref, g_ref, o_ref):
    x = x_ref[...]
    ms = jnp.mean(x * x, axis=-1, keepdims=True)
    y = x * lax.rsqrt(ms + EPS)
    o_ref[...] = (y * g_ref[...]).astype(o_ref.dtype)


def _rmsnorm(x, g3, layer, out_dtype):
    m, d = x.shape
    tm = 256
    return pl.pallas_call(
        _rmsnorm_kernel,
        grid=(m // tm,),
        in_specs=[pl.BlockSpec((tm, d), lambda i: (i, 0)),
                  pl.BlockSpec((None, 1, d), lambda i: (layer, 0, 0))],
        out_specs=pl.BlockSpec((tm, d), lambda i: (i, 0)),
        out_shape=jax.ShapeDtypeStruct((m, d), out_dtype),
        compiler_params=_params("arbitrary"),
        name="rmsnorm",
    )(x, g3)


class _Weight:
    def __init__(self, src, layer, kblk=0, col0=0):
        self.src, self.layer, self.kblk, self.col0 = src, layer, kblk, col0

    def spec(self, tk, tn):
        layer, kb, jb = self.layer, self.kblk, self.col0 // tn
        return pl.BlockSpec((None, tk, tn), lambda i, j: (layer, kb, jb + j))


def _rope_tile(x, c, s1, s2):
    return (x * c + pltpu.roll(x, 8, 1) * s1
            + pltpu.roll(x, LANES - 8, 1) * s2)


def _rider_in(x):
    return pl.BlockSpec(x.shape, lambda i, j: (0, 0), pipeline_mode=pl.Buffered(1))


def _rider_out(rows, tn, ni):
    return pl.BlockSpec((rows, tn),
                        lambda i, j: (0, jnp.where(i == ni - 1, j, 0)))


def _on_last_row_block(fn):
    pl.when(pl.program_id(0) == pl.num_programs(0) - 1)(fn)


def _lane_chunks(n):
    return [slice(c * LANES, (c + 1) * LANES) for c in range(n // LANES)]


def _proj_kernel(*refs, rope, norm):
    w_ref = refs[1]
    n_tab = 3 if rope else 0
    a_ref, tabs = refs[0], refs[2:2 + n_tab]
    sa_ref, stabs = refs[2 + n_tab], refs[3 + n_tab:3 + 2 * n_tab]
    r_ref, sr_ref = refs[3 + 2 * n_tab:5 + 2 * n_tab] if norm else (None, None)
    o_ref, so_ref = refs[-2:]

    def run(a_ref, tabs, r_ref, o_ref):
        acc = jnp.dot(a_ref[...], w_ref[...].astype(BF16),
                      preferred_element_type=F32)
        for sl in _lane_chunks(acc.shape[1]):
            x = acc[:, sl]
            if norm:
                x = x * r_ref[...]
            if rope:
                x = _rope_tile(x, *(t[...] for t in tabs))
            o_ref[:, sl] = x.astype(o_ref.dtype)

    run(a_ref, tabs, r_ref, o_ref)
    _on_last_row_block(lambda: run(sa_ref, stabs, sr_ref, so_ref))


def _proj(a, a_s, w, ncols, out_dtype, rope=None, rope_s=None, norm=None):
    m, k = a.shape
    ms = a_s.shape[0]
    tm = _pick_tm(m)
    tn = min(DENSE_TN, ncols)
    ni = m // tm
    in_specs = [pl.BlockSpec((tm, k), lambda i, j: (i, 0)), w.spec(k, tn)]
    args = [a, w.src]
    if rope is not None:
        nrep = rope[0].shape[0] // tm
        for t in rope:
            in_specs.append(pl.BlockSpec((tm, LANES), lambda i, j: (i % nrep, 0)))
            args.append(t)
    for x in (a_s,) + tuple(rope_s or ()):
        in_specs.append(_rider_in(x))
        args.append(x)
    if norm is not None:
        r, r_s = norm
        in_specs += [pl.BlockSpec((tm, LANES), lambda i, j: (i, 0)), _rider_in(r_s)]
        args += [r, r_s]
    return pl.pallas_call(
        functools.partial(_proj_kernel, rope=rope is not None,
                          norm=norm is not None),
        grid=(ni, ncols // tn),
        in_specs=in_specs,
        out_specs=[pl.BlockSpec((tm, tn), lambda i, j: (i, j)),
                   _rider_out(ms, tn, ni)],
        out_shape=[jax.ShapeDtypeStruct((m, ncols), out_dtype),
                   jax.ShapeDtypeStruct((ms, ncols), out_dtype)],
        compiler_params=_params("arbitrary", "arbitrary"),
        name="proj",
    )(*args)


def _gate_up_kernel(*refs, norm, cast):
    a_ref, wg_ref, wu_ref = refs[:3]
    r_ref = refs[3] if norm else None
    o_ref = refs[-2] if cast else refs[-1]
    a = a_ref[...]
    g = jnp.dot(a, wg_ref[...].astype(BF16), preferred_element_type=F32)
    u = jnp.dot(a, wu_ref[...].astype(BF16), preferred_element_type=F32)
    for sl in _lane_chunks(g.shape[1]):
        gc, uc = g[:, sl], u[:, sl]
        if norm:
            gc, uc = gc * r_ref[...], uc * r_ref[...]
        o_ref[:, sl] = (gc * jax.nn.sigmoid(gc) * uc).astype(o_ref.dtype)
    if cast:
        refs[-1][...] = refs[-3][...].astype(BF16)


def _gate_up(a, wg, wu, f, norm=None, cast=None):
    m, k = a.shape
    tm = GATE_UP_TM if m % GATE_UP_TM == 0 else m
    tn = GATE_UP_TN
    ni, nj = m // tm, f // tn
    in_specs = [pl.BlockSpec((tm, k), lambda i, j: (i, 0)),
                wg.spec(k, tn), wu.spec(k, tn)]
    args = [a, wg.src, wu.src]
    out_specs = [pl.BlockSpec((tm, tn), lambda i, j: (i, j))]
    out_shape = [jax.ShapeDtypeStruct((m, f), BF16)]
    if norm is not None:
        in_specs.append(pl.BlockSpec((tm, LANES), lambda i, j: (i, 0),
                                     pipeline_mode=pl.Buffered(1)))
        args.append(norm)
    if cast is not None:
        w3, layer = cast
        rows, n = w3.shape[1] // (ni * nj), w3.shape[2]
        assert rows * ni * nj == w3.shape[1] and rows % 16 == 0
        in_specs.append(pl.BlockSpec((None, rows, n),
                                     lambda i, j: (layer, i * nj + j, 0)))
        args.append(w3)
        out_specs.append(pl.BlockSpec((rows, n), lambda i, j: (i * nj + j, 0)))
        out_shape.append(jax.ShapeDtypeStruct(w3.shape[1:], BF16))
    out = pl.pallas_call(
        functools.partial(_gate_up_kernel, norm=norm is not None,
                          cast=cast is not None),
        grid=(ni, nj),
        in_specs=in_specs,
        out_specs=out_specs,
        out_shape=out_shape,
        compiler_params=_params("arbitrary", "arbitrary"),
        name="gate_up",
    )(*args)
    return out if cast is not None else out[0]


def _mm_res_kernel(*refs, nterms, stats):
    n = nterms
    w_refs = refs[n:2 * n]
    gain_ref = refs[3 * n + 2] if stats else None
    outs = refs[3 * n + 2 + bool(stats):]

    def run(a_refs, res_ref, o_ref, b_ref, ss_ref):
        acc = res_ref[...]
        for a_ref, w_ref in zip(a_refs, w_refs):
            acc = acc + jnp.dot(a_ref[...], w_ref[...].astype(BF16),
                                preferred_element_type=F32)
        o_ref[...] = acc
        if stats:
            _row_stats(acc, gain_ref, b_ref, ss_ref)

    p_extra = (outs[2], outs[3]) if stats else (None, None)
    s_extra = (outs[4], outs[5]) if stats else (None, None)
    run(refs[:n], refs[2 * n], outs[0], *p_extra)
    _on_last_row_block(
        lambda: run(refs[2 * n + 1:3 * n + 1], refs[3 * n + 1], outs[1], *s_extra))


def _mm_res(terms, tk, res, res_s, tn, gain=None):
    stats = gain is not None
    m, n = res.shape
    ms = res_s.shape[0]
    tm = _pick_tm(m)
    ni = m // tm
    in_specs, args = [], []
    for a, _, ka, _ in terms:
        in_specs.append(pl.BlockSpec((tm, tk), lambda i, j, ka=ka: (i, ka)))
        args.append(a)
    for _, _, _, w in terms:
        in_specs.append(w.spec(tk, tn))
        args.append(w.src)
    tile = pl.BlockSpec((tm, tn), lambda i, j: (i, j))
    tile_s = _rider_out(ms, tn, ni)
    in_specs.append(tile)
    args.append(res)
    for _, a_s, ka, _ in terms:
        in_specs.append(pl.BlockSpec((ms, tk), lambda i, j, ka=ka: (0, ka),
                                     pipeline_mode=pl.Buffered(1)))
        args.append(a_s)
    in_specs.append(tile_s)
    args.append(res_s)
    out_specs = [tile, tile_s]
    out_shape = [jax.ShapeDtypeStruct((m, n), F32),
                 jax.ShapeDtypeStruct((ms, n), F32)]
    if stats:
        in_specs.append(pl.BlockSpec((1, tn), lambda i, j: (0, j)))
        args.append(gain.astype(F32).reshape(1, n))
        out_specs += [tile, pl.BlockSpec((tm, LANES), lambda i, j: (i, 0)),
                      tile_s, pl.BlockSpec((ms, LANES), lambda i, j: (0, 0))]
        out_shape += [jax.ShapeDtypeStruct((m, n), BF16),
                      jax.ShapeDtypeStruct((m, LANES), F32),
                      jax.ShapeDtypeStruct((ms, n), BF16),
                      jax.ShapeDtypeStruct((ms, LANES), F32)]
    return pl.pallas_call(
        functools.partial(_mm_res_kernel, nterms=len(terms), stats=stats),
        grid=(ni, n // tn),
        in_specs=in_specs,
        out_specs=out_specs,
        out_shape=out_shape,
        compiler_params=_params("arbitrary", "arbitrary"),
        name="mm_res",
    )(*args)


def _row_stats(acc, gain_ref, b_ref, ss_ref):
    b_ref[...] = (acc * gain_ref[...]).astype(BF16)
    part = sum(acc[:, sl] * acc[:, sl] for sl in _lane_chunks(acc.shape[1]))
    j = pl.program_id(1)

    @pl.when(j == 0)
    def _():
        ss_ref[...] = part

    @pl.when(j != 0)
    def _():
        ss_ref[...] += part


def _down_kernel(*refs, stats):
    a_refs, w_ref, res_ref = refs[0:2], refs[2], refs[3]
    sa_refs, sres_ref = refs[4:6], refs[6]
    gain_ref = refs[7] if stats else None
    outs = refs[7 + bool(stats):]
    k = pl.program_id(2)

    def run(x_refs, r_ref, o_ref, b_ref, ss_ref):
        @pl.when(k == 0)
        def _():
            o_ref[...] = r_ref[...] + jnp.dot(
                x_refs[0][...], w_ref[...].astype(BF16),
                preferred_element_type=F32)

        @pl.when(k == 1)
        def _():
            acc = o_ref[...] + jnp.dot(
                x_refs[1][...], w_ref[...].astype(BF16),
                preferred_element_type=F32)
            o_ref[...] = acc
            if stats:
                _row_stats(acc, gain_ref, b_ref, ss_ref)

    p_extra = (outs[2], outs[3]) if stats else (None, None)
    s_extra = (outs[4], outs[5]) if stats else (None, None)
    run(a_refs, res_ref, outs[0], *p_extra)
    _on_last_row_block(lambda: run(sa_refs, sres_ref, outs[1], *s_extra))


def _down(act, act_s, w, res, res_s, gain=None):
    stats = gain is not None
    m, n = res.shape
    ms = res_s.shape[0]
    tk = act.shape[1] // 2
    tm, tn = _pick_tm(m), DENSE_TN
    ni = m // tm
    one = dict(pipeline_mode=pl.Buffered(1))
    last = ni - 1
    tile = pl.BlockSpec((tm, tn), lambda i, j, k: (i, j))
    tile_s = pl.BlockSpec((ms, tn),
                          lambda i, j, k: (0, jnp.where(i == last, j, 0)))
    in_specs = (
        [pl.BlockSpec((tm, tk), lambda i, j, k, h=h: (i, h), **one) for h in (0, 1)]
        + [pl.BlockSpec((tk, tn), lambda i, j, k: (k, j)), tile]
        + [pl.BlockSpec((ms, tk), lambda i, j, k, h=h: (0, h), **one) for h in (0, 1)]
        + [tile_s])
    args = [act, act, w, res, act_s, act_s, res_s]
    out_specs = [tile, tile_s]
    out_shape = [jax.ShapeDtypeStruct((m, n), F32),
                 jax.ShapeDtypeStruct((ms, n), F32)]
    if stats:
        in_specs.append(pl.BlockSpec((1, tn), lambda i, j, k: (0, j)))
        args.append(gain.astype(F32).reshape(1, n))
        out_specs += [tile, pl.BlockSpec((tm, LANES), lambda i, j, k: (i, 0)),
                      tile_s, pl.BlockSpec((ms, LANES), lambda i, j, k: (0, 0))]
        out_shape += [jax.ShapeDtypeStruct((m, n), BF16),
                      jax.ShapeDtypeStruct((m, LANES), F32),
                      jax.ShapeDtypeStruct((ms, n), BF16),
                      jax.ShapeDtypeStruct((ms, LANES), F32)]
    return pl.pallas_call(
        functools.partial(_down_kernel, stats=stats),
        grid=(ni, n // tn, 2),
        in_specs=in_specs,
        out_specs=out_specs,
        out_shape=out_shape,
        compiler_params=_params("arbitrary", "arbitrary", "arbitrary"),
        name="down",
    )(*args)


def _row_scale_kernel(ss_ref, o_ref, *, d):
    tot = jnp.sum(ss_ref[...], axis=1, keepdims=True)
    o_ref[...] = jnp.broadcast_to(lax.rsqrt(tot / d + EPS), o_ref.shape)


def _row_scale(ss, d):
    m = ss.shape[0]
    tm = _pick_tm(m)
    blk = pl.BlockSpec((tm, LANES), lambda i: (i, 0))
    return pl.pallas_call(
        functools.partial(_row_scale_kernel, d=d),
        grid=(m // tm,),
        in_specs=[blk],
        out_specs=blk,
        out_shape=jax.ShapeDtypeStruct(ss.shape, F32),
        compiler_params=_params("arbitrary"),
        name="row_scale",
    )(ss)


_DN_T = (((1,), (1,)), ((), ()))


def _head(h, d=HEAD_DIM_AB):
    return slice(h * d, (h + 1) * d)


def _heads_major(x):
    return pltpu.einshape("phd->hpd", x)


def _softmax_stage(ss, sinks=None):
    ms = [jnp.max(s, axis=1, keepdims=True) for s in ss]
    if sinks is not None:
        ms = [jnp.maximum(m, sk) for m, sk in zip(ms, sinks)]
    ps = [jnp.exp(s - m) for s, m in zip(ss, ms)]
    ls = [jnp.sum(p, axis=1, keepdims=True) for p in ps]
    if sinks is not None:
        ls = [l + jnp.exp(sk - m) for l, sk, m in zip(ls, sinks, ms)]
    return [p.astype(BF16) for p in ps], ls


def _sb_logs(z, valid):
    lb = jnp.minimum(z, 0.0) - jnp.log(1.0 + jnp.exp(-jnp.abs(z)))
    l1m = lb - z
    if valid is not None:
        l1m = jnp.where(valid, l1m, 0.0)
    return lb, l1m


def _split_hi_lo(x):
    hi = x.astype(BF16)
    lo = (x - hi.astype(F32)).astype(BF16)
    return jnp.concatenate([hi, lo], axis=1)


def _suffix_matrix(n):
    u = np.tril(np.ones((n, n), np.float32), -1)
    return jnp.asarray(np.concatenate([u, u], axis=0), BF16)


def _sb_prompt_kernel(q_ref, k_ref, v_ref, uu_ref, o_ref, kc_ref, vc_ref,
                      kb_ref, vb_ref, oacc_ref, cacc_ref, *, scale):
    qi = pl.program_id(2)
    tq = q_ref.shape[0]
    nh = oacc_ref.shape[0]

    @pl.when(qi == 0)
    def _():
        kb_ref[...] = k_ref[...].astype(BF16)
        vb_ref[...] = v_ref[...].astype(BF16)

    rows = pl.ds(pl.multiple_of(qi * tq, tq), tq)
    kc_ref[...] = k_ref[rows, :].reshape(kc_ref.shape)
    vc_ref[...] = v_ref[rows, :].reshape(vc_ref.shape)

    oacc_ref[...] = jnp.zeros(oacc_ref.shape, F32)
    cacc_ref[...] = jnp.zeros(cacc_ref.shape, F32)

    def tile(j, masked):
        start = pl.multiple_of(j * tq, tq)
        valid = None
        if masked:
            row = lax.broadcasted_iota(jnp.int32, (tq, tq), 0)
            col = lax.broadcasted_iota(jnp.int32, (tq, tq), 1)
            valid = col < row
        zs = [lax.dot_general(q_ref[:, _head(h)],
                              kb_ref[pl.ds(start, tq), _head(h)], _DN_T,
                              preferred_element_type=F32) * scale
              for h in range(nh)]
        logs = [_sb_logs(z, valid) for z in zs]
        sufs = [jnp.dot(_split_hi_lo(l1m), uu_ref[...],
                        preferred_element_type=F32) for _, l1m in logs]
        ws = [jnp.exp(lb + suf + cacc_ref[h])
              for h, ((lb, _), suf) in enumerate(zip(logs, sufs))]
        if masked:
            ws = [jnp.where(valid, w, 0.0) for w in ws]
        for h in range(nh):
            oacc_ref[h] += jnp.dot(ws[h].astype(BF16),
                                   vb_ref[pl.ds(start, tq), _head(h)],
                                   preferred_element_type=F32)
            cacc_ref[h] += jnp.sum(logs[h][1], axis=1, keepdims=True)

    tile(qi, True)

    def body(t, c):
        tile(qi - 1 - t, False)
        return c

    lax.fori_loop(0, qi, body, 0)
    for h in range(nh):
        o_ref[:, _head(h)] = oacc_ref[h].astype(o_ref.dtype)


def _sb_prompt(q, k, v, batch, seq):
    m, w = q.shape
    d = HEAD_DIM_AB
    nh = SB_HEADS
    tq = SB_TILE
    nq = seq // tq
    qo = pl.BlockSpec((tq, nh * d), lambda b, g, i: (b * nq + i, g))
    kv = pl.BlockSpec((seq, nh * d), lambda b, g, i: (b, g))
    cache = pl.BlockSpec((None, tq, nh, d), lambda b, g, i: (b, i, g, 0))
    cache_shape = jax.ShapeDtypeStruct((batch, seq, w // d, d), F32)
    return pl.pallas_call(
        functools.partial(_sb_prompt_kernel, scale=d ** -0.5),
        grid=(batch, w // (nh * d), nq),
        in_specs=[qo, kv, kv, pl.BlockSpec((2 * tq, tq), lambda b, g, i: (0, 0))],
        out_specs=[qo, cache, cache],
        out_shape=[jax.ShapeDtypeStruct((m, w), BF16), cache_shape, cache_shape],
        scratch_shapes=[pltpu.VMEM((seq, nh * d), BF16),
                        pltpu.VMEM((seq, nh * d), BF16),
                        pltpu.VMEM((nh, tq, d), F32),
                        pltpu.VMEM((nh, tq, 1), F32)],
        compiler_params=_params("arbitrary", "arbitrary", "arbitrary"),
        name="sb_prompt",
    )(q, k, v, _suffix_matrix(tq))


def _sb_step_kernel(q_ref, kn_ref, vn_ref, kc_ref, vc_ref, uu_ref, o_ref,
                    kb_ref, vb_ref, *, scale):
    t = q_ref.shape[0]
    past, nh, d = kc_ref.shape
    tk = uu_ref.shape[1]
    n = past + tk
    kb_ref[:, 0:past, :] = _heads_major(kc_ref[...]).astype(BF16)
    vb_ref[:, 0:past, :] = _heads_major(vc_ref[...]).astype(BF16)
    for h in range(nh):
        kb_ref[h, past:past + t] = kn_ref[:, _head(h)].astype(BF16)
        vb_ref[h, past:past + t] = vn_ref[:, _head(h)].astype(BF16)
        kb_ref[h, past + t:n] = jnp.zeros((tk - t, d), BF16)
        vb_ref[h, past + t:n] = jnp.zeros((tk - t, d), BF16)

    z = jnp.concatenate(
        [lax.dot_general(q_ref[:, _head(h)], kb_ref[h], _DN_T,
                         preferred_element_type=F32) for h in range(nh)],
        axis=0) * scale
    row = lax.broadcasted_iota(jnp.int32, z.shape, 0) & (t - 1)
    col = lax.broadcasted_iota(jnp.int32, z.shape, 1)
    valid = col < past + row
    lb, l1m = _sb_logs(z, valid)
    ntiles = n // tk
    sufs = [jnp.dot(_split_hi_lo(l1m[:, j * tk:(j + 1) * tk]), uu_ref[...],
                    preferred_element_type=F32) for j in range(ntiles)]
    tots = [jnp.sum(l1m[:, j * tk:(j + 1) * tk], axis=1, keepdims=True)
            for j in range(ntiles)]
    carries = [None] * ntiles
    carry = jnp.zeros_like(tots[0])
    for j in reversed(range(ntiles)):
        carries[j] = carry
        carry = carry + tots[j]
    after = jnp.concatenate([sufs[j] + carries[j] for j in range(ntiles)],
                            axis=1)
    w = jnp.where(valid, jnp.exp(lb + after), 0.0).astype(BF16)
    for h in range(nh):
        o = jnp.dot(w[h * t:(h + 1) * t], vb_ref[h], preferred_element_type=F32)
        o_ref[:, _head(h)] = o.astype(o_ref.dtype)


def _sb_step(q, k_new, v_new, k_cache, v_cache, batch):
    m, w = q.shape
    _, past, heads, d = k_cache.shape
    t = m // batch
    tk = SB_STEP_KEYS
    nh = STEP_HEADS
    assert past % tk == 0 and t <= tk and t & (t - 1) == 0
    row = pl.BlockSpec((t, nh * d), lambda b, g: (b, g))
    cache = pl.BlockSpec((None, past, nh, d), lambda b, g: (b, 0, g, 0))
    return pl.pallas_call(
        functools.partial(_sb_step_kernel, scale=d ** -0.5),
        grid=(batch, heads // nh),
        in_specs=[row, row, row, cache, cache,
                  pl.BlockSpec((2 * tk, tk), lambda b, g: (0, 0))],
        out_specs=row,
        out_shape=jax.ShapeDtypeStruct((m, w), BF16),
        scratch_shapes=[pltpu.VMEM((nh, past + tk, d), BF16),
                        pltpu.VMEM((nh, past + tk, d), BF16)],
        compiler_params=_params("arbitrary", "arbitrary"),
        name="sb_step",
    )(q, k_new, v_new, k_cache, v_cache, _suffix_matrix(tk))


def _band_bias_kernel(r_ref, o_ref, *, n_valid):
    tq, w = o_ref.shape
    wr = r_ref.shape[1]
    t = jnp.broadcast_to(r_ref[...], (tq, wr))
    t = pltpu.roll(t, 0, 1, stride=1, stride_axis=0)
    row = lax.broadcasted_iota(jnp.int32, (tq, w), 0)
    col = lax.broadcasted_iota(jnp.int32, (tq, w), 1)
    qc = _chunk_of(row)
    kc = _chunk_of(col)
    valid = (kc >= qc) & (kc <= qc + BAND_PREV_CHUNKS) & (col < n_valid)
    o_ref[...] = jnp.where(valid, t[:, :w], NEG)


def _band_bias(rel, tq, n_valid):
    heads = rel.shape[0]
    w = BAND_HIST + BAND_TQ
    wr = w + LANES
    p = np.arange(wr)
    delta = np.where(p < w, p, p - wr)
    idx = np.clip(BAND_HIST - delta, -REL_CLIP, REL_CLIP) + REL_CLIP
    r = jnp.take(rel.astype(F32), jnp.asarray(idx, jnp.int32), axis=1)
    r = r.reshape(heads, 1, wr)
    return pl.pallas_call(
        functools.partial(_band_bias_kernel, n_valid=n_valid),
        grid=(heads,),
        in_specs=[pl.BlockSpec((None, 1, wr), lambda h: (h, 0, 0))],
        out_specs=pl.BlockSpec((None, tq, w), lambda h: (h, 0, 0)),
        out_shape=jax.ShapeDtypeStruct((heads, tq, w), F32),
        compiler_params=_params("arbitrary"),
        name="band_bias",
    )(r)


def _band_prompt_kernel(q_ref, k_ref, v_ref, bias_ref, o_ref, kp_ref, vp_ref,
                        *, scale, tq):
    seq, wd = k_ref.shape
    d = HEAD_DIM_AB
    nh = wd // d
    w = bias_ref.shape[1]
    hist = w - tq
    kp_ref[0:hist] = jnp.zeros((hist, wd), BF16)
    vp_ref[0:hist] = jnp.zeros((hist, wd), BF16)
    kp_ref[hist:hist + seq] = k_ref[...].astype(BF16)
    vp_ref[hist:hist + seq] = v_ref[...].astype(BF16)
    col = lax.broadcasted_iota(jnp.int32, (tq, w), 1)

    def tile(i, at_start):
        start = pl.multiple_of(i * tq, tq)
        ss = []
        for h in range(nh):
            s = lax.dot_general(q_ref[pl.ds(start, tq), _head(h)],
                                kp_ref[pl.ds(start, w), _head(h)], _DN_T,
                                preferred_element_type=F32)
            s = s * scale + bias_ref[h * tq:(h + 1) * tq, :]
            if at_start:
                s = jnp.where(col >= hist - start, s, NEG)
            ss.append(s)
        ps, ls = _softmax_stage(ss)
        for h in range(nh):
            o = jnp.dot(ps[h], vp_ref[pl.ds(start, w), _head(h)],
                        preferred_element_type=F32) / ls[h]
            o_ref[pl.ds(start, tq), _head(h)] = o.astype(o_ref.dtype)

    def loop(lo, hi, at_start):
        def body(i, c):
            tile(i, at_start)
            return c
        lax.fori_loop(lo, hi, body, 0)

    n_tiles = seq // tq
    n_start = min(hist // tq, n_tiles)
    loop(0, n_start, True)
    loop(n_start, n_tiles, False)


def _band_prompt(q, k, v, bias, batch, seq):
    m, wd = q.shape
    d = HEAD_DIM_AB
    nh = BAND_HEADS
    heads, tq, w = bias.shape
    bias = bias.reshape(heads // nh, nh * tq, w)
    blk = pl.BlockSpec((seq, nh * d), lambda b, g: (b, g))
    return pl.pallas_call(
        functools.partial(_band_prompt_kernel, scale=d ** -0.5, tq=tq),
        grid=(batch, heads // nh),
        in_specs=[blk, blk, blk,
                  pl.BlockSpec((None, nh * tq, w), lambda b, g: (g, 0, 0))],
        out_specs=blk,
        out_shape=jax.ShapeDtypeStruct((m, wd), BF16),
        scratch_shapes=[pltpu.VMEM((BAND_HIST + seq, nh * d), BF16),
                        pltpu.VMEM((BAND_HIST + seq, nh * d), BF16)],
        compiler_params=_params("arbitrary", "arbitrary"),
        name="band_prompt",
    )(q, k, v, bias)


def _roll_cache(new_ref, cache, new_rows_ref, t):
    past, nh, d = cache.shape
    new_ref[0:past - t] = cache[t:]
    for h in range(nh):
        new_ref[past - t:past, h, :] = new_rows_ref[:, _head(h, d)]


def _band_step_kernel(q_ref, kn_ref, vn_ref, kc_ref, vc_ref, bias_ref, o_ref,
                      nk_ref, nv_ref, kb_ref, vb_ref, *, scale):
    t = q_ref.shape[0]
    past, nh, d = kc_ref.shape
    w = bias_ref.shape[1]
    kc, vc = kc_ref[...], vc_ref[...]
    _roll_cache(nk_ref, kc, kn_ref, t)
    _roll_cache(nv_ref, vc, vn_ref, t)
    kb_ref[:, 0:past, :] = _heads_major(kc).astype(BF16)
    vb_ref[:, 0:past, :] = _heads_major(vc).astype(BF16)
    for h in range(nh):
        kb_ref[h, past:past + t] = kn_ref[:, _head(h)].astype(BF16)
        vb_ref[h, past:past + t] = vn_ref[:, _head(h)].astype(BF16)
        kb_ref[h, past + t:w] = jnp.zeros((w - past - t, d), BF16)
        vb_ref[h, past + t:w] = jnp.zeros((w - past - t, d), BF16)
    ss = [lax.dot_general(q_ref[:, _head(h)], kb_ref[h], _DN_T,
                          preferred_element_type=F32) * scale
          + bias_ref[h * t:(h + 1) * t, :] for h in range(nh)]
    ps, ls = _softmax_stage(ss)
    for h in range(nh):
        o = jnp.dot(ps[h], vb_ref[h], preferred_element_type=F32) / ls[h]
        o_ref[:, _head(h)] = o.astype(o_ref.dtype)


def _band_step(q, k_new, v_new, k_cache, v_cache, bias, batch):
    m, wd = q.shape
    _, past, heads, d = k_cache.shape
    t = m // batch
    nh = STEP_HEADS
    w = bias.shape[2]
    assert past == BAND_HIST and past + t <= w
    bias = bias.reshape(heads // nh, nh * t, w)
    row = pl.BlockSpec((t, nh * d), lambda b, g: (b, g))
    cache = pl.BlockSpec((None, past, nh, d), lambda b, g: (b, 0, g, 0))
    return pl.pallas_call(
        functools.partial(_band_step_kernel, scale=d ** -0.5),
        grid=(batch, heads // nh),
        in_specs=[row, row, row, cache, cache,
                  pl.BlockSpec((None, nh * t, w), lambda b, g: (g, 0, 0))],
        out_specs=[row, cache, cache],
        out_shape=[jax.ShapeDtypeStruct((m, wd), BF16),
                   jax.ShapeDtypeStruct(k_cache.shape, F32),
                   jax.ShapeDtypeStruct(v_cache.shape, F32)],
        scratch_shapes=[pltpu.VMEM((nh, w, d), BF16),
                        pltpu.VMEM((nh, w, d), BF16)],
        compiler_params=_params("arbitrary", "arbitrary"),
        name="band_step",
    )(q, k_new, v_new, k_cache, v_cache, bias)


def _split_pair(blk):
    rolled = pltpu.roll(blk, HEAD_DIM_C, 1)
    lane = lax.broadcasted_iota(jnp.int32, blk.shape, 1)
    low = lane < HEAD_DIM_C
    zero = jnp.zeros_like(blk)
    return [jnp.where(low, blk, zero), jnp.where(low, zero, rolled),
            jnp.where(low, rolled, zero), jnp.where(low, zero, blk)]


def _swa_stage(qs, kls, khs, vls, vhs, valid, sinks, scale):
    ss = []
    for q, kl, kh in zip(qs, kls, khs):
        for kx in (kl, kh):
            s = lax.dot_general(q, kx, _DN_T, preferred_element_type=F32)
            ss.append(jnp.where(valid, s * scale, NEG))
    ps, ls = _softmax_stage(ss, sinks)
    return [jnp.dot(ps[2 * i], vls[i], preferred_element_type=F32) / ls[2 * i]
            + jnp.dot(ps[2 * i + 1], vhs[i], preferred_element_type=F32)
            / ls[2 * i + 1] for i in range(len(qs))]


def _swa_prompt_kernel(sinks_ref, q_ref, k_ref, v_ref, o_ref, ks_ref, vs_ref,
                       *, scale, tq):
    hp = pl.program_id(1)
    seq = k_ref.shape[0]
    band = WINDOW + tq
    npair = q_ref.shape[1] // LANES
    group = npair // 2

    for src, dst in ((k_ref, ks_ref), (v_ref, vs_ref)):
        for n, x in enumerate(_split_pair(src[...])):
            dst[n, 0:WINDOW] = jnp.zeros((WINDOW, LANES), BF16)
            dst[n, WINDOW:WINDOW + seq] = x.astype(BF16)

    row = lax.broadcasted_iota(jnp.int32, (tq, band), 0)
    col = lax.broadcasted_iota(jnp.int32, (tq, band), 1)
    qc = _chunk_of(row)
    kc = _chunk_of(col)
    in_band = (kc >= qc) & (kc <= qc + SWA_PREV_CHUNKS)
    sinks = [sinks_ref[hp * 2 * npair + n] for n in range(2 * npair)]

    def body(i, c):
        start = pl.multiple_of(i * tq, tq)
        valid = in_band & (col >= WINDOW - start)
        kv = [[r[n, pl.ds(start, band), :] for n in range(4)]
              for r in (ks_ref, vs_ref)]
        qs = [q_ref[pl.ds(start, tq), _head(p, LANES)] for p in range(npair)]
        sel = [2 * (p // group) for p in range(npair)]
        outs = _swa_stage(qs, [kv[0][s] for s in sel], [kv[0][s + 1] for s in sel],
                          [kv[1][s] for s in sel], [kv[1][s + 1] for s in sel],
                          valid, sinks, scale)
        for p in range(npair):
            o_ref[pl.ds(start, tq), _head(p, LANES)] = outs[p].astype(o_ref.dtype)
        return c

    lax.fori_loop(0, seq // tq, body, 0)


def _swa_prompt(q, k, v, sinks, batch, seq):
    m, wq = q.shape
    wk = k.shape[1]
    npairs = wk // LANES
    qw = wq // npairs
    qo = pl.BlockSpec((seq, qw), lambda b, h, s: (b, h))
    kv = pl.BlockSpec((seq, LANES), lambda b, h, s: (b, h))
    grid_spec = pltpu.PrefetchScalarGridSpec(
        num_scalar_prefetch=1,
        grid=(batch, npairs),
        in_specs=[qo, kv, kv],
        out_specs=qo,
        scratch_shapes=[pltpu.VMEM((4, WINDOW + seq, LANES), BF16),
                        pltpu.VMEM((4, WINDOW + seq, LANES), BF16)],
    )
    return pl.pallas_call(
        functools.partial(_swa_prompt_kernel, scale=HEAD_DIM_C ** -0.5,
                          tq=SWA_TQ),
        grid_spec=grid_spec,
        out_shape=jax.ShapeDtypeStruct((m, wq), BF16),
        compiler_params=_params("arbitrary", "arbitrary"),
        name="swa_prompt",
    )(sinks, q, k, v)


def _swa_step_kernel(sinks_ref, q_ref, kn_ref, vn_ref, kc_ref, vc_ref, o_ref,
                     nk_ref, nv_ref, *, scale):
    t = q_ref.shape[0]
    past, wk = kc_ref.shape
    band = 2 * LANES
    nkv = wk // LANES
    npair = q_ref.shape[1] // LANES
    group = npair // (2 * nkv)
    kc, vc = kc_ref[...], vc_ref[...]
    kn, vn = kn_ref[...], vn_ref[...]
    nk_ref[0:past - t] = kc[t:]
    nv_ref[0:past - t] = vc[t:]
    nk_ref[past - t:past] = kn
    nv_ref[past - t:past] = vn
    pad = jnp.zeros((band - past - t, wk), F32)
    k_all = jnp.concatenate([kc, kn, pad], axis=0)
    v_all = jnp.concatenate([vc, vn, pad], axis=0)
    col = lax.broadcasted_iota(jnp.int32, (t, band), 1)
    valid = col < past + t
    kx = [[x.astype(BF16) for x in _split_pair(k_all[:, _head(g, LANES)])]
          for g in range(nkv)]
    vx = [[x.astype(BF16) for x in _split_pair(v_all[:, _head(g, LANES)])]
          for g in range(nkv)]
    qs = [q_ref[:, _head(p, LANES)] for p in range(npair)]
    where = [(p // (2 * group), 2 * ((p // group) % 2)) for p in range(npair)]
    sinks = [sinks_ref[n] for n in range(2 * npair)]
    outs = _swa_stage(qs, [kx[g][s] for g, s in where],
                      [kx[g][s + 1] for g, s in where],
                      [vx[g][s] for g, s in where],
                      [vx[g][s + 1] for g, s in where], valid, sinks, scale)
    for p in range(npair):
        o_ref[:, _head(p, LANES)] = outs[p].astype(o_ref.dtype)


def _swa_step(q, k_new, v_new, k_cache, v_cache, sinks, batch):
    m, wq = q.shape
    wk = k_new.shape[1]
    t = m // batch
    past = k_cache.shape[1]
    assert past + t <= 2 * LANES
    qrow = pl.BlockSpec((t, wq), lambda b, s: (b, 0))
    krow = pl.BlockSpec((t, wk), lambda b, s: (b, 0))
    cache = pl.BlockSpec((None, past, wk), lambda b, s: (b, 0, 0))
    grid_spec = pltpu.PrefetchScalarGridSpec(
        num_scalar_prefetch=1,
        grid=(batch,),
        in_specs=[qrow, krow, krow, cache, cache],
        out_specs=[qrow, cache, cache],
    )
    return pl.pallas_call(
        functools.partial(_swa_step_kernel, scale=HEAD_DIM_C ** -0.5),
        grid_spec=grid_spec,
        out_shape=[jax.ShapeDtypeStruct((m, wq), BF16),
                   jax.ShapeDtypeStruct(k_cache.shape, F32),
                   jax.ShapeDtypeStruct(v_cache.shape, F32)],
        compiler_params=_params("arbitrary"),
        name="swa_step",
    )(sinks, q, k_new, v_new, k_cache, v_cache)


def _rope_tables(pos):
    half = ROPE_DIM // 2
    inv = jnp.power(ROPE_THETA, -jnp.arange(half, dtype=F32) / half)
    ang = pos.astype(F32)[:, None] * inv[None, :]
    cos, sin = jnp.cos(ang), jnp.sin(ang)
    n = pos.shape[0]
    reps = LANES // HEAD_DIM_C
    rest = HEAD_DIM_C - ROPE_DIM
    ones = jnp.ones((n, rest), F32)
    zeros = jnp.zeros((n, rest), F32)
    zh = jnp.zeros((n, half), F32)
    c = jnp.tile(jnp.concatenate([cos, cos, ones], axis=1), (1, reps))
    s1 = jnp.tile(jnp.concatenate([zh, sin, zeros], axis=1), (1, reps))
    s2 = jnp.tile(jnp.concatenate([-sin, zh, zeros], axis=1), (1, reps))
    return c, s1, s2


def _ffn(hp, hs, pre, w_gate, w_up, w_down, layer, next_gain):
    f = w_gate.shape[-1]
    d = hp.shape[1]
    wg, wu = _Weight(w_gate, layer), _Weight(w_up, layer)
    a_p, ss_p, a_s, ss_s = pre
    act_p, wd = _gate_up(a_p, wg, wu, f, norm=_row_scale(ss_p, d),
                         cast=(w_down, layer))
    act_s = _gate_up(a_s, wg, wu, f, norm=_row_scale(ss_s, d))
    return _down(act_p, act_s, wd, hp, hs, gain=next_gain)


def kernel(x_prompt, x_sample, cache_sb_k, cache_sb_v, cache_band_k, cache_band_v, cache_swa_k, cache_swa_v, norm_mix, norm_ffn, norm_final, w_in_ab, w_out_ab, rel_bias, w_qkv_c, w_out_c, sinks, w_gate, w_up, w_down):
    bp, seq, dm = x_prompt.shape
    bs, t, _ = x_sample.shape
    depth = norm_mix.shape[0]
    past = cache_sb_k.shape[2]
    hp = x_prompt.reshape(bp * seq, dm)
    hs = x_sample.reshape(bs * t, dm)
    norm_mix3 = norm_mix.reshape(depth, 1, dm)
    norm_final3 = norm_final.reshape(1, 1, dm)

    w_sb = cache_sb_k.shape[3] * cache_sb_k.shape[4]
    w_band = cache_band_k.shape[3] * cache_band_k.shape[4]
    wq_c = w_out_c.shape[1]
    wk_c = cache_swa_k.shape[3] * cache_swa_k.shape[4]
    p_band = cache_band_k.shape[2]
    p_swa = cache_swa_k.shape[2]

    rope_p = _rope_tables(jnp.arange(seq))
    rope_s = tuple(jnp.tile(x, (bs, 1)) for x in _rope_tables(past + jnp.arange(t)))

    outs = {name: [] for name in (
        "sbk_p", "sbv_p", "bk_p", "bv_p", "ck_p", "cv_p",
        "sbk_s", "sbv_s", "bk_s", "bv_s", "ck_s", "cv_s")}

    pre = None
    for l in range(depth):
        if pre is None:
            a_p = _rmsnorm(hp, norm_mix3, l, BF16)
            a_s = _rmsnorm(hs, norm_mix3, l, BF16)
            norm = None
        else:
            a_p, ss_p, a_s, ss_s = pre
            norm = (_row_scale(ss_p, dm), _row_scale(ss_s, dm))
        if l % 2 == 0:
            e = l // 2
            cols = (0, w_sb, 2 * w_sb, 3 * w_sb, 3 * w_sb + w_band, 3 * w_sb + 2 * w_band)
            bias_p = _band_bias(rel_bias[e], BAND_TQ, BAND_HIST + BAND_TQ)
            bias_s = _band_bias(rel_bias[e], t, p_band + t)

            widths = (w_sb, w_sb, w_sb, w_band, w_band, w_band)
            dtypes = (BF16, F32, F32, BF16, F32, F32)

            w_in = [_Weight(w_in_ab, e, 0, c) for c in cols]
            w_out = (_Weight(w_out_ab, e, 0), _Weight(w_out_ab, e, 1))

            (qa, qa_s), (ka, ka_s), (va, va_s), (qb, qb_s), (kb, kb_s), (vb, vb_s) = (
                _proj(a_p, a_s, w, n, dt, norm=norm)
                for w, n, dt in zip(w_in, widths, dtypes))
            oa, ka_cache, va_cache = _sb_prompt(qa, ka, va, bp, seq)
            ob = _band_prompt(qb, kb, vb, bias_p, bp, seq)
            oa_s = _sb_step(qa_s, ka_s, va_s, cache_sb_k[e], cache_sb_v[e], bs)
            ob_s, nbk, nbv = _band_step(qb_s, kb_s, vb_s, cache_band_k[e],
                                        cache_band_v[e], bias_s, bs)
            hp, hs, *pre = _mm_res(
                [(oa, oa_s, 0, w_out[0]), (ob, ob_s, 0, w_out[1])], w_sb,
                hp, hs, DENSE_TN, gain=norm_ffn[l])
            nb = min(BAND_HIST, seq)
            outs["sbk_p"].append(ka_cache)
            outs["sbv_p"].append(va_cache)
            outs["bk_p"].append(kb.reshape(bp, seq, *cache_band_k.shape[3:])[:, seq - nb:])
            outs["bv_p"].append(vb.reshape(bp, seq, *cache_band_k.shape[3:])[:, seq - nb:])
            outs["sbk_s"].append(ka_s.reshape(bs, t, *cache_sb_k.shape[3:]))
            outs["sbv_s"].append(va_s.reshape(bs, t, *cache_sb_k.shape[3:]))
            outs["bk_s"].append(nbk)
            outs["bv_s"].append(nbv)
        else:
            o = l // 2

            widths = (wq_c, wk_c, wk_c)
            dtypes = (BF16, F32, F32)

            w_in = [_Weight(w_qkv_c, o, 0, c) for c in (0, wq_c, wq_c + wk_c)]
            w_out = _Weight(w_out_c, o, 0)

            (q, q_s), (k, k_s), (v, v_s) = (
                _proj(a_p, a_s, w, n, dt, rope=rp, rope_s=rs, norm=norm)
                for w, n, dt, rp, rs in zip(w_in, widths, dtypes,
                                            (rope_p, rope_p, None),
                                            (rope_s, rope_s, None)))
            oc = _swa_prompt(q, k, v, sinks[o], bp, seq)
            oc_s, nck, ncv = _swa_step(
                q_s, k_s, v_s, cache_swa_k[o].reshape(bs, p_swa, wk_c),
                cache_swa_v[o].reshape(bs, p_swa, wk_c), sinks[o], bs)
            hp, hs, *pre = _mm_res([(oc, oc_s, 0, w_out)], wq_c, hp, hs,
                                   DENSE_TN, gain=norm_ffn[l])
            nw = min(WINDOW, seq)
            outs["ck_p"].append(k.reshape(bp, seq, *cache_swa_k.shape[3:])[:, seq - nw:])
            outs["cv_p"].append(v.reshape(bp, seq, *cache_swa_k.shape[3:])[:, seq - nw:])
            outs["ck_s"].append(nck.reshape(cache_swa_k.shape[1:]))
            outs["cv_s"].append(ncv.reshape(cache_swa_v.shape[1:]))
        next_gain = norm_mix[l + 1] if l + 1 < depth else None
        hp, hs, *pre = _ffn(hp, hs, pre, w_gate, w_up, w_down, l, next_gain)

    y_prompt = _rmsnorm(hp, norm_final3, 0, F32).reshape(bp, seq, dm)
    y_sample = _rmsnorm(hs, norm_final3, 0, F32).reshape(bs, t, dm)
    st = lambda name: jnp.stack(outs[name])
    return (y_prompt, y_sample,
            st("sbk_p"), st("sbv_p"), st("bk_p"), st("bv_p"), st("ck_p"), st("cv_p"),
            st("sbk_s"), st("sbv_s"), st("bk_s"), st("bv_s"), st("ck_s"), st("cv_s"))
```

```python
import functools
import math

import numpy as np
import jax
import jax.numpy as jnp
from jax import lax
from jax.experimental import pallas as pl
from jax.experimental.pallas import tpu as pltpu

F32 = jnp.float32
BF16 = jnp.bfloat16

CHUNK = 64
EPS = 1e-6
HEAD_DIM_AB = 128
BAND_PREV_CHUNKS = 8
BAND_HIST = BAND_PREV_CHUNKS * CHUNK
REL_CLIP = 2 * CHUNK
HEAD_DIM_C = 64
SWA_PREV_CHUNKS = 2
WINDOW = SWA_PREV_CHUNKS * CHUNK
ROPE_THETA = 500000.0
ROPE_DIM = HEAD_DIM_C // 4

LANES = 128
V7X_VMEM_LIMIT_BYTES = 60 * 1024 * 1024

DENSE_TM = 1024
GATE_UP_TM = 2048
GATE_UP_TN = 256
DENSE_TN = 512

NEG = -1e30

SB_TILE = 256
SB_HEADS = 8
SB_STEP_KEYS = 128
BAND_TQ = 128
BAND_HEADS = 4
STEP_HEADS = 8
SWA_TQ = 128


def _params(*sem):
    return pltpu.CompilerParams(dimension_semantics=sem,
                                vmem_limit_bytes=V7X_VMEM_LIMIT_BYTES)


def _chunk_of(pos):
    return lax.shift_right_logical(pos, int(math.log2(CHUNK)))


def _pick_tm(m):
    return DENSE_TM if m % DENSE_TM == 0 else m


def _rmsnorm_kernel(x_ref, g_ref, o_ref):
    x = x_ref[...]
    ms = jnp.mean(x * x, axis=-1, keepdims=True)
    y = x * lax.rsqrt(ms + EPS)
    o_ref[...] = (y * g_ref[...]).astype(o_ref.dtype)


def _rmsnorm(x, g3, layer, out_dtype):
    m, d = x.shape
    tm = 256
    return pl.pallas_call(
        _rmsnorm_kernel,
        grid=(m // tm,),
        in_specs=[pl.BlockSpec((tm, d), lambda i: (i, 0)),
                  pl.BlockSpec((None, 1, d), lambda i: (layer, 0, 0))],
        out_specs=pl.BlockSpec((tm, d), lambda i: (i, 0)),
        out_shape=jax.ShapeDtypeStruct((m, d), out_dtype),
        compiler_params=_params("arbitrary"),
        name="rmsnorm",
    )(x, g3)


class _Weight:
    def __init__(self, src, layer, kblk=0, col0=0):
        self.src, self.layer, self.kblk, self.col0 = src, layer, kblk, col0

    def spec(self, tk, tn):
        layer, kb, jb = self.layer, self.kblk, self.col0 // tn
        return pl.BlockSpec((None, tk, tn), lambda i, j: (layer, kb, jb + j))


def _rope_tile(x, c, s1, s2):
    return (x * c + pltpu.roll(x, 8, 1) * s1
            + pltpu.roll(x, LANES - 8, 1) * s2)


def _rider_in(x):
    return pl.BlockSpec(x.shape, lambda i, j: (0, 0), pipeline_mode=pl.Buffered(1))


def _rider_out(rows, tn, ni):
    return pl.BlockSpec((rows, tn),
                        lambda i, j: (0, jnp.where(i == ni - 1, j, 0)))


def _on_last_row_block(fn):
    pl.when(pl.program_id(0) == pl.num_programs(0) - 1)(fn)


def _lane_chunks(n):
    return [slice(c * LANES, (c + 1) * LANES) for c in range(n // LANES)]


def _proj_kernel(*refs, rope, norm):
    w_ref = refs[1]
    n_tab = 3 if rope else 0
    a_ref, tabs = refs[0], refs[2:2 + n_tab]
    sa_ref, stabs = refs[2 + n_tab], refs[3 + n_tab:3 + 2 * n_tab]
    r_ref, sr_ref = refs[3 + 2 * n_tab:5 + 2 * n_tab] if norm else (None, None)
    o_ref, so_ref = refs[-2:]

    def run(a_ref, tabs, r_ref, o_ref):
        acc = jnp.dot(a_ref[...], w_ref[...].astype(BF16),
                      preferred_element_type=F32)
        for sl in _lane_chunks(acc.shape[1]):
            x = acc[:, sl]
            if norm:
                x = x * r_ref[...]
            if rope:
                x = _rope_tile(x, *(t[...] for t in tabs))
            o_ref[:, sl] = x.astype(o_ref.dtype)

    run(a_ref, tabs, r_ref, o_ref)
    _on_last_row_block(lambda: run(sa_ref, stabs, sr_ref, so_ref))


def _proj(a, a_s, w, ncols, out_dtype, rope=None, rope_s=None, norm=None):
    m, k = a.shape
    ms = a_s.shape[0]
    tm = _pick_tm(m)
    tn = min(DENSE_TN, ncols)
    ni = m // tm
    in_specs = [pl.BlockSpec((tm, k), lambda i, j: (i, 0)), w.spec(k, tn)]
    args = [a, w.src]
    if rope is not None:
        nrep = rope[0].shape[0] // tm
        for t in rope:
            in_specs.append(pl.BlockSpec((tm, LANES), lambda i, j: (i % nrep, 0)))
            args.append(t)
    for x in (a_s,) + tuple(rope_s or ()):
        in_specs.append(_rider_in(x))
        args.append(x)
    if norm is not None:
        r, r_s = norm
        in_specs += [pl.BlockSpec((tm, LANES), lambda i, j: (i, 0)), _rider_in(r_s)]
        args += [r, r_s]
    return pl.pallas_call(
        functools.partial(_proj_kernel, rope=rope is not None,
                          norm=norm is not None),
        grid=(ni, ncols // tn),
        in_specs=in_specs,
        out_specs=[pl.BlockSpec((tm, tn), lambda i, j: (i, j)),
                   _rider_out(ms, tn, ni)],
        out_shape=[jax.ShapeDtypeStruct((m, ncols), out_dtype),
                   jax.ShapeDtypeStruct((ms, ncols), out_dtype)],
        compiler_params=_params("arbitrary", "arbitrary"),
        name="proj",
    )(*args)


def _gate_up_kernel(*refs, norm, cast):
    a_ref, wg_ref, wu_ref = refs[:3]
    r_ref = refs[3] if norm else None
    o_ref = refs[-2] if cast else refs[-1]
    a = a_ref[...]
    g = jnp.dot(a, wg_ref[...].astype(BF16), preferred_element_type=F32)
    u = jnp.dot(a, wu_ref[...].astype(BF16), preferred_element_type=F32)
    for sl in _lane_chunks(g.shape[1]):
        gc, uc = g[:, sl], u[:, sl]
        if norm:
            gc, uc = gc * r_ref[...], uc * r_ref[...]
        o_ref[:, sl] = (gc * jax.nn.sigmoid(gc) * uc).astype(o_ref.dtype)
    if cast:
        refs[-1][...] = refs[-3][...].astype(BF16)


def _gate_up(a, wg, wu, f, norm=None, cast=None):
    m, k = a.shape
    tm = GATE_UP_TM if m % GATE_UP_TM == 0 else m
    tn = GATE_UP_TN
    ni, nj = m // tm, f // tn
    in_specs = [pl.BlockSpec((tm, k), lambda i, j: (i, 0)),
                wg.spec(k, tn), wu.spec(k, tn)]
    args = [a, wg.src, wu.src]
    out_specs = [pl.BlockSpec((tm, tn), lambda i, j: (i, j))]
    out_shape = [jax.ShapeDtypeStruct((m, f), BF16)]
    if norm is not None:
        in_specs.append(pl.BlockSpec((tm, LANES), lambda i, j: (i, 0),
                                     pipeline_mode=pl.Buffered(1)))
        args.append(norm)
    if cast is not None:
        w3, layer = cast
        rows, n = w3.shape[1] // (ni * nj), w3.shape[2]
        assert rows * ni * nj == w3.shape[1] and rows % 16 == 0
        in_specs.append(pl.BlockSpec((None, rows, n),
                                     lambda i, j: (layer, i * nj + j, 0)))
        args.append(w3)
        out_specs.append(pl.BlockSpec((rows, n), lambda i, j: (i * nj + j, 0)))
        out_shape.append(jax.ShapeDtypeStruct(w3.shape[1:], BF16))
    out = pl.pallas_call(
        functools.partial(_gate_up_kernel, norm=norm is not None,
                          cast=cast is not None),
        grid=(ni, nj),
        in_specs=in_specs,
        out_specs=out_specs,
        out_shape=out_shape,
        compiler_params=_params("arbitrary", "arbitrary"),
        name="gate_up",
    )(*args)
    return out if cast is not None else out[0]


def _mm_res_kernel(*refs, nterms, stats):
    n = nterms
    w_refs = refs[n:2 * n]
    gain_ref = refs[3 * n + 2] if stats else None
    outs = refs[3 * n + 2 + bool(stats):]

    def run(a_refs, res_ref, o_ref, b_ref, ss_ref):
        acc = res_ref[...]
        for a_ref, w_ref in zip(a_refs, w_refs):
            acc = acc + jnp.dot(a_ref[...], w_ref[...].astype(BF16),
                                preferred_element_type=F32)
        o_ref[...] = acc
        if stats:
            _row_stats(acc, gain_ref, b_ref, ss_ref)

    p_extra = (outs[2], outs[3]) if stats else (None, None)
    s_extra = (outs[4], outs[5]) if stats else (None, None)
    run(refs[:n], refs[2 * n], outs[0], *p_extra)
    _on_last_row_block(
        lambda: run(refs[2 * n + 1:3 * n + 1], refs[3 * n + 1], outs[1], *s_extra))


def _mm_res(terms, tk, res, res_s, tn, gain=None):
    stats = gain is not None
    m, n = res.shape
    ms = res_s.shape[0]
    tm = _pick_tm(m)
    ni = m // tm
    in_specs, args = [], []
    for a, _, ka, _ in terms:
        in_specs.append(pl.BlockSpec((tm, tk), lambda i, j, ka=ka: (i, ka)))
        args.append(a)
    for _, _, _, w in terms:
        in_specs.append(w.spec(tk, tn))
        args.append(w.src)
    tile = pl.BlockSpec((tm, tn), lambda i, j: (i, j))
    tile_s = _rider_out(ms, tn, ni)
    in_specs.append(tile)
    args.append(res)
    for _, a_s, ka, _ in terms:
        in_specs.append(pl.BlockSpec((ms, tk), lambda i, j, ka=ka: (0, ka),
                                     pipeline_mode=pl.Buffered(1)))
        args.append(a_s)
    in_specs.append(tile_s)
    args.append(res_s)
    out_specs = [tile, tile_s]
    out_shape = [jax.ShapeDtypeStruct((m, n), F32),
                 jax.ShapeDtypeStruct((ms, n), F32)]
    if stats:
        in_specs.append(pl.BlockSpec((1, tn), lambda i, j: (0, j)))
        args.append(gain.astype(F32).reshape(1, n))
        out_specs += [tile, pl.BlockSpec((tm, LANES), lambda i, j: (i, 0)),
                      tile_s, pl.BlockSpec((ms, LANES), lambda i, j: (0, 0))]
        out_shape += [jax.ShapeDtypeStruct((m, n), BF16),
                      jax.ShapeDtypeStruct((m, LANES), F32),
                      jax.ShapeDtypeStruct((ms, n), BF16),
                      jax.ShapeDtypeStruct((ms, LANES), F32)]
    return pl.pallas_call(
        functools.partial(_mm_res_kernel, nterms=len(terms), stats=stats),
        grid=(ni, n // tn),
        in_specs=in_specs,
        out_specs=out_specs,
        out_shape=out_shape,
        compiler_params=_params("arbitrary", "arbitrary"),
        name="mm_res",
    )(*args)


def _row_stats(acc, gain_ref, b_ref, ss_ref):
    b_ref[...] = (acc * gain_ref[...]).astype(BF16)
    part = sum(acc[:, sl] * acc[:, sl] for sl in _lane_chunks(acc.shape[1]))
    j = pl.program_id(1)

    @pl.when(j == 0)
    def _():
        ss_ref[...] = part

    @pl.when(j != 0)
    def _():
        ss_ref[...] += part


def _down_kernel(*refs, stats):
    a_refs, w_ref, res_ref = refs[0:2], refs[2], refs[3]
    sa_refs, sres_ref = refs[4:6], refs[6]
    gain_ref = refs[7] if stats else None
    outs = refs[7 + bool(stats):]
    k = pl.program_id(2)

    def run(x_refs, r_ref, o_ref, b_ref, ss_ref):
        @pl.when(k == 0)
        def _():
            o_ref[...] = r_ref[...] + jnp.dot(
                x_refs[0][...], w_ref[...].astype(BF16),
                preferred_element_type=F32)

        @pl.when(k == 1)
        def _():
            acc = o_ref[...] + jnp.dot(
                x_refs[1][...], w_ref[...].astype(BF16),
                preferred_element_type=F32)
            o_ref[...] = acc
            if stats:
                _row_stats(acc, gain_ref, b_ref, ss_ref)

    p_extra = (outs[2], outs[3]) if stats else (None, None)
    s_extra = (outs[4], outs[5]) if stats else (None, None)
    run(a_refs, res_ref, outs[0], *p_extra)
    _on_last_row_block(lambda: run(sa_refs, sres_ref, outs[1], *s_extra))


def _down(act, act_s, w, res, res_s, gain=None):
    stats = gain is not None
    m, n = res.shape
    ms = res_s.shape[0]
    tk = act.shape[1] // 2
    tm, tn = _pick_tm(m), DENSE_TN
    ni = m // tm
    one = dict(pipeline_mode=pl.Buffered(1))
    last = ni - 1
    tile = pl.BlockSpec((tm, tn), lambda i, j, k: (i, j))
    tile_s = pl.BlockSpec((ms, tn),
                          lambda i, j, k: (0, jnp.where(i == last, j, 0)))
    in_specs = (
        [pl.BlockSpec((tm, tk), lambda i, j, k, h=h: (i, h), **one) for h in (0, 1)]
        + [pl.BlockSpec((tk, tn), lambda i, j, k: (k, j)), tile]
        + [pl.BlockSpec((ms, tk), lambda i, j, k, h=h: (0, h), **one) for h in (0, 1)]
        + [tile_s])
    args = [act, act, w, res, act_s, act_s, res_s]
    out_specs = [tile, tile_s]
    out_shape = [jax.ShapeDtypeStruct((m, n), F32),
                 jax.ShapeDtypeStruct((ms, n), F32)]
    if stats:
        in_specs.append(pl.BlockSpec((1, tn), lambda i, j, k: (0, j)))
        args.append(gain.astype(F32).reshape(1, n))
        out_specs += [tile, pl.BlockSpec((tm, LANES), lambda i, j, k: (i, 0)),
                      tile_s, pl.BlockSpec((ms, LANES), lambda i, j, k: (0, 0))]
        out_shape += [jax.ShapeDtypeStruct((m, n), BF16),
                      jax.ShapeDtypeStruct((m, LANES), F32),
                      jax.ShapeDtypeStruct((ms, n), BF16),
                      jax.ShapeDtypeStruct((ms, LANES), F32)]
    return pl.pallas_call(
        functools.partial(_down_kernel, stats=stats),
        grid=(ni, n // tn, 2),
        in_specs=in_specs,
        out_specs=out_specs,
        out_shape=out_shape,
        compiler_params=_params("arbitrary", "arbitrary", "arbitrary"),
        name="down",
    )(*args)


def _row_scale_kernel(ss_ref, o_ref, *, d):
    tot = jnp.sum(ss_ref[...], axis=1, keepdims=True)
    o_ref[...] = jnp.broadcast_to(lax.rsqrt(tot / d + EPS), o_ref.shape)


def _row_scale(ss, d):
    m = ss.shape[0]
    tm = _pick_tm(m)
    blk = pl.BlockSpec((tm, LANES), lambda i: (i, 0))
    return pl.pallas_call(
        functools.partial(_row_scale_kernel, d=d),
        grid=(m // tm,),
        in_specs=[blk],
        out_specs=blk,
        out_shape=jax.ShapeDtypeStruct(ss.shape, F32),
        compiler_params=_params("arbitrary"),
        name="row_scale",
    )(ss)


_DN_T = (((1,), (1,)), ((), ()))


def _head(h, d=HEAD_DIM_AB):
    return slice(h * d, (h + 1) * d)


def _heads_major(x):
    return pltpu.einshape("phd->hpd", x)


def _softmax_stage(ss, sinks=None):
    ms = [jnp.max(s, axis=1, keepdims=True) for s in ss]
    if sinks is not None:
        ms = [jnp.maximum(m, sk) for m, sk in zip(ms, sinks)]
    ps = [jnp.exp(s - m) for s, m in zip(ss, ms)]
    ls = [jnp.sum(p, axis=1, keepdims=True) for p in ps]
    if sinks is not None:
        ls = [l + jnp.exp(sk - m) for l, sk, m in zip(ls, sinks, ms)]
    return [p.astype(BF16) for p in ps], ls


def _sb_logs(z, valid):
    lb = jnp.minimum(z, 0.0) - jnp.log(1.0 + jnp.exp(-jnp.abs(z)))
    l1m = lb - z
    if valid is not None:
        l1m = jnp.where(valid, l1m, 0.0)
    return lb, l1m


def _split_hi_lo(x):
    hi = x.astype(BF16)
    lo = (x - hi.astype(F32)).astype(BF16)
    return jnp.concatenate([hi, lo], axis=1)


def _suffix_matrix(n):
    u = np.tril(np.ones((n, n), np.float32), -1)
    return jnp.asarray(np.concatenate([u, u], axis=0), BF16)


def _sb_prompt_kernel(q_ref, k_ref, v_ref, uu_ref, o_ref, kc_ref, vc_ref,
                      kb_ref, vb_ref, oacc_ref, cacc_ref, *, scale):
    qi = pl.program_id(2)
    tq = q_ref.shape[0]
    nh = oacc_ref.shape[0]

    @pl.when(qi == 0)
    def _():
        kb_ref[...] = k_ref[...].astype(BF16)
        vb_ref[...] = v_ref[...].astype(BF16)

    rows = pl.ds(pl.multiple_of(qi * tq, tq), tq)
    kc_ref[...] = k_ref[rows, :].reshape(kc_ref.shape)
    vc_ref[...] = v_ref[rows, :].reshape(vc_ref.shape)

    oacc_ref[...] = jnp.zeros(oacc_ref.shape, F32)
    cacc_ref[...] = jnp.zeros(cacc_ref.shape, F32)

    def tile(j, masked):
        start = pl.multiple_of(j * tq, tq)
        valid = None
        if masked:
            row = lax.broadcasted_iota(jnp.int32, (tq, tq), 0)
            col = lax.broadcasted_iota(jnp.int32, (tq, tq), 1)
            valid = col < row
        zs = [lax.dot_general(q_ref[:, _head(h)],
                              kb_ref[pl.ds(start, tq), _head(h)], _DN_T,
                              preferred_element_type=F32) * scale
              for h in range(nh)]
        logs = [_sb_logs(z, valid) for z in zs]
        sufs = [jnp.dot(_split_hi_lo(l1m), uu_ref[...],
                        preferred_element_type=F32) for _, l1m in logs]
        ws = [jnp.exp(lb + suf + cacc_ref[h])
              for h, ((lb, _), suf) in enumerate(zip(logs, sufs))]
        if masked:
            ws = [jnp.where(valid, w, 0.0) for w in ws]
        for h in range(nh):
            oacc_ref[h] += jnp.dot(ws[h].astype(BF16),
                                   vb_ref[pl.ds(start, tq), _head(h)],
                                   preferred_element_type=F32)
            cacc_ref[h] += jnp.sum(logs[h][1], axis=1, keepdims=True)

    tile(qi, True)

    def body(t, c):
        tile(qi - 1 - t, False)
        return c

    lax.fori_loop(0, qi, body, 0)
    for h in range(nh):
        o_ref[:, _head(h)] = oacc_ref[h].astype(o_ref.dtype)


def _sb_prompt(q, k, v, batch, seq):
    m, w = q.shape
    d = HEAD_DIM_AB
    nh = SB_HEADS
    tq = SB_TILE
    nq = seq // tq
    qo = pl.BlockSpec((tq, nh * d), lambda b, g, i: (b * nq + i, g))
    kv = pl.BlockSpec((seq, nh * d), lambda b, g, i: (b, g))
    cache = pl.BlockSpec((None, tq, nh, d), lambda b, g, i: (b, i, g, 0))
    cache_shape = jax.ShapeDtypeStruct((batch, seq, w // d, d), F32)
    return pl.pallas_call(
        functools.partial(_sb_prompt_kernel, scale=d ** -0.5),
        grid=(batch, w // (nh * d), nq),
        in_specs=[qo, kv, kv, pl.BlockSpec((2 * tq, tq), lambda b, g, i: (0, 0))],
        out_specs=[qo, cache, cache],
        out_shape=[jax.ShapeDtypeStruct((m, w), BF16), cache_shape, cache_shape],
        scratch_shapes=[pltpu.VMEM((seq, nh * d), BF16),
                        pltpu.VMEM((seq, nh * d), BF16),
                        pltpu.VMEM((nh, tq, d), F32),
                        pltpu.VMEM((nh, tq, 1), F32)],
        compiler_params=_params("arbitrary", "arbitrary", "arbitrary"),
        name="sb_prompt",
    )(q, k, v, _suffix_matrix(tq))


def _sb_step_kernel(q_ref, kn_ref, vn_ref, kc_ref, vc_ref, uu_ref, o_ref,
                    kb_ref, vb_ref, *, scale):
    t = q_ref.shape[0]
    past, nh, d = kc_ref.shape
    tk = uu_ref.shape[1]
    n = past + tk
    kb_ref[:, 0:past, :] = _heads_major(kc_ref[...]).astype(BF16)
    vb_ref[:, 0:past, :] = _heads_major(vc_ref[...]).astype(BF16)
    for h in range(nh):
        kb_ref[h, past:past + t] = kn_ref[:, _head(h)].astype(BF16)
        vb_ref[h, past:past + t] = vn_ref[:, _head(h)].astype(BF16)
        kb_ref[h, past + t:n] = jnp.zeros((tk - t, d), BF16)
        vb_ref[h, past + t:n] = jnp.zeros((tk - t, d), BF16)

    z = jnp.concatenate(
        [lax.dot_general(q_ref[:, _head(h)], kb_ref[h], _DN_T,
                         preferred_element_type=F32) for h in range(nh)],
        axis=0) * scale
    row = lax.broadcasted_iota(jnp.int32, z.shape, 0) & (t - 1)
    col = lax.broadcasted_iota(jnp.int32, z.shape, 1)
    valid = col < past + row
    lb, l1m = _sb_logs(z, valid)
    ntiles = n // tk
    sufs = [jnp.dot(_split_hi_lo(l1m[:, j * tk:(j + 1) * tk]), uu_ref[...],
                    preferred_element_type=F32) for j in range(ntiles)]
    tots = [jnp.sum(l1m[:, j * tk:(j + 1) * tk], axis=1, keepdims=True)
            for j in range(ntiles)]
    carries = [None] * ntiles
    carry = jnp.zeros_like(tots[0])
    for j in reversed(range(ntiles)):
        carries[j] = carry
        carry = carry + tots[j]
    after = jnp.concatenate([sufs[j] + carries[j] for j in range(ntiles)],
                            axis=1)
    w = jnp.where(valid, jnp.exp(lb + after), 0.0).astype(BF16)
    for h in range(nh):
        o = jnp.dot(w[h * t:(h + 1) * t], vb_ref[h], preferred_element_type=F32)
        o_ref[:, _head(h)] = o.astype(o_ref.dtype)


def _sb_step(q, k_new, v_new, k_cache, v_cache, batch):
    m, w = q.shape
    _, past, heads, d = k_cache.shape
    t = m // batch
    tk = SB_STEP_KEYS
    nh = STEP_HEADS
    assert past % tk == 0 and t <= tk and t & (t - 1) == 0
    row = pl.BlockSpec((t, nh * d), lambda b, g: (b, g))
    cache = pl.BlockSpec((None, past, nh, d), lambda b, g: (b, 0, g, 0))
    return pl.pallas_call(
        functools.partial(_sb_step_kernel, scale=d ** -0.5),
        grid=(batch, heads // nh),
        in_specs=[row, row, row, cache, cache,
                  pl.BlockSpec((2 * tk, tk), lambda b, g: (0, 0))],
        out_specs=row,
        out_shape=jax.ShapeDtypeStruct((m, w), BF16),
        scratch_shapes=[pltpu.VMEM((nh, past + tk, d), BF16),
                        pltpu.VMEM((nh, past + tk, d), BF16)],
        compiler_params=_params("arbitrary", "arbitrary"),
        name="sb_step",
    )(q, k_new, v_new, k_cache, v_cache, _suffix_matrix(tk))


def _band_bias_kernel(r_ref, o_ref, *, n_valid):
    tq, w = o_ref.shape
    wr = r_ref.shape[1]
    t = jnp.broadcast_to(r_ref[...], (tq, wr))
    t = pltpu.roll(t, 0, 1, stride=1, stride_axis=0)
    row = lax.broadcasted_iota(jnp.int32, (tq, w), 0)
    col = lax.broadcasted_iota(jnp.int32, (tq, w), 1)
    qc = _chunk_of(row)
    kc = _chunk_of(col)
    valid = (kc >= qc) & (kc <= qc + BAND_PREV_CHUNKS) & (col < n_valid)
    o_ref[...] = jnp.where(valid, t[:, :w], NEG)


def _band_bias(rel, tq, n_valid):
    heads = rel.shape[0]
    w = BAND_HIST + BAND_TQ
    wr = w + LANES
    p = np.arange(wr)
    delta = np.where(p < w, p, p - wr)
    idx = np.clip(BAND_HIST - delta, -REL_CLIP, REL_CLIP) + REL_CLIP
    r = jnp.take(rel.astype(F32), jnp.asarray(idx, jnp.int32), axis=1)
    r = r.reshape(heads, 1, wr)
    return pl.pallas_call(
        functools.partial(_band_bias_kernel, n_valid=n_valid),
        grid=(heads,),
        in_specs=[pl.BlockSpec((None, 1, wr), lambda h: (h, 0, 0))],
        out_specs=pl.BlockSpec((None, tq, w), lambda h: (h, 0, 0)),
        out_shape=jax.ShapeDtypeStruct((heads, tq, w), F32),
        compiler_params=_params("arbitrary"),
        name="band_bias",
    )(r)


def _band_prompt_kernel(q_ref, k_ref, v_ref, bias_ref, o_ref, kp_ref, vp_ref,
                        *, scale, tq):
    seq, wd = k_ref.shape
    d = HEAD_DIM_AB
    nh = wd // d
    w = bias_ref.shape[1]
    hist = w - tq
    kp_ref[0:hist] = jnp.zeros((hist, wd), BF16)
    vp_ref[0:hist] = jnp.zeros((hist, wd), BF16)
    kp_ref[hist:hist + seq] = k_ref[...].astype(BF16)
    vp_ref[hist:hist + seq] = v_ref[...].astype(BF16)
    col = lax.broadcasted_iota(jnp.int32, (tq, w), 1)

    def tile(i, at_start):
        start = pl.multiple_of(i * tq, tq)
        ss = []
        for h in range(nh):
            s = lax.dot_general(q_ref[pl.ds(start, tq), _head(h)],
                                kp_ref[pl.ds(start, w), _head(h)], _DN_T,
                                preferred_element_type=F32)
            s = s * scale + bias_ref[h * tq:(h + 1) * tq, :]
            if at_start:
                s = jnp.where(col >= hist - start, s, NEG)
            ss.append(s)
        ps, ls = _softmax_stage(ss)
        for h in range(nh):
            o = jnp.dot(ps[h], vp_ref[pl.ds(start, w), _head(h)],
                        preferred_element_type=F32) / ls[h]
            o_ref[pl.ds(start, tq), _head(h)] = o.astype(o_ref.dtype)

    def loop(lo, hi, at_start):
        def body(i, c):
            tile(i, at_start)
            return c
        lax.fori_loop(lo, hi, body, 0)

    n_tiles = seq // tq
    n_start = min(hist // tq, n_tiles)
    loop(0, n_start, True)
    loop(n_start, n_tiles, False)


def _band_prompt(q, k, v, bias, batch, seq):
    m, wd = q.shape
    d = HEAD_DIM_AB
    nh = BAND_HEADS
    heads, tq, w = bias.shape
    bias = bias.reshape(heads // nh, nh * tq, w)
    blk = pl.BlockSpec((seq, nh * d), lambda b, g: (b, g))
    return pl.pallas_call(
        functools.partial(_band_prompt_kernel, scale=d ** -0.5, tq=tq),
        grid=(batch, heads // nh),
        in_specs=[blk, blk, blk,
                  pl.BlockSpec((None, nh * tq, w), lambda b, g: (g, 0, 0))],
        out_specs=blk,
        out_shape=jax.ShapeDtypeStruct((m, wd), BF16),
        scratch_shapes=[pltpu.VMEM((BAND_HIST + seq, nh * d), BF16),
                        pltpu.VMEM((BAND_HIST + seq, nh * d), BF16)],
        compiler_params=_params("arbitrary", "arbitrary"),
        name="band_prompt",
    )(q, k, v, bias)


def _cache_tail_kernel(x_ref, o_ref):
    o_ref[...] = x_ref[...].reshape(o_ref.shape)


def _cache_tail(x, batch, seq, rows):
    d = HEAD_DIM_AB
    nh = STEP_HEADS
    heads = x.shape[1] // d
    assert seq % rows == 0
    last = seq // rows - 1
    return pl.pallas_call(
        _cache_tail_kernel,
        grid=(batch, heads // nh),
        in_specs=[pl.BlockSpec((rows, nh * d),
                               lambda b, g: (b * (last + 1) + last, g))],
        out_specs=pl.BlockSpec((None, rows, nh, d), lambda b, g: (b, 0, g, 0)),
        out_shape=jax.ShapeDtypeStruct((batch, rows, heads, d), x.dtype),
        compiler_params=_params("arbitrary", "arbitrary"),
        name="cache_tail",
    )(x)


def _roll_cache(new_ref, cache, new_rows_ref, t):
    past, nh, d = cache.shape
    new_ref[0:past - t] = cache[t:]
    for h in range(nh):
        new_ref[past - t:past, h, :] = new_rows_ref[:, _head(h, d)]


def _band_step_kernel(q_ref, kn_ref, vn_ref, kc_ref, vc_ref, bias_ref, o_ref,
                      nk_ref, nv_ref, kb_ref, vb_ref, *, scale):
    t = q_ref.shape[0]
    past, nh, d = kc_ref.shape
    w = bias_ref.shape[1]
    kc, vc = kc_ref[...], vc_ref[...]
    _roll_cache(nk_ref, kc, kn_ref, t)
    _roll_cache(nv_ref, vc, vn_ref, t)
    kb_ref[:, 0:past, :] = _heads_major(kc).astype(BF16)
    vb_ref[:, 0:past, :] = _heads_major(vc).astype(BF16)
    for h in range(nh):
        kb_ref[h, past:past + t] = kn_ref[:, _head(h)].astype(BF16)
        vb_ref[h, past:past + t] = vn_ref[:, _head(h)].astype(BF16)
        kb_ref[h, past + t:w] = jnp.zeros((w - past - t, d), BF16)
        vb_ref[h, past + t:w] = jnp.zeros((w - past - t, d), BF16)
    ss = [lax.dot_general(q_ref[:, _head(h)], kb_ref[h], _DN_T,
                          preferred_element_type=F32) * scale
          + bias_ref[h * t:(h + 1) * t, :] for h in range(nh)]
    ps, ls = _softmax_stage(ss)
    for h in range(nh):
        o = jnp.dot(ps[h], vb_ref[h], preferred_element_type=F32) / ls[h]
        o_ref[:, _head(h)] = o.astype(o_ref.dtype)


def _band_step(q, k_new, v_new, k_cache, v_cache, bias, batch):
    m, wd = q.shape
    _, past, heads, d = k_cache.shape
    t = m // batch
    nh = STEP_HEADS
    w = bias.shape[2]
    assert past == BAND_HIST and past + t <= w
    bias = bias.reshape(heads // nh, nh * t, w)
    row = pl.BlockSpec((t, nh * d), lambda b, g: (b, g))
    cache = pl.BlockSpec((None, past, nh, d), lambda b, g: (b, 0, g, 0))
    return pl.pallas_call(
        functools.partial(_band_step_kernel, scale=d ** -0.5),
        grid=(batch, heads // nh),
        in_specs=[row, row, row, cache, cache,
                  pl.BlockSpec((None, nh * t, w), lambda b, g: (g, 0, 0))],
        out_specs=[row, cache, cache],
        out_shape=[jax.ShapeDtypeStruct((m, wd), BF16),
                   jax.ShapeDtypeStruct(k_cache.shape, F32),
                   jax.ShapeDtypeStruct(v_cache.shape, F32)],
        scratch_shapes=[pltpu.VMEM((nh, w, d), BF16),
                        pltpu.VMEM((nh, w, d), BF16)],
        compiler_params=_params("arbitrary", "arbitrary"),
        name="band_step",
    )(q, k_new, v_new, k_cache, v_cache, bias)


def _split_pair(blk):
    rolled = pltpu.roll(blk, HEAD_DIM_C, 1)
    lane = lax.broadcasted_iota(jnp.int32, blk.shape, 1)
    low = lane < HEAD_DIM_C
    zero = jnp.zeros_like(blk)
    return [jnp.where(low, blk, zero), jnp.where(low, zero, rolled),
            jnp.where(low, rolled, zero), jnp.where(low, zero, blk)]


def _swa_stage(qs, kls, khs, vls, vhs, valid, sinks, scale):
    ss = []
    for q, kl, kh in zip(qs, kls, khs):
        for kx in (kl, kh):
            s = lax.dot_general(q, kx, _DN_T, preferred_element_type=F32)
            ss.append(jnp.where(valid, s * scale, NEG))
    ps, ls = _softmax_stage(ss, sinks)
    return [jnp.dot(ps[2 * i], vls[i], preferred_element_type=F32) / ls[2 * i]
            + jnp.dot(ps[2 * i + 1], vhs[i], preferred_element_type=F32)
            / ls[2 * i + 1] for i in range(len(qs))]


def _swa_prompt_kernel(sinks_ref, q_ref, k_ref, v_ref, o_ref, ks_ref, vs_ref,
                       *, scale, tq):
    hp = pl.program_id(1)
    seq = k_ref.shape[0]
    band = WINDOW + tq
    npair = q_ref.shape[1] // LANES
    group = npair // 2

    for src, dst in ((k_ref, ks_ref), (v_ref, vs_ref)):
        for n, x in enumerate(_split_pair(src[...])):
            dst[n, 0:WINDOW] = jnp.zeros((WINDOW, LANES), BF16)
            dst[n, WINDOW:WINDOW + seq] = x.astype(BF16)

    row = lax.broadcasted_iota(jnp.int32, (tq, band), 0)
    col = lax.broadcasted_iota(jnp.int32, (tq, band), 1)
    qc = _chunk_of(row)
    kc = _chunk_of(col)
    in_band = (kc >= qc) & (kc <= qc + SWA_PREV_CHUNKS)
    sinks = [sinks_ref[hp * 2 * npair + n] for n in range(2 * npair)]

    def body(i, c):
        start = pl.multiple_of(i * tq, tq)
        valid = in_band & (col >= WINDOW - start)
        kv = [[r[n, pl.ds(start, band), :] for n in range(4)]
              for r in (ks_ref, vs_ref)]
        qs = [q_ref[pl.ds(start, tq), _head(p, LANES)] for p in range(npair)]
        sel = [2 * (p // group) for p in range(npair)]
        outs = _swa_stage(qs, [kv[0][s] for s in sel], [kv[0][s + 1] for s in sel],
                          [kv[1][s] for s in sel], [kv[1][s + 1] for s in sel],
                          valid, sinks, scale)
        for p in range(npair):
            o_ref[pl.ds(start, tq), _head(p, LANES)] = outs[p].astype(o_ref.dtype)
        return c

    lax.fori_loop(0, seq // tq, body, 0)


def _swa_prompt(q, k, v, sinks, batch, seq):
    m, wq = q.shape
    wk = k.shape[1]
    npairs = wk // LANES
    qw = wq // npairs
    qo = pl.BlockSpec((seq, qw), lambda b, h, s: (b, h))
    kv = pl.BlockSpec((seq, LANES), lambda b, h, s: (b, h))
    grid_spec = pltpu.PrefetchScalarGridSpec(
        num_scalar_prefetch=1,
        grid=(batch, npairs),
        in_specs=[qo, kv, kv],
        out_specs=qo,
        scratch_shapes=[pltpu.VMEM((4, WINDOW + seq, LANES), BF16),
                        pltpu.VMEM((4, WINDOW + seq, LANES), BF16)],
    )
    return pl.pallas_call(
        functools.partial(_swa_prompt_kernel, scale=HEAD_DIM_C ** -0.5,
                          tq=SWA_TQ),
        grid_spec=grid_spec,
        out_shape=jax.ShapeDtypeStruct((m, wq), BF16),
        compiler_params=_params("arbitrary", "arbitrary"),
        name="swa_prompt",
    )(sinks, q, k, v)


def _swa_step_kernel(sinks_ref, q_ref, kn_ref, vn_ref, kc_ref, vc_ref, o_ref,
                     nk_ref, nv_ref, *, scale):
    t = q_ref.shape[0]
    past, wk = kc_ref.shape
    band = 2 * LANES
    nkv = wk // LANES
    npair = q_ref.shape[1] // LANES
    group = npair // (2 * nkv)
    kc, vc = kc_ref[...], vc_ref[...]
    kn, vn = kn_ref[...], vn_ref[...]
    nk_ref[0:past - t] = kc[t:]
    nv_ref[0:past - t] = vc[t:]
    nk_ref[past - t:past] = kn
    nv_ref[past - t:past] = vn
    pad = jnp.zeros((band - past - t, wk), F32)
    k_all = jnp.concatenate([kc, kn, pad], axis=0)
    v_all = jnp.concatenate([vc, vn, pad], axis=0)
    col = lax.broadcasted_iota(jnp.int32, (t, band), 1)
    valid = col < past + t
    kx = [[x.astype(BF16) for x in _split_pair(k_all[:, _head(g, LANES)])]
          for g in range(nkv)]
    vx = [[x.astype(BF16) for x in _split_pair(v_all[:, _head(g, LANES)])]
          for g in range(nkv)]
    qs = [q_ref[:, _head(p, LANES)] for p in range(npair)]
    where = [(p // (2 * group), 2 * ((p // group) % 2)) for p in range(npair)]
    sinks = [sinks_ref[n] for n in range(2 * npair)]
    outs = _swa_stage(qs, [kx[g][s] for g, s in where],
                      [kx[g][s + 1] for g, s in where],
                      [vx[g][s] for g, s in where],
                      [vx[g][s + 1] for g, s in where], valid, sinks, scale)
    for p in range(npair):
        o_ref[:, _head(p, LANES)] = outs[p].astype(o_ref.dtype)


def _swa_step(q, k_new, v_new, k_cache, v_cache, sinks, batch):
    m, wq = q.shape
    wk = k_new.shape[1]
    t = m // batch
    past = k_cache.shape[1]
    assert past + t <= 2 * LANES
    qrow = pl.BlockSpec((t, wq), lambda b, s: (b, 0))
    krow = pl.BlockSpec((t, wk), lambda b, s: (b, 0))
    cache = pl.BlockSpec((None, past, wk), lambda b, s: (b, 0, 0))
    grid_spec = pltpu.PrefetchScalarGridSpec(
        num_scalar_prefetch=1,
        grid=(batch,),
        in_specs=[qrow, krow, krow, cache, cache],
        out_specs=[qrow, cache, cache],
    )
    return pl.pallas_call(
        functools.partial(_swa_step_kernel, scale=HEAD_DIM_C ** -0.5),
        grid_spec=grid_spec,
        out_shape=[jax.ShapeDtypeStruct((m, wq), BF16),
                   jax.ShapeDtypeStruct(k_cache.shape, F32),
                   jax.ShapeDtypeStruct(v_cache.shape, F32)],
        compiler_params=_params("arbitrary"),
        name="swa_step",
    )(sinks, q, k_new, v_new, k_cache, v_cache)


def _rope_tables(pos):
    half = ROPE_DIM // 2
    inv = jnp.power(ROPE_THETA, -jnp.arange(half, dtype=F32) / half)
    ang = pos.astype(F32)[:, None] * inv[None, :]
    cos, sin = jnp.cos(ang), jnp.sin(ang)
    n = pos.shape[0]
    reps = LANES // HEAD_DIM_C
    rest = HEAD_DIM_C - ROPE_DIM
    ones = jnp.ones((n, rest), F32)
    zeros = jnp.zeros((n, rest), F32)
    zh = jnp.zeros((n, half), F32)
    c = jnp.tile(jnp.concatenate([cos, cos, ones], axis=1), (1, reps))
    s1 = jnp.tile(jnp.concatenate([zh, sin, zeros], axis=1), (1, reps))
    s2 = jnp.tile(jnp.concatenate([-sin, zh, zeros], axis=1), (1, reps))
    return c, s1, s2


def _ffn(hp, hs, pre, w_gate, w_up, w_down, layer, next_gain):
    f = w_gate.shape[-1]
    d = hp.shape[1]
    wg, wu = _Weight(w_gate, layer), _Weight(w_up, layer)
    a_p, ss_p, a_s, ss_s = pre
    act_p, wd = _gate_up(a_p, wg, wu, f, norm=_row_scale(ss_p, d),
                         cast=(w_down, layer))
    act_s = _gate_up(a_s, wg, wu, f, norm=_row_scale(ss_s, d))
    return _down(act_p, act_s, wd, hp, hs, gain=next_gain)


def kernel(x_prompt, x_sample, cache_sb_k, cache_sb_v, cache_band_k, cache_band_v, cache_swa_k, cache_swa_v, norm_mix, norm_ffn, norm_final, w_in_ab, w_out_ab, rel_bias, w_qkv_c, w_out_c, sinks, w_gate, w_up, w_down):
    bp, seq, dm = x_prompt.shape
    bs, t, _ = x_sample.shape
    depth = norm_mix.shape[0]
    past = cache_sb_k.shape[2]
    hp = x_prompt.reshape(bp * seq, dm)
    hs = x_sample.reshape(bs * t, dm)
    norm_mix3 = norm_mix.reshape(depth, 1, dm)
    norm_final3 = norm_final.reshape(1, 1, dm)

    w_sb = cache_sb_k.shape[3] * cache_sb_k.shape[4]
    w_band = cache_band_k.shape[3] * cache_band_k.shape[4]
    wq_c = w_out_c.shape[1]
    wk_c = cache_swa_k.shape[3] * cache_swa_k.shape[4]
    p_band = cache_band_k.shape[2]
    p_swa = cache_swa_k.shape[2]

    rope_p = _rope_tables(jnp.arange(seq))
    rope_s = tuple(jnp.tile(x, (bs, 1)) for x in _rope_tables(past + jnp.arange(t)))

    outs = {name: [] for name in (
        "sbk_p", "sbv_p", "bk_p", "bv_p", "ck_p", "cv_p",
        "sbk_s", "sbv_s", "bk_s", "bv_s", "ck_s", "cv_s")}

    pre = None
    for l in range(depth):
        if pre is None:
            a_p = _rmsnorm(hp, norm_mix3, l, BF16)
            a_s = _rmsnorm(hs, norm_mix3, l, BF16)
            norm = None
        else:
            a_p, ss_p, a_s, ss_s = pre
            norm = (_row_scale(ss_p, dm), _row_scale(ss_s, dm))
        if l % 2 == 0:
            e = l // 2
            cols = (0, w_sb, 2 * w_sb, 3 * w_sb, 3 * w_sb + w_band, 3 * w_sb + 2 * w_band)
            bias_p = _band_bias(rel_bias[e], BAND_TQ, BAND_HIST + BAND_TQ)
            bias_s = _band_bias(rel_bias[e], t, p_band + t)

            widths = (w_sb, w_sb, w_sb, w_band, w_band, w_band)
            dtypes = (BF16, F32, F32, BF16, F32, F32)

            w_in = [_Weight(w_in_ab, e, 0, c) for c in cols]
            w_out = (_Weight(w_out_ab, e, 0), _Weight(w_out_ab, e, 1))

            (qa, qa_s), (ka, ka_s), (va, va_s), (qb, qb_s), (kb, kb_s), (vb, vb_s) = (
                _proj(a_p, a_s, w, n, dt, norm=norm)
                for w, n, dt in zip(w_in, widths, dtypes))
            oa, ka_cache, va_cache = _sb_prompt(qa, ka, va, bp, seq)
            ob = _band_prompt(qb, kb, vb, bias_p, bp, seq)
            oa_s = _sb_step(qa_s, ka_s, va_s, cache_sb_k[e], cache_sb_v[e], bs)
            ob_s, nbk, nbv = _band_step(qb_s, kb_s, vb_s, cache_band_k[e],
                                        cache_band_v[e], bias_s, bs)
            hp, hs, *pre = _mm_res(
                [(oa, oa_s, 0, w_out[0]), (ob, ob_s, 0, w_out[1])], w_sb,
                hp, hs, DENSE_TN, gain=norm_ffn[l])
            nb = min(BAND_HIST, seq)
            outs["sbk_p"].append(ka_cache)
            outs["sbv_p"].append(va_cache)
            outs["bk_p"].append(_cache_tail(kb, bp, seq, nb))
            outs["bv_p"].append(_cache_tail(vb, bp, seq, nb))
            outs["sbk_s"].append(ka_s.reshape(bs, t, *cache_sb_k.shape[3:]))
            outs["sbv_s"].append(va_s.reshape(bs, t, *cache_sb_k.shape[3:]))
            outs["bk_s"].append(nbk)
            outs["bv_s"].append(nbv)
        else:
            o = l // 2

            widths = (wq_c, wk_c, wk_c)
            dtypes = (BF16, F32, F32)

            w_in = [_Weight(w_qkv_c, o, 0, c) for c in (0, wq_c, wq_c + wk_c)]
            w_out = _Weight(w_out_c, o, 0)

            (q, q_s), (k, k_s), (v, v_s) = (
                _proj(a_p, a_s, w, n, dt, rope=rp, rope_s=rs, norm=norm)
                for w, n, dt, rp, rs in zip(w_in, widths, dtypes,
                                            (rope_p, rope_p, None),
                                            (rope_s, rope_s, None)))
            oc = _swa_prompt(q, k, v, sinks[o], bp, seq)
            oc_s, nck, ncv = _swa_step(
                q_s, k_s, v_s, cache_swa_k[o].reshape(bs, p_swa, wk_c),
                cache_swa_v[o].reshape(bs, p_swa, wk_c), sinks[o], bs)
            hp, hs, *pre = _mm_res([(oc, oc_s, 0, w_out)], wq_c, hp, hs,
                                   DENSE_TN, gain=norm_ffn[l])
            nw = min(WINDOW, seq)
            outs["ck_p"].append(k.reshape(bp, seq, *cache_swa_k.shape[3:])[:, seq - nw:])
            outs["cv_p"].append(v.reshape(bp, seq, *cache_swa_k.shape[3:])[:, seq - nw:])
            outs["ck_s"].append(nck.reshape(cache_swa_k.shape[1:]))
            outs["cv_s"].append(ncv.reshape(cache_swa_v.shape[1:]))
        next_gain = norm_mix[l + 1] if l + 1 < depth else None
        hp, hs, *pre = _ffn(hp, hs, pre, w_gate, w_up, w_down, l, next_gain)

    y_prompt = _rmsnorm(hp, norm_final3, 0, F32).reshape(bp, seq, dm)
    y_sample = _rmsnorm(hs, norm_final3, 0, F32).reshape(bs, t, dm)
    st = lambda name: jnp.stack(outs[name])
    return (y_prompt, y_sample,
            st("sbk_p"), st("sbv_p"), st("bk_p"), st("bv_p"), st("ck_p"), st("cv_p"),
            st("sbk_s"), st("sbv_s"), st("bk_s"), st("bv_s"), st("ck_s"), st("cv_s"))
```

```python
import functools
import math

import numpy as np
import jax
import jax.numpy as jnp
from jax import lax
from jax.experimental import pallas as pl
from jax.experimental.pallas import tpu as pltpu

F32 = jnp.float32
BF16 = jnp.bfloat16

CHUNK = 64
EPS = 1e-6
HEAD_DIM_AB = 128
BAND_PREV_CHUNKS = 8
BAND_HIST = BAND_PREV_CHUNKS * CHUNK
REL_CLIP = 2 * CHUNK
HEAD_DIM_C = 64
SWA_PREV_CHUNKS = 2
WINDOW = SWA_PREV_CHUNKS * CHUNK
ROPE_THETA = 500000.0
ROPE_DIM = HEAD_DIM_C // 4

LANES = 128
V7X_VMEM_LIMIT_BYTES = 60 * 1024 * 1024

DENSE_TM = 1024
GATE_UP_TM = 2048
GATE_UP_TN = 256
DENSE_TN = 512

NEG = -1e30

SB_TILE = 256
SB_HEADS = 8
SB_STEP_KEYS = 128
BAND_TQ = 128
BAND_HEADS = 4
STEP_HEADS = 8
SWA_TQ = 128


def _params(*sem):
    return pltpu.CompilerParams(dimension_semantics=sem,
                                vmem_limit_bytes=V7X_VMEM_LIMIT_BYTES)


def _chunk_of(pos):
    return lax.shift_right_logical(pos, int(math.log2(CHUNK)))


def _pick_tm(m):
    return DENSE_TM if m % DENSE_TM == 0 else m


def _rmsnorm_kernel(x_ref, g_ref, o_ref):
    x = x_ref[...]
    ms = jnp.mean(x * x, axis=-1, keepdims=True)
    y = x * lax.rsqrt(ms + EPS)
    o_ref[...] = (y * g_ref[...]).astype(o_ref.dtype)


def _rmsnorm(x, g3, layer, out_dtype):
    m, d = x.shape
    tm = 256
    return pl.pallas_call(
        _rmsnorm_kernel,
        grid=(m // tm,),
        in_specs=[pl.BlockSpec((tm, d), lambda i: (i, 0)),
                  pl.BlockSpec((None, 1, d), lambda i: (layer, 0, 0))],
        out_specs=pl.BlockSpec((tm, d), lambda i: (i, 0)),
        out_shape=jax.ShapeDtypeStruct((m, d), out_dtype),
        compiler_params=_params("arbitrary"),
        name="rmsnorm",
    )(x, g3)


class _Weight:
    def __init__(self, src, layer, kblk=0, col0=0):
        self.src, self.layer, self.kblk, self.col0 = src, layer, kblk, col0

    def spec(self, tk, tn):
        layer, kb, jb = self.layer, self.kblk, self.col0 // tn
        return pl.BlockSpec((None, tk, tn), lambda i, j: (layer, kb, jb + j))


def _rope_tile(x, c, s1, s2):
    return (x * c + pltpu.roll(x, 8, 1) * s1
            + pltpu.roll(x, LANES - 8, 1) * s2)


def _rider_in(x):
    return pl.BlockSpec(x.shape, lambda i, j: (0, 0), pipeline_mode=pl.Buffered(1))


def _rider_out(rows, tn, ni):
    return pl.BlockSpec((rows, tn),
                        lambda i, j: (0, jnp.where(i == ni - 1, j, 0)))


def _on_last_row_block(fn):
    pl.when(pl.program_id(0) == pl.num_programs(0) - 1)(fn)


def _lane_chunks(n):
    return [slice(c * LANES, (c + 1) * LANES) for c in range(n // LANES)]


def _proj_kernel(*refs, rope, norm):
    w_ref = refs[1]
    n_tab = 3 if rope else 0
    a_ref, tabs = refs[0], refs[2:2 + n_tab]
    sa_ref, stabs = refs[2 + n_tab], refs[3 + n_tab:3 + 2 * n_tab]
    r_ref, sr_ref = refs[3 + 2 * n_tab:5 + 2 * n_tab] if norm else (None, None)
    o_ref, so_ref = refs[-2:]

    def run(a_ref, tabs, r_ref, o_ref):
        acc = jnp.dot(a_ref[...], w_ref[...].astype(BF16),
                      preferred_element_type=F32)
        for sl in _lane_chunks(acc.shape[1]):
            x = acc[:, sl]
            if norm:
                x = x * r_ref[...]
            if rope:
                x = _rope_tile(x, *(t[...] for t in tabs))
            o_ref[:, sl] = x.astype(o_ref.dtype)

    run(a_ref, tabs, r_ref, o_ref)
    _on_last_row_block(lambda: run(sa_ref, stabs, sr_ref, so_ref))


def _proj(a, a_s, w, ncols, out_dtype, rope=None, rope_s=None, norm=None):
    m, k = a.shape
    ms = a_s.shape[0]
    tm = _pick_tm(m)
    tn = min(DENSE_TN, ncols)
    ni = m // tm
    in_specs = [pl.BlockSpec((tm, k), lambda i, j: (i, 0)), w.spec(k, tn)]
    args = [a, w.src]
    if rope is not None:
        nrep = rope[0].shape[0] // tm
        for t in rope:
            in_specs.append(pl.BlockSpec((tm, LANES), lambda i, j: (i % nrep, 0)))
            args.append(t)
    for x in (a_s,) + tuple(rope_s or ()):
        in_specs.append(_rider_in(x))
        args.append(x)
    if norm is not None:
        r, r_s = norm
        in_specs += [pl.BlockSpec((tm, LANES), lambda i, j: (i, 0)), _rider_in(r_s)]
        args += [r, r_s]
    return pl.pallas_call(
        functools.partial(_proj_kernel, rope=rope is not None,
                          norm=norm is not None),
        grid=(ni, ncols // tn),
        in_specs=in_specs,
        out_specs=[pl.BlockSpec((tm, tn), lambda i, j: (i, j)),
                   _rider_out(ms, tn, ni)],
        out_shape=[jax.ShapeDtypeStruct((m, ncols), out_dtype),
                   jax.ShapeDtypeStruct((ms, ncols), out_dtype)],
        compiler_params=_params("arbitrary", "arbitrary"),
        name="proj",
    )(*args)


def _gate_up_kernel(*refs, norm, cast):
    a_ref, wg_ref, wu_ref = refs[:3]
    r_ref = refs[3] if norm else None
    o_ref = refs[-2] if cast else refs[-1]
    a = a_ref[...]
    g = jnp.dot(a, wg_ref[...].astype(BF16), preferred_element_type=F32)
    u = jnp.dot(a, wu_ref[...].astype(BF16), preferred_element_type=F32)
    for sl in _lane_chunks(g.shape[1]):
        gc, uc = g[:, sl], u[:, sl]
        if norm:
            gc, uc = gc * r_ref[...], uc * r_ref[...]
        o_ref[:, sl] = (gc * jax.nn.sigmoid(gc) * uc).astype(o_ref.dtype)
    if cast:
        refs[-1][...] = refs[-3][...].astype(BF16)


def _gate_up(a, wg, wu, f, norm=None, cast=None):
    m, k = a.shape
    tm = GATE_UP_TM if m % GATE_UP_TM == 0 else m
    tn = GATE_UP_TN
    ni, nj = m // tm, f // tn
    in_specs = [pl.BlockSpec((tm, k), lambda i, j: (i, 0)),
                wg.spec(k, tn), wu.spec(k, tn)]
    args = [a, wg.src, wu.src]
    out_specs = [pl.BlockSpec((tm, tn), lambda i, j: (i, j))]
    out_shape = [jax.ShapeDtypeStruct((m, f), BF16)]
    if norm is not None:
        in_specs.append(pl.BlockSpec((tm, LANES), lambda i, j: (i, 0),
                                     pipeline_mode=pl.Buffered(1)))
        args.append(norm)
    if cast is not None:
        w3, layer = cast
        rows, n = w3.shape[1] // (ni * nj), w3.shape[2]
        assert rows * ni * nj == w3.shape[1] and rows % 16 == 0
        in_specs.append(pl.BlockSpec((None, rows, n),
                                     lambda i, j: (layer, i * nj + j, 0)))
        args.append(w3)
        out_specs.append(pl.BlockSpec((rows, n), lambda i, j: (i * nj + j, 0)))
        out_shape.append(jax.ShapeDtypeStruct(w3.shape[1:], BF16))
    out = pl.pallas_call(
        functools.partial(_gate_up_kernel, norm=norm is not None,
                          cast=cast is not None),
        grid=(ni, nj),
        in_specs=in_specs,
        out_specs=out_specs,
        out_shape=out_shape,
        compiler_params=_params("arbitrary", "arbitrary"),
        name="gate_up",
    )(*args)
    return out if cast is not None else out[0]


def _mm_res_kernel(*refs, nterms, stats):
    n = nterms
    w_refs = refs[n:2 * n]
    gain_ref = refs[3 * n + 2] if stats else None
    outs = refs[3 * n + 2 + bool(stats):]

    def run(a_refs, res_ref, o_ref, b_ref, ss_ref):
        acc = res_ref[...]
        for a_ref, w_ref in zip(a_refs, w_refs):
            acc = acc + jnp.dot(a_ref[...], w_ref[...].astype(BF16),
                                preferred_element_type=F32)
        o_ref[...] = acc
        if stats:
            _row_stats(acc, gain_ref, b_ref, ss_ref)

    p_extra = (outs[2], outs[3]) if stats else (None, None)
    s_extra = (outs[4], outs[5]) if stats else (None, None)
    run(refs[:n], refs[2 * n], outs[0], *p_extra)
    _on_last_row_block(
        lambda: run(refs[2 * n + 1:3 * n + 1], refs[3 * n + 1], outs[1], *s_extra))


def _mm_res(terms, tk, res, res_s, tn, gain=None):
    stats = gain is not None
    m, n = res.shape
    ms = res_s.shape[0]
    tm = _pick_tm(m)
    ni = m // tm
    in_specs, args = [], []
    for a, _, ka, _ in terms:
        in_specs.append(pl.BlockSpec((tm, tk), lambda i, j, ka=ka: (i, ka)))
        args.append(a)
    for _, _, _, w in terms:
        in_specs.append(w.spec(tk, tn))
        args.append(w.src)
    tile = pl.BlockSpec((tm, tn), lambda i, j: (i, j))
    tile_s = _rider_out(ms, tn, ni)
    in_specs.append(tile)
    args.append(res)
    for _, a_s, ka, _ in terms:
        in_specs.append(pl.BlockSpec((ms, tk), lambda i, j, ka=ka: (0, ka),
                                     pipeline_mode=pl.Buffered(1)))
        args.append(a_s)
    in_specs.append(tile_s)
    args.append(res_s)
    out_specs = [tile, tile_s]
    out_shape = [jax.ShapeDtypeStruct((m, n), F32),
                 jax.ShapeDtypeStruct((ms, n), F32)]
    if stats:
        in_specs.append(pl.BlockSpec((1, tn), lambda i, j: (0, j)))
        args.append(gain.astype(F32).reshape(1, n))
        out_specs += [tile, pl.BlockSpec((tm, LANES), lambda i, j: (i, 0)),
                      tile_s, pl.BlockSpec((ms, LANES), lambda i, j: (0, 0))]
        out_shape += [jax.ShapeDtypeStruct((m, n), BF16),
                      jax.ShapeDtypeStruct((m, LANES), F32),
                      jax.ShapeDtypeStruct((ms, n), BF16),
                      jax.ShapeDtypeStruct((ms, LANES), F32)]
    return pl.pallas_call(
        functools.partial(_mm_res_kernel, nterms=len(terms), stats=stats),
        grid=(ni, n // tn),
        in_specs=in_specs,
        out_specs=out_specs,
        out_shape=out_shape,
        compiler_params=_params("arbitrary", "arbitrary"),
        name="mm_res",
    )(*args)


def _row_stats(acc, gain_ref, b_ref, ss_ref):
    b_ref[...] = (acc * gain_ref[...]).astype(BF16)
    part = sum(acc[:, sl] * acc[:, sl] for sl in _lane_chunks(acc.shape[1]))
    j = pl.program_id(1)

    @pl.when(j == 0)
    def _():
        ss_ref[...] = part

    @pl.when(j != 0)
    def _():
        ss_ref[...] += part


def _down_kernel(*refs, stats):
    a_refs, w_ref, res_ref = refs[0:2], refs[2], refs[3]
    sa_refs, sres_ref = refs[4:6], refs[6]
    gain_ref = refs[7] if stats else None
    outs = refs[7 + bool(stats):]
    k = pl.program_id(2)

    def run(x_refs, r_ref, o_ref, b_ref, ss_ref):
        @pl.when(k == 0)
        def _():
            o_ref[...] = r_ref[...] + jnp.dot(
                x_refs[0][...], w_ref[...].astype(BF16),
                preferred_element_type=F32)

        @pl.when(k == 1)
        def _():
            acc = o_ref[...] + jnp.dot(
                x_refs[1][...], w_ref[...].astype(BF16),
                preferred_element_type=F32)
            o_ref[...] = acc
            if stats:
                _row_stats(acc, gain_ref, b_ref, ss_ref)

    p_extra = (outs[2], outs[3]) if stats else (None, None)
    s_extra = (outs[4], outs[5]) if stats else (None, None)
    run(a_refs, res_ref, outs[0], *p_extra)
    _on_last_row_block(lambda: run(sa_refs, sres_ref, outs[1], *s_extra))


def _down(act, act_s, w, res, res_s, gain=None):
    stats = gain is not None
    m, n = res.shape
    ms = res_s.shape[0]
    tk = act.shape[1] // 2
    tm, tn = _pick_tm(m), DENSE_TN
    ni = m // tm
    one = dict(pipeline_mode=pl.Buffered(1))
    last = ni - 1
    tile = pl.BlockSpec((tm, tn), lambda i, j, k: (i, j))
    tile_s = pl.BlockSpec((ms, tn),
                          lambda i, j, k: (0, jnp.where(i == last, j, 0)))
    in_specs = (
        [pl.BlockSpec((tm, tk), lambda i, j, k, h=h: (i, h), **one) for h in (0, 1)]
        + [pl.BlockSpec((tk, tn), lambda i, j, k: (k, j)), tile]
        + [pl.BlockSpec((ms, tk), lambda i, j, k, h=h: (0, h), **one) for h in (0, 1)]
        + [tile_s])
    args = [act, act, w, res, act_s, act_s, res_s]
    out_specs = [tile, tile_s]
    out_shape = [jax.ShapeDtypeStruct((m, n), F32),
                 jax.ShapeDtypeStruct((ms, n), F32)]
    if stats:
        in_specs.append(pl.BlockSpec((1, tn), lambda i, j, k: (0, j)))
        args.append(gain.astype(F32).reshape(1, n))
        out_specs += [tile, pl.BlockSpec((tm, LANES), lambda i, j, k: (i, 0)),
                      tile_s, pl.BlockSpec((ms, LANES), lambda i, j, k: (0, 0))]
        out_shape += [jax.ShapeDtypeStruct((m, n), BF16),
                      jax.ShapeDtypeStruct((m, LANES), F32),
                      jax.ShapeDtypeStruct((ms, n), BF16),
                      jax.ShapeDtypeStruct((ms, LANES), F32)]
    return pl.pallas_call(
        functools.partial(_down_kernel, stats=stats),
        grid=(ni, n // tn, 2),
        in_specs=in_specs,
        out_specs=out_specs,
        out_shape=out_shape,
        compiler_params=_params("arbitrary", "arbitrary", "arbitrary"),
        name="down",
    )(*args)


def _row_scale_kernel(ss_ref, o_ref, *, d):
    tot = jnp.sum(ss_ref[...], axis=1, keepdims=True)
    o_ref[...] = jnp.broadcast_to(lax.rsqrt(tot / d + EPS), o_ref.shape)


def _row_scale(ss, d):
    m = ss.shape[0]
    tm = _pick_tm(m)
    blk = pl.BlockSpec((tm, LANES), lambda i: (i, 0))
    return pl.pallas_call(
        functools.partial(_row_scale_kernel, d=d),
        grid=(m // tm,),
        in_specs=[blk],
        out_specs=blk,
        out_shape=jax.ShapeDtypeStruct(ss.shape, F32),
        compiler_params=_params("arbitrary"),
        name="row_scale",
    )(ss)


_DN_T = (((1,), (1,)), ((), ()))


def _head(h, d=HEAD_DIM_AB):
    return slice(h * d, (h + 1) * d)


def _heads_major(x):
    return pltpu.einshape("phd->hpd", x)


def _softmax_stage(ss, sinks=None):
    ms = [jnp.max(s, axis=1, keepdims=True) for s in ss]
    if sinks is not None:
        ms = [jnp.maximum(m, sk) for m, sk in zip(ms, sinks)]
    ps = [jnp.exp(s - m) for s, m in zip(ss, ms)]
    ls = [jnp.sum(p, axis=1, keepdims=True) for p in ps]
    if sinks is not None:
        ls = [l + jnp.exp(sk - m) for l, sk, m in zip(ls, sinks, ms)]
    return [p.astype(BF16) for p in ps], ls


def _sb_logs(z, valid):
    lb = jnp.minimum(z, 0.0) - jnp.log(1.0 + jnp.exp(-jnp.abs(z)))
    l1m = lb - z
    if valid is not None:
        l1m = jnp.where(valid, l1m, 0.0)
    return lb, l1m


def _split_hi_lo(x):
    hi = x.astype(BF16)
    lo = (x - hi.astype(F32)).astype(BF16)
    return jnp.concatenate([hi, lo], axis=1)


def _suffix_matrix(n):
    u = np.tril(np.ones((n, n), np.float32), -1)
    return jnp.asarray(np.concatenate([u, u], axis=0), BF16)


def _sb_prompt_kernel(q_ref, k_ref, v_ref, uu_ref, o_ref, kc_ref, vc_ref,
                      kb_ref, vb_ref, oacc_ref, cacc_ref, *, scale):
    qi = pl.program_id(2)
    tq = q_ref.shape[0]
    nh = oacc_ref.shape[0]

    @pl.when(qi == 0)
    def _():
        kb_ref[...] = k_ref[...].astype(BF16)
        vb_ref[...] = v_ref[...].astype(BF16)

    rows = pl.ds(pl.multiple_of(qi * tq, tq), tq)
    kc_ref[...] = k_ref[rows, :].reshape(kc_ref.shape)
    vc_ref[...] = v_ref[rows, :].reshape(vc_ref.shape)

    oacc_ref[...] = jnp.zeros(oacc_ref.shape, F32)
    cacc_ref[...] = jnp.zeros(cacc_ref.shape, F32)

    def tile(j, masked):
        start = pl.multiple_of(j * tq, tq)
        valid = None
        if masked:
            row = lax.broadcasted_iota(jnp.int32, (tq, tq), 0)
            col = lax.broadcasted_iota(jnp.int32, (tq, tq), 1)
            valid = col < row
        zs = [lax.dot_general(q_ref[:, _head(h)],
                              kb_ref[pl.ds(start, tq), _head(h)], _DN_T,
                              preferred_element_type=F32) * scale
              for h in range(nh)]
        logs = [_sb_logs(z, valid) for z in zs]
        sufs = [jnp.dot(_split_hi_lo(l1m), uu_ref[...],
                        preferred_element_type=F32) for _, l1m in logs]
        ws = [jnp.exp(lb + suf + cacc_ref[h])
              for h, ((lb, _), suf) in enumerate(zip(logs, sufs))]
        if masked:
            ws = [jnp.where(valid, w, 0.0) for w in ws]
        for h in range(nh):
            oacc_ref[h] += jnp.dot(ws[h].astype(BF16),
                                   vb_ref[pl.ds(start, tq), _head(h)],
                                   preferred_element_type=F32)
            cacc_ref[h] += jnp.sum(logs[h][1], axis=1, keepdims=True)

    tile(qi, True)

    def body(t, c):
        tile(qi - 1 - t, False)
        return c

    lax.fori_loop(0, qi, body, 0)
    for h in range(nh):
        o_ref[:, _head(h)] = oacc_ref[h].astype(o_ref.dtype)


def _sb_prompt(q, k, v, batch, seq):
    m, w = q.shape
    d = HEAD_DIM_AB
    nh = SB_HEADS
    tq = SB_TILE
    nq = seq // tq
    qo = pl.BlockSpec((tq, nh * d), lambda b, g, i: (b * nq + i, g))
    kv = pl.BlockSpec((seq, nh * d), lambda b, g, i: (b, g))
    cache = pl.BlockSpec((None, tq, nh, d), lambda b, g, i: (b, i, g, 0))
    cache_shape = jax.ShapeDtypeStruct((batch, seq, w // d, d), F32)
    return pl.pallas_call(
        functools.partial(_sb_prompt_kernel, scale=d ** -0.5),
        grid=(batch, w // (nh * d), nq),
        in_specs=[qo, kv, kv, pl.BlockSpec((2 * tq, tq), lambda b, g, i: (0, 0))],
        out_specs=[qo, cache, cache],
        out_shape=[jax.ShapeDtypeStruct((m, w), BF16), cache_shape, cache_shape],
        scratch_shapes=[pltpu.VMEM((seq, nh * d), BF16),
                        pltpu.VMEM((seq, nh * d), BF16),
                        pltpu.VMEM((nh, tq, d), F32),
                        pltpu.VMEM((nh, tq, 1), F32)],
        compiler_params=_params("arbitrary", "arbitrary", "arbitrary"),
        name="sb_prompt",
    )(q, k, v, _suffix_matrix(tq))


def _sb_step_kernel(q_ref, kn_ref, vn_ref, kc_ref, vc_ref, uu_ref, o_ref,
                    knc_ref, vnc_ref, kb_ref, vb_ref, *, scale):
    t = q_ref.shape[0]
    past, nh, d = kc_ref.shape
    tk = uu_ref.shape[1]
    n = past + tk
    knc_ref[...] = kn_ref[...].reshape(knc_ref.shape)
    vnc_ref[...] = vn_ref[...].reshape(vnc_ref.shape)
    kb_ref[:, 0:past, :] = _heads_major(kc_ref[...]).astype(BF16)
    vb_ref[:, 0:past, :] = _heads_major(vc_ref[...]).astype(BF16)
    for h in range(nh):
        kb_ref[h, past:past + t] = kn_ref[:, _head(h)].astype(BF16)
        vb_ref[h, past:past + t] = vn_ref[:, _head(h)].astype(BF16)
        kb_ref[h, past + t:n] = jnp.zeros((tk - t, d), BF16)
        vb_ref[h, past + t:n] = jnp.zeros((tk - t, d), BF16)

    z = jnp.concatenate(
        [lax.dot_general(q_ref[:, _head(h)], kb_ref[h], _DN_T,
                         preferred_element_type=F32) for h in range(nh)],
        axis=0) * scale
    row = lax.broadcasted_iota(jnp.int32, z.shape, 0) & (t - 1)
    col = lax.broadcasted_iota(jnp.int32, z.shape, 1)
    valid = col < past + row
    lb, l1m = _sb_logs(z, valid)
    ntiles = n // tk
    sufs = [jnp.dot(_split_hi_lo(l1m[:, j * tk:(j + 1) * tk]), uu_ref[...],
                    preferred_element_type=F32) for j in range(ntiles)]
    tots = [jnp.sum(l1m[:, j * tk:(j + 1) * tk], axis=1, keepdims=True)
            for j in range(ntiles)]
    carries = [None] * ntiles
    carry = jnp.zeros_like(tots[0])
    for j in reversed(range(ntiles)):
        carries[j] = carry
        carry = carry + tots[j]
    after = jnp.concatenate([sufs[j] + carries[j] for j in range(ntiles)],
                            axis=1)
    w = jnp.where(valid, jnp.exp(lb + after), 0.0).astype(BF16)
    for h in range(nh):
        o = jnp.dot(w[h * t:(h + 1) * t], vb_ref[h], preferred_element_type=F32)
        o_ref[:, _head(h)] = o.astype(o_ref.dtype)


def _sb_step(q, k_new, v_new, k_cache, v_cache, batch):
    m, w = q.shape
    _, past, heads, d = k_cache.shape
    t = m // batch
    tk = SB_STEP_KEYS
    nh = STEP_HEADS
    assert past % tk == 0 and t <= tk and t & (t - 1) == 0
    row = pl.BlockSpec((t, nh * d), lambda b, g: (b, g))
    cache = pl.BlockSpec((None, past, nh, d), lambda b, g: (b, 0, g, 0))
    new = pl.BlockSpec((None, t, nh, d), lambda b, g: (b, 0, g, 0))
    new_shape = jax.ShapeDtypeStruct((batch, t, heads, d), F32)
    return pl.pallas_call(
        functools.partial(_sb_step_kernel, scale=d ** -0.5),
        grid=(batch, heads // nh),
        in_specs=[row, row, row, cache, cache,
                  pl.BlockSpec((2 * tk, tk), lambda b, g: (0, 0))],
        out_specs=[row, new, new],
        out_shape=[jax.ShapeDtypeStruct((m, w), BF16), new_shape, new_shape],
        scratch_shapes=[pltpu.VMEM((nh, past + tk, d), BF16),
                        pltpu.VMEM((nh, past + tk, d), BF16)],
        compiler_params=_params("arbitrary", "arbitrary"),
        name="sb_step",
    )(q, k_new, v_new, k_cache, v_cache, _suffix_matrix(tk))


def _band_bias_kernel(r_ref, o_ref, *, n_valid):
    tq, w = o_ref.shape
    wr = r_ref.shape[1]
    t = jnp.broadcast_to(r_ref[...], (tq, wr))
    t = pltpu.roll(t, 0, 1, stride=1, stride_axis=0)
    row = lax.broadcasted_iota(jnp.int32, (tq, w), 0)
    col = lax.broadcasted_iota(jnp.int32, (tq, w), 1)
    qc = _chunk_of(row)
    kc = _chunk_of(col)
    valid = (kc >= qc) & (kc <= qc + BAND_PREV_CHUNKS) & (col < n_valid)
    o_ref[...] = jnp.where(valid, t[:, :w], NEG)


def _band_bias(rel, tq, n_valid):
    heads = rel.shape[0]
    w = BAND_HIST + BAND_TQ
    wr = w + LANES
    p = np.arange(wr)
    delta = np.where(p < w, p, p - wr)
    idx = np.clip(BAND_HIST - delta, -REL_CLIP, REL_CLIP) + REL_CLIP
    r = jnp.take(rel.astype(F32), jnp.asarray(idx, jnp.int32), axis=1)
    r = r.reshape(heads, 1, wr)
    return pl.pallas_call(
        functools.partial(_band_bias_kernel, n_valid=n_valid),
        grid=(heads,),
        in_specs=[pl.BlockSpec((None, 1, wr), lambda h: (h, 0, 0))],
        out_specs=pl.BlockSpec((None, tq, w), lambda h: (h, 0, 0)),
        out_shape=jax.ShapeDtypeStruct((heads, tq, w), F32),
        compiler_params=_params("arbitrary"),
        name="band_bias",
    )(r)


def _band_prompt_kernel(q_ref, k_ref, v_ref, bias_ref, o_ref, kp_ref, vp_ref,
                        *, scale, tq):
    seq, wd = k_ref.shape
    d = HEAD_DIM_AB
    nh = wd // d
    w = bias_ref.shape[1]
    hist = w - tq
    kp_ref[0:hist] = jnp.zeros((hist, wd), BF16)
    vp_ref[0:hist] = jnp.zeros((hist, wd), BF16)
    kp_ref[hist:hist + seq] = k_ref[...].astype(BF16)
    vp_ref[hist:hist + seq] = v_ref[...].astype(BF16)
    col = lax.broadcasted_iota(jnp.int32, (tq, w), 1)

    def tile(i, at_start):
        start = pl.multiple_of(i * tq, tq)
        ss = []
        for h in range(nh):
            s = lax.dot_general(q_ref[pl.ds(start, tq), _head(h)],
                                kp_ref[pl.ds(start, w), _head(h)], _DN_T,
                                preferred_element_type=F32)
            s = s * scale + bias_ref[h * tq:(h + 1) * tq, :]
            if at_start:
                s = jnp.where(col >= hist - start, s, NEG)
            ss.append(s)
        ps, ls = _softmax_stage(ss)
        for h in range(nh):
            o = jnp.dot(ps[h], vp_ref[pl.ds(start, w), _head(h)],
                        preferred_element_type=F32) / ls[h]
            o_ref[pl.ds(start, tq), _head(h)] = o.astype(o_ref.dtype)

    def loop(lo, hi, at_start):
        def body(i, c):
            tile(i, at_start)
            return c
        lax.fori_loop(lo, hi, body, 0)

    n_tiles = seq // tq
    n_start = min(hist // tq, n_tiles)
    loop(0, n_start, True)
    loop(n_start, n_tiles, False)


def _band_prompt(q, k, v, bias, batch, seq):
    m, wd = q.shape
    d = HEAD_DIM_AB
    nh = BAND_HEADS
    heads, tq, w = bias.shape
    bias = bias.reshape(heads // nh, nh * tq, w)
    blk = pl.BlockSpec((seq, nh * d), lambda b, g: (b, g))
    return pl.pallas_call(
        functools.partial(_band_prompt_kernel, scale=d ** -0.5, tq=tq),
        grid=(batch, heads // nh),
        in_specs=[blk, blk, blk,
                  pl.BlockSpec((None, nh * tq, w), lambda b, g: (g, 0, 0))],
        out_specs=blk,
        out_shape=jax.ShapeDtypeStruct((m, wd), BF16),
        scratch_shapes=[pltpu.VMEM((BAND_HIST + seq, nh * d), BF16),
                        pltpu.VMEM((BAND_HIST + seq, nh * d), BF16)],
        compiler_params=_params("arbitrary", "arbitrary"),
        name="band_prompt",
    )(q, k, v, bias)


def _cache_tail_kernel(x_ref, o_ref):
    o_ref[...] = x_ref[...].reshape(o_ref.shape)


def _cache_tail(x, batch, seq, rows):
    d = HEAD_DIM_AB
    nh = STEP_HEADS
    heads = x.shape[1] // d
    assert seq % rows == 0
    last = seq // rows - 1
    return pl.pallas_call(
        _cache_tail_kernel,
        grid=(batch, heads // nh),
        in_specs=[pl.BlockSpec((rows, nh * d),
                               lambda b, g: (b * (last + 1) + last, g))],
        out_specs=pl.BlockSpec((None, rows, nh, d), lambda b, g: (b, 0, g, 0)),
        out_shape=jax.ShapeDtypeStruct((batch, rows, heads, d), x.dtype),
        compiler_params=_params("arbitrary", "arbitrary"),
        name="cache_tail",
    )(x)


def _roll_cache(new_ref, cache, new_rows_ref, t):
    past, nh, d = cache.shape
    new_ref[0:past - t] = cache[t:]
    for h in range(nh):
        new_ref[past - t:past, h, :] = new_rows_ref[:, _head(h, d)]


def _band_step_kernel(q_ref, kn_ref, vn_ref, kc_ref, vc_ref, bias_ref, o_ref,
                      nk_ref, nv_ref, kb_ref, vb_ref, *, scale):
    t = q_ref.shape[0]
    past, nh, d = kc_ref.shape
    w = bias_ref.shape[1]
    kc, vc = kc_ref[...], vc_ref[...]
    _roll_cache(nk_ref, kc, kn_ref, t)
    _roll_cache(nv_ref, vc, vn_ref, t)
    kb_ref[:, 0:past, :] = _heads_major(kc).astype(BF16)
    vb_ref[:, 0:past, :] = _heads_major(vc).astype(BF16)
    for h in range(nh):
        kb_ref[h, past:past + t] = kn_ref[:, _head(h)].astype(BF16)
        vb_ref[h, past:past + t] = vn_ref[:, _head(h)].astype(BF16)
        kb_ref[h, past + t:w] = jnp.zeros((w - past - t, d), BF16)
        vb_ref[h, past + t:w] = jnp.zeros((w - past - t, d), BF16)
    ss = [lax.dot_general(q_ref[:, _head(h)], kb_ref[h], _DN_T,
                          preferred_element_type=F32) * scale
          + bias_ref[h * t:(h + 1) * t, :] for h in range(nh)]
    ps, ls = _softmax_stage(ss)
    for h in range(nh):
        o = jnp.dot(ps[h], vb_ref[h], preferred_element_type=F32) / ls[h]
        o_ref[:, _head(h)] = o.astype(o_ref.dtype)


def _band_step(q, k_new, v_new, k_cache, v_cache, bias, batch):
    m, wd = q.shape
    _, past, heads, d = k_cache.shape
    t = m // batch
    nh = STEP_HEADS
    w = bias.shape[2]
    assert past == BAND_HIST and past + t <= w
    bias = bias.reshape(heads // nh, nh * t, w)
    row = pl.BlockSpec((t, nh * d), lambda b, g: (b, g))
    cache = pl.BlockSpec((None, past, nh, d), lambda b, g: (b, 0, g, 0))
    return pl.pallas_call(
        functools.partial(_band_step_kernel, scale=d ** -0.5),
        grid=(batch, heads // nh),
        in_specs=[row, row, row, cache, cache,
                  pl.BlockSpec((None, nh * t, w), lambda b, g: (g, 0, 0))],
        out_specs=[row, cache, cache],
        out_shape=[jax.ShapeDtypeStruct((m, wd), BF16),
                   jax.ShapeDtypeStruct(k_cache.shape, F32),
                   jax.ShapeDtypeStruct(v_cache.shape, F32)],
        scratch_shapes=[pltpu.VMEM((nh, w, d), BF16),
                        pltpu.VMEM((nh, w, d), BF16)],
        compiler_params=_params("arbitrary", "arbitrary"),
        name="band_step",
    )(q, k_new, v_new, k_cache, v_cache, bias)


def _split_pair(blk):
    rolled = pltpu.roll(blk, HEAD_DIM_C, 1)
    lane = lax.broadcasted_iota(jnp.int32, blk.shape, 1)
    low = lane < HEAD_DIM_C
    zero = jnp.zeros_like(blk)
    return [jnp.where(low, blk, zero), jnp.where(low, zero, rolled),
            jnp.where(low, rolled, zero), jnp.where(low, zero, blk)]


def _swa_stage(qs, kls, khs, vls, vhs, valid, sinks, scale):
    ss = []
    for q, kl, kh in zip(qs, kls, khs):
        for kx in (kl, kh):
            s = lax.dot_general(q, kx, _DN_T, preferred_element_type=F32)
            ss.append(jnp.where(valid, s * scale, NEG))
    ps, ls = _softmax_stage(ss, sinks)
    return [jnp.dot(ps[2 * i], vls[i], preferred_element_type=F32) / ls[2 * i]
            + jnp.dot(ps[2 * i + 1], vhs[i], preferred_element_type=F32)
            / ls[2 * i + 1] for i in range(len(qs))]


def _swa_prompt_kernel(sinks_ref, q_ref, k_ref, v_ref, o_ref, ks_ref, vs_ref,
                       *, scale, tq):
    hp = pl.program_id(1)
    seq = k_ref.shape[0]
    band = WINDOW + tq
    npair = q_ref.shape[1] // LANES
    group = npair // 2

    for src, dst in ((k_ref, ks_ref), (v_ref, vs_ref)):
        for n, x in enumerate(_split_pair(src[...])):
            dst[n, 0:WINDOW] = jnp.zeros((WINDOW, LANES), BF16)
            dst[n, WINDOW:WINDOW + seq] = x.astype(BF16)

    row = lax.broadcasted_iota(jnp.int32, (tq, band), 0)
    col = lax.broadcasted_iota(jnp.int32, (tq, band), 1)
    qc = _chunk_of(row)
    kc = _chunk_of(col)
    in_band = (kc >= qc) & (kc <= qc + SWA_PREV_CHUNKS)
    sinks = [sinks_ref[hp * 2 * npair + n] for n in range(2 * npair)]

    def body(i, c):
        start = pl.multiple_of(i * tq, tq)
        valid = in_band & (col >= WINDOW - start)
        kv = [[r[n, pl.ds(start, band), :] for n in range(4)]
              for r in (ks_ref, vs_ref)]
        qs = [q_ref[pl.ds(start, tq), _head(p, LANES)] for p in range(npair)]
        sel = [2 * (p // group) for p in range(npair)]
        outs = _swa_stage(qs, [kv[0][s] for s in sel], [kv[0][s + 1] for s in sel],
                          [kv[1][s] for s in sel], [kv[1][s + 1] for s in sel],
                          valid, sinks, scale)
        for p in range(npair):
            o_ref[pl.ds(start, tq), _head(p, LANES)] = outs[p].astype(o_ref.dtype)
        return c

    lax.fori_loop(0, seq // tq, body, 0)


def _swa_prompt(q, k, v, sinks, batch, seq):
    m, wq = q.shape
    wk = k.shape[1]
    npairs = wk // LANES
    qw = wq // npairs
    qo = pl.BlockSpec((seq, qw), lambda b, h, s: (b, h))
    kv = pl.BlockSpec((seq, LANES), lambda b, h, s: (b, h))
    grid_spec = pltpu.PrefetchScalarGridSpec(
        num_scalar_prefetch=1,
        grid=(batch, npairs),
        in_specs=[qo, kv, kv],
        out_specs=qo,
        scratch_shapes=[pltpu.VMEM((4, WINDOW + seq, LANES), BF16),
                        pltpu.VMEM((4, WINDOW + seq, LANES), BF16)],
    )
    return pl.pallas_call(
        functools.partial(_swa_prompt_kernel, scale=HEAD_DIM_C ** -0.5,
                          tq=SWA_TQ),
        grid_spec=grid_spec,
        out_shape=jax.ShapeDtypeStruct((m, wq), BF16),
        compiler_params=_params("arbitrary", "arbitrary"),
        name="swa_prompt",
    )(sinks, q, k, v)


def _swa_step_kernel(sinks_ref, q_ref, kn_ref, vn_ref, kc_ref, vc_ref, o_ref,
                     nk_ref, nv_ref, *, scale):
    t = q_ref.shape[0]
    past, wk = kc_ref.shape
    band = 2 * LANES
    nkv = wk // LANES
    npair = q_ref.shape[1] // LANES
    group = npair // (2 * nkv)
    kc, vc = kc_ref[...], vc_ref[...]
    kn, vn = kn_ref[...], vn_ref[...]
    nk_ref[0:past - t] = kc[t:]
    nv_ref[0:past - t] = vc[t:]
    nk_ref[past - t:past] = kn
    nv_ref[past - t:past] = vn
    pad = jnp.zeros((band - past - t, wk), F32)
    k_all = jnp.concatenate([kc, kn, pad], axis=0)
    v_all = jnp.concatenate([vc, vn, pad], axis=0)
    col = lax.broadcasted_iota(jnp.int32, (t, band), 1)
    valid = col < past + t
    kx = [[x.astype(BF16) for x in _split_pair(k_all[:, _head(g, LANES)])]
          for g in range(nkv)]
    vx = [[x.astype(BF16) for x in _split_pair(v_all[:, _head(g, LANES)])]
          for g in range(nkv)]
    qs = [q_ref[:, _head(p, LANES)] for p in range(npair)]
    where = [(p // (2 * group), 2 * ((p // group) % 2)) for p in range(npair)]
    sinks = [sinks_ref[n] for n in range(2 * npair)]
    outs = _swa_stage(qs, [kx[g][s] for g, s in where],
                      [kx[g][s + 1] for g, s in where],
                      [vx[g][s] for g, s in where],
                      [vx[g][s + 1] for g, s in where], valid, sinks, scale)
    for p in range(npair):
        o_ref[:, _head(p, LANES)] = outs[p].astype(o_ref.dtype)


def _swa_step(q, k_new, v_new, k_cache, v_cache, sinks, batch):
    m, wq = q.shape
    wk = k_new.shape[1]
    t = m // batch
    past = k_cache.shape[1]
    assert past + t <= 2 * LANES
    qrow = pl.BlockSpec((t, wq), lambda b, s: (b, 0))
    krow = pl.BlockSpec((t, wk), lambda b, s: (b, 0))
    cache = pl.BlockSpec((None, past, wk), lambda b, s: (b, 0, 0))
    grid_spec = pltpu.PrefetchScalarGridSpec(
        num_scalar_prefetch=1,
        grid=(batch,),
        in_specs=[qrow, krow, krow, cache, cache],
        out_specs=[qrow, cache, cache],
    )
    return pl.pallas_call(
        functools.partial(_swa_step_kernel, scale=HEAD_DIM_C ** -0.5),
        grid_spec=grid_spec,
        out_shape=[jax.ShapeDtypeStruct((m, wq), BF16),
                   jax.ShapeDtypeStruct(k_cache.shape, F32),
                   jax.ShapeDtypeStruct(v_cache.shape, F32)],
        compiler_params=_params("arbitrary"),
        name="swa_step",
    )(sinks, q, k_new, v_new, k_cache, v_cache)


def _rope_tables(pos):
    half = ROPE_DIM // 2
    inv = jnp.power(ROPE_THETA, -jnp.arange(half, dtype=F32) / half)
    ang = pos.astype(F32)[:, None] * inv[None, :]
    cos, sin = jnp.cos(ang), jnp.sin(ang)
    n = pos.shape[0]
    reps = LANES // HEAD_DIM_C
    rest = HEAD_DIM_C - ROPE_DIM
    ones = jnp.ones((n, rest), F32)
    zeros = jnp.zeros((n, rest), F32)
    zh = jnp.zeros((n, half), F32)
    c = jnp.tile(jnp.concatenate([cos, cos, ones], axis=1), (1, reps))
    s1 = jnp.tile(jnp.concatenate([zh, sin, zeros], axis=1), (1, reps))
    s2 = jnp.tile(jnp.concatenate([-sin, zh, zeros], axis=1), (1, reps))
    return c, s1, s2


def _ffn(hp, hs, pre, w_gate, w_up, w_down, layer, next_gain):
    f = w_gate.shape[-1]
    d = hp.shape[1]
    wg, wu = _Weight(w_gate, layer), _Weight(w_up, layer)
    a_p, ss_p, a_s, ss_s = pre
    act_p, wd = _gate_up(a_p, wg, wu, f, norm=_row_scale(ss_p, d),
                         cast=(w_down, layer))
    act_s = _gate_up(a_s, wg, wu, f, norm=_row_scale(ss_s, d))
    return _down(act_p, act_s, wd, hp, hs, gain=next_gain)


def kernel(x_prompt, x_sample, cache_sb_k, cache_sb_v, cache_band_k, cache_band_v, cache_swa_k, cache_swa_v, norm_mix, norm_ffn, norm_final, w_in_ab, w_out_ab, rel_bias, w_qkv_c, w_out_c, sinks, w_gate, w_up, w_down):
    bp, seq, dm = x_prompt.shape
    bs, t, _ = x_sample.shape
    depth = norm_mix.shape[0]
    past = cache_sb_k.shape[2]
    hp = x_prompt.reshape(bp * seq, dm)
    hs = x_sample.reshape(bs * t, dm)
    norm_mix3 = norm_mix.reshape(depth, 1, dm)
    norm_final3 = norm_final.reshape(1, 1, dm)

    w_sb = cache_sb_k.shape[3] * cache_sb_k.shape[4]
    w_band = cache_band_k.shape[3] * cache_band_k.shape[4]
    wq_c = w_out_c.shape[1]
    wk_c = cache_swa_k.shape[3] * cache_swa_k.shape[4]
    p_band = cache_band_k.shape[2]
    p_swa = cache_swa_k.shape[2]

    rope_p = _rope_tables(jnp.arange(seq))
    rope_s = tuple(jnp.tile(x, (bs, 1)) for x in _rope_tables(past + jnp.arange(t)))

    outs = {name: [] for name in (
        "sbk_p", "sbv_p", "bk_p", "bv_p", "ck_p", "cv_p",
        "sbk_s", "sbv_s", "bk_s", "bv_s", "ck_s", "cv_s")}

    pre = None
    for l in range(depth):
        if pre is None:
            a_p = _rmsnorm(hp, norm_mix3, l, BF16)
            a_s = _rmsnorm(hs, norm_mix3, l, BF16)
            norm = None
        else:
            a_p, ss_p, a_s, ss_s = pre
            norm = (_row_scale(ss_p, dm), _row_scale(ss_s, dm))
        if l % 2 == 0:
            e = l // 2
            cols = (0, w_sb, 2 * w_sb, 3 * w_sb, 3 * w_sb + w_band, 3 * w_sb + 2 * w_band)
            bias_p = _band_bias(rel_bias[e], BAND_TQ, BAND_HIST + BAND_TQ)
            bias_s = _band_bias(rel_bias[e], t, p_band + t)

            widths = (w_sb, w_sb, w_sb, w_band, w_band, w_band)
            dtypes = (BF16, F32, F32, BF16, F32, F32)

            w_in = [_Weight(w_in_ab, e, 0, c) for c in cols]
            w_out = (_Weight(w_out_ab, e, 0), _Weight(w_out_ab, e, 1))

            (qa, qa_s), (ka, ka_s), (va, va_s), (qb, qb_s), (kb, kb_s), (vb, vb_s) = (
                _proj(a_p, a_s, w, n, dt, norm=norm)
                for w, n, dt in zip(w_in, widths, dtypes))
            oa, ka_cache, va_cache = _sb_prompt(qa, ka, va, bp, seq)
            ob = _band_prompt(qb, kb, vb, bias_p, bp, seq)
            oa_s, ka_new, va_new = _sb_step(qa_s, ka_s, va_s, cache_sb_k[e],
                                            cache_sb_v[e], bs)
            ob_s, nbk, nbv = _band_step(qb_s, kb_s, vb_s, cache_band_k[e],
                                        cache_band_v[e], bias_s, bs)
            hp, hs, *pre = _mm_res(
                [(oa, oa_s, 0, w_out[0]), (ob, ob_s, 0, w_out[1])], w_sb,
                hp, hs, DENSE_TN, gain=norm_ffn[l])
            nb = min(BAND_HIST, seq)
            outs["sbk_p"].append(ka_cache)
            outs["sbv_p"].append(va_cache)
            outs["bk_p"].append(_cache_tail(kb, bp, seq, nb))
            outs["bv_p"].append(_cache_tail(vb, bp, seq, nb))
            outs["sbk_s"].append(ka_new)
            outs["sbv_s"].append(va_new)
            outs["bk_s"].append(nbk)
            outs["bv_s"].append(nbv)
        else:
            o = l // 2

            widths = (wq_c, wk_c, wk_c)
            dtypes = (BF16, F32, F32)

            w_in = [_Weight(w_qkv_c, o, 0, c) for c in (0, wq_c, wq_c + wk_c)]
            w_out = _Weight(w_out_c, o, 0)

            (q, q_s), (k, k_s), (v, v_s) = (
                _proj(a_p, a_s, w, n, dt, rope=rp, rope_s=rs, norm=norm)
                for w, n, dt, rp, rs in zip(w_in, widths, dtypes,
                                            (rope_p, rope_p, None),
                                            (rope_s, rope_s, None)))
            oc = _swa_prompt(q, k, v, sinks[o], bp, seq)
            oc_s, nck, ncv = _swa_step(
                q_s, k_s, v_s, cache_swa_k[o].reshape(bs, p_swa, wk_c),
                cache_swa_v[o].reshape(bs, p_swa, wk_c), sinks[o], bs)
            hp, hs, *pre = _mm_res([(oc, oc_s, 0, w_out)], wq_c, hp, hs,
                                   DENSE_TN, gain=norm_ffn[l])
            nw = min(WINDOW, seq)
            outs["ck_p"].append(k.reshape(bp, seq, *cache_swa_k.shape[3:])[:, seq - nw:])
            outs["cv_p"].append(v.reshape(bp, seq, *cache_swa_k.shape[3:])[:, seq - nw:])
            outs["ck_s"].append(nck.reshape(cache_swa_k.shape[1:]))
            outs["cv_s"].append(ncv.reshape(cache_swa_v.shape[1:]))
        next_gain = norm_mix[l + 1] if l + 1 < depth else None
        hp, hs, *pre = _ffn(hp, hs, pre, w_gate, w_up, w_down, l, next_gain)

    y_prompt = _rmsnorm(hp, norm_final3, 0, F32).reshape(bp, seq, dm)
    y_sample = _rmsnorm(hs, norm_final3, 0, F32).reshape(bs, t, dm)
    st = lambda name: jnp.stack(outs[name])
    return (y_prompt, y_sample,
            st("sbk_p"), st("sbv_p"), st("bk_p"), st("bv_p"), st("ck_p"), st("cv_p"),
            st("sbk_s"), st("sbv_s"), st("bk_s"), st("bv_s"), st("ck_s"), st("cv_s"))
```

```python
import functools
import math

import numpy as np
import jax
import jax.numpy as jnp
from jax import lax
from jax.experimental import pallas as pl
from jax.experimental.pallas import tpu as pltpu

F32 = jnp.float32
BF16 = jnp.bfloat16

CHUNK = 64
EPS = 1e-6
HEAD_DIM_AB = 128
BAND_PREV_CHUNKS = 8
BAND_HIST = BAND_PREV_CHUNKS * CHUNK
REL_CLIP = 2 * CHUNK
HEAD_DIM_C = 64
SWA_PREV_CHUNKS = 2
WINDOW = SWA_PREV_CHUNKS * CHUNK
ROPE_THETA = 500000.0
ROPE_DIM = HEAD_DIM_C // 4

LANES = 128
V7X_VMEM_LIMIT_BYTES = 60 * 1024 * 1024

DENSE_TM = 1024
GATE_UP_TM = 2048
GATE_UP_TN = 256
DENSE_TN = 512

NEG = -1e30

SB_TILE = 256
SB_HEADS = 8
SB_STEP_KEYS = 128
BAND_TQ = 128
BAND_HEADS = 4
STEP_HEADS = 8
SWA_TQ = 128


def _params(*sem):
    return pltpu.CompilerParams(dimension_semantics=sem,
                                vmem_limit_bytes=V7X_VMEM_LIMIT_BYTES)


def _chunk_of(pos):
    return lax.shift_right_logical(pos, int(math.log2(CHUNK)))


def _pick_tm(m):
    return DENSE_TM if m % DENSE_TM == 0 else m


def _rmsnorm_kernel(x_ref, g_ref, o_ref):
    x = x_ref[...]
    ms = jnp.mean(x * x, axis=-1, keepdims=True)
    y = x * lax.rsqrt(ms + EPS)
    o_ref[...] = (y * g_ref[...]).astype(o_ref.dtype)


def _rmsnorm(x, g3, layer, out_dtype):
    m, d = x.shape
    tm = 256
    return pl.pallas_call(
        _rmsnorm_kernel,
        grid=(m // tm,),
        in_specs=[pl.BlockSpec((tm, d), lambda i: (i, 0)),
                  pl.BlockSpec((None, 1, d), lambda i: (layer, 0, 0))],
        out_specs=pl.BlockSpec((tm, d), lambda i: (i, 0)),
        out_shape=jax.ShapeDtypeStruct((m, d), out_dtype),
        compiler_params=_params("arbitrary"),
        name="rmsnorm",
    )(x, g3)


class _Weight:
    def __init__(self, src, layer, kblk=0, col0=0):
        self.src, self.layer, self.kblk, self.col0 = src, layer, kblk, col0

    def spec(self, tk, tn):
        layer, kb, jb = self.layer, self.kblk, self.col0 // tn
        return pl.BlockSpec((None, tk, tn), lambda i, j: (layer, kb, jb + j))


def _rope_tile(x, c, s1, s2):
    return (x * c + pltpu.roll(x, 8, 1) * s1
            + pltpu.roll(x, LANES - 8, 1) * s2)


def _rider_in(x):
    return pl.BlockSpec(x.shape, lambda i, j: (0, 0), pipeline_mode=pl.Buffered(1))


def _rider_out(rows, tn, ni):
    return pl.BlockSpec((rows, tn),
                        lambda i, j: (0, jnp.where(i == ni - 1, j, 0)))


def _on_last_row_block(fn):
    pl.when(pl.program_id(0) == pl.num_programs(0) - 1)(fn)


def _lane_chunks(n):
    return [slice(c * LANES, (c + 1) * LANES) for c in range(n // LANES)]


def _proj_kernel(*refs, rope, norm):
    w_ref = refs[1]
    n_tab = 3 if rope else 0
    a_ref, tabs = refs[0], refs[2:2 + n_tab]
    sa_ref, stabs = refs[2 + n_tab], refs[3 + n_tab:3 + 2 * n_tab]
    r_ref, sr_ref = refs[3 + 2 * n_tab:5 + 2 * n_tab] if norm else (None, None)
    o_ref, so_ref = refs[-2:]

    def run(a_ref, tabs, r_ref, o_ref):
        acc = jnp.dot(a_ref[...], w_ref[...].astype(BF16),
                      preferred_element_type=F32)
        for sl in _lane_chunks(acc.shape[1]):
            x = acc[:, sl]
            if norm:
                x = x * r_ref[...]
            if rope:
                x = _rope_tile(x, *(t[...] for t in tabs))
            o_ref[:, sl] = x.astype(o_ref.dtype)

    run(a_ref, tabs, r_ref, o_ref)
    _on_last_row_block(lambda: run(sa_ref, stabs, sr_ref, so_ref))


def _proj(a, a_s, w, ncols, out_dtype, rope=None, rope_s=None, norm=None):
    m, k = a.shape
    ms = a_s.shape[0]
    tm = _pick_tm(m)
    tn = min(DENSE_TN, ncols)
    ni = m // tm
    in_specs = [pl.BlockSpec((tm, k), lambda i, j: (i, 0)), w.spec(k, tn)]
    args = [a, w.src]
    if rope is not None:
        nrep = rope[0].shape[0] // tm
        for t in rope:
            in_specs.append(pl.BlockSpec((tm, LANES), lambda i, j: (i % nrep, 0)))
            args.append(t)
    for x in (a_s,) + tuple(rope_s or ()):
        in_specs.append(_rider_in(x))
        args.append(x)
    if norm is not None:
        r, r_s = norm
        in_specs += [pl.BlockSpec((tm, LANES), lambda i, j: (i, 0)), _rider_in(r_s)]
        args += [r, r_s]
    return pl.pallas_call(
        functools.partial(_proj_kernel, rope=rope is not None,
                          norm=norm is not None),
        grid=(ni, ncols // tn),
        in_specs=in_specs,
        out_specs=[pl.BlockSpec((tm, tn), lambda i, j: (i, j)),
                   _rider_out(ms, tn, ni)],
        out_shape=[jax.ShapeDtypeStruct((m, ncols), out_dtype),
                   jax.ShapeDtypeStruct((ms, ncols), out_dtype)],
        compiler_params=_params("arbitrary", "arbitrary"),
        name="proj",
    )(*args)


def _gate_up_kernel(*refs, norm, cast):
    a_ref, wg_ref, wu_ref = refs[:3]
    r_ref = refs[3] if norm else None
    o_ref = refs[-2] if cast else refs[-1]
    a = a_ref[...]
    g = jnp.dot(a, wg_ref[...].astype(BF16), preferred_element_type=F32)
    u = jnp.dot(a, wu_ref[...].astype(BF16), preferred_element_type=F32)
    for sl in _lane_chunks(g.shape[1]):
        gc, uc = g[:, sl], u[:, sl]
        if norm:
            gc, uc = gc * r_ref[...], uc * r_ref[...]
        o_ref[:, sl] = (gc * jax.nn.sigmoid(gc) * uc).astype(o_ref.dtype)
    if cast:
        refs[-1][...] = refs[-3][...].astype(BF16)


def _gate_up(a, wg, wu, f, norm=None, cast=None):
    m, k = a.shape
    tm = GATE_UP_TM if m % GATE_UP_TM == 0 else m
    tn = GATE_UP_TN
    ni, nj = m // tm, f // tn
    in_specs = [pl.BlockSpec((tm, k), lambda i, j: (i, 0)),
                wg.spec(k, tn), wu.spec(k, tn)]
    args = [a, wg.src, wu.src]
    out_specs = [pl.BlockSpec((tm, tn), lambda i, j: (i, j))]
    out_shape = [jax.ShapeDtypeStruct((m, f), BF16)]
    if norm is not None:
        in_specs.append(pl.BlockSpec((tm, LANES), lambda i, j: (i, 0),
                                     pipeline_mode=pl.Buffered(1)))
        args.append(norm)
    if cast is not None:
        w3, layer = cast
        rows, n = w3.shape[1] // (ni * nj), w3.shape[2]
        assert rows * ni * nj == w3.shape[1] and rows % 16 == 0
        in_specs.append(pl.BlockSpec((None, rows, n),
                                     lambda i, j: (layer, i * nj + j, 0)))
        args.append(w3)
        out_specs.append(pl.BlockSpec((rows, n), lambda i, j: (i * nj + j, 0)))
        out_shape.append(jax.ShapeDtypeStruct(w3.shape[1:], BF16))
    out = pl.pallas_call(
        functools.partial(_gate_up_kernel, norm=norm is not None,
                          cast=cast is not None),
        grid=(ni, nj),
        in_specs=in_specs,
        out_specs=out_specs,
        out_shape=out_shape,
        compiler_params=_params("arbitrary", "arbitrary"),
        name="gate_up",
    )(*args)
    return out if cast is not None else out[0]


def _mm_res_kernel(*refs, nterms, stats):
    n = nterms
    w_refs = refs[n:2 * n]
    gain_ref = refs[3 * n + 2] if stats else None
    outs = refs[3 * n + 2 + bool(stats):]

    def run(a_refs, res_ref, o_ref, b_ref, ss_ref):
        acc = res_ref[...]
        for a_ref, w_ref in zip(a_refs, w_refs):
            acc = acc + jnp.dot(a_ref[...], w_ref[...].astype(BF16),
                                preferred_element_type=F32)
        o_ref[...] = acc
        if stats:
            _row_stats(acc, gain_ref, b_ref, ss_ref)

    p_extra = (outs[2], outs[3]) if stats else (None, None)
    s_extra = (outs[4], outs[5]) if stats else (None, None)
    run(refs[:n], refs[2 * n], outs[0], *p_extra)
    _on_last_row_block(
        lambda: run(refs[2 * n + 1:3 * n + 1], refs[3 * n + 1], outs[1], *s_extra))


def _mm_res(terms, tk, res, res_s, tn, gain=None):
    stats = gain is not None
    m, n = res.shape
    ms = res_s.shape[0]
    tm = _pick_tm(m)
    ni = m // tm
    in_specs, args = [], []
    for a, _, ka, _ in terms:
        in_specs.append(pl.BlockSpec((tm, tk), lambda i, j, ka=ka: (i, ka)))
        args.append(a)
    for _, _, _, w in terms:
        in_specs.append(w.spec(tk, tn))
        args.append(w.src)
    tile = pl.BlockSpec((tm, tn), lambda i, j: (i, j))
    tile_s = _rider_out(ms, tn, ni)
    in_specs.append(tile)
    args.append(res)
    for _, a_s, ka, _ in terms:
        in_specs.append(pl.BlockSpec((ms, tk), lambda i, j, ka=ka: (0, ka),
                                     pipeline_mode=pl.Buffered(1)))
        args.append(a_s)
    in_specs.append(tile_s)
    args.append(res_s)
    out_specs = [tile, tile_s]
    out_shape = [jax.ShapeDtypeStruct((m, n), F32),
                 jax.ShapeDtypeStruct((ms, n), F32)]
    if stats:
        in_specs.append(pl.BlockSpec((1, tn), lambda i, j: (0, j)))
        args.append(gain.astype(F32).reshape(1, n))
        out_specs += [tile, pl.BlockSpec((tm, LANES), lambda i, j: (i, 0)),
                      tile_s, pl.BlockSpec((ms, LANES), lambda i, j: (0, 0))]
        out_shape += [jax.ShapeDtypeStruct((m, n), BF16),
                      jax.ShapeDtypeStruct((m, LANES), F32),
                      jax.ShapeDtypeStruct((ms, n), BF16),
                      jax.ShapeDtypeStruct((ms, LANES), F32)]
    return pl.pallas_call(
        functools.partial(_mm_res_kernel, nterms=len(terms), stats=stats),
        grid=(ni, n // tn),
        in_specs=in_specs,
        out_specs=out_specs,
        out_shape=out_shape,
        compiler_params=_params("arbitrary", "arbitrary"),
        name="mm_res",
    )(*args)


def _row_stats(acc, gain_ref, b_ref, ss_ref):
    b_ref[...] = (acc * gain_ref[...]).astype(BF16)
    part = sum(acc[:, sl] * acc[:, sl] for sl in _lane_chunks(acc.shape[1]))
    j = pl.program_id(1)

    @pl.when(j == 0)
    def _():
        ss_ref[...] = part

    @pl.when(j != 0)
    def _():
        ss_ref[...] += part


def _down_kernel(*refs, stats):
    a_refs, w_ref, res_ref = refs[0:2], refs[2], refs[3]
    sa_refs, sres_ref = refs[4:6], refs[6]
    gain_ref = refs[7] if stats else None
    outs = refs[7 + bool(stats):]
    k = pl.program_id(2)

    def run(x_refs, r_ref, o_ref, b_ref, ss_ref):
        @pl.when(k == 0)
        def _():
            o_ref[...] = r_ref[...] + jnp.dot(
                x_refs[0][...], w_ref[...].astype(BF16),
                preferred_element_type=F32)

        @pl.when(k == 1)
        def _():
            acc = o_ref[...] + jnp.dot(
                x_refs[1][...], w_ref[...].astype(BF16),
                preferred_element_type=F32)
            o_ref[...] = acc
            if stats:
                _row_stats(acc, gain_ref, b_ref, ss_ref)

    p_extra = (outs[2], outs[3]) if stats else (None, None)
    s_extra = (outs[4], outs[5]) if stats else (None, None)
    run(a_refs, res_ref, outs[0], *p_extra)
    _on_last_row_block(lambda: run(sa_refs, sres_ref, outs[1], *s_extra))


def _down(act, act_s, w, res, res_s, gain=None):
    stats = gain is not None
    m, n = res.shape
    ms = res_s.shape[0]
    tk = act.shape[1] // 2
    tm, tn = _pick_tm(m), DENSE_TN
    ni = m // tm
    one = dict(pipeline_mode=pl.Buffered(1))
    last = ni - 1
    tile = pl.BlockSpec((tm, tn), lambda i, j, k: (i, j))
    tile_s = pl.BlockSpec((ms, tn),
                          lambda i, j, k: (0, jnp.where(i == last, j, 0)))
    in_specs = (
        [pl.BlockSpec((tm, tk), lambda i, j, k, h=h: (i, h), **one) for h in (0, 1)]
        + [pl.BlockSpec((tk, tn), lambda i, j, k: (k, j)), tile]
        + [pl.BlockSpec((ms, tk), lambda i, j, k, h=h: (0, h), **one) for h in (0, 1)]
        + [tile_s])
    args = [act, act, w, res, act_s, act_s, res_s]
    out_specs = [tile, tile_s]
    out_shape = [jax.ShapeDtypeStruct((m, n), F32),
                 jax.ShapeDtypeStruct((ms, n), F32)]
    if stats:
        in_specs.append(pl.BlockSpec((1, tn), lambda i, j, k: (0, j)))
        args.append(gain.astype(F32).reshape(1, n))
        out_specs += [tile, pl.BlockSpec((tm, LANES), lambda i, j, k: (i, 0)),
                      tile_s, pl.BlockSpec((ms, LANES), lambda i, j, k: (0, 0))]
        out_shape += [jax.ShapeDtypeStruct((m, n), BF16),
                      jax.ShapeDtypeStruct((m, LANES), F32),
                      jax.ShapeDtypeStruct((ms, n), BF16),
                      jax.ShapeDtypeStruct((ms, LANES), F32)]
    return pl.pallas_call(
        functools.partial(_down_kernel, stats=stats),
        grid=(ni, n // tn, 2),
        in_specs=in_specs,
        out_specs=out_specs,
        out_shape=out_shape,
        compiler_params=_params("arbitrary", "arbitrary", "arbitrary"),
        name="down",
    )(*args)


def _row_scale_kernel(ss_ref, o_ref, *, d):
    tot = jnp.sum(ss_ref[...], axis=1, keepdims=True)
    o_ref[...] = jnp.broadcast_to(lax.rsqrt(tot / d + EPS), o_ref.shape)


def _row_scale(ss, d):
    m = ss.shape[0]
    tm = _pick_tm(m)
    blk = pl.BlockSpec((tm, LANES), lambda i: (i, 0))
    return pl.pallas_call(
        functools.partial(_row_scale_kernel, d=d),
        grid=(m // tm,),
        in_specs=[blk],
        out_specs=blk,
        out_shape=jax.ShapeDtypeStruct(ss.shape, F32),
        compiler_params=_params("arbitrary"),
        name="row_scale",
    )(ss)


_DN_T = (((1,), (1,)), ((), ()))


def _head(h, d=HEAD_DIM_AB):
    return slice(h * d, (h + 1) * d)


def _heads_major(x):
    return pltpu.einshape("phd->hpd", x)


def _softmax_stage(ss, sinks=None):
    ms = [jnp.max(s, axis=1, keepdims=True) for s in ss]
    if sinks is not None:
        ms = [jnp.maximum(m, sk) for m, sk in zip(ms, sinks)]
    ps = [jnp.exp(s - m) for s, m in zip(ss, ms)]
    ls = [jnp.sum(p, axis=1, keepdims=True) for p in ps]
    if sinks is not None:
        ls = [l + jnp.exp(sk - m) for l, sk, m in zip(ls, sinks, ms)]
    return [p.astype(BF16) for p in ps], ls


def _sb_logs(z, valid):
    lb = jnp.minimum(z, 0.0) - jnp.log(1.0 + jnp.exp(-jnp.abs(z)))
    l1m = lb - z
    if valid is not None:
        l1m = jnp.where(valid, l1m, 0.0)
    return lb, l1m


def _split_hi_lo(x):
    hi = x.astype(BF16)
    lo = (x - hi.astype(F32)).astype(BF16)
    return jnp.concatenate([hi, lo], axis=1)


def _suffix_matrix(n):
    u = np.tril(np.ones((n, n), np.float32), -1)
    return jnp.asarray(np.concatenate([u, u], axis=0), BF16)


def _sb_prompt_kernel(q_ref, k_ref, v_ref, uu_ref, o_ref, kc_ref, vc_ref,
                      kb_ref, vb_ref, oacc_ref, cacc_ref, *, scale):
    qi = pl.program_id(2)
    tq = q_ref.shape[0]
    nh = oacc_ref.shape[0]

    @pl.when(qi == 0)
    def _():
        kb_ref[...] = k_ref[...].astype(BF16)
        vb_ref[...] = v_ref[...].astype(BF16)

    rows = pl.ds(pl.multiple_of(qi * tq, tq), tq)
    kc_ref[...] = k_ref[rows, :].reshape(kc_ref.shape)
    vc_ref[...] = v_ref[rows, :].reshape(vc_ref.shape)

    oacc_ref[...] = jnp.zeros(oacc_ref.shape, F32)
    cacc_ref[...] = jnp.zeros(cacc_ref.shape, F32)

    def tile(j, masked):
        start = pl.multiple_of(j * tq, tq)
        valid = None
        if masked:
            row = lax.broadcasted_iota(jnp.int32, (tq, tq), 0)
            col = lax.broadcasted_iota(jnp.int32, (tq, tq), 1)
            valid = col < row
        zs = [lax.dot_general(q_ref[:, _head(h)],
                              kb_ref[pl.ds(start, tq), _head(h)], _DN_T,
                              preferred_element_type=F32) * scale
              for h in range(nh)]
        logs = [_sb_logs(z, valid) for z in zs]
        sufs = [jnp.dot(_split_hi_lo(l1m), uu_ref[...],
                        preferred_element_type=F32) for _, l1m in logs]
        ws = [jnp.exp(lb + suf + cacc_ref[h])
              for h, ((lb, _), suf) in enumerate(zip(logs, sufs))]
        if masked:
            ws = [jnp.where(valid, w, 0.0) for w in ws]
        for h in range(nh):
            oacc_ref[h] += jnp.dot(ws[h].astype(BF16),
                                   vb_ref[pl.ds(start, tq), _head(h)],
                                   preferred_element_type=F32)
            cacc_ref[h] += jnp.sum(logs[h][1], axis=1, keepdims=True)

    tile(qi, True)

    def body(t, c):
        tile(qi - 1 - t, False)
        return c

    lax.fori_loop(0, qi, body, 0)
    for h in range(nh):
        o_ref[:, _head(h)] = oacc_ref[h].astype(o_ref.dtype)


def _sb_prompt(q, k, v, batch, seq):
    m, w = q.shape
    d = HEAD_DIM_AB
    nh = SB_HEADS
    tq = SB_TILE
    nq = seq // tq
    qo = pl.BlockSpec((tq, nh * d), lambda b, g, i: (b * nq + i, g))
    kv = pl.BlockSpec((seq, nh * d), lambda b, g, i: (b, g))
    cache = pl.BlockSpec((None, tq, nh, d), lambda b, g, i: (b, i, g, 0))
    cache_shape = jax.ShapeDtypeStruct((batch, seq, w // d, d), F32)
    return pl.pallas_call(
        functools.partial(_sb_prompt_kernel, scale=d ** -0.5),
        grid=(batch, w // (nh * d), nq),
        in_specs=[qo, kv, kv, pl.BlockSpec((2 * tq, tq), lambda b, g, i: (0, 0))],
        out_specs=[qo, cache, cache],
        out_shape=[jax.ShapeDtypeStruct((m, w), BF16), cache_shape, cache_shape],
        scratch_shapes=[pltpu.VMEM((seq, nh * d), BF16),
                        pltpu.VMEM((seq, nh * d), BF16),
                        pltpu.VMEM((nh, tq, d), F32),
                        pltpu.VMEM((nh, tq, 1), F32)],
        compiler_params=_params("arbitrary", "arbitrary", "arbitrary"),
        name="sb_prompt",
    )(q, k, v, _suffix_matrix(tq))


def _sb_step_kernel(q_ref, kn_ref, vn_ref, kc_ref, vc_ref, uu_ref, o_ref,
                    knc_ref, vnc_ref, kb_ref, vb_ref, *, scale):
    t = q_ref.shape[0]
    past, nh, d = kc_ref.shape
    tk = uu_ref.shape[1]
    n = past + tk
    knc_ref[...] = kn_ref[...].reshape(knc_ref.shape)
    vnc_ref[...] = vn_ref[...].reshape(vnc_ref.shape)
    kb_ref[:, 0:past, :] = _heads_major(kc_ref[...]).astype(BF16)
    vb_ref[:, 0:past, :] = _heads_major(vc_ref[...]).astype(BF16)
    for h in range(nh):
        kb_ref[h, past:past + t] = kn_ref[:, _head(h)].astype(BF16)
        vb_ref[h, past:past + t] = vn_ref[:, _head(h)].astype(BF16)
        kb_ref[h, past + t:n] = jnp.zeros((tk - t, d), BF16)
        vb_ref[h, past + t:n] = jnp.zeros((tk - t, d), BF16)

    z = jnp.concatenate(
        [lax.dot_general(q_ref[:, _head(h)], kb_ref[h], _DN_T,
                         preferred_element_type=F32) for h in range(nh)],
        axis=0) * scale
    row = lax.broadcasted_iota(jnp.int32, z.shape, 0) & (t - 1)
    col = lax.broadcasted_iota(jnp.int32, z.shape, 1)
    valid = col < past + row
    lb, l1m = _sb_logs(z, valid)
    ntiles = n // tk
    sufs = [jnp.dot(_split_hi_lo(l1m[:, j * tk:(j + 1) * tk]), uu_ref[...],
                    preferred_element_type=F32) for j in range(ntiles)]
    tots = [jnp.sum(l1m[:, j * tk:(j + 1) * tk], axis=1, keepdims=True)
            for j in range(ntiles)]
    carries = [None] * ntiles
    carry = jnp.zeros_like(tots[0])
    for j in reversed(range(ntiles)):
        carries[j] = carry
        carry = carry + tots[j]
    after = jnp.concatenate([sufs[j] + carries[j] for j in range(ntiles)],
                            axis=1)
    w = jnp.where(valid, jnp.exp(lb + after), 0.0).astype(BF16)
    for h in range(nh):
        o = jnp.dot(w[h * t:(h + 1) * t], vb_ref[h], preferred_element_type=F32)
        o_ref[:, _head(h)] = o.astype(o_ref.dtype)


def _sb_step(q, k_new, v_new, k_cache, v_cache, batch):
    m, w = q.shape
    _, past, heads, d = k_cache.shape
    t = m // batch
    tk = SB_STEP_KEYS
    nh = STEP_HEADS
    assert past % tk == 0 and t <= tk and t & (t - 1) == 0
    row = pl.BlockSpec((t, nh * d), lambda b, g: (b, g))
    cache = pl.BlockSpec((None, past, nh, d), lambda b, g: (b, 0, g, 0))
    new = pl.BlockSpec((None, t, nh, d), lambda b, g: (b, 0, g, 0))
    new_shape = jax.ShapeDtypeStruct((batch, t, heads, d), F32)
    return pl.pallas_call(
        functools.partial(_sb_step_kernel, scale=d ** -0.5),
        grid=(batch, heads // nh),
        in_specs=[row, row, row, cache, cache,
                  pl.BlockSpec((2 * tk, tk), lambda b, g: (0, 0))],
        out_specs=[row, new, new],
        out_shape=[jax.ShapeDtypeStruct((m, w), BF16), new_shape, new_shape],
        scratch_shapes=[pltpu.VMEM((nh, past + tk, d), BF16),
                        pltpu.VMEM((nh, past + tk, d), BF16)],
        compiler_params=_params("arbitrary", "arbitrary"),
        name="sb_step",
    )(q, k_new, v_new, k_cache, v_cache, _suffix_matrix(tk))


def _band_bias_kernel(r_ref, o_ref, *, n_valid):
    tq, w = o_ref.shape
    wr = r_ref.shape[1]
    t = jnp.broadcast_to(r_ref[...], (tq, wr))
    t = pltpu.roll(t, 0, 1, stride=1, stride_axis=0)
    row = lax.broadcasted_iota(jnp.int32, (tq, w), 0)
    col = lax.broadcasted_iota(jnp.int32, (tq, w), 1)
    qc = _chunk_of(row)
    kc = _chunk_of(col)
    valid = (kc >= qc) & (kc <= qc + BAND_PREV_CHUNKS) & (col < n_valid)
    o_ref[...] = jnp.where(valid, t[:, :w], NEG)


def _band_bias(rel, tq, n_valid):
    heads = rel.shape[0]
    w = BAND_HIST + BAND_TQ
    wr = w + LANES
    p = np.arange(wr)
    delta = np.where(p < w, p, p - wr)
    idx = np.clip(BAND_HIST - delta, -REL_CLIP, REL_CLIP) + REL_CLIP
    r = jnp.take(rel.astype(F32), jnp.asarray(idx, jnp.int32), axis=1)
    r = r.reshape(heads, 1, wr)
    return pl.pallas_call(
        functools.partial(_band_bias_kernel, n_valid=n_valid),
        grid=(heads,),
        in_specs=[pl.BlockSpec((None, 1, wr), lambda h: (h, 0, 0))],
        out_specs=pl.BlockSpec((None, tq, w), lambda h: (h, 0, 0)),
        out_shape=jax.ShapeDtypeStruct((heads, tq, w), F32),
        compiler_params=_params("arbitrary"),
        name="band_bias",
    )(r)


def _band_prompt_kernel(q_ref, k_ref, v_ref, bias_ref, o_ref, kp_ref, vp_ref,
                        *, scale, tq):
    seq, wd = k_ref.shape
    d = HEAD_DIM_AB
    nh = wd // d
    w = bias_ref.shape[1]
    hist = w - tq
    kp_ref[0:hist] = jnp.zeros((hist, wd), BF16)
    vp_ref[0:hist] = jnp.zeros((hist, wd), BF16)
    kp_ref[hist:hist + seq] = k_ref[...].astype(BF16)
    vp_ref[hist:hist + seq] = v_ref[...].astype(BF16)
    col = lax.broadcasted_iota(jnp.int32, (tq, w), 1)

    def tile(i, at_start):
        start = pl.multiple_of(i * tq, tq)
        ss = []
        for h in range(nh):
            s = lax.dot_general(q_ref[pl.ds(start, tq), _head(h)],
                                kp_ref[pl.ds(start, w), _head(h)], _DN_T,
                                preferred_element_type=F32)
            s = s * scale + bias_ref[h * tq:(h + 1) * tq, :]
            if at_start:
                s = jnp.where(col >= hist - start, s, NEG)
            ss.append(s)
        ps, ls = _softmax_stage(ss)
        for h in range(nh):
            o = jnp.dot(ps[h], vp_ref[pl.ds(start, w), _head(h)],
                        preferred_element_type=F32) / ls[h]
            o_ref[pl.ds(start, tq), _head(h)] = o.astype(o_ref.dtype)

    def loop(lo, hi, at_start):
        def body(i, c):
            tile(i, at_start)
            return c
        lax.fori_loop(lo, hi, body, 0)

    n_tiles = seq // tq
    n_start = min(hist // tq, n_tiles)
    loop(0, n_start, True)
    loop(n_start, n_tiles, False)


def _band_prompt(q, k, v, bias, batch, seq):
    m, wd = q.shape
    d = HEAD_DIM_AB
    nh = BAND_HEADS
    heads, tq, w = bias.shape
    bias = bias.reshape(heads // nh, nh * tq, w)
    blk = pl.BlockSpec((seq, nh * d), lambda b, g: (b, g))
    return pl.pallas_call(
        functools.partial(_band_prompt_kernel, scale=d ** -0.5, tq=tq),
        grid=(batch, heads // nh),
        in_specs=[blk, blk, blk,
                  pl.BlockSpec((None, nh * tq, w), lambda b, g: (g, 0, 0))],
        out_specs=blk,
        out_shape=jax.ShapeDtypeStruct((m, wd), BF16),
        scratch_shapes=[pltpu.VMEM((BAND_HIST + seq, nh * d), BF16),
                        pltpu.VMEM((BAND_HIST + seq, nh * d), BF16)],
        compiler_params=_params("arbitrary", "arbitrary"),
        name="band_prompt",
    )(q, k, v, bias)


def _cache_tail_kernel(x_ref, o_ref):
    o_ref[...] = x_ref[...].reshape(o_ref.shape)


def _cache_tail(x, batch, seq, rows):
    d = HEAD_DIM_AB
    nh = STEP_HEADS
    heads = x.shape[1] // d
    assert seq % rows == 0
    last = seq // rows - 1
    return pl.pallas_call(
        _cache_tail_kernel,
        grid=(batch, heads // nh),
        in_specs=[pl.BlockSpec((rows, nh * d),
                               lambda b, g: (b * (last + 1) + last, g))],
        out_specs=pl.BlockSpec((None, rows, nh, d), lambda b, g: (b, 0, g, 0)),
        out_shape=jax.ShapeDtypeStruct((batch, rows, heads, d), x.dtype),
        compiler_params=_params("arbitrary", "arbitrary"),
        name="cache_tail",
    )(x)


def _roll_cache(new_ref, cache, new_rows_ref, t):
    past, nh, d = cache.shape
    new_ref[0:past - t] = cache[t:]
    for h in range(nh):
        new_ref[past - t:past, h, :] = new_rows_ref[:, _head(h, d)]


def _band_step_kernel(q_ref, kn_ref, vn_ref, kc_ref, vc_ref, bias_ref, o_ref,
                      nk_ref, nv_ref, kb_ref, vb_ref, *, scale):
    t = q_ref.shape[0]
    past, nh, d = kc_ref.shape
    w = bias_ref.shape[1]
    kc, vc = kc_ref[...], vc_ref[...]
    _roll_cache(nk_ref, kc, kn_ref, t)
    _roll_cache(nv_ref, vc, vn_ref, t)
    kb_ref[:, 0:past, :] = _heads_major(kc).astype(BF16)
    vb_ref[:, 0:past, :] = _heads_major(vc).astype(BF16)
    for h in range(nh):
        kb_ref[h, past:past + t] = kn_ref[:, _head(h)].astype(BF16)
        vb_ref[h, past:past + t] = vn_ref[:, _head(h)].astype(BF16)
        kb_ref[h, past + t:w] = jnp.zeros((w - past - t, d), BF16)
        vb_ref[h, past + t:w] = jnp.zeros((w - past - t, d), BF16)
    ss = [lax.dot_general(q_ref[:, _head(h)], kb_ref[h], _DN_T,
                          preferred_element_type=F32) * scale
          + bias_ref[h * t:(h + 1) * t, :] for h in range(nh)]
    ps, ls = _softmax_stage(ss)
    for h in range(nh):
        o = jnp.dot(ps[h], vb_ref[h], preferred_element_type=F32) / ls[h]
        o_ref[:, _head(h)] = o.astype(o_ref.dtype)


def _band_step(q, k_new, v_new, k_cache, v_cache, bias, batch):
    m, wd = q.shape
    _, past, heads, d = k_cache.shape
    t = m // batch
    nh = STEP_HEADS
    w = bias.shape[2]
    assert past == BAND_HIST and past + t <= w
    bias = bias.reshape(heads // nh, nh * t, w)
    row = pl.BlockSpec((t, nh * d), lambda b, g: (b, g))
    cache = pl.BlockSpec((None, past, nh, d), lambda b, g: (b, 0, g, 0))
    return pl.pallas_call(
        functools.partial(_band_step_kernel, scale=d ** -0.5),
        grid=(batch, heads // nh),
        in_specs=[row, row, row, cache, cache,
                  pl.BlockSpec((None, nh * t, w), lambda b, g: (g, 0, 0))],
        out_specs=[row, cache, cache],
        out_shape=[jax.ShapeDtypeStruct((m, wd), BF16),
                   jax.ShapeDtypeStruct(k_cache.shape, F32),
                   jax.ShapeDtypeStruct(v_cache.shape, F32)],
        scratch_shapes=[pltpu.VMEM((nh, w, d), BF16),
                        pltpu.VMEM((nh, w, d), BF16)],
        compiler_params=_params("arbitrary", "arbitrary"),
        name="band_step",
    )(q, k_new, v_new, k_cache, v_cache, bias)


def _split_pair(blk):
    rolled = pltpu.roll(blk, HEAD_DIM_C, 1)
    lane = lax.broadcasted_iota(jnp.int32, blk.shape, 1)
    low = lane < HEAD_DIM_C
    zero = jnp.zeros_like(blk)
    return [jnp.where(low, blk, zero), jnp.where(low, zero, rolled),
            jnp.where(low, rolled, zero), jnp.where(low, zero, blk)]


def _swa_stage(qs, kls, khs, vls, vhs, valid, sinks, scale):
    ss = []
    for q, kl, kh in zip(qs, kls, khs):
        for kx in (kl, kh):
            s = lax.dot_general(q, kx, _DN_T, preferred_element_type=F32)
            ss.append(jnp.where(valid, s * scale, NEG))
    ps, ls = _softmax_stage(ss, sinks)
    return [jnp.dot(ps[2 * i], vls[i], preferred_element_type=F32) / ls[2 * i]
            + jnp.dot(ps[2 * i + 1], vhs[i], preferred_element_type=F32)
            / ls[2 * i + 1] for i in range(len(qs))]


def _swa_prompt_kernel(sinks_ref, q_ref, k_ref, v_ref, o_ref, ks_ref, vs_ref,
                       *, scale, tq):
    hp = pl.program_id(1)
    seq = k_ref.shape[0]
    band = WINDOW + tq
    npair = q_ref.shape[1] // LANES
    group = npair // 2

    for src, dst in ((k_ref, ks_ref), (v_ref, vs_ref)):
        for n, x in enumerate(_split_pair(src[...])):
            dst[n, 0:WINDOW] = jnp.zeros((WINDOW, LANES), BF16)
            dst[n, WINDOW:WINDOW + seq] = x.astype(BF16)

    row = lax.broadcasted_iota(jnp.int32, (tq, band), 0)
    col = lax.broadcasted_iota(jnp.int32, (tq, band), 1)
    qc = _chunk_of(row)
    kc = _chunk_of(col)
    in_band = (kc >= qc) & (kc <= qc + SWA_PREV_CHUNKS)
    sinks = [sinks_ref[hp * 2 * npair + n] for n in range(2 * npair)]

    def body(i, c):
        start = pl.multiple_of(i * tq, tq)
        valid = in_band & (col >= WINDOW - start)
        kv = [[r[n, pl.ds(start, band), :] for n in range(4)]
              for r in (ks_ref, vs_ref)]
        qs = [q_ref[pl.ds(start, tq), _head(p, LANES)] for p in range(npair)]
        sel = [2 * (p // group) for p in range(npair)]
        outs = _swa_stage(qs, [kv[0][s] for s in sel], [kv[0][s + 1] for s in sel],
                          [kv[1][s] for s in sel], [kv[1][s + 1] for s in sel],
                          valid, sinks, scale)
        for p in range(npair):
            o_ref[pl.ds(start, tq), _head(p, LANES)] = outs[p].astype(o_ref.dtype)
        return c

    lax.fori_loop(0, seq // tq, body, 0)


def _swa_prompt(q, k, v, sinks, batch, seq):
    m, wq = q.shape
    wk = k.shape[1]
    npairs = wk // LANES
    qw = wq // npairs
    qo = pl.BlockSpec((seq, qw), lambda b, h, s: (b, h))
    kv = pl.BlockSpec((seq, LANES), lambda b, h, s: (b, h))
    grid_spec = pltpu.PrefetchScalarGridSpec(
        num_scalar_prefetch=1,
        grid=(batch, npairs),
        in_specs=[qo, kv, kv],
        out_specs=qo,
        scratch_shapes=[pltpu.VMEM((4, WINDOW + seq, LANES), BF16),
                        pltpu.VMEM((4, WINDOW + seq, LANES), BF16)],
    )
    return pl.pallas_call(
        functools.partial(_swa_prompt_kernel, scale=HEAD_DIM_C ** -0.5,
                          tq=SWA_TQ),
        grid_spec=grid_spec,
        out_shape=jax.ShapeDtypeStruct((m, wq), BF16),
        compiler_params=_params("arbitrary", "arbitrary"),
        name="swa_prompt",
    )(sinks, q, k, v)


def _swa_step_kernel(sinks_ref, q_ref, kn_ref, vn_ref, kc_ref, vc_ref, o_ref,
                     nk_ref, nv_ref, *, scale):
    t = q_ref.shape[0]
    past = kc_ref.shape[0]
    wk = kn_ref.shape[1]
    band = 2 * LANES
    nkv = wk // LANES
    npair = q_ref.shape[1] // LANES
    group = npair // (2 * nkv)
    kc, vc = kc_ref[...].reshape(past, wk), vc_ref[...].reshape(past, wk)
    kn, vn = kn_ref[...], vn_ref[...]
    nk_ref[...] = jnp.concatenate([kc[t:], kn], axis=0).reshape(nk_ref.shape)
    nv_ref[...] = jnp.concatenate([vc[t:], vn], axis=0).reshape(nv_ref.shape)
    pad = jnp.zeros((band - past - t, wk), F32)
    k_all = jnp.concatenate([kc, kn, pad], axis=0)
    v_all = jnp.concatenate([vc, vn, pad], axis=0)
    col = lax.broadcasted_iota(jnp.int32, (t, band), 1)
    valid = col < past + t
    kx = [[x.astype(BF16) for x in _split_pair(k_all[:, _head(g, LANES)])]
          for g in range(nkv)]
    vx = [[x.astype(BF16) for x in _split_pair(v_all[:, _head(g, LANES)])]
          for g in range(nkv)]
    qs = [q_ref[:, _head(p, LANES)] for p in range(npair)]
    where = [(p // (2 * group), 2 * ((p // group) % 2)) for p in range(npair)]
    sinks = [sinks_ref[n] for n in range(2 * npair)]
    outs = _swa_stage(qs, [kx[g][s] for g, s in where],
                      [kx[g][s + 1] for g, s in where],
                      [vx[g][s] for g, s in where],
                      [vx[g][s + 1] for g, s in where], valid, sinks, scale)
    for p in range(npair):
        o_ref[:, _head(p, LANES)] = outs[p].astype(o_ref.dtype)


def _swa_step(q, k_new, v_new, k_cache, v_cache, sinks, batch):
    m, wq = q.shape
    wk = k_new.shape[1]
    t = m // batch
    past = k_cache.shape[1]
    assert past + t <= 2 * LANES
    qrow = pl.BlockSpec((t, wq), lambda b, s: (b, 0))
    krow = pl.BlockSpec((t, wk), lambda b, s: (b, 0))
    cache = pl.BlockSpec((None,) + k_cache.shape[1:], lambda b, s: (b, 0, 0, 0))
    grid_spec = pltpu.PrefetchScalarGridSpec(
        num_scalar_prefetch=1,
        grid=(batch,),
        in_specs=[qrow, krow, krow, cache, cache],
        out_specs=[qrow, cache, cache],
    )
    return pl.pallas_call(
        functools.partial(_swa_step_kernel, scale=HEAD_DIM_C ** -0.5),
        grid_spec=grid_spec,
        out_shape=[jax.ShapeDtypeStruct((m, wq), BF16),
                   jax.ShapeDtypeStruct(k_cache.shape, F32),
                   jax.ShapeDtypeStruct(v_cache.shape, F32)],
        compiler_params=_params("arbitrary"),
        name="swa_step",
    )(sinks, q, k_new, v_new, k_cache, v_cache)


def _rope_tables(pos):
    half = ROPE_DIM // 2
    inv = jnp.power(ROPE_THETA, -jnp.arange(half, dtype=F32) / half)
    ang = pos.astype(F32)[:, None] * inv[None, :]
    cos, sin = jnp.cos(ang), jnp.sin(ang)
    n = pos.shape[0]
    reps = LANES // HEAD_DIM_C
    rest = HEAD_DIM_C - ROPE_DIM
    ones = jnp.ones((n, rest), F32)
    zeros = jnp.zeros((n, rest), F32)
    zh = jnp.zeros((n, half), F32)
    c = jnp.tile(jnp.concatenate([cos, cos, ones], axis=1), (1, reps))
    s1 = jnp.tile(jnp.concatenate([zh, sin, zeros], axis=1), (1, reps))
    s2 = jnp.tile(jnp.concatenate([-sin, zh, zeros], axis=1), (1, reps))
    return c, s1, s2


def _ffn(hp, hs, pre, w_gate, w_up, w_down, layer, next_gain):
    f = w_gate.shape[-1]
    d = hp.shape[1]
    wg, wu = _Weight(w_gate, layer), _Weight(w_up, layer)
    a_p, ss_p, a_s, ss_s = pre
    act_p, wd = _gate_up(a_p, wg, wu, f, norm=_row_scale(ss_p, d),
                         cast=(w_down, layer))
    act_s = _gate_up(a_s, wg, wu, f, norm=_row_scale(ss_s, d))
    return _down(act_p, act_s, wd, hp, hs, gain=next_gain)


def kernel(x_prompt, x_sample, cache_sb_k, cache_sb_v, cache_band_k, cache_band_v, cache_swa_k, cache_swa_v, norm_mix, norm_ffn, norm_final, w_in_ab, w_out_ab, rel_bias, w_qkv_c, w_out_c, sinks, w_gate, w_up, w_down):
    bp, seq, dm = x_prompt.shape
    bs, t, _ = x_sample.shape
    depth = norm_mix.shape[0]
    past = cache_sb_k.shape[2]
    hp = x_prompt.reshape(bp * seq, dm)
    hs = x_sample.reshape(bs * t, dm)
    norm_mix3 = norm_mix.reshape(depth, 1, dm)
    norm_final3 = norm_final.reshape(1, 1, dm)

    w_sb = cache_sb_k.shape[3] * cache_sb_k.shape[4]
    w_band = cache_band_k.shape[3] * cache_band_k.shape[4]
    wq_c = w_out_c.shape[1]
    wk_c = cache_swa_k.shape[3] * cache_swa_k.shape[4]
    p_band = cache_band_k.shape[2]

    rope_p = _rope_tables(jnp.arange(seq))
    rope_s = tuple(jnp.tile(x, (bs, 1)) for x in _rope_tables(past + jnp.arange(t)))

    outs = {name: [] for name in (
        "sbk_p", "sbv_p", "bk_p", "bv_p", "ck_p", "cv_p",
        "sbk_s", "sbv_s", "bk_s", "bv_s", "ck_s", "cv_s")}

    pre = None
    for l in range(depth):
        if pre is None:
            a_p = _rmsnorm(hp, norm_mix3, l, BF16)
            a_s = _rmsnorm(hs, norm_mix3, l, BF16)
            norm = None
        else:
            a_p, ss_p, a_s, ss_s = pre
            norm = (_row_scale(ss_p, dm), _row_scale(ss_s, dm))
        if l % 2 == 0:
            e = l // 2
            cols = (0, w_sb, 2 * w_sb, 3 * w_sb, 3 * w_sb + w_band, 3 * w_sb + 2 * w_band)
            bias_p = _band_bias(rel_bias[e], BAND_TQ, BAND_HIST + BAND_TQ)
            bias_s = _band_bias(rel_bias[e], t, p_band + t)

            widths = (w_sb, w_sb, w_sb, w_band, w_band, w_band)
            dtypes = (BF16, F32, F32, BF16, F32, F32)

            w_in = [_Weight(w_in_ab, e, 0, c) for c in cols]
            w_out = (_Weight(w_out_ab, e, 0), _Weight(w_out_ab, e, 1))

            (qa, qa_s), (ka, ka_s), (va, va_s), (qb, qb_s), (kb, kb_s), (vb, vb_s) = (
                _proj(a_p, a_s, w, n, dt, norm=norm)
                for w, n, dt in zip(w_in, widths, dtypes))
            oa, ka_cache, va_cache = _sb_prompt(qa, ka, va, bp, seq)
            ob = _band_prompt(qb, kb, vb, bias_p, bp, seq)
            oa_s, ka_new, va_new = _sb_step(qa_s, ka_s, va_s, cache_sb_k[e],
                                            cache_sb_v[e], bs)
            ob_s, nbk, nbv = _band_step(qb_s, kb_s, vb_s, cache_band_k[e],
                                        cache_band_v[e], bias_s, bs)
            hp, hs, *pre = _mm_res(
                [(oa, oa_s, 0, w_out[0]), (ob, ob_s, 0, w_out[1])], w_sb,
                hp, hs, DENSE_TN, gain=norm_ffn[l])
            nb = min(BAND_HIST, seq)
            outs["sbk_p"].append(ka_cache)
            outs["sbv_p"].append(va_cache)
            outs["bk_p"].append(_cache_tail(kb, bp, seq, nb))
            outs["bv_p"].append(_cache_tail(vb, bp, seq, nb))
            outs["sbk_s"].append(ka_new)
            outs["sbv_s"].append(va_new)
            outs["bk_s"].append(nbk)
            outs["bv_s"].append(nbv)
        else:
            o = l // 2

            widths = (wq_c, wk_c, wk_c)
            dtypes = (BF16, F32, F32)

            w_in = [_Weight(w_qkv_c, o, 0, c) for c in (0, wq_c, wq_c + wk_c)]
            w_out = _Weight(w_out_c, o, 0)

            (q, q_s), (k, k_s), (v, v_s) = (
                _proj(a_p, a_s, w, n, dt, rope=rp, rope_s=rs, norm=norm)
                for w, n, dt, rp, rs in zip(w_in, widths, dtypes,
                                            (rope_p, rope_p, None),
                                            (rope_s, rope_s, None)))
            oc = _swa_prompt(q, k, v, sinks[o], bp, seq)
            oc_s, nck, ncv = _swa_step(q_s, k_s, v_s, cache_swa_k[o],
                                       cache_swa_v[o], sinks[o], bs)
            hp, hs, *pre = _mm_res([(oc, oc_s, 0, w_out)], wq_c, hp, hs,
                                   DENSE_TN, gain=norm_ffn[l])
            nw = min(WINDOW, seq)
            outs["ck_p"].append(k.reshape(bp, seq, *cache_swa_k.shape[3:])[:, seq - nw:])
            outs["cv_p"].append(v.reshape(bp, seq, *cache_swa_k.shape[3:])[:, seq - nw:])
            outs["ck_s"].append(nck)
            outs["cv_s"].append(ncv)
        next_gain = norm_mix[l + 1] if l + 1 < depth else None
        hp, hs, *pre = _ffn(hp, hs, pre, w_gate, w_up, w_down, l, next_gain)

    y_prompt = _rmsnorm(hp, norm_final3, 0, F32).reshape(bp, seq, dm)
    y_sample = _rmsnorm(hs, norm_final3, 0, F32).reshape(bs, t, dm)
    st = lambda name: jnp.stack(outs[name])
    return (y_prompt, y_sample,
            st("sbk_p"), st("sbv_p"), st("bk_p"), st("bv_p"), st("ck_p"), st("cv_p"),
            st("sbk_s"), st("sbv_s"), st("bk_s"), st("bv_s"), st("ck_s"), st("cv_s"))
```

```python
import functools
import math

import numpy as np
import jax
import jax.numpy as jnp
from jax import lax
from jax.experimental import pallas as pl
from jax.experimental.pallas import tpu as pltpu

F32 = jnp.float32
BF16 = jnp.bfloat16

CHUNK = 64
EPS = 1e-6
HEAD_DIM_AB = 128
BAND_PREV_CHUNKS = 8
BAND_HIST = BAND_PREV_CHUNKS * CHUNK
REL_CLIP = 2 * CHUNK
HEAD_DIM_C = 64
SWA_PREV_CHUNKS = 2
WINDOW = SWA_PREV_CHUNKS * CHUNK
ROPE_THETA = 500000.0
ROPE_DIM = HEAD_DIM_C // 4

LANES = 128
V7X_VMEM_LIMIT_BYTES = 60 * 1024 * 1024

DENSE_TM = 1024
GATE_UP_TM = 2048
GATE_UP_TN = 256
DENSE_TN = 512

NEG = -1e30

SB_TILE = 256
SB_HEADS = 8
SB_STEP_KEYS = 128
BAND_TQ = 128
BAND_HEADS = 4
STEP_HEADS = 8
SWA_TQ = 128


def _params(*sem):
    return pltpu.CompilerParams(dimension_semantics=sem,
                                vmem_limit_bytes=V7X_VMEM_LIMIT_BYTES)


def _chunk_of(pos):
    return lax.shift_right_logical(pos, int(math.log2(CHUNK)))


def _pick_tm(m):
    return DENSE_TM if m % DENSE_TM == 0 else m


def _rmsnorm_kernel(x_ref, g_ref, o_ref):
    x = x_ref[...]
    ms = jnp.mean(x * x, axis=-1, keepdims=True)
    y = x * lax.rsqrt(ms + EPS)
    o_ref[...] = (y * g_ref[...]).astype(o_ref.dtype)


def _rmsnorm(x, g3, layer, out_dtype):
    m, d = x.shape
    tm = 256
    return pl.pallas_call(
        _rmsnorm_kernel,
        grid=(m // tm,),
        in_specs=[pl.BlockSpec((tm, d), lambda i: (i, 0)),
                  pl.BlockSpec((None, 1, d), lambda i: (layer, 0, 0))],
        out_specs=pl.BlockSpec((tm, d), lambda i: (i, 0)),
        out_shape=jax.ShapeDtypeStruct((m, d), out_dtype),
        compiler_params=_params("arbitrary"),
        name="rmsnorm",
    )(x, g3)


class _Weight:
    def __init__(self, src, layer, kblk=0, col0=0):
        self.src, self.layer, self.kblk, self.col0 = src, layer, kblk, col0

    def spec(self, tk, tn):
        layer, kb, jb = self.layer, self.kblk, self.col0 // tn
        return pl.BlockSpec((None, tk, tn), lambda i, j: (layer, kb, jb + j))


def _rope_tile(x, c, s1, s2):
    return (x * c + pltpu.roll(x, 8, 1) * s1
            + pltpu.roll(x, LANES - 8, 1) * s2)


def _rider_in(x):
    return pl.BlockSpec(x.shape, lambda i, j: (0, 0), pipeline_mode=pl.Buffered(1))


def _rider_out(rows, tn, ni):
    return pl.BlockSpec((rows, tn),
                        lambda i, j: (0, jnp.where(i == ni - 1, j, 0)))


def _on_last_row_block(fn):
    pl.when(pl.program_id(0) == pl.num_programs(0) - 1)(fn)


def _lane_chunks(n):
    return [slice(c * LANES, (c + 1) * LANES) for c in range(n // LANES)]


def _proj_kernel(*refs, rope, norm):
    w_ref = refs[1]
    n_tab = 3 if rope else 0
    a_ref, tabs = refs[0], refs[2:2 + n_tab]
    sa_ref, stabs = refs[2 + n_tab], refs[3 + n_tab:3 + 2 * n_tab]
    r_ref, sr_ref = refs[3 + 2 * n_tab:5 + 2 * n_tab] if norm else (None, None)
    o_ref, so_ref = refs[-2:]

    def run(a_ref, tabs, r_ref, o_ref):
        acc = jnp.dot(a_ref[...], w_ref[...].astype(BF16),
                      preferred_element_type=F32)
        for sl in _lane_chunks(acc.shape[1]):
            x = acc[:, sl]
            if norm:
                x = x * r_ref[...]
            if rope:
                x = _rope_tile(x, *(t[...] for t in tabs))
            o_ref[:, sl] = x.astype(o_ref.dtype)

    run(a_ref, tabs, r_ref, o_ref)
    _on_last_row_block(lambda: run(sa_ref, stabs, sr_ref, so_ref))


def _proj(a, a_s, w, ncols, out_dtype, rope=None, rope_s=None, norm=None):
    m, k = a.shape
    ms = a_s.shape[0]
    tm = _pick_tm(m)
    tn = min(DENSE_TN, ncols)
    ni = m // tm
    in_specs = [pl.BlockSpec((tm, k), lambda i, j: (i, 0)), w.spec(k, tn)]
    args = [a, w.src]
    if rope is not None:
        nrep = rope[0].shape[0] // tm
        for t in rope:
            in_specs.append(pl.BlockSpec((tm, LANES), lambda i, j: (i % nrep, 0)))
            args.append(t)
    for x in (a_s,) + tuple(rope_s or ()):
        in_specs.append(_rider_in(x))
        args.append(x)
    if norm is not None:
        r, r_s = norm
        in_specs += [pl.BlockSpec((tm, LANES), lambda i, j: (i, 0)), _rider_in(r_s)]
        args += [r, r_s]
    return pl.pallas_call(
        functools.partial(_proj_kernel, rope=rope is not None,
                          norm=norm is not None),
        grid=(ni, ncols // tn),
        in_specs=in_specs,
        out_specs=[pl.BlockSpec((tm, tn), lambda i, j: (i, j)),
                   _rider_out(ms, tn, ni)],
        out_shape=[jax.ShapeDtypeStruct((m, ncols), out_dtype),
                   jax.ShapeDtypeStruct((ms, ncols), out_dtype)],
        compiler_params=_params("arbitrary", "arbitrary"),
        name="proj",
    )(*args)


def _gate_up_kernel(*refs, norm, cast):
    a_ref, wg_ref, wu_ref = refs[:3]
    r_ref = refs[3] if norm else None
    o_ref = refs[-2] if cast else refs[-1]
    a = a_ref[...]
    g = jnp.dot(a, wg_ref[...].astype(BF16), preferred_element_type=F32)
    u = jnp.dot(a, wu_ref[...].astype(BF16), preferred_element_type=F32)
    for sl in _lane_chunks(g.shape[1]):
        gc, uc = g[:, sl], u[:, sl]
        if norm:
            gc, uc = gc * r_ref[...], uc * r_ref[...]
        o_ref[:, sl] = (gc * jax.nn.sigmoid(gc) * uc).astype(o_ref.dtype)
    if cast:
        refs[-1][...] = refs[-3][...].astype(BF16)


def _gate_up(a, wg, wu, f, norm=None, cast=None):
    m, k = a.shape
    tm = GATE_UP_TM if m % GATE_UP_TM == 0 else m
    tn = GATE_UP_TN
    ni, nj = m // tm, f // tn
    in_specs = [pl.BlockSpec((tm, k), lambda i, j: (i, 0)),
                wg.spec(k, tn), wu.spec(k, tn)]
    args = [a, wg.src, wu.src]
    out_specs = [pl.BlockSpec((tm, tn), lambda i, j: (i, j))]
    out_shape = [jax.ShapeDtypeStruct((m, f), BF16)]
    if norm is not None:
        in_specs.append(pl.BlockSpec((tm, LANES), lambda i, j: (i, 0),
                                     pipeline_mode=pl.Buffered(1)))
        args.append(norm)
    if cast is not None:
        w3, layer = cast
        rows, n = w3.shape[1] // (ni * nj), w3.shape[2]
        assert rows * ni * nj == w3.shape[1] and rows % 16 == 0
        in_specs.append(pl.BlockSpec((None, rows, n),
                                     lambda i, j: (layer, i * nj + j, 0)))
        args.append(w3)
        out_specs.append(pl.BlockSpec((rows, n), lambda i, j: (i * nj + j, 0)))
        out_shape.append(jax.ShapeDtypeStruct(w3.shape[1:], BF16))
    out = pl.pallas_call(
        functools.partial(_gate_up_kernel, norm=norm is not None,
                          cast=cast is not None),
        grid=(ni, nj),
        in_specs=in_specs,
        out_specs=out_specs,
        out_shape=out_shape,
        compiler_params=_params("arbitrary", "arbitrary"),
        name="gate_up",
    )(*args)
    return out if cast is not None else out[0]


def _mm_res_kernel(*refs, nterms, stats):
    n = nterms
    w_refs = refs[n:2 * n]
    gain_ref = refs[3 * n + 2] if stats else None
    outs = refs[3 * n + 2 + bool(stats):]

    def run(a_refs, res_ref, o_ref, b_ref, ss_ref):
        acc = res_ref[...]
        for a_ref, w_ref in zip(a_refs, w_refs):
            acc = acc + jnp.dot(a_ref[...], w_ref[...].astype(BF16),
                                preferred_element_type=F32)
        o_ref[...] = acc
        if stats:
            _row_stats(acc, gain_ref, b_ref, ss_ref)

    p_extra = (outs[2], outs[3]) if stats else (None, None)
    s_extra = (outs[4], outs[5]) if stats else (None, None)
    run(refs[:n], refs[2 * n], outs[0], *p_extra)
    _on_last_row_block(
        lambda: run(refs[2 * n + 1:3 * n + 1], refs[3 * n + 1], outs[1], *s_extra))


def _mm_res(terms, tk, res, res_s, tn, gain=None):
    stats = gain is not None
    m, n = res.shape
    ms = res_s.shape[0]
    tm = _pick_tm(m)
    ni = m // tm
    in_specs, args = [], []
    for a, _, ka, _ in terms:
        in_specs.append(pl.BlockSpec((tm, tk), lambda i, j, ka=ka: (i, ka)))
        args.append(a)
    for _, _, _, w in terms:
        in_specs.append(w.spec(tk, tn))
        args.append(w.src)
    tile = pl.BlockSpec((tm, tn), lambda i, j: (i, j))
    tile_s = _rider_out(ms, tn, ni)
    in_specs.append(tile)
    args.append(res)
    for _, a_s, ka, _ in terms:
        in_specs.append(pl.BlockSpec((ms, tk), lambda i, j, ka=ka: (0, ka),
                                     pipeline_mode=pl.Buffered(1)))
        args.append(a_s)
    in_specs.append(tile_s)
    args.append(res_s)
    out_specs = [tile, tile_s]
    out_shape = [jax.ShapeDtypeStruct((m, n), F32),
                 jax.ShapeDtypeStruct((ms, n), F32)]
    if stats:
        in_specs.append(pl.BlockSpec((1, tn), lambda i, j: (0, j)))
        args.append(gain.astype(F32).reshape(1, n))
        out_specs += [tile, pl.BlockSpec((tm, LANES), lambda i, j: (i, 0)),
                      tile_s, pl.BlockSpec((ms, LANES), lambda i, j: (0, 0))]
        out_shape += [jax.ShapeDtypeStruct((m, n), BF16),
                      jax.ShapeDtypeStruct((m, LANES), F32),
                      jax.ShapeDtypeStruct((ms, n), BF16),
                      jax.ShapeDtypeStruct((ms, LANES), F32)]
    return pl.pallas_call(
        functools.partial(_mm_res_kernel, nterms=len(terms), stats=stats),
        grid=(ni, n // tn),
        in_specs=in_specs,
        out_specs=out_specs,
        out_shape=out_shape,
        compiler_params=_params("arbitrary", "arbitrary"),
        name="mm_res",
    )(*args)


def _row_stats(acc, gain_ref, b_ref, ss_ref):
    b_ref[...] = (acc * gain_ref[...]).astype(BF16)
    part = sum(acc[:, sl] * acc[:, sl] for sl in _lane_chunks(acc.shape[1]))
    j = pl.program_id(1)

    @pl.when(j == 0)
    def _():
        ss_ref[...] = part

    @pl.when(j != 0)
    def _():
        ss_ref[...] += part


def _down_kernel(*refs, stats):
    a_refs, w_ref, res_ref = refs[0:2], refs[2], refs[3]
    sa_refs, sres_ref = refs[4:6], refs[6]
    gain_ref = refs[7] if stats else None
    outs = refs[7 + bool(stats):]
    k = pl.program_id(2)

    def run(x_refs, r_ref, o_ref, b_ref, ss_ref):
        @pl.when(k == 0)
        def _():
            o_ref[...] = r_ref[...] + jnp.dot(
                x_refs[0][...], w_ref[...].astype(BF16),
                preferred_element_type=F32)

        @pl.when(k == 1)
        def _():
            acc = o_ref[...] + jnp.dot(
                x_refs[1][...], w_ref[...].astype(BF16),
                preferred_element_type=F32)
            o_ref[...] = acc
            if stats:
                _row_stats(acc, gain_ref, b_ref, ss_ref)

    p_extra = (outs[2], outs[3]) if stats else (None, None)
    s_extra = (outs[4], outs[5]) if stats else (None, None)
    run(a_refs, res_ref, outs[0], *p_extra)
    _on_last_row_block(lambda: run(sa_refs, sres_ref, outs[1], *s_extra))


def _down(act, act_s, w, res, res_s, gain=None):
    stats = gain is not None
    m, n = res.shape
    ms = res_s.shape[0]
    tk = act.shape[1] // 2
    tm, tn = _pick_tm(m), DENSE_TN
    ni = m // tm
    one = dict(pipeline_mode=pl.Buffered(1))
    last = ni - 1
    tile = pl.BlockSpec((tm, tn), lambda i, j, k: (i, j))
    tile_s = pl.BlockSpec((ms, tn),
                          lambda i, j, k: (0, jnp.where(i == last, j, 0)))
    in_specs = (
        [pl.BlockSpec((tm, tk), lambda i, j, k, h=h: (i, h), **one) for h in (0, 1)]
        + [pl.BlockSpec((tk, tn), lambda i, j, k: (k, j)), tile]
        + [pl.BlockSpec((ms, tk), lambda i, j, k, h=h: (0, h), **one) for h in (0, 1)]
        + [tile_s])
    args = [act, act, w, res, act_s, act_s, res_s]
    out_specs = [tile, tile_s]
    out_shape = [jax.ShapeDtypeStruct((m, n), F32),
                 jax.ShapeDtypeStruct((ms, n), F32)]
    if stats:
        in_specs.append(pl.BlockSpec((1, tn), lambda i, j, k: (0, j)))
        args.append(gain.astype(F32).reshape(1, n))
        out_specs += [tile, pl.BlockSpec((tm, LANES), lambda i, j, k: (i, 0)),
                      tile_s, pl.BlockSpec((ms, LANES), lambda i, j, k: (0, 0))]
        out_shape += [jax.ShapeDtypeStruct((m, n), BF16),
                      jax.ShapeDtypeStruct((m, LANES), F32),
                      jax.ShapeDtypeStruct((ms, n), BF16),
                      jax.ShapeDtypeStruct((ms, LANES), F32)]
    return pl.pallas_call(
        functools.partial(_down_kernel, stats=stats),
        grid=(ni, n // tn, 2),
        in_specs=in_specs,
        out_specs=out_specs,
        out_shape=out_shape,
        compiler_params=_params("arbitrary", "arbitrary", "arbitrary"),
        name="down",
    )(*args)


def _row_scale_kernel(ss_ref, o_ref, *, d):
    tot = jnp.sum(ss_ref[...], axis=1, keepdims=True)
    o_ref[...] = jnp.broadcast_to(lax.rsqrt(tot / d + EPS), o_ref.shape)


def _row_scale(ss, d):
    m = ss.shape[0]
    tm = _pick_tm(m)
    blk = pl.BlockSpec((tm, LANES), lambda i: (i, 0))
    return pl.pallas_call(
        functools.partial(_row_scale_kernel, d=d),
        grid=(m // tm,),
        in_specs=[blk],
        out_specs=blk,
        out_shape=jax.ShapeDtypeStruct(ss.shape, F32),
        compiler_params=_params("arbitrary"),
        name="row_scale",
    )(ss)


_DN_T = (((1,), (1,)), ((), ()))


def _head(h, d=HEAD_DIM_AB):
    return slice(h * d, (h + 1) * d)


def _heads_major(x):
    return pltpu.einshape("phd->hpd", x)


def _softmax_stage(ss, sinks=None):
    ms = [jnp.max(s, axis=1, keepdims=True) for s in ss]
    if sinks is not None:
        ms = [jnp.maximum(m, sk) for m, sk in zip(ms, sinks)]
    ps = [jnp.exp(s - m) for s, m in zip(ss, ms)]
    ls = [jnp.sum(p, axis=1, keepdims=True) for p in ps]
    if sinks is not None:
        ls = [l + jnp.exp(sk - m) for l, sk, m in zip(ls, sinks, ms)]
    return [p.astype(BF16) for p in ps], ls


def _sb_logs(z, valid):
    lb = jnp.minimum(z, 0.0) - jnp.log(1.0 + jnp.exp(-jnp.abs(z)))
    l1m = lb - z
    if valid is not None:
        l1m = jnp.where(valid, l1m, 0.0)
    return lb, l1m


def _split_hi_lo(x):
    hi = x.astype(BF16)
    lo = (x - hi.astype(F32)).astype(BF16)
    return jnp.concatenate([hi, lo], axis=1)


def _suffix_matrix(n):
    u = np.tril(np.ones((n, n), np.float32), -1)
    return jnp.asarray(np.concatenate([u, u], axis=0), BF16)


def _sb_prompt_kernel(q_ref, k_ref, v_ref, uu_ref, o_ref, kc_ref, vc_ref,
                      kb_ref, vb_ref, oacc_ref, cacc_ref, *, scale):
    qi = pl.program_id(2)
    tq = q_ref.shape[0]
    nh = oacc_ref.shape[0]

    @pl.when(qi == 0)
    def _():
        kb_ref[...] = k_ref[...].astype(BF16)
        vb_ref[...] = v_ref[...].astype(BF16)

    rows = pl.ds(pl.multiple_of(qi * tq, tq), tq)
    kc_ref[...] = k_ref[rows, :].reshape(kc_ref.shape)
    vc_ref[...] = v_ref[rows, :].reshape(vc_ref.shape)

    oacc_ref[...] = jnp.zeros(oacc_ref.shape, F32)
    cacc_ref[...] = jnp.zeros(cacc_ref.shape, F32)

    def tile(j, masked):
        start = pl.multiple_of(j * tq, tq)
        valid = None
        if masked:
            row = lax.broadcasted_iota(jnp.int32, (tq, tq), 0)
            col = lax.broadcasted_iota(jnp.int32, (tq, tq), 1)
            valid = col < row
        zs = [lax.dot_general(q_ref[:, _head(h)],
                              kb_ref[pl.ds(start, tq), _head(h)], _DN_T,
                              preferred_element_type=F32) * scale
              for h in range(nh)]
        logs = [_sb_logs(z, valid) for z in zs]
        sufs = [jnp.dot(_split_hi_lo(l1m), uu_ref[...],
                        preferred_element_type=F32) for _, l1m in logs]
        ws = [jnp.exp(lb + suf + cacc_ref[h])
              for h, ((lb, _), suf) in enumerate(zip(logs, sufs))]
        if masked:
            ws = [jnp.where(valid, w, 0.0) for w in ws]
        for h in range(nh):
            oacc_ref[h] += jnp.dot(ws[h].astype(BF16),
                                   vb_ref[pl.ds(start, tq), _head(h)],
                                   preferred_element_type=F32)
            cacc_ref[h] += jnp.sum(logs[h][1], axis=1, keepdims=True)

    tile(qi, True)

    def body(t, c):
        tile(qi - 1 - t, False)
        return c

    lax.fori_loop(0, qi, body, 0)
    for h in range(nh):
        o_ref[:, _head(h)] = oacc_ref[h].astype(o_ref.dtype)


def _sb_prompt(q, k, v, batch, seq):
    m, w = q.shape
    d = HEAD_DIM_AB
    nh = SB_HEADS
    tq = SB_TILE
    nq = seq // tq
    qo = pl.BlockSpec((tq, nh * d), lambda b, g, i: (b * nq + i, g))
    kv = pl.BlockSpec((seq, nh * d), lambda b, g, i: (b, g))
    cache = pl.BlockSpec((None, tq, nh, d), lambda b, g, i: (b, i, g, 0))
    cache_shape = jax.ShapeDtypeStruct((batch, seq, w // d, d), F32)
    return pl.pallas_call(
        functools.partial(_sb_prompt_kernel, scale=d ** -0.5),
        grid=(batch, w // (nh * d), nq),
        in_specs=[qo, kv, kv, pl.BlockSpec((2 * tq, tq), lambda b, g, i: (0, 0))],
        out_specs=[qo, cache, cache],
        out_shape=[jax.ShapeDtypeStruct((m, w), BF16), cache_shape, cache_shape],
        scratch_shapes=[pltpu.VMEM((seq, nh * d), BF16),
                        pltpu.VMEM((seq, nh * d), BF16),
                        pltpu.VMEM((nh, tq, d), F32),
                        pltpu.VMEM((nh, tq, 1), F32)],
        compiler_params=_params("arbitrary", "arbitrary", "arbitrary"),
        name="sb_prompt",
    )(q, k, v, _suffix_matrix(tq))


def _sb_step_kernel(q_ref, kn_ref, vn_ref, kc_ref, vc_ref, uu_ref, o_ref,
                    knc_ref, vnc_ref, kb_ref, vb_ref, *, scale):
    t = q_ref.shape[0]
    past, nh, d = kc_ref.shape
    tk = uu_ref.shape[1]
    n = past + tk
    knc_ref[...] = kn_ref[...].reshape(knc_ref.shape)
    vnc_ref[...] = vn_ref[...].reshape(vnc_ref.shape)
    kb_ref[:, 0:past, :] = _heads_major(kc_ref[...]).astype(BF16)
    vb_ref[:, 0:past, :] = _heads_major(vc_ref[...]).astype(BF16)
    for h in range(nh):
        kb_ref[h, past:past + t] = kn_ref[:, _head(h)].astype(BF16)
        vb_ref[h, past:past + t] = vn_ref[:, _head(h)].astype(BF16)
        kb_ref[h, past + t:n] = jnp.zeros((tk - t, d), BF16)
        vb_ref[h, past + t:n] = jnp.zeros((tk - t, d), BF16)

    z = jnp.concatenate(
        [lax.dot_general(q_ref[:, _head(h)], kb_ref[h], _DN_T,
                         preferred_element_type=F32) for h in range(nh)],
        axis=0) * scale
    row = lax.broadcasted_iota(jnp.int32, z.shape, 0) & (t - 1)
    col = lax.broadcasted_iota(jnp.int32, z.shape, 1)
    valid = col < past + row
    lb, l1m = _sb_logs(z, valid)
    ntiles = n // tk
    sufs = [jnp.dot(_split_hi_lo(l1m[:, j * tk:(j + 1) * tk]), uu_ref[...],
                    preferred_element_type=F32) for j in range(ntiles)]
    tots = [jnp.sum(l1m[:, j * tk:(j + 1) * tk], axis=1, keepdims=True)
            for j in range(ntiles)]
    carries = [None] * ntiles
    carry = jnp.zeros_like(tots[0])
    for j in reversed(range(ntiles)):
        carries[j] = carry
        carry = carry + tots[j]
    after = jnp.concatenate([sufs[j] + carries[j] for j in range(ntiles)],
                            axis=1)
    w = jnp.where(valid, jnp.exp(lb + after), 0.0).astype(BF16)
    for h in range(nh):
        o = jnp.dot(w[h * t:(h + 1) * t], vb_ref[h], preferred_element_type=F32)
        o_ref[:, _head(h)] = o.astype(o_ref.dtype)


def _sb_step(q, k_new, v_new, k_cache, v_cache, batch):
    m, w = q.shape
    _, past, heads, d = k_cache.shape
    t = m // batch
    tk = SB_STEP_KEYS
    nh = STEP_HEADS
    assert past % tk == 0 and t <= tk and t & (t - 1) == 0
    row = pl.BlockSpec((t, nh * d), lambda b, g: (b, g))
    cache = pl.BlockSpec((None, past, nh, d), lambda b, g: (b, 0, g, 0))
    new = pl.BlockSpec((None, t, nh, d), lambda b, g: (b, 0, g, 0))
    new_shape = jax.ShapeDtypeStruct((batch, t, heads, d), F32)
    return pl.pallas_call(
        functools.partial(_sb_step_kernel, scale=d ** -0.5),
        grid=(batch, heads // nh),
        in_specs=[row, row, row, cache, cache,
                  pl.BlockSpec((2 * tk, tk), lambda b, g: (0, 0))],
        out_specs=[row, new, new],
        out_shape=[jax.ShapeDtypeStruct((m, w), BF16), new_shape, new_shape],
        scratch_shapes=[pltpu.VMEM((nh, past + tk, d), BF16),
                        pltpu.VMEM((nh, past + tk, d), BF16)],
        compiler_params=_params("arbitrary", "arbitrary"),
        name="sb_step",
    )(q, k_new, v_new, k_cache, v_cache, _suffix_matrix(tk))


def _band_bias_kernel(r_ref, o_ref, *, n_valid):
    tq, w = o_ref.shape
    wr = r_ref.shape[1]
    t = jnp.broadcast_to(r_ref[...], (tq, wr))
    t = pltpu.roll(t, 0, 1, stride=1, stride_axis=0)
    row = lax.broadcasted_iota(jnp.int32, (tq, w), 0)
    col = lax.broadcasted_iota(jnp.int32, (tq, w), 1)
    qc = _chunk_of(row)
    kc = _chunk_of(col)
    valid = (kc >= qc) & (kc <= qc + BAND_PREV_CHUNKS) & (col < n_valid)
    o_ref[...] = jnp.where(valid, t[:, :w], NEG)


def _band_bias(rel, tq, n_valid):
    heads = rel.shape[0]
    w = BAND_HIST + BAND_TQ
    wr = w + LANES
    p = np.arange(wr)
    delta = np.where(p < w, p, p - wr)
    idx = np.clip(BAND_HIST - delta, -REL_CLIP, REL_CLIP) + REL_CLIP
    r = jnp.take(rel.astype(F32), jnp.asarray(idx, jnp.int32), axis=1)
    r = r.reshape(heads, 1, wr)
    return pl.pallas_call(
        functools.partial(_band_bias_kernel, n_valid=n_valid),
        grid=(heads,),
        in_specs=[pl.BlockSpec((None, 1, wr), lambda h: (h, 0, 0))],
        out_specs=pl.BlockSpec((None, tq, w), lambda h: (h, 0, 0)),
        out_shape=jax.ShapeDtypeStruct((heads, tq, w), F32),
        compiler_params=_params("arbitrary"),
        name="band_bias",
    )(r)


def _band_prompt_kernel(q_ref, k_ref, v_ref, bias_ref, o_ref, kp_ref, vp_ref,
                        *, scale, tq):
    seq, wd = k_ref.shape
    d = HEAD_DIM_AB
    nh = wd // d
    w = bias_ref.shape[1]
    hist = w - tq
    kp_ref[0:hist] = jnp.zeros((hist, wd), BF16)
    vp_ref[0:hist] = jnp.zeros((hist, wd), BF16)
    kp_ref[hist:hist + seq] = k_ref[...].astype(BF16)
    vp_ref[hist:hist + seq] = v_ref[...].astype(BF16)
    col = lax.broadcasted_iota(jnp.int32, (tq, w), 1)

    def tile(i, at_start):
        start = pl.multiple_of(i * tq, tq)
        ss = []
        for h in range(nh):
            s = lax.dot_general(q_ref[pl.ds(start, tq), _head(h)],
                                kp_ref[pl.ds(start, w), _head(h)], _DN_T,
                                preferred_element_type=F32)
            s = s * scale + bias_ref[h * tq:(h + 1) * tq, :]
            if at_start:
                s = jnp.where(col >= hist - start, s, NEG)
            ss.append(s)
        ps, ls = _softmax_stage(ss)
        for h in range(nh):
            o = jnp.dot(ps[h], vp_ref[pl.ds(start, w), _head(h)],
                        preferred_element_type=F32) / ls[h]
            o_ref[pl.ds(start, tq), _head(h)] = o.astype(o_ref.dtype)

    def loop(lo, hi, at_start):
        def body(i, c):
            tile(i, at_start)
            return c
        lax.fori_loop(lo, hi, body, 0)

    n_tiles = seq // tq
    n_start = min(hist // tq, n_tiles)
    loop(0, n_start, True)
    loop(n_start, n_tiles, False)


def _band_prompt(q, k, v, bias, batch, seq):
    m, wd = q.shape
    d = HEAD_DIM_AB
    nh = BAND_HEADS
    heads, tq, w = bias.shape
    bias = bias.reshape(heads // nh, nh * tq, w)
    blk = pl.BlockSpec((seq, nh * d), lambda b, g: (b, g))
    return pl.pallas_call(
        functools.partial(_band_prompt_kernel, scale=d ** -0.5, tq=tq),
        grid=(batch, heads // nh),
        in_specs=[blk, blk, blk,
                  pl.BlockSpec((None, nh * tq, w), lambda b, g: (g, 0, 0))],
        out_specs=blk,
        out_shape=jax.ShapeDtypeStruct((m, wd), BF16),
        scratch_shapes=[pltpu.VMEM((BAND_HIST + seq, nh * d), BF16),
                        pltpu.VMEM((BAND_HIST + seq, nh * d), BF16)],
        compiler_params=_params("arbitrary", "arbitrary"),
        name="band_prompt",
    )(q, k, v, bias)


def _cache_tail_kernel(x_ref, o_ref):
    o_ref[...] = x_ref[...].reshape(o_ref.shape)


def _cache_tail(x, batch, seq, rows):
    d = HEAD_DIM_AB
    nh = STEP_HEADS
    heads = x.shape[1] // d
    assert seq % rows == 0
    last = seq // rows - 1
    return pl.pallas_call(
        _cache_tail_kernel,
        grid=(batch, heads // nh),
        in_specs=[pl.BlockSpec((rows, nh * d),
                               lambda b, g: (b * (last + 1) + last, g))],
        out_specs=pl.BlockSpec((None, rows, nh, d), lambda b, g: (b, 0, g, 0)),
        out_shape=jax.ShapeDtypeStruct((batch, rows, heads, d), x.dtype),
        compiler_params=_params("arbitrary", "arbitrary"),
        name="cache_tail",
    )(x)


def _roll_cache(new_ref, cache, new_rows_ref, t):
    past, nh, d = cache.shape
    new_ref[0:past - t] = cache[t:]
    for h in range(nh):
        new_ref[past - t:past, h, :] = new_rows_ref[:, _head(h, d)]


def _band_step_kernel(q_ref, kn_ref, vn_ref, kc_ref, vc_ref, bias_ref, o_ref,
                      nk_ref, nv_ref, kb_ref, vb_ref, *, scale):
    t = q_ref.shape[0]
    past, nh, d = kc_ref.shape
    w = bias_ref.shape[1]
    kc, vc = kc_ref[...], vc_ref[...]
    _roll_cache(nk_ref, kc, kn_ref, t)
    _roll_cache(nv_ref, vc, vn_ref, t)
    kb_ref[:, 0:past, :] = _heads_major(kc).astype(BF16)
    vb_ref[:, 0:past, :] = _heads_major(vc).astype(BF16)
    for h in range(nh):
        kb_ref[h, past:past + t] = kn_ref[:, _head(h)].astype(BF16)
        vb_ref[h, past:past + t] = vn_ref[:, _head(h)].astype(BF16)
        kb_ref[h, past + t:w] = jnp.zeros((w - past - t, d), BF16)
        vb_ref[h, past + t:w] = jnp.zeros((w - past - t, d), BF16)
    ss = [lax.dot_general(q_ref[:, _head(h)], kb_ref[h], _DN_T,
                          preferred_element_type=F32) * scale
          + bias_ref[h * t:(h + 1) * t, :] for h in range(nh)]
    ps, ls = _softmax_stage(ss)
    for h in range(nh):
        o = jnp.dot(ps[h], vb_ref[h], preferred_element_type=F32) / ls[h]
        o_ref[:, _head(h)] = o.astype(o_ref.dtype)


def _band_step(q, k_new, v_new, k_cache, v_cache, bias, batch):
    m, wd = q.shape
    _, past, heads, d = k_cache.shape
    t = m // batch
    nh = STEP_HEADS
    w = bias.shape[2]
    assert past == BAND_HIST and past + t <= w
    bias = bias.reshape(heads // nh, nh * t, w)
    row = pl.BlockSpec((t, nh * d), lambda b, g: (b, g))
    cache = pl.BlockSpec((None, past, nh, d), lambda b, g: (b, 0, g, 0))
    return pl.pallas_call(
        functools.partial(_band_step_kernel, scale=d ** -0.5),
        grid=(batch, heads // nh),
        in_specs=[row, row, row, cache, cache,
                  pl.BlockSpec((None, nh * t, w), lambda b, g: (g, 0, 0))],
        out_specs=[row, cache, cache],
        out_shape=[jax.ShapeDtypeStruct((m, wd), BF16),
                   jax.ShapeDtypeStruct(k_cache.shape, F32),
                   jax.ShapeDtypeStruct(v_cache.shape, F32)],
        scratch_shapes=[pltpu.VMEM((nh, w, d), BF16),
                        pltpu.VMEM((nh, w, d), BF16)],
        compiler_params=_params("arbitrary", "arbitrary"),
        name="band_step",
    )(q, k_new, v_new, k_cache, v_cache, bias)


def _split_pair(blk):
    rolled = pltpu.roll(blk, HEAD_DIM_C, 1)
    lane = lax.broadcasted_iota(jnp.int32, blk.shape, 1)
    low = lane < HEAD_DIM_C
    zero = jnp.zeros_like(blk)
    return [jnp.where(low, blk, zero), jnp.where(low, zero, rolled),
            jnp.where(low, rolled, zero), jnp.where(low, zero, blk)]


def _swa_stage(qs, kls, khs, vls, vhs, valid, sinks, scale):
    ss = []
    for q, kl, kh in zip(qs, kls, khs):
        for kx in (kl, kh):
            s = lax.dot_general(q, kx, _DN_T, preferred_element_type=F32)
            ss.append(jnp.where(valid, s * scale, NEG))
    ps, ls = _softmax_stage(ss, sinks)
    return [jnp.dot(ps[2 * i], vls[i], preferred_element_type=F32) / ls[2 * i]
            + jnp.dot(ps[2 * i + 1], vhs[i], preferred_element_type=F32)
            / ls[2 * i + 1] for i in range(len(qs))]


def _swa_prompt_kernel(sinks_ref, q_ref, k_ref, v_ref, o_ref, ks_ref, vs_ref,
                       *, scale, tq):
    hp = pl.program_id(1)
    seq = k_ref.shape[0]
    band = WINDOW + tq
    npair = q_ref.shape[1] // LANES
    group = npair // 2

    for src, dst in ((k_ref, ks_ref), (v_ref, vs_ref)):
        for n, x in enumerate(_split_pair(src[...])):
            dst[n, 0:WINDOW] = jnp.zeros((WINDOW, LANES), BF16)
            dst[n, WINDOW:WINDOW + seq] = x.astype(BF16)

    row = lax.broadcasted_iota(jnp.int32, (tq, band), 0)
    col = lax.broadcasted_iota(jnp.int32, (tq, band), 1)
    qc = _chunk_of(row)
    kc = _chunk_of(col)
    in_band = (kc >= qc) & (kc <= qc + SWA_PREV_CHUNKS)
    sinks = [sinks_ref[hp * 2 * npair + n] for n in range(2 * npair)]

    def body(i, c):
        start = pl.multiple_of(i * tq, tq)
        valid = in_band & (col >= WINDOW - start)
        kv = [[r[n, pl.ds(start, band), :] for n in range(4)]
              for r in (ks_ref, vs_ref)]
        qs = [q_ref[pl.ds(start, tq), _head(p, LANES)] for p in range(npair)]
        sel = [2 * (p // group) for p in range(npair)]
        outs = _swa_stage(qs, [kv[0][s] for s in sel], [kv[0][s + 1] for s in sel],
                          [kv[1][s] for s in sel], [kv[1][s + 1] for s in sel],
                          valid, sinks, scale)
        for p in range(npair):
            o_ref[pl.ds(start, tq), _head(p, LANES)] = outs[p].astype(o_ref.dtype)
        return c

    lax.fori_loop(0, seq // tq, body, 0)


def _swa_prompt(q, k, v, sinks, batch, seq):
    m, wq = q.shape
    wk = k.shape[1]
    npairs = wk // LANES
    qw = wq // npairs
    qo = pl.BlockSpec((seq, qw), lambda b, h, s: (b, h))
    kv = pl.BlockSpec((seq, LANES), lambda b, h, s: (b, h))
    grid_spec = pltpu.PrefetchScalarGridSpec(
        num_scalar_prefetch=1,
        grid=(batch, npairs),
        in_specs=[qo, kv, kv],
        out_specs=qo,
        scratch_shapes=[pltpu.VMEM((4, WINDOW + seq, LANES), BF16),
                        pltpu.VMEM((4, WINDOW + seq, LANES), BF16)],
    )
    return pl.pallas_call(
        functools.partial(_swa_prompt_kernel, scale=HEAD_DIM_C ** -0.5,
                          tq=SWA_TQ),
        grid_spec=grid_spec,
        out_shape=jax.ShapeDtypeStruct((m, wq), BF16),
        compiler_params=_params("arbitrary", "arbitrary"),
        name="swa_prompt",
    )(sinks, q, k, v)


def _swa_step_kernel(sinks_ref, q_ref, kn_ref, vn_ref, kc_ref, vc_ref, o_ref,
                     nk_ref, nv_ref, *, scale):
    t = q_ref.shape[0]
    past = kc_ref.shape[0]
    wk = kn_ref.shape[1]
    band = 2 * LANES
    nkv = wk // LANES
    npair = q_ref.shape[1] // LANES
    group = npair // (2 * nkv)
    kc, vc = kc_ref[...].reshape(past, wk), vc_ref[...].reshape(past, wk)
    kn, vn = kn_ref[...], vn_ref[...]
    nk_ref[...] = jnp.concatenate([kc[t:], kn], axis=0).reshape(nk_ref.shape)
    nv_ref[...] = jnp.concatenate([vc[t:], vn], axis=0).reshape(nv_ref.shape)
    pad = jnp.zeros((band - past - t, wk), F32)
    k_all = jnp.concatenate([kc, kn, pad], axis=0)
    v_all = jnp.concatenate([vc, vn, pad], axis=0)
    col = lax.broadcasted_iota(jnp.int32, (t, band), 1)
    valid = col < past + t
    kx = [[x.astype(BF16) for x in _split_pair(k_all[:, _head(g, LANES)])]
          for g in range(nkv)]
    vx = [[x.astype(BF16) for x in _split_pair(v_all[:, _head(g, LANES)])]
          for g in range(nkv)]
    qs = [q_ref[:, _head(p, LANES)] for p in range(npair)]
    where = [(p // (2 * group), 2 * ((p // group) % 2)) for p in range(npair)]
    sinks = [sinks_ref[n] for n in range(2 * npair)]
    outs = _swa_stage(qs, [kx[g][s] for g, s in where],
                      [kx[g][s + 1] for g, s in where],
                      [vx[g][s] for g, s in where],
                      [vx[g][s + 1] for g, s in where], valid, sinks, scale)
    for p in range(npair):
        o_ref[:, _head(p, LANES)] = outs[p].astype(o_ref.dtype)


def _swa_step(q, k_new, v_new, k_cache, v_cache, sinks, batch):
    m, wq = q.shape
    wk = k_new.shape[1]
    t = m // batch
    past = k_cache.shape[1]
    assert past + t <= 2 * LANES
    qrow = pl.BlockSpec((t, wq), lambda b, s: (b, 0))
    krow = pl.BlockSpec((t, wk), lambda b, s: (b, 0))
    cache = pl.BlockSpec((None,) + k_cache.shape[1:], lambda b, s: (b, 0, 0, 0))
    grid_spec = pltpu.PrefetchScalarGridSpec(
        num_scalar_prefetch=1,
        grid=(batch,),
        in_specs=[qrow, krow, krow, cache, cache],
        out_specs=[qrow, cache, cache],
    )
    return pl.pallas_call(
        functools.partial(_swa_step_kernel, scale=HEAD_DIM_C ** -0.5),
        grid_spec=grid_spec,
        out_shape=[jax.ShapeDtypeStruct((m, wq), BF16),
                   jax.ShapeDtypeStruct(k_cache.shape, F32),
                   jax.ShapeDtypeStruct(v_cache.shape, F32)],
        compiler_params=_params("arbitrary"),
        name="swa_step",
    )(sinks, q, k_new, v_new, k_cache, v_cache)


def _rope_tables(pos):
    half = ROPE_DIM // 2
    inv = jnp.power(ROPE_THETA, -jnp.arange(half, dtype=F32) / half)
    ang = pos.astype(F32)[:, None] * inv[None, :]
    cos, sin = jnp.cos(ang), jnp.sin(ang)
    n = pos.shape[0]
    reps = LANES // HEAD_DIM_C
    rest = HEAD_DIM_C - ROPE_DIM
    ones = jnp.ones((n, rest), F32)
    zeros = jnp.zeros((n, rest), F32)
    zh = jnp.zeros((n, half), F32)
    c = jnp.tile(jnp.concatenate([cos, cos, ones], axis=1), (1, reps))
    s1 = jnp.tile(jnp.concatenate([zh, sin, zeros], axis=1), (1, reps))
    s2 = jnp.tile(jnp.concatenate([-sin, zh, zeros], axis=1), (1, reps))
    return c, s1, s2


def _ffn(hp, hs, pre, w_gate, w_up, w_down, layer, next_gain):
    f = w_gate.shape[-1]
    d = hp.shape[1]
    wg, wu = _Weight(w_gate, layer), _Weight(w_up, layer)
    a_p, ss_p, a_s, ss_s = pre
    act_p, wd = _gate_up(a_p, wg, wu, f, norm=_row_scale(ss_p, d),
                         cast=(w_down, layer))
    act_s = _gate_up(a_s, wg, wu, f, norm=_row_scale(ss_s, d))
    return _down(act_p, act_s, wd, hp, hs, gain=next_gain)


def kernel(x_prompt, x_sample, cache_sb_k, cache_sb_v, cache_band_k, cache_band_v, cache_swa_k, cache_swa_v, norm_mix, norm_ffn, norm_final, w_in_ab, w_out_ab, rel_bias, w_qkv_c, w_out_c, sinks, w_gate, w_up, w_down):
    bp, seq, dm = x_prompt.shape
    bs, t, _ = x_sample.shape
    depth = norm_mix.shape[0]
    past = cache_sb_k.shape[2]
    hp = x_prompt.reshape(bp * seq, dm)
    hs = x_sample.reshape(bs * t, dm)
    norm_mix3 = norm_mix.reshape(depth, 1, dm)
    norm_final3 = norm_final.reshape(1, 1, dm)

    w_sb = cache_sb_k.shape[3] * cache_sb_k.shape[4]
    w_band = cache_band_k.shape[3] * cache_band_k.shape[4]
    wq_c = w_out_c.shape[1]
    wk_c = cache_swa_k.shape[3] * cache_swa_k.shape[4]
    p_band = cache_band_k.shape[2]

    rope_p = _rope_tables(jnp.arange(seq))
    rope_s = tuple(jnp.tile(x, (bs, 1)) for x in _rope_tables(past + jnp.arange(t)))

    outs = {name: [] for name in (
        "sbk_p", "sbv_p", "bk_p", "bv_p", "ck_p", "cv_p",
        "sbk_s", "sbv_s", "bk_s", "bv_s", "ck_s", "cv_s")}

    pre = None
    for l in range(depth):
        if pre is None:
            a_p = _rmsnorm(hp, norm_mix3, l, BF16)
            a_s = _rmsnorm(hs, norm_mix3, l, BF16)
            norm = None
        else:
            a_p, ss_p, a_s, ss_s = pre
            norm = (_row_scale(ss_p, dm), _row_scale(ss_s, dm))
        if l % 2 == 0:
            e = l // 2
            cols = (0, w_sb, 2 * w_sb, 3 * w_sb, 3 * w_sb + w_band, 3 * w_sb + 2 * w_band)
            bias_p = _band_bias(rel_bias[e], BAND_TQ, BAND_HIST + BAND_TQ)
            bias_s = _band_bias(rel_bias[e], t, p_band + t)

            widths = (w_sb, w_sb, w_sb, w_band, w_band, w_band)
            dtypes = (BF16, F32, F32, BF16, F32, F32)

            w_in = [_Weight(w_in_ab, e, 0, c) for c in cols]
            w_out = (_Weight(w_out_ab, e, 0), _Weight(w_out_ab, e, 1))

            (qa, qa_s), (ka, ka_s), (va, va_s), (qb, qb_s), (kb, kb_s), (vb, vb_s) = (
                _proj(a_p, a_s, w, n, dt, norm=norm)
                for w, n, dt in zip(w_in, widths, dtypes))
            oa, ka_cache, va_cache = _sb_prompt(qa, ka, va, bp, seq)
            ob = _band_prompt(qb, kb, vb, bias_p, bp, seq)
            oa_s, ka_new, va_new = _sb_step(qa_s, ka_s, va_s, cache_sb_k[e],
                                            cache_sb_v[e], bs)
            ob_s, nbk, nbv = _band_step(qb_s, kb_s, vb_s, cache_band_k[e],
                                        cache_band_v[e], bias_s, bs)
            hp, hs, *pre = _mm_res(
                [(oa, oa_s, 0, w_out[0]), (ob, ob_s, 0, w_out[1])], w_sb,
                hp, hs, DENSE_TN, gain=norm_ffn[l])
            nb = min(BAND_HIST, seq)
            outs["sbk_p"].append(ka_cache)
            outs["sbv_p"].append(va_cache)
            outs["bk_p"].append(_cache_tail(kb, bp, seq, nb))
            outs["bv_p"].append(_cache_tail(vb, bp, seq, nb))
            outs["sbk_s"].append(ka_new)
            outs["sbv_s"].append(va_new)
            outs["bk_s"].append(nbk)
            outs["bv_s"].append(nbv)
        else:
            o = l // 2

            widths = (wq_c, wk_c, wk_c)
            dtypes = (BF16, F32, F32)

            w_in = [_Weight(w_qkv_c, o, 0, c) for c in (0, wq_c, wq_c + wk_c)]
            w_out = _Weight(w_out_c, o, 0)

            (q, q_s), (k, k_s), (v, v_s) = (
                _proj(a_p, a_s, w, n, dt, rope=rp, rope_s=rs, norm=norm)
                for w, n, dt, rp, rs in zip(w_in, widths, dtypes,
                                            (rope_p, rope_p, None),
                                            (rope_s, rope_s, None)))
            oc = _swa_prompt(q, k, v, sinks[o], bp, seq)
            oc_s, nck, ncv = _swa_step(q_s, k_s, v_s, cache_swa_k[o],
                                       cache_swa_v[o], sinks[o], bs)
            hp, hs, *pre = _mm_res([(oc, oc_s, 0, w_out)], wq_c, hp, hs,
                                   DENSE_TN, gain=norm_ffn[l])
            nw = min(WINDOW, seq)
            tail_shape = (bp, nw) + cache_swa_k.shape[3:]
            outs["ck_p"].append(k.reshape(bp, seq, wk_c)[:, seq - nw:].reshape(tail_shape))
            outs["cv_p"].append(v.reshape(bp, seq, wk_c)[:, seq - nw:].reshape(tail_shape))
            outs["ck_s"].append(nck)
            outs["cv_s"].append(ncv)
        next_gain = norm_mix[l + 1] if l + 1 < depth else None
        hp, hs, *pre = _ffn(hp, hs, pre, w_gate, w_up, w_down, l, next_gain)

    y_prompt = _rmsnorm(hp, norm_final3, 0, F32).reshape(bp, seq, dm)
    y_sample = _rmsnorm(hs, norm_final3, 0, F32).reshape(bs, t, dm)
    st = lambda name: jnp.stack(outs[name])
    return (y_prompt, y_sample,
            st("sbk_p"), st("sbv_p"), st("bk_p"), st("bv_p"), st("ck_p"), st("cv_p"),
            st("sbk_s"), st("sbv_s"), st("bk_s"), st("bv_s"), st("ck_s"), st("cv_s"))
```

```python
import functools
import math

import numpy as np
import jax
import jax.numpy as jnp
from jax import lax
from jax.experimental import pallas as pl
from jax.experimental.pallas import tpu as pltpu

F32 = jnp.float32
BF16 = jnp.bfloat16

CHUNK = 64
EPS = 1e-6
HEAD_DIM_AB = 128
BAND_PREV_CHUNKS = 8
BAND_HIST = BAND_PREV_CHUNKS * CHUNK
REL_CLIP = 2 * CHUNK
HEAD_DIM_C = 64
SWA_PREV_CHUNKS = 2
WINDOW = SWA_PREV_CHUNKS * CHUNK
ROPE_THETA = 500000.0
ROPE_DIM = HEAD_DIM_C // 4

LANES = 128
V7X_VMEM_LIMIT_BYTES = 60 * 1024 * 1024

DENSE_TM = 1024
GATE_UP_TM = 2048
GATE_UP_TN = 256
DENSE_TN = 512

NEG = -1e30

SB_TILE = 256
SB_HEADS = 8
SB_STEP_KEYS = 128
BAND_TQ = 128
BAND_HEADS = 4
STEP_HEADS = 8
SWA_TQ = 128


def _params(*sem):
    return pltpu.CompilerParams(dimension_semantics=sem,
                                vmem_limit_bytes=V7X_VMEM_LIMIT_BYTES)


def _chunk_of(pos):
    return lax.shift_right_logical(pos, int(math.log2(CHUNK)))


def _pick_tm(m):
    return DENSE_TM if m % DENSE_TM == 0 else m


def _rmsnorm_kernel(x_ref, g_ref, o_ref):
    x = x_ref[...]
    ms = jnp.mean(x * x, axis=-1, keepdims=True)
    y = x * lax.rsqrt(ms + EPS)
    o_ref[...] = (y * g_ref[...]).astype(o_ref.dtype)


def _rmsnorm(x, g3, layer, out_dtype):
    m, d = x.shape
    tm = 512
    return pl.pallas_call(
        _rmsnorm_kernel,
        grid=(m // tm,),
        in_specs=[pl.BlockSpec((tm, d), lambda i: (i, 0)),
                  pl.BlockSpec((None, 1, d), lambda i: (layer, 0, 0))],
        out_specs=pl.BlockSpec((tm, d), lambda i: (i, 0)),
        out_shape=jax.ShapeDtypeStruct((m, d), out_dtype),
        compiler_params=_params("arbitrary"),
        name="rmsnorm",
    )(x, g3)


class _Weight:
    def __init__(self, src, layer, kblk=0, col0=0):
        self.src, self.layer, self.kblk, self.col0 = src, layer, kblk, col0

    def spec(self, tk, tn):
        layer, kb, jb = self.layer, self.kblk, self.col0 // tn
        return pl.BlockSpec((None, tk, tn), lambda i, j: (layer, kb, jb + j))


def _rope_tile(x, c, s1, s2):
    return (x * c + pltpu.roll(x, 8, 1) * s1
            + pltpu.roll(x, LANES - 8, 1) * s2)


def _rider_in(x):
    return pl.BlockSpec(x.shape, lambda i, j: (0, 0), pipeline_mode=pl.Buffered(1))


def _rider_out(rows, tn, ni):
    return pl.BlockSpec((rows, tn),
                        lambda i, j: (0, jnp.where(i == ni - 1, j, 0)))


def _on_last_row_block(fn):
    pl.when(pl.program_id(0) == pl.num_programs(0) - 1)(fn)


def _lane_chunks(n):
    return [slice(c * LANES, (c + 1) * LANES) for c in range(n // LANES)]


def _proj_kernel(*refs, rope, norm):
    w_ref = refs[1]
    n_tab = 3 if rope else 0
    a_ref, tabs = refs[0], refs[2:2 + n_tab]
    sa_ref, stabs = refs[2 + n_tab], refs[3 + n_tab:3 + 2 * n_tab]
    r_ref, sr_ref = refs[3 + 2 * n_tab:5 + 2 * n_tab] if norm else (None, None)
    o_ref, so_ref = refs[-2:]

    def run(a_ref, tabs, r_ref, o_ref):
        acc = jnp.dot(a_ref[...], w_ref[...].astype(BF16),
                      preferred_element_type=F32)
        for sl in _lane_chunks(acc.shape[1]):
            x = acc[:, sl]
            if norm:
                x = x * r_ref[...]
            if rope:
                x = _rope_tile(x, *(t[...] for t in tabs))
            o_ref[:, sl] = x.astype(o_ref.dtype)

    run(a_ref, tabs, r_ref, o_ref)
    _on_last_row_block(lambda: run(sa_ref, stabs, sr_ref, so_ref))


def _proj(a, a_s, w, ncols, out_dtype, rope=None, rope_s=None, norm=None):
    m, k = a.shape
    ms = a_s.shape[0]
    tm = _pick_tm(m)
    tn = min(DENSE_TN, ncols)
    ni = m // tm
    in_specs = [pl.BlockSpec((tm, k), lambda i, j: (i, 0)), w.spec(k, tn)]
    args = [a, w.src]
    if rope is not None:
        nrep = rope[0].shape[0] // tm
        for t in rope:
            in_specs.append(pl.BlockSpec((tm, LANES), lambda i, j: (i % nrep, 0)))
            args.append(t)
    for x in (a_s,) + tuple(rope_s or ()):
        in_specs.append(_rider_in(x))
        args.append(x)
    if norm is not None:
        r, r_s = norm
        in_specs += [pl.BlockSpec((tm, LANES), lambda i, j: (i, 0)), _rider_in(r_s)]
        args += [r, r_s]
    return pl.pallas_call(
        functools.partial(_proj_kernel, rope=rope is not None,
                          norm=norm is not None),
        grid=(ni, ncols // tn),
        in_specs=in_specs,
        out_specs=[pl.BlockSpec((tm, tn), lambda i, j: (i, j)),
                   _rider_out(ms, tn, ni)],
        out_shape=[jax.ShapeDtypeStruct((m, ncols), out_dtype),
                   jax.ShapeDtypeStruct((ms, ncols), out_dtype)],
        compiler_params=_params("arbitrary", "arbitrary"),
        name="proj",
    )(*args)


def _gate_up_kernel(*refs, norm, cast):
    a_ref, wg_ref, wu_ref = refs[:3]
    r_ref = refs[3] if norm else None
    o_ref = refs[-2] if cast else refs[-1]
    a = a_ref[...]
    g = jnp.dot(a, wg_ref[...].astype(BF16), preferred_element_type=F32)
    u = jnp.dot(a, wu_ref[...].astype(BF16), preferred_element_type=F32)
    for sl in _lane_chunks(g.shape[1]):
        gc, uc = g[:, sl], u[:, sl]
        if norm:
            gc, uc = gc * r_ref[...], uc * r_ref[...]
        o_ref[:, sl] = (gc * jax.nn.sigmoid(gc) * uc).astype(o_ref.dtype)
    if cast:
        refs[-1][...] = refs[-3][...].astype(BF16)


def _gate_up(a, wg, wu, f, norm=None, cast=None):
    m, k = a.shape
    tm = GATE_UP_TM if m % GATE_UP_TM == 0 else m
    tn = GATE_UP_TN
    ni, nj = m // tm, f // tn
    in_specs = [pl.BlockSpec((tm, k), lambda i, j: (i, 0)),
                wg.spec(k, tn), wu.spec(k, tn)]
    args = [a, wg.src, wu.src]
    out_specs = [pl.BlockSpec((tm, tn), lambda i, j: (i, j))]
    out_shape = [jax.ShapeDtypeStruct((m, f), BF16)]
    if norm is not None:
        in_specs.append(pl.BlockSpec((tm, LANES), lambda i, j: (i, 0),
                                     pipeline_mode=pl.Buffered(1)))
        args.append(norm)
    if cast is not None:
        w3, layer = cast
        rows, n = w3.shape[1] // (ni * nj), w3.shape[2]
        assert rows * ni * nj == w3.shape[1] and rows % 16 == 0
        in_specs.append(pl.BlockSpec((None, rows, n),
                                     lambda i, j: (layer, i * nj + j, 0)))
        args.append(w3)
        out_specs.append(pl.BlockSpec((rows, n), lambda i, j: (i * nj + j, 0)))
        out_shape.append(jax.ShapeDtypeStruct(w3.shape[1:], BF16))
    out = pl.pallas_call(
        functools.partial(_gate_up_kernel, norm=norm is not None,
                          cast=cast is not None),
        grid=(ni, nj),
        in_specs=in_specs,
        out_specs=out_specs,
        out_shape=out_shape,
        compiler_params=_params("arbitrary", "arbitrary"),
        name="gate_up",
    )(*args)
    return out if cast is not None else out[0]


def _mm_res_kernel(*refs, nterms, stats):
    n = nterms
    w_refs = refs[n:2 * n]
    gain_ref = refs[3 * n + 2] if stats else None
    outs = refs[3 * n + 2 + bool(stats):]

    def run(a_refs, res_ref, o_ref, b_ref, ss_ref):
        acc = res_ref[...]
        for a_ref, w_ref in zip(a_refs, w_refs):
            acc = acc + jnp.dot(a_ref[...], w_ref[...].astype(BF16),
                                preferred_element_type=F32)
        o_ref[...] = acc
        if stats:
            _row_stats(acc, gain_ref, b_ref, ss_ref)

    p_extra = (outs[2], outs[3]) if stats else (None, None)
    s_extra = (outs[4], outs[5]) if stats else (None, None)
    run(refs[:n], refs[2 * n], outs[0], *p_extra)
    _on_last_row_block(
        lambda: run(refs[2 * n + 1:3 * n + 1], refs[3 * n + 1], outs[1], *s_extra))


def _mm_res(terms, tk, res, res_s, tn, gain=None):
    stats = gain is not None
    m, n = res.shape
    ms = res_s.shape[0]
    tm = _pick_tm(m)
    ni = m // tm
    in_specs, args = [], []
    for a, _, ka, _ in terms:
        in_specs.append(pl.BlockSpec((tm, tk), lambda i, j, ka=ka: (i, ka)))
        args.append(a)
    for _, _, _, w in terms:
        in_specs.append(w.spec(tk, tn))
        args.append(w.src)
    tile = pl.BlockSpec((tm, tn), lambda i, j: (i, j))
    tile_s = _rider_out(ms, tn, ni)
    in_specs.append(tile)
    args.append(res)
    for _, a_s, ka, _ in terms:
        in_specs.append(pl.BlockSpec((ms, tk), lambda i, j, ka=ka: (0, ka),
                                     pipeline_mode=pl.Buffered(1)))
        args.append(a_s)
    in_specs.append(tile_s)
    args.append(res_s)
    out_specs = [tile, tile_s]
    out_shape = [jax.ShapeDtypeStruct((m, n), F32),
                 jax.ShapeDtypeStruct((ms, n), F32)]
    if stats:
        in_specs.append(pl.BlockSpec((1, tn), lambda i, j: (0, j)))
        args.append(gain.astype(F32).reshape(1, n))
        out_specs += [tile, pl.BlockSpec((tm, LANES), lambda i, j: (i, 0)),
                      tile_s, pl.BlockSpec((ms, LANES), lambda i, j: (0, 0))]
        out_shape += [jax.ShapeDtypeStruct((m, n), BF16),
                      jax.ShapeDtypeStruct((m, LANES), F32),
                      jax.ShapeDtypeStruct((ms, n), BF16),
                      jax.ShapeDtypeStruct((ms, LANES), F32)]
    return pl.pallas_call(
        functools.partial(_mm_res_kernel, nterms=len(terms), stats=stats),
        grid=(ni, n // tn),
        in_specs=in_specs,
        out_specs=out_specs,
        out_shape=out_shape,
        compiler_params=_params("arbitrary", "arbitrary"),
        name="mm_res",
    )(*args)


def _row_stats(acc, gain_ref, b_ref, ss_ref):
    b_ref[...] = (acc * gain_ref[...]).astype(BF16)
    part = sum(acc[:, sl] * acc[:, sl] for sl in _lane_chunks(acc.shape[1]))
    j = pl.program_id(1)

    @pl.when(j == 0)
    def _():
        ss_ref[...] = part

    @pl.when(j != 0)
    def _():
        ss_ref[...] += part


def _down_kernel(*refs, stats):
    a_refs, w_ref, res_ref = refs[0:2], refs[2], refs[3]
    sa_refs, sres_ref = refs[4:6], refs[6]
    gain_ref = refs[7] if stats else None
    outs = refs[7 + bool(stats):]
    k = pl.program_id(2)

    def run(x_refs, r_ref, o_ref, b_ref, ss_ref):
        @pl.when(k == 0)
        def _():
            o_ref[...] = r_ref[...] + jnp.dot(
                x_refs[0][...], w_ref[...].astype(BF16),
                preferred_element_type=F32)

        @pl.when(k == 1)
        def _():
            acc = o_ref[...] + jnp.dot(
                x_refs[1][...], w_ref[...].astype(BF16),
                preferred_element_type=F32)
            o_ref[...] = acc
            if stats:
                _row_stats(acc, gain_ref, b_ref, ss_ref)

    p_extra = (outs[2], outs[3]) if stats else (None, None)
    s_extra = (outs[4], outs[5]) if stats else (None, None)
    run(a_refs, res_ref, outs[0], *p_extra)
    _on_last_row_block(lambda: run(sa_refs, sres_ref, outs[1], *s_extra))


def _down(act, act_s, w, res, res_s, gain=None):
    stats = gain is not None
    m, n = res.shape
    ms = res_s.shape[0]
    tk = act.shape[1] // 2
    tm, tn = _pick_tm(m), DENSE_TN
    ni = m // tm
    one = dict(pipeline_mode=pl.Buffered(1))
    last = ni - 1
    tile = pl.BlockSpec((tm, tn), lambda i, j, k: (i, j))
    tile_s = pl.BlockSpec((ms, tn),
                          lambda i, j, k: (0, jnp.where(i == last, j, 0)))
    in_specs = (
        [pl.BlockSpec((tm, tk), lambda i, j, k, h=h: (i, h), **one) for h in (0, 1)]
        + [pl.BlockSpec((tk, tn), lambda i, j, k: (k, j)), tile]
        + [pl.BlockSpec((ms, tk), lambda i, j, k, h=h: (0, h), **one) for h in (0, 1)]
        + [tile_s])
    args = [act, act, w, res, act_s, act_s, res_s]
    out_specs = [tile, tile_s]
    out_shape = [jax.ShapeDtypeStruct((m, n), F32),
                 jax.ShapeDtypeStruct((ms, n), F32)]
    if stats:
        in_specs.append(pl.BlockSpec((1, tn), lambda i, j, k: (0, j)))
        args.append(gain.astype(F32).reshape(1, n))
        out_specs += [tile, pl.BlockSpec((tm, LANES), lambda i, j, k: (i, 0)),
                      tile_s, pl.BlockSpec((ms, LANES), lambda i, j, k: (0, 0))]
        out_shape += [jax.ShapeDtypeStruct((m, n), BF16),
                      jax.ShapeDtypeStruct((m, LANES), F32),
                      jax.ShapeDtypeStruct((ms, n), BF16),
                      jax.ShapeDtypeStruct((ms, LANES), F32)]
    return pl.pallas_call(
        functools.partial(_down_kernel, stats=stats),
        grid=(ni, n // tn, 2),
        in_specs=in_specs,
        out_specs=out_specs,
        out_shape=out_shape,
        compiler_params=_params("arbitrary", "arbitrary", "arbitrary"),
        name="down",
    )(*args)


def _row_scale_kernel(ss_ref, o_ref, *, d):
    tot = jnp.sum(ss_ref[...], axis=1, keepdims=True)
    o_ref[...] = jnp.broadcast_to(lax.rsqrt(tot / d + EPS), o_ref.shape)


def _row_scale(ss, d):
    m = ss.shape[0]
    tm = _pick_tm(m)
    blk = pl.BlockSpec((tm, LANES), lambda i: (i, 0))
    return pl.pallas_call(
        functools.partial(_row_scale_kernel, d=d),
        grid=(m // tm,),
        in_specs=[blk],
        out_specs=blk,
        out_shape=jax.ShapeDtypeStruct(ss.shape, F32),
        compiler_params=_params("arbitrary"),
        name="row_scale",
    )(ss)


_DN_T = (((1,), (1,)), ((), ()))


def _head(h, d=HEAD_DIM_AB):
    return slice(h * d, (h + 1) * d)


def _heads_major(x):
    return pltpu.einshape("phd->hpd", x)


def _softmax_stage(ss, sinks=None):
    ms = [jnp.max(s, axis=1, keepdims=True) for s in ss]
    if sinks is not None:
        ms = [jnp.maximum(m, sk) for m, sk in zip(ms, sinks)]
    ps = [jnp.exp(s - m) for s, m in zip(ss, ms)]
    ls = [jnp.sum(p, axis=1, keepdims=True) for p in ps]
    if sinks is not None:
        ls = [l + jnp.exp(sk - m) for l, sk, m in zip(ls, sinks, ms)]
    return [p.astype(BF16) for p in ps], ls


def _sb_logs(z, valid):
    lb = jnp.minimum(z, 0.0) - jnp.log(1.0 + jnp.exp(-jnp.abs(z)))
    l1m = lb - z
    if valid is not None:
        l1m = jnp.where(valid, l1m, 0.0)
    return lb, l1m


def _split_hi_lo(x):
    hi = x.astype(BF16)
    lo = (x - hi.astype(F32)).astype(BF16)
    return jnp.concatenate([hi, lo], axis=1)


def _suffix_matrix(n):
    u = np.tril(np.ones((n, n), np.float32), -1)
    return jnp.asarray(np.concatenate([u, u], axis=0), BF16)


def _sb_prompt_kernel(q_ref, k_ref, v_ref, uu_ref, o_ref, kc_ref, vc_ref,
                      kb_ref, vb_ref, oacc_ref, cacc_ref, *, scale):
    qi = pl.program_id(2)
    tq = q_ref.shape[0]
    nh = oacc_ref.shape[0]

    @pl.when(qi == 0)
    def _():
        kb_ref[...] = k_ref[...].astype(BF16)
        vb_ref[...] = v_ref[...].astype(BF16)

    rows = pl.ds(pl.multiple_of(qi * tq, tq), tq)
    kc_ref[...] = k_ref[rows, :].reshape(kc_ref.shape)
    vc_ref[...] = v_ref[rows, :].reshape(vc_ref.shape)

    oacc_ref[...] = jnp.zeros(oacc_ref.shape, F32)
    cacc_ref[...] = jnp.zeros(cacc_ref.shape, F32)

    def tile(j, masked):
        start = pl.multiple_of(j * tq, tq)
        valid = None
        if masked:
            row = lax.broadcasted_iota(jnp.int32, (tq, tq), 0)
            col = lax.broadcasted_iota(jnp.int32, (tq, tq), 1)
            valid = col < row
        zs = [lax.dot_general(q_ref[:, _head(h)],
                              kb_ref[pl.ds(start, tq), _head(h)], _DN_T,
                              preferred_element_type=F32) * scale
              for h in range(nh)]
        logs = [_sb_logs(z, valid) for z in zs]
        sufs = [jnp.dot(_split_hi_lo(l1m), uu_ref[...],
                        preferred_element_type=F32) for _, l1m in logs]
        ws = [jnp.exp(lb + suf + cacc_ref[h])
              for h, ((lb, _), suf) in enumerate(zip(logs, sufs))]
        if masked:
            ws = [jnp.where(valid, w, 0.0) for w in ws]
        for h in range(nh):
            oacc_ref[h] += jnp.dot(ws[h].astype(BF16),
                                   vb_ref[pl.ds(start, tq), _head(h)],
                                   preferred_element_type=F32)
            cacc_ref[h] += jnp.sum(logs[h][1], axis=1, keepdims=True)

    tile(qi, True)

    def body(t, c):
        tile(qi - 1 - t, False)
        return c

    lax.fori_loop(0, qi, body, 0)
    for h in range(nh):
        o_ref[:, _head(h)] = oacc_ref[h].astype(o_ref.dtype)


def _sb_prompt(q, k, v, batch, seq):
    m, w = q.shape
    d = HEAD_DIM_AB
    nh = SB_HEADS
    tq = SB_TILE
    nq = seq // tq
    qo = pl.BlockSpec((tq, nh * d), lambda b, g, i: (b * nq + i, g))
    kv = pl.BlockSpec((seq, nh * d), lambda b, g, i: (b, g))
    cache = pl.BlockSpec((None, tq, nh, d), lambda b, g, i: (b, i, g, 0))
    cache_shape = jax.ShapeDtypeStruct((batch, seq, w // d, d), F32)
    return pl.pallas_call(
        functools.partial(_sb_prompt_kernel, scale=d ** -0.5),
        grid=(batch, w // (nh * d), nq),
        in_specs=[qo, kv, kv, pl.BlockSpec((2 * tq, tq), lambda b, g, i: (0, 0))],
        out_specs=[qo, cache, cache],
        out_shape=[jax.ShapeDtypeStruct((m, w), BF16), cache_shape, cache_shape],
        scratch_shapes=[pltpu.VMEM((seq, nh * d), BF16),
                        pltpu.VMEM((seq, nh * d), BF16),
                        pltpu.VMEM((nh, tq, d), F32),
                        pltpu.VMEM((nh, tq, 1), F32)],
        compiler_params=_params("arbitrary", "arbitrary", "arbitrary"),
        name="sb_prompt",
    )(q, k, v, _suffix_matrix(tq))


def _sb_step_kernel(q_ref, kn_ref, vn_ref, kc_ref, vc_ref, uu_ref, o_ref,
                    knc_ref, vnc_ref, kb_ref, vb_ref, *, scale):
    t = q_ref.shape[0]
    past, nh, d = kc_ref.shape
    tk = uu_ref.shape[1]
    n = past + tk
    knc_ref[...] = kn_ref[...].reshape(knc_ref.shape)
    vnc_ref[...] = vn_ref[...].reshape(vnc_ref.shape)
    kb_ref[:, 0:past, :] = _heads_major(kc_ref[...]).astype(BF16)
    vb_ref[:, 0:past, :] = _heads_major(vc_ref[...]).astype(BF16)
    for h in range(nh):
        kb_ref[h, past:past + t] = kn_ref[:, _head(h)].astype(BF16)
        vb_ref[h, past:past + t] = vn_ref[:, _head(h)].astype(BF16)
        kb_ref[h, past + t:n] = jnp.zeros((tk - t, d), BF16)
        vb_ref[h, past + t:n] = jnp.zeros((tk - t, d), BF16)

    z = jnp.concatenate(
        [lax.dot_general(q_ref[:, _head(h)], kb_ref[h], _DN_T,
                         preferred_element_type=F32) for h in range(nh)],
        axis=0) * scale
    row = lax.broadcasted_iota(jnp.int32, z.shape, 0) & (t - 1)
    col = lax.broadcasted_iota(jnp.int32, z.shape, 1)
    valid = col < past + row
    lb, l1m = _sb_logs(z, valid)
    ntiles = n // tk
    sufs = [jnp.dot(_split_hi_lo(l1m[:, j * tk:(j + 1) * tk]), uu_ref[...],
                    preferred_element_type=F32) for j in range(ntiles)]
    tots = [jnp.sum(l1m[:, j * tk:(j + 1) * tk], axis=1, keepdims=True)
            for j in range(ntiles)]
    carries = [None] * ntiles
    carry = jnp.zeros_like(tots[0])
    for j in reversed(range(ntiles)):
        carries[j] = carry
        carry = carry + tots[j]
    after = jnp.concatenate([sufs[j] + carries[j] for j in range(ntiles)],
                            axis=1)
    w = jnp.where(valid, jnp.exp(lb + after), 0.0).astype(BF16)
    for h in range(nh):
        o = jnp.dot(w[h * t:(h + 1) * t], vb_ref[h], preferred_element_type=F32)
        o_ref[:, _head(h)] = o.astype(o_ref.dtype)


def _sb_step(q, k_new, v_new, k_cache, v_cache, batch):
    m, w = q.shape
    _, past, heads, d = k_cache.shape
    t = m // batch
    tk = SB_STEP_KEYS
    nh = STEP_HEADS
    assert past % tk == 0 and t <= tk and t & (t - 1) == 0
    row = pl.BlockSpec((t, nh * d), lambda b, g: (b, g))
    cache = pl.BlockSpec((None, past, nh, d), lambda b, g: (b, 0, g, 0))
    new = pl.BlockSpec((None, t, nh, d), lambda b, g: (b, 0, g, 0))
    new_shape = jax.ShapeDtypeStruct((batch, t, heads, d), F32)
    return pl.pallas_call(
        functools.partial(_sb_step_kernel, scale=d ** -0.5),
        grid=(batch, heads // nh),
        in_specs=[row, row, row, cache, cache,
                  pl.BlockSpec((2 * tk, tk), lambda b, g: (0, 0))],
        out_specs=[row, new, new],
        out_shape=[jax.ShapeDtypeStruct((m, w), BF16), new_shape, new_shape],
        scratch_shapes=[pltpu.VMEM((nh, past + tk, d), BF16),
                        pltpu.VMEM((nh, past + tk, d), BF16)],
        compiler_params=_params("arbitrary", "arbitrary"),
        name="sb_step",
    )(q, k_new, v_new, k_cache, v_cache, _suffix_matrix(tk))


def _band_bias_kernel(r_ref, o_ref, *, n_valid):
    tq, w = o_ref.shape
    wr = r_ref.shape[1]
    t = jnp.broadcast_to(r_ref[...], (tq, wr))
    t = pltpu.roll(t, 0, 1, stride=1, stride_axis=0)
    row = lax.broadcasted_iota(jnp.int32, (tq, w), 0)
    col = lax.broadcasted_iota(jnp.int32, (tq, w), 1)
    qc = _chunk_of(row)
    kc = _chunk_of(col)
    valid = (kc >= qc) & (kc <= qc + BAND_PREV_CHUNKS) & (col < n_valid)
    o_ref[...] = jnp.where(valid, t[:, :w], NEG)


def _band_bias(rel, tq, n_valid):
    heads = rel.shape[0]
    w = BAND_HIST + BAND_TQ
    wr = w + LANES
    p = np.arange(wr)
    delta = np.where(p < w, p, p - wr)
    idx = np.clip(BAND_HIST - delta, -REL_CLIP, REL_CLIP) + REL_CLIP
    r = jnp.take(rel.astype(F32), jnp.asarray(idx, jnp.int32), axis=1)
    r = r.reshape(heads, 1, wr)
    return pl.pallas_call(
        functools.partial(_band_bias_kernel, n_valid=n_valid),
        grid=(heads,),
        in_specs=[pl.BlockSpec((None, 1, wr), lambda h: (h, 0, 0))],
        out_specs=pl.BlockSpec((None, tq, w), lambda h: (h, 0, 0)),
        out_shape=jax.ShapeDtypeStruct((heads, tq, w), F32),
        compiler_params=_params("arbitrary"),
        name="band_bias",
    )(r)


def _band_prompt_kernel(q_ref, k_ref, v_ref, bias_ref, o_ref, kp_ref, vp_ref,
                        *, scale, tq):
    seq, wd = k_ref.shape
    d = HEAD_DIM_AB
    nh = wd // d
    w = bias_ref.shape[1]
    hist = w - tq
    kp_ref[0:hist] = jnp.zeros((hist, wd), BF16)
    vp_ref[0:hist] = jnp.zeros((hist, wd), BF16)
    kp_ref[hist:hist + seq] = k_ref[...].astype(BF16)
    vp_ref[hist:hist + seq] = v_ref[...].astype(BF16)
    col = lax.broadcasted_iota(jnp.int32, (tq, w), 1)

    def tile(i, at_start):
        start = pl.multiple_of(i * tq, tq)
        ss = []
        for h in range(nh):
            s = lax.dot_general(q_ref[pl.ds(start, tq), _head(h)],
                                kp_ref[pl.ds(start, w), _head(h)], _DN_T,
                                preferred_element_type=F32)
            s = s * scale + bias_ref[h * tq:(h + 1) * tq, :]
            if at_start:
                s = jnp.where(col >= hist - start, s, NEG)
            ss.append(s)
        ps, ls = _softmax_stage(ss)
        for h in range(nh):
            o = jnp.dot(ps[h], vp_ref[pl.ds(start, w), _head(h)],
                        preferred_element_type=F32) / ls[h]
            o_ref[pl.ds(start, tq), _head(h)] = o.astype(o_ref.dtype)

    def loop(lo, hi, at_start):
        def body(i, c):
            tile(i, at_start)
            return c
        lax.fori_loop(lo, hi, body, 0)

    n_tiles = seq // tq
    n_start = min(hist // tq, n_tiles)
    loop(0, n_start, True)
    loop(n_start, n_tiles, False)


def _band_prompt(q, k, v, bias, batch, seq):
    m, wd = q.shape
    d = HEAD_DIM_AB
    nh = BAND_HEADS
    heads, tq, w = bias.shape
    bias = bias.reshape(heads // nh, nh * tq, w)
    blk = pl.BlockSpec((seq, nh * d), lambda b, g: (b, g))
    return pl.pallas_call(
        functools.partial(_band_prompt_kernel, scale=d ** -0.5, tq=tq),
        grid=(batch, heads // nh),
        in_specs=[blk, blk, blk,
                  pl.BlockSpec((None, nh * tq, w), lambda b, g: (g, 0, 0))],
        out_specs=blk,
        out_shape=jax.ShapeDtypeStruct((m, wd), BF16),
        scratch_shapes=[pltpu.VMEM((BAND_HIST + seq, nh * d), BF16),
                        pltpu.VMEM((BAND_HIST + seq, nh * d), BF16)],
        compiler_params=_params("arbitrary", "arbitrary"),
        name="band_prompt",
    )(q, k, v, bias)


def _cache_tail_kernel(x_ref, o_ref):
    o_ref[...] = x_ref[...].reshape(o_ref.shape)


def _cache_tail(x, batch, seq, rows):
    d = HEAD_DIM_AB
    nh = STEP_HEADS
    heads = x.shape[1] // d
    assert seq % rows == 0
    last = seq // rows - 1
    return pl.pallas_call(
        _cache_tail_kernel,
        grid=(batch, heads // nh),
        in_specs=[pl.BlockSpec((rows, nh * d),
                               lambda b, g: (b * (last + 1) + last, g))],
        out_specs=pl.BlockSpec((None, rows, nh, d), lambda b, g: (b, 0, g, 0)),
        out_shape=jax.ShapeDtypeStruct((batch, rows, heads, d), x.dtype),
        compiler_params=_params("arbitrary", "arbitrary"),
        name="cache_tail",
    )(x)


def _roll_cache(new_ref, cache, new_rows_ref, t):
    past, nh, d = cache.shape
    new_ref[0:past - t] = cache[t:]
    for h in range(nh):
        new_ref[past - t:past, h, :] = new_rows_ref[:, _head(h, d)]


def _band_step_kernel(q_ref, kn_ref, vn_ref, kc_ref, vc_ref, bias_ref, o_ref,
                      nk_ref, nv_ref, kb_ref, vb_ref, *, scale):
    t = q_ref.shape[0]
    past, nh, d = kc_ref.shape
    w = bias_ref.shape[1]
    kc, vc = kc_ref[...], vc_ref[...]
    _roll_cache(nk_ref, kc, kn_ref, t)
    _roll_cache(nv_ref, vc, vn_ref, t)
    kb_ref[:, 0:past, :] = _heads_major(kc).astype(BF16)
    vb_ref[:, 0:past, :] = _heads_major(vc).astype(BF16)
    for h in range(nh):
        kb_ref[h, past:past + t] = kn_ref[:, _head(h)].astype(BF16)
        vb_ref[h, past:past + t] = vn_ref[:, _head(h)].astype(BF16)
        kb_ref[h, past + t:w] = jnp.zeros((w - past - t, d), BF16)
        vb_ref[h, past + t:w] = jnp.zeros((w - past - t, d), BF16)
    ss = [lax.dot_general(q_ref[:, _head(h)], kb_ref[h], _DN_T,
                          preferred_element_type=F32) * scale
          + bias_ref[h * t:(h + 1) * t, :] for h in range(nh)]
    ps, ls = _softmax_stage(ss)
    for h in range(nh):
        o = jnp.dot(ps[h], vb_ref[h], preferred_element_type=F32) / ls[h]
        o_ref[:, _head(h)] = o.astype(o_ref.dtype)


def _band_step(q, k_new, v_new, k_cache, v_cache, bias, batch):
    m, wd = q.shape
    _, past, heads, d = k_cache.shape
    t = m // batch
    nh = STEP_HEADS
    w = bias.shape[2]
    assert past == BAND_HIST and past + t <= w
    bias = bias.reshape(heads // nh, nh * t, w)
    row = pl.BlockSpec((t, nh * d), lambda b, g: (b, g))
    cache = pl.BlockSpec((None, past, nh, d), lambda b, g: (b, 0, g, 0))
    return pl.pallas_call(
        functools.partial(_band_step_kernel, scale=d ** -0.5),
        grid=(batch, heads // nh),
        in_specs=[row, row, row, cache, cache,
                  pl.BlockSpec((None, nh * t, w), lambda b, g: (g, 0, 0))],
        out_specs=[row, cache, cache],
        out_shape=[jax.ShapeDtypeStruct((m, wd), BF16),
                   jax.ShapeDtypeStruct(k_cache.shape, F32),
                   jax.ShapeDtypeStruct(v_cache.shape, F32)],
        scratch_shapes=[pltpu.VMEM((nh, w, d), BF16),
                        pltpu.VMEM((nh, w, d), BF16)],
        compiler_params=_params("arbitrary", "arbitrary"),
        name="band_step",
    )(q, k_new, v_new, k_cache, v_cache, bias)


def _split_pair(blk):
    rolled = pltpu.roll(blk, HEAD_DIM_C, 1)
    lane = lax.broadcasted_iota(jnp.int32, blk.shape, 1)
    low = lane < HEAD_DIM_C
    zero = jnp.zeros_like(blk)
    return [jnp.where(low, blk, zero), jnp.where(low, zero, rolled),
            jnp.where(low, rolled, zero), jnp.where(low, zero, blk)]


def _swa_stage(qs, kls, khs, vls, vhs, valid, sinks, scale):
    ss = []
    for q, kl, kh in zip(qs, kls, khs):
        for kx in (kl, kh):
            s = lax.dot_general(q, kx, _DN_T, preferred_element_type=F32)
            ss.append(jnp.where(valid, s * scale, NEG))
    ps, ls = _softmax_stage(ss, sinks)
    return [jnp.dot(ps[2 * i], vls[i], preferred_element_type=F32) / ls[2 * i]
            + jnp.dot(ps[2 * i + 1], vhs[i], preferred_element_type=F32)
            / ls[2 * i + 1] for i in range(len(qs))]


def _swa_prompt_kernel(sinks_ref, q_ref, k_ref, v_ref, o_ref, ks_ref, vs_ref,
                       *, scale, tq):
    hp = pl.program_id(1)
    seq = k_ref.shape[0]
    band = WINDOW + tq
    npair = q_ref.shape[1] // LANES
    group = npair // 2

    for src, dst in ((k_ref, ks_ref), (v_ref, vs_ref)):
        for n, x in enumerate(_split_pair(src[...])):
            dst[n, 0:WINDOW] = jnp.zeros((WINDOW, LANES), BF16)
            dst[n, WINDOW:WINDOW + seq] = x.astype(BF16)

    row = lax.broadcasted_iota(jnp.int32, (tq, band), 0)
    col = lax.broadcasted_iota(jnp.int32, (tq, band), 1)
    qc = _chunk_of(row)
    kc = _chunk_of(col)
    in_band = (kc >= qc) & (kc <= qc + SWA_PREV_CHUNKS)
    sinks = [sinks_ref[hp * 2 * npair + n] for n in range(2 * npair)]

    def body(i, c):
        start = pl.multiple_of(i * tq, tq)
        valid = in_band & (col >= WINDOW - start)
        kv = [[r[n, pl.ds(start, band), :] for n in range(4)]
              for r in (ks_ref, vs_ref)]
        qs = [q_ref[pl.ds(start, tq), _head(p, LANES)] for p in range(npair)]
        sel = [2 * (p // group) for p in range(npair)]
        outs = _swa_stage(qs, [kv[0][s] for s in sel], [kv[0][s + 1] for s in sel],
                          [kv[1][s] for s in sel], [kv[1][s + 1] for s in sel],
                          valid, sinks, scale)
        for p in range(npair):
            o_ref[pl.ds(start, tq), _head(p, LANES)] = outs[p].astype(o_ref.dtype)
        return c

    lax.fori_loop(0, seq // tq, body, 0)


def _swa_prompt(q, k, v, sinks, batch, seq):
    m, wq = q.shape
    wk = k.shape[1]
    npairs = wk // LANES
    qw = wq // npairs
    qo = pl.BlockSpec((seq, qw), lambda b, h, s: (b, h))
    kv = pl.BlockSpec((seq, LANES), lambda b, h, s: (b, h))
    grid_spec = pltpu.PrefetchScalarGridSpec(
        num_scalar_prefetch=1,
        grid=(batch, npairs),
        in_specs=[qo, kv, kv],
        out_specs=qo,
        scratch_shapes=[pltpu.VMEM((4, WINDOW + seq, LANES), BF16),
                        pltpu.VMEM((4, WINDOW + seq, LANES), BF16)],
    )
    return pl.pallas_call(
        functools.partial(_swa_prompt_kernel, scale=HEAD_DIM_C ** -0.5,
                          tq=SWA_TQ),
        grid_spec=grid_spec,
        out_shape=jax.ShapeDtypeStruct((m, wq), BF16),
        compiler_params=_params("arbitrary", "arbitrary"),
        name="swa_prompt",
    )(sinks, q, k, v)


def _swa_step_kernel(sinks_ref, q_ref, kn_ref, vn_ref, kc_ref, vc_ref, o_ref,
                     nk_ref, nv_ref, *, scale):
    t = q_ref.shape[0]
    past = kc_ref.shape[0]
    wk = kn_ref.shape[1]
    band = 2 * LANES
    nkv = wk // LANES
    npair = q_ref.shape[1] // LANES
    group = npair // (2 * nkv)
    kc, vc = kc_ref[...].reshape(past, wk), vc_ref[...].reshape(past, wk)
    kn, vn = kn_ref[...], vn_ref[...]
    nk_ref[...] = jnp.concatenate([kc[t:], kn], axis=0).reshape(nk_ref.shape)
    nv_ref[...] = jnp.concatenate([vc[t:], vn], axis=0).reshape(nv_ref.shape)
    pad = jnp.zeros((band - past - t, wk), F32)
    k_all = jnp.concatenate([kc, kn, pad], axis=0)
    v_all = jnp.concatenate([vc, vn, pad], axis=0)
    col = lax.broadcasted_iota(jnp.int32, (t, band), 1)
    valid = col < past + t
    kx = [[x.astype(BF16) for x in _split_pair(k_all[:, _head(g, LANES)])]
          for g in range(nkv)]
    vx = [[x.astype(BF16) for x in _split_pair(v_all[:, _head(g, LANES)])]
          for g in range(nkv)]
    qs = [q_ref[:, _head(p, LANES)] for p in range(npair)]
    where = [(p // (2 * group), 2 * ((p // group) % 2)) for p in range(npair)]
    sinks = [sinks_ref[n] for n in range(2 * npair)]
    outs = _swa_stage(qs, [kx[g][s] for g, s in where],
                      [kx[g][s + 1] for g, s in where],
                      [vx[g][s] for g, s in where],
                      [vx[g][s + 1] for g, s in where], valid, sinks, scale)
    for p in range(npair):
        o_ref[:, _head(p, LANES)] = outs[p].astype(o_ref.dtype)


def _swa_step(q, k_new, v_new, k_cache, v_cache, sinks, batch):
    m, wq = q.shape
    wk = k_new.shape[1]
    t = m // batch
    past = k_cache.shape[1]
    assert past + t <= 2 * LANES
    qrow = pl.BlockSpec((t, wq), lambda b, s: (b, 0))
    krow = pl.BlockSpec((t, wk), lambda b, s: (b, 0))
    cache = pl.BlockSpec((None,) + k_cache.shape[1:], lambda b, s: (b, 0, 0, 0))
    grid_spec = pltpu.PrefetchScalarGridSpec(
        num_scalar_prefetch=1,
        grid=(batch,),
        in_specs=[qrow, krow, krow, cache, cache],
        out_specs=[qrow, cache, cache],
    )
    return pl.pallas_call(
        functools.partial(_swa_step_kernel, scale=HEAD_DIM_C ** -0.5),
        grid_spec=grid_spec,
        out_shape=[jax.ShapeDtypeStruct((m, wq), BF16),
                   jax.ShapeDtypeStruct(k_cache.shape, F32),
                   jax.ShapeDtypeStruct(v_cache.shape, F32)],
        compiler_params=_params("arbitrary"),
        name="swa_step",
    )(sinks, q, k_new, v_new, k_cache, v_cache)


def _rope_tables(pos):
    half = ROPE_DIM // 2
    inv = jnp.power(ROPE_THETA, -jnp.arange(half, dtype=F32) / half)
    ang = pos.astype(F32)[:, None] * inv[None, :]
    cos, sin = jnp.cos(ang), jnp.sin(ang)
    n = pos.shape[0]
    reps = LANES // HEAD_DIM_C
    rest = HEAD_DIM_C - ROPE_DIM
    ones = jnp.ones((n, rest), F32)
    zeros = jnp.zeros((n, rest), F32)
    zh = jnp.zeros((n, half), F32)
    c = jnp.tile(jnp.concatenate([cos, cos, ones], axis=1), (1, reps))
    s1 = jnp.tile(jnp.concatenate([zh, sin, zeros], axis=1), (1, reps))
    s2 = jnp.tile(jnp.concatenate([-sin, zh, zeros], axis=1), (1, reps))
    return c, s1, s2


def _ffn(hp, hs, pre, w_gate, w_up, w_down, layer, next_gain):
    f = w_gate.shape[-1]
    d = hp.shape[1]
    wg, wu = _Weight(w_gate, layer), _Weight(w_up, layer)
    a_p, ss_p, a_s, ss_s = pre
    act_p, wd = _gate_up(a_p, wg, wu, f, norm=_row_scale(ss_p, d),
                         cast=(w_down, layer))
    act_s = _gate_up(a_s, wg, wu, f, norm=_row_scale(ss_s, d))
    return _down(act_p, act_s, wd, hp, hs, gain=next_gain)


def kernel(x_prompt, x_sample, cache_sb_k, cache_sb_v, cache_band_k, cache_band_v, cache_swa_k, cache_swa_v, norm_mix, norm_ffn, norm_final, w_in_ab, w_out_ab, rel_bias, w_qkv_c, w_out_c, sinks, w_gate, w_up, w_down):
    bp, seq, dm = x_prompt.shape
    bs, t, _ = x_sample.shape
    depth = norm_mix.shape[0]
    past = cache_sb_k.shape[2]
    hp = x_prompt.reshape(bp * seq, dm)
    hs = x_sample.reshape(bs * t, dm)
    norm_mix3 = norm_mix.reshape(depth, 1, dm)
    norm_final3 = norm_final.reshape(1, 1, dm)

    w_sb = cache_sb_k.shape[3] * cache_sb_k.shape[4]
    w_band = cache_band_k.shape[3] * cache_band_k.shape[4]
    wq_c = w_out_c.shape[1]
    wk_c = cache_swa_k.shape[3] * cache_swa_k.shape[4]
    p_band = cache_band_k.shape[2]

    rope_p = _rope_tables(jnp.arange(seq))
    rope_s = tuple(jnp.tile(x, (bs, 1)) for x in _rope_tables(past + jnp.arange(t)))

    outs = {name: [] for name in (
        "sbk_p", "sbv_p", "bk_p", "bv_p", "ck_p", "cv_p",
        "sbk_s", "sbv_s", "bk_s", "bv_s", "ck_s", "cv_s")}

    pre = None
    for l in range(depth):
        if pre is None:
            a_p = _rmsnorm(hp, norm_mix3, l, BF16)
            a_s = _rmsnorm(hs, norm_mix3, l, BF16)
            norm = None
        else:
            a_p, ss_p, a_s, ss_s = pre
            norm = (_row_scale(ss_p, dm), _row_scale(ss_s, dm))
        if l % 2 == 0:
            e = l // 2
            cols = (0, w_sb, 2 * w_sb, 3 * w_sb, 3 * w_sb + w_band, 3 * w_sb + 2 * w_band)
            bias_p = _band_bias(rel_bias[e], BAND_TQ, BAND_HIST + BAND_TQ)
            bias_s = _band_bias(rel_bias[e], t, p_band + t)

            widths = (w_sb, w_sb, w_sb, w_band, w_band, w_band)
            dtypes = (BF16, F32, F32, BF16, F32, F32)

            w_in = [_Weight(w_in_ab, e, 0, c) for c in cols]
            w_out = (_Weight(w_out_ab, e, 0), _Weight(w_out_ab, e, 1))

            (qa, qa_s), (ka, ka_s), (va, va_s), (qb, qb_s), (kb, kb_s), (vb, vb_s) = (
                _proj(a_p, a_s, w, n, dt, norm=norm)
                for w, n, dt in zip(w_in, widths, dtypes))
            oa, ka_cache, va_cache = _sb_prompt(qa, ka, va, bp, seq)
            ob = _band_prompt(qb, kb, vb, bias_p, bp, seq)
            oa_s, ka_new, va_new = _sb_step(qa_s, ka_s, va_s, cache_sb_k[e],
                                            cache_sb_v[e], bs)
            ob_s, nbk, nbv = _band_step(qb_s, kb_s, vb_s, cache_band_k[e],
                                        cache_band_v[e], bias_s, bs)
            hp, hs, *pre = _mm_res(
                [(oa, oa_s, 0, w_out[0]), (ob, ob_s, 0, w_out[1])], w_sb,
                hp, hs, DENSE_TN, gain=norm_ffn[l])
            nb = min(BAND_HIST, seq)
            outs["sbk_p"].append(ka_cache)
            outs["sbv_p"].append(va_cache)
            outs["bk_p"].append(_cache_tail(kb, bp, seq, nb))
            outs["bv_p"].append(_cache_tail(vb, bp, seq, nb))
            outs["sbk_s"].append(ka_new)
            outs["sbv_s"].append(va_new)
            outs["bk_s"].append(nbk)
            outs["bv_s"].append(nbv)
        else:
            o = l // 2

            widths = (wq_c, wk_c, wk_c)
            dtypes = (BF16, F32, F32)

            w_in = [_Weight(w_qkv_c, o, 0, c) for c in (0, wq_c, wq_c + wk_c)]
            w_out = _Weight(w_out_c, o, 0)

            (q, q_s), (k, k_s), (v, v_s) = (
                _proj(a_p, a_s, w, n, dt, rope=rp, rope_s=rs, norm=norm)
                for w, n, dt, rp, rs in zip(w_in, widths, dtypes,
                                            (rope_p, rope_p, None),
                                            (rope_s, rope_s, None)))
            oc = _swa_prompt(q, k, v, sinks[o], bp, seq)
            oc_s, nck, ncv = _swa_step(q_s, k_s, v_s, cache_swa_k[o],
                                       cache_swa_v[o], sinks[o], bs)
            hp, hs, *pre = _mm_res([(oc, oc_s, 0, w_out)], wq_c, hp, hs,
                                   DENSE_TN, gain=norm_ffn[l])
            nw = min(WINDOW, seq)
            tail_shape = (bp, nw) + cache_swa_k.shape[3:]
            outs["ck_p"].append(k.reshape(bp, seq, wk_c)[:, seq - nw:].reshape(tail_shape))
            outs["cv_p"].append(v.reshape(bp, seq, wk_c)[:, seq - nw:].reshape(tail_shape))
            outs["ck_s"].append(nck)
            outs["cv_s"].append(ncv)
        next_gain = norm_mix[l + 1] if l + 1 < depth else None
        hp, hs, *pre = _ffn(hp, hs, pre, w_gate, w_up, w_down, l, next_gain)

    y_prompt = _rmsnorm(hp, norm_final3, 0, F32).reshape(bp, seq, dm)
    y_sample = _rmsnorm(hs, norm_final3, 0, F32).reshape(bs, t, dm)
    st = lambda name: jnp.stack(outs[name])
    return (y_prompt, y_sample,
            st("sbk_p"), st("sbv_p"), st("bk_p"), st("bv_p"), st("ck_p"), st("cv_p"),
            st("sbk_s"), st("sbv_s"), st("bk_s"), st("bv_s"), st("ck_s"), st("cv_s"))
```
